```python
import math
import jax, jax.numpy as jnp
from jax import lax
import numpy as np

D_MODEL = 1024
BATCH = 4
SEQ = 8192
DEPTH = 1

D_MIX = D_MODEL
S5_WIDTH = D_MIX // 2
S5_GROUP = 16
S5_GROUPS = S5_WIDTH // S5_GROUP
S5_STATE = 64
ML_WIDTH = D_MIX - S5_WIDTH
ML_HEADS = 4
ML_HEAD_DIM = ML_WIDTH // ML_HEADS
ML_CHUNK = 64
CONV_WIDTH = 4
D_IN = S5_WIDTH + 4 * ML_WIDTH + 2 * ML_HEADS
N_EXPERTS = 32
TOP_K = 4
D_FF = D_MODEL
SWIGLU_LIMIT = 7.0
SWIGLU_ALPHA = 1.702
MOE_BLOCK = 128
RMS_EPS = 1e-5
LN_EPS = 1e-6

kernel_name = "hymba_s5_mlstm_moe_block"


def rmsnorm(x, g):
    xf = x.astype(jnp.float32)
    y = xf * lax.rsqrt(jnp.mean(xf * xf, axis=-1, keepdims=True) + RMS_EPS)
    return (y * g.astype(jnp.float32)).astype(x.dtype)


def s5_mixer(u, log_dt, a_re, a_im, b_re, b_im, c_re, c_im, d_skip, w_glu, b_glu):
    f32 = jnp.float32
    bsz, seq, _ = u.shape
    uf = u.astype(f32).reshape(bsz, seq, S5_GROUPS, S5_GROUP)
    a = lax.complex(jnp.minimum(a_re.astype(f32), -1e-4), a_im.astype(f32))
    dt = jnp.exp(log_dt.astype(f32))[:, None]
    a_bar = jnp.exp(dt * a)
    b_mat = lax.complex(b_re.astype(f32), b_im.astype(f32))
    b_bar = ((a_bar - 1.0) / a)[..., None] * b_mat
    c_mat = lax.complex(c_re.astype(f32), c_im.astype(f32))
    bu = jnp.einsum('gnp,bsgp->bsgn', b_bar, uf.astype(jnp.complex64))
    a_seq = jnp.broadcast_to(a_bar, (1, seq, S5_GROUPS, S5_STATE))

    def combine(left, right):
        a_l, x_l = left
        a_r, x_r = right
        return a_r * a_l, a_r * x_l + x_r

    _, states = lax.associative_scan(combine, (a_seq, bu), axis=1)
    y = jnp.einsum('gpn,bsgn->bsgp', c_mat, states).real \
        + d_skip.astype(f32).reshape(S5_GROUPS, S5_GROUP) * uf
    y = y.reshape(bsz, seq, S5_WIDTH)
    g = jax.nn.gelu(y)
    out = g * jax.nn.sigmoid(g @ w_glu.astype(f32) + b_glu.astype(f32))
    return out.astype(u.dtype)


def causal_conv(x, w, b):
    seq = x.shape[1]
    xp = jnp.pad(x, ((0, 0), (CONV_WIDTH - 1, 0), (0, 0)))
    out = b
    for j in range(CONV_WIDTH):
        out = out + w[j] * xp[:, j:j + seq]
    return out


def mlstm_chunkwise(q, k, v, i_pre, f_pre):
    bsz, nh, seq, dh = q.shape
    nc = seq // ML_CHUNK
    L = ML_CHUNK

    def to_chunks(t):
        return jnp.moveaxis(t.reshape(bsz, nh, nc, L, *t.shape[3:]), 2, 0)

    qc = to_chunks(q)
    kc = to_chunks(k * (1.0 / math.sqrt(dh)))
    vc = to_chunks(v)
    ic = to_chunks(i_pre)
    lfc = to_chunks(jax.nn.log_sigmoid(f_pre))
    causal = jnp.tril(jnp.ones((L, L), dtype=bool))

    def step(carry, inp):
        c_prev, n_prev, m_prev = carry
        q_c, k_c, v_c, i_c, lf_c = inp
        b = jnp.cumsum(lf_c, axis=-1)
        log_d = jnp.where(causal, b[..., :, None] - b[..., None, :] + i_c[..., None, :], -jnp.inf)
        log_inter = b + m_prev[..., None]
        m = jnp.maximum(log_inter, jnp.max(log_d, axis=-1))
        w_intra = jnp.exp(log_d - m[..., None])
        w_inter = jnp.exp(log_inter - m)
        s = jnp.einsum('bhtd,bhsd->bhts', q_c, k_c) * w_intra
        num = w_inter[..., None] * jnp.einsum('bhtd,bhde->bhte', q_c, c_prev) \
            + jnp.einsum('bhts,bhse->bhte', s, v_c)
        den = w_inter * jnp.einsum('bhtd,bhd->bht', q_c, n_prev) + jnp.sum(s, axis=-1)
        h = num / jnp.maximum(jnp.abs(den), jnp.exp(-m))[..., None]
        b_last = b[..., -1]
        log_w_state = b_last[..., None] - b + i_c
        m_next = jnp.maximum(b_last + m_prev, jnp.max(log_w_state, axis=-1))
        decay = jnp.exp(b_last + m_prev - m_next)
        w_state = jnp.exp(log_w_state - m_next[..., None])
        c_next = decay[..., None, None] * c_prev + jnp.einsum('bhs,bhsd,bhse->bhde', w_state, k_c, v_c)
        n_next = decay[..., None] * n_prev + jnp.einsum('bhs,bhsd->bhd', w_state, k_c)
        return (c_next, n_next, m_next), h

    init = (jnp.zeros((bsz, nh, dh, dh), jnp.float32),
            jnp.zeros((bsz, nh, dh), jnp.float32),
            jnp.zeros((bsz, nh), jnp.float32))
    _, hs = lax.scan(step, init, (qc, kc, vc, ic, lfc))
    return jnp.moveaxis(hs, 0, 2).reshape(bsz, nh, seq, dh)


def mlstm_mixer(q_pre, k_pre, v, o_pre, gate_pre, conv_w, conv_b, b_gates, norm_g):
    f32 = jnp.float32
    bsz, seq, _ = v.shape
    qk = jnp.concatenate([q_pre, k_pre], axis=-1).astype(f32)
    qk = jax.nn.silu(causal_conv(qk, conv_w.astype(f32), conv_b.astype(f32)))
    q, k = jnp.split(qk, 2, axis=-1)

    def heads(t):
        return t.reshape(bsz, seq, ML_HEADS, ML_HEAD_DIM).transpose(0, 2, 1, 3)

    gates = (gate_pre.astype(f32) + b_gates.astype(f32)).transpose(0, 2, 1)
    i_pre, f_pre = gates[:, :ML_HEADS], gates[:, ML_HEADS:]
    h = mlstm_chunkwise(heads(q), heads(k), heads(v.astype(f32)), i_pre, f_pre)
    mu = jnp.mean(h, axis=-1, keepdims=True)
    var = jnp.mean(jnp.square(h - mu), axis=-1, keepdims=True)
    h = (h - mu) * lax.rsqrt(var + LN_EPS)
    h = h.transpose(0, 2, 1, 3).reshape(bsz, seq, ML_WIDTH) * norm_g.astype(f32)
    out = h * jax.nn.sigmoid(o_pre.astype(f32))
    return out.astype(v.dtype)


def moe_ffn(x, w_router, b_router, w_up, b_up, w_down, b_down):
    bsz, seq, d = x.shape
    t = x.reshape(-1, d)
    n_tok = t.shape[0]
    logits = (t @ w_router + b_router).astype(jnp.float32)
    top_vals, top_idx = lax.top_k(logits, TOP_K)
    gate = jax.nn.softmax(top_vals, axis=-1)
    n_assign = n_tok * TOP_K
    flat_e = top_idx.reshape(-1)
    order = jnp.argsort(flat_e)
    sorted_e = flat_e[order]
    sorted_tok = (order // TOP_K).astype(jnp.int32)
    sorted_gate = gate.reshape(-1)[order]
    counts = jnp.bincount(flat_e, length=N_EXPERTS)
    start = jnp.cumsum(counts) - counts
    padded = (counts + MOE_BLOCK - 1) // MOE_BLOCK * MOE_BLOCK
    pad_end = jnp.cumsum(padded)
    pad_start = pad_end - padded
    dest = pad_start[sorted_e] + (jnp.arange(n_assign) - start[sorted_e])
    n_slots = (n_assign + MOE_BLOCK - 1) // MOE_BLOCK * MOE_BLOCK + N_EXPERTS * MOE_BLOCK
    n_blocks = n_slots // MOE_BLOCK
    slot_tok = jnp.zeros((n_slots,), jnp.int32).at[dest].set(sorted_tok)
    block_start = jnp.arange(n_blocks) * MOE_BLOCK
    block_e = jnp.minimum(jnp.sum(pad_end[None, :] <= block_start[:, None], axis=-1), N_EXPERTS - 1)
    xs = t[slot_tok].reshape(n_blocks, MOE_BLOCK, d)

    def expert_block(args):
        xb, e = args
        h = xb @ w_up[e] + b_up[e]
        g = jnp.minimum(h[:, :D_FF], SWIGLU_LIMIT)
        lin = jnp.clip(h[:, D_FF:], -SWIGLU_LIMIT, SWIGLU_LIMIT)
        act = g * jax.nn.sigmoid(SWIGLU_ALPHA * g) * (lin + 1.0)
        return act @ w_down[e] + b_down[e]

    ys = lax.map(expert_block, (xs, block_e)).reshape(n_slots, d)
    y_assign = ys[dest].astype(jnp.float32) * sorted_gate[:, None]
    y = jax.ops.segment_sum(y_assign, sorted_tok, num_segments=n_tok)
    return y.reshape(bsz, seq, d).astype(x.dtype)


def setup_inputs(seed: int = 0) -> dict:
    key = jax.random.key(seed)
    ks = jax.random.split(key, 28)
    nrm = jax.random.normal
    f32 = jnp.float32
    L_ = DEPTH
    x = nrm(ks[0], (BATCH, SEQ, D_MODEL), f32)
    norm_mix_g = 1.0 + 0.01 * nrm(ks[1], (L_, D_MODEL), f32)
    w_in = nrm(ks[2], (L_, D_MODEL, D_IN), f32) * D_MODEL ** -0.5
    s5_log_dt = jax.random.uniform(ks[3], (L_, S5_GROUPS), f32, math.log(1e-3), math.log(1e-1))
    s5_a_re = -0.5 + 0.01 * nrm(ks[4], (L_, S5_GROUPS, S5_STATE), f32)
    s5_a_im = jnp.pi * jnp.arange(S5_STATE, dtype=f32) + 0.01 * nrm(ks[5], (L_, S5_GROUPS, S5_STATE), f32)
    s5_b_re = nrm(ks[6], (L_, S5_GROUPS, S5_STATE, S5_GROUP), f32) * (2 * S5_GROUP) ** -0.5
    s5_b_im = nrm(ks[7], (L_, S5_GROUPS, S5_STATE, S5_GROUP), f32) * (2 * S5_GROUP) ** -0.5
    s5_c_re = nrm(ks[8], (L_, S5_GROUPS, S5_GROUP, S5_STATE), f32) * (2 * S5_STATE) ** -0.5
    s5_c_im = nrm(ks[9], (L_, S5_GROUPS, S5_GROUP, S5_STATE), f32) * (2 * S5_STATE) ** -0.5
    s5_d = nrm(ks[10], (L_, S5_WIDTH), f32)
    s5_w_glu = nrm(ks[11], (L_, S5_WIDTH, S5_WIDTH), f32) * S5_WIDTH ** -0.5
    s5_b_glu = 0.01 * nrm(ks[12], (L_, S5_WIDTH), f32)
    ml_conv_w = nrm(ks[13], (L_, CONV_WIDTH, 2 * ML_WIDTH), f32) * CONV_WIDTH ** -0.5
    ml_conv_b = 0.01 * nrm(ks[14], (L_, 2 * ML_WIDTH), f32)
    ml_b_gates = jnp.concatenate([
        0.1 * nrm(ks[15], (L_, ML_HEADS), f32),
        jnp.linspace(3.0, 6.0, ML_HEADS, dtype=f32) + 0.01 * nrm(ks[16], (L_, ML_HEADS), f32)], axis=-1)
    ml_norm_g = 1.0 + 0.01 * nrm(ks[17], (L_, ML_WIDTH), f32)
    w_out = nrm(ks[18], (L_, D_MIX, D_MODEL), f32) * D_MIX ** -0.5
    norm_ffn_g = 1.0 + 0.01 * nrm(ks[19], (L_, D_MODEL), f32)
    w_router = nrm(ks[20], (L_, D_MODEL, N_EXPERTS), f32) * D_MODEL ** -0.5
    b_router = 0.01 * nrm(ks[21], (L_, N_EXPERTS), f32)
    w_up = nrm(ks[22], (L_, N_EXPERTS, D_MODEL, 2 * D_FF), f32) * D_MODEL ** -0.5
    b_up = 0.01 * nrm(ks[23], (L_, N_EXPERTS, 2 * D_FF), f32)
    w_down = nrm(ks[24], (L_, N_EXPERTS, D_FF, D_MODEL), f32) * D_FF ** -0.5
    b_down = 0.01 * nrm(ks[25], (L_, N_EXPERTS, D_MODEL), f32)
    norm_final_g = 1.0 + 0.01 * nrm(ks[26], (D_MODEL,), f32)
    return {"x": x, "norm_mix_g": norm_mix_g, "w_in": w_in, "s5_log_dt": s5_log_dt,
            "s5_a_re": s5_a_re, "s5_a_im": s5_a_im, "s5_b_re": s5_b_re, "s5_b_im": s5_b_im,
            "s5_c_re": s5_c_re, "s5_c_im": s5_c_im, "s5_d": s5_d, "s5_w_glu": s5_w_glu,
            "s5_b_glu": s5_b_glu, "ml_conv_w": ml_conv_w, "ml_conv_b": ml_conv_b,
            "ml_b_gates": ml_b_gates, "ml_norm_g": ml_norm_g, "w_out": w_out,
            "norm_ffn_g": norm_ffn_g, "w_router": w_router, "b_router": b_router,
            "w_up": w_up, "b_up": b_up, "w_down": w_down, "b_down": b_down,
            "norm_final_g": norm_final_g}


def reference(x, norm_mix_g, w_in, s5_log_dt, s5_a_re, s5_a_im, s5_b_re, s5_b_im, s5_c_re,
              s5_c_im, s5_d, s5_w_glu, s5_b_glu, ml_conv_w, ml_conv_b, ml_b_gates, ml_norm_g,
              w_out, norm_ffn_g, w_router, b_router, w_up, b_up, w_down, b_down, norm_final_g):
    h = x
    o_q = S5_WIDTH
    o_k = o_q + ML_WIDTH
    o_v = o_k + ML_WIDTH
    o_o = o_v + ML_WIDTH
    o_g = o_o + ML_WIDTH
    for l in range(DEPTH):
        hn = rmsnorm(h, norm_mix_g[l])
        proj = hn @ w_in[l]
        y_s5 = s5_mixer(proj[..., :o_q], s5_log_dt[l], s5_a_re[l], s5_a_im[l], s5_b_re[l],
                        s5_b_im[l], s5_c_re[l], s5_c_im[l], s5_d[l], s5_w_glu[l], s5_b_glu[l])
        y_ml = mlstm_mixer(proj[..., o_q:o_k], proj[..., o_k:o_v], proj[..., o_v:o_o],
                           proj[..., o_o:o_g], proj[..., o_g:], ml_conv_w[l], ml_conv_b[l],
                           ml_b_gates[l], ml_norm_g[l])
        mix = jnp.concatenate([y_s5, y_ml], axis=-1)
        h = h + mix @ w_out[l]
        h = h + moe_ffn(rmsnorm(h, norm_ffn_g[l]), w_router[l], b_router[l], w_up[l], b_up[l],
                        w_down[l], b_down[l])
    return rmsnorm(h, norm_final_g)
```

```python
import functools
import math

import jax
import jax.numpy as jnp
from jax import lax
from jax.experimental import pallas as pl
from jax.experimental.pallas import tpu as pltpu

F32 = jnp.float32
BF16 = jnp.bfloat16
I32 = jnp.int32

S5_GROUP = 16
S5_STATE = 64
ML_HEADS = 4
CONV_WIDTH = 4
N_EXPERTS = 32
TOP_K = 4
SWIGLU_LIMIT = 7.0
SWIGLU_ALPHA = 1.702
RMS_EPS = 1e-5
LN_EPS = 1e-6

LANES = 128
SUBLANES = 8
S5_CHUNK = LANES
ML_CHUNK = 128
PROJ_TILE = 512
EXPERT_TILE = 256
COMBINE_TILE = 256
VMEM_LIMIT = 56 * 1024 * 1024

_NT = (((1,), (1,)), ((), ()))
_TN = (((0,), (0,)), ((), ()))


def _dot(a, b):
    return jnp.dot(a, b, preferred_element_type=F32)


def _dot_nt(a, b):
    return lax.dot_general(a, b, _NT, preferred_element_type=F32)


def _dot_tn(a, b):
    return lax.dot_general(a, b, _TN, preferred_element_type=F32)


def _split3(x):
    p1 = x.astype(BF16)
    r1 = x - p1.astype(F32)
    p2 = r1.astype(BF16)
    r2 = r1 - p2.astype(F32)
    return p1, p2, r2.astype(BF16)


def _sigmoid(x):
    return 1.0 / (1.0 + jnp.exp(-x))


def _inproj_kernel(x_ref, g_ref, wnat_ref, wut_ref, wgt_ref, bg_ref,
                   qk_ref, v_ref, o_ref, ut_ref, gt_ref, *, width):
    x = x_ref[...]
    ms = jnp.mean(x * x, axis=-1, keepdims=True)
    hn = (x * lax.rsqrt(ms + RMS_EPS) * g_ref[...]).astype(BF16)
    nat = _dot(hn, wnat_ref[...])
    qk_ref[...] = nat[:, :2 * width].astype(BF16)
    v_ref[...] = nat[:, 2 * width:3 * width].astype(BF16)
    o_ref[...] = nat[:, 3 * width:].astype(BF16)
    ut_ref[0] = _dot_nt(wut_ref[...], hn).astype(BF16)
    gt_ref[0] = _dot_nt(wgt_ref[...], hn) + bg_ref[...]


def _in_proj(x2d, norm_g, w_in, b_gates, bsz, seq):
    n_tok, d = x2d.shape
    s5w = d // 2
    mlw = d - s5w
    tm = min(PROJ_TILE, seq)
    tpb = seq // tm
    w_bf = w_in.astype(BF16)
    w_nat = w_bf[:, s5w:s5w + 4 * mlw]
    w_ut = w_bf[:, :s5w].T
    n_gate = 2 * ML_HEADS
    w_gt = jnp.zeros((16, d), BF16).at[:n_gate].set(w_bf[:, s5w + 4 * mlw:].T)
    b_g = jnp.zeros((16, 1), F32).at[:n_gate, 0].set(b_gates.astype(F32))
    grid = (n_tok // tm,)
    full = lambda i: (0, 0)
    return pl.pallas_call(
        functools.partial(_inproj_kernel, width=mlw),
        grid=grid,
        in_specs=[
            pl.BlockSpec((tm, d), lambda i: (i, 0)),
            pl.BlockSpec((1, d), full),
            pl.BlockSpec((d, 4 * mlw), full),
            pl.BlockSpec((s5w, d), full),
            pl.BlockSpec((16, d), full),
            pl.BlockSpec((16, 1), full),
        ],
        out_specs=[
            pl.BlockSpec((tm, 2 * mlw), lambda i: (i, 0)),
            pl.BlockSpec((tm, mlw), lambda i: (i, 0)),
            pl.BlockSpec((tm, mlw), lambda i: (i, 0)),
            pl.BlockSpec((1, s5w, tm), lambda i: (i // tpb, 0, i % tpb)),
            pl.BlockSpec((1, 16, tm), lambda i: (i // tpb, 0, i % tpb)),
        ],
        out_shape=[
            jax.ShapeDtypeStruct((n_tok, 2 * mlw), BF16),
            jax.ShapeDtypeStruct((n_tok, mlw), BF16),
            jax.ShapeDtypeStruct((n_tok, mlw), BF16),
            jax.ShapeDtypeStruct((bsz, s5w, seq), BF16),
            jax.ShapeDtypeStruct((bsz, 16, seq), F32),
        ],
        compiler_params=pltpu.CompilerParams(
            dimension_semantics=("parallel",), vmem_limit_bytes=VMEM_LIMIT),
        name="in_proj",
    )(x2d, norm_g.reshape(1, d).astype(F32), w_nat, w_ut, w_gt, b_g)


def _s5_kernel(d_ref, x_ref, prm_ref, c_ref, bt_ref, out_ref, kmat_scr, rhs_scr, *, bsz, nc):
    L = S5_CHUNK
    P = S5_GROUP
    N = S5_STATE
    grp = pl.program_id(0)
    lane = lax.broadcasted_iota(I32, (1, 2 * N), 1)
    lo = lane < N
    a_re = jnp.minimum(prm_ref[0, 0:1, :], -1e-4)
    a_im = prm_ref[0, 1:2, :]
    dt = jnp.exp(prm_ref[0, 2:3, :])
    zr = dt * a_re
    zi = dt * a_im

    def powtab(e, swap=False):
        ang = e * zi
        pick_cos = (lane >= N) if swap else lo
        return jnp.exp(e * zr) * jnp.where(pick_cos, jnp.cos(ang), jnp.sin(ang))

    er = jnp.exp(zr) * jnp.cos(zi)
    ei = jnp.exp(zr) * jnp.sin(zi)
    den = a_re * a_re + a_im * a_im
    coef_r = ((er - 1.0) * a_re + ei * a_im) / den
    coef_i = (ei * a_re - (er - 1.0) * a_im) / den
    c_r = c_ref[0, 0]
    c_i = c_ref[0, 1]
    bb_r = coef_r * bt_ref[0, 0] - coef_i * bt_ref[0, 1]
    bb_i = coef_r * bt_ref[0, 1] + coef_i * bt_ref[0, 0]

    tau = lax.broadcasted_iota(I32, (L, 1), 0).astype(F32)

    cb_rows = []
    for q in range(P):
        cbr = c_r * bb_r[q:q + 1] - c_i * bb_i[q:q + 1]
        cbi = c_r * bb_i[q:q + 1] + c_i * bb_r[q:q + 1]
        cb_rows.append(jnp.where(lo, cbr, -cbi))
    cb = jnp.concatenate(cb_rows, axis=0)
    pt0 = powtab(tau)
    c1, c2, c3 = _split3(cb)
    t1, t2, t3 = _split3(pt0)
    kmat_scr[...] = (_dot_nt(c1, t1) + _dot_nt(c1, t2) + _dot_nt(c2, t1)
                     + _dot_nt(c2, t2) + _dot_nt(c1, t3) + _dot_nt(c3, t1))

    rr = lax.broadcasted_iota(I32, (L, L), 0)
    cc = lax.broadcasted_iota(I32, (L, L), 1)
    causal = cc >= rr

    def build_q(q, carry):
        for p in range(P):
            krow = kmat_scr[pl.ds(q * P + p, 1), :]
            kb = jnp.broadcast_to(krow, (L, L))
            toe = pltpu.roll(kb, 0, 1, stride=1, stride_axis=0)
            rhs_scr[pl.ds(pl.multiple_of(q * L, L), L), p * L:(p + 1) * L] = (
                jnp.where(causal, toe, 0.0).astype(BF16))
        return carry

    lax.fori_loop(0, P, build_q, 0)

    pt_rev = powtab(L - 1.0 - tau)
    pt_rev_sw = powtab(L - 1.0 - tau, swap=True)
    f_rows = []
    for q in range(P):
        a1 = bb_r[q:q + 1]
        a2 = jnp.where(lo, -bb_i[q:q + 1], bb_i[q:q + 1])
        f_rows.append((a1 * pt_rev + a2 * pt_rev_sw).astype(BF16))
    fmat = jnp.concatenate(f_rows, axis=0)
    pt1 = powtab(tau + 1.0)
    pt1_sw = powtab(tau + 1.0, swap=True)
    e_rows = []
    for p in range(P):
        b1 = jnp.where(lo, c_r[p:p + 1], -c_r[p:p + 1])
        b2 = -c_i[p:p + 1]
        e_rows.append((b1 * pt1 + b2 * pt1_sw).astype(BF16))
    emat_t = jnp.concatenate(e_rows, axis=0)

    lhs = jnp.concatenate(
        [jnp.concatenate([x_ref[b, q] for q in range(P)], axis=1) for b in range(bsz)], axis=0)
    y = _dot(lhs, rhs_scr[...])
    s_end = _dot(lhs, fmat)

    m_rows = bsz * nc
    ridx = lax.broadcasted_iota(I32, (m_rows, 1), 0)
    cidx = ridx % nc
    h = jnp.where(cidx >= 1, pltpu.roll(s_end, 1, 0), 0.0)
    d = 1
    while d < nc:
        e = float(d * L)
        pr = jnp.exp(e * zr) * jnp.cos(e * zi)
        pi = jnp.exp(e * zr) * jnp.sin(e * zi)
        sh = jnp.where(cidx >= d, pltpu.roll(h, d, 0), 0.0)
        h = h + pr * sh + jnp.where(lo, -pi, pi) * pltpu.roll(sh, N, 1)
        d *= 2
    y = y + _dot_nt(h.astype(BF16), emat_t)

    for b in range(bsz):
        for p in range(P):
            yp = (y[b * nc:(b + 1) * nc, p * L:(p + 1) * L]
                  + d_ref[grp * P + p] * x_ref[b, p].astype(F32))
            out_ref[b, p] = jax.nn.gelu(yp).astype(BF16)


def _s5(ut, log_dt, a_re, a_im, b_re, b_im, c_re, c_im, d_skip):
    bsz, s5w, seq = ut.shape
    L = S5_CHUNK
    nc = seq // L
    groups = s5w // S5_GROUP
    n = S5_STATE
    x4 = ut.reshape(bsz, s5w, nc, L)
    dup = lambda t: jnp.concatenate([t, t], axis=-1).astype(F32)
    prm = jnp.zeros((groups, SUBLANES, 2 * n), F32)
    prm = prm.at[:, 0].set(dup(a_re)).at[:, 1].set(dup(a_im))
    prm = prm.at[:, 2].set(jnp.broadcast_to(log_dt.astype(F32)[:, None], (groups, 2 * n)))
    cpar = jnp.stack([dup(c_re), dup(c_im)], axis=1)
    btpar = jnp.stack([dup(jnp.swapaxes(b_re, 1, 2)), dup(jnp.swapaxes(b_im, 1, 2))], axis=1)
    out = pl.pallas_call(
        functools.partial(_s5_kernel, bsz=bsz, nc=nc),
        grid_spec=pltpu.PrefetchScalarGridSpec(
            num_scalar_prefetch=1,
            grid=(groups,),
            in_specs=[
                pl.BlockSpec((bsz, S5_GROUP, nc, L), lambda g, d: (0, g, 0, 0)),
                pl.BlockSpec((1, SUBLANES, 2 * n), lambda g, d: (g, 0, 0)),
                pl.BlockSpec((1, 2, S5_GROUP, 2 * n), lambda g, d: (g, 0, 0, 0)),
                pl.BlockSpec((1, 2, S5_GROUP, 2 * n), lambda g, d: (g, 0, 0, 0)),
            ],
            out_specs=pl.BlockSpec((bsz, S5_GROUP, nc, L), lambda g, d: (0, g, 0, 0)),
            scratch_shapes=[
                pltpu.VMEM((S5_GROUP * S5_GROUP, L), F32),
                pltpu.VMEM((S5_GROUP * L, S5_GROUP * L), BF16),
            ],
        ),
        out_shape=jax.ShapeDtypeStruct((bsz, s5w, nc, L), BF16),
        compiler_params=pltpu.CompilerParams(
            dimension_semantics=("parallel",), vmem_limit_bytes=VMEM_LIMIT),
        name="s5",
    )(d_skip.astype(F32), x4, prm, cpar, btpar)
    return out.reshape(bsz, s5w, seq)


def _log_sigmoid(x):
    return jnp.minimum(x, 0.0) - jnp.log(1.0 + jnp.exp(-jnp.abs(x)))


def _mlstm_kernel(qk_ref, v_ref, o_ref, gt_ref, cw_ref, cb_ref, ng_ref, y_ref,
                  tail_scr, c_scr, m_scr, *, bsz, chunk, width):
    L = chunk
    H = ML_HEADS
    dh = width // H
    step = pl.program_id(0)

    @pl.when(step == 0)
    def _():
        tail_scr[...] = jnp.zeros_like(tail_scr)
        c_scr[...] = jnp.zeros_like(c_scr)
        m_scr[...] = jnp.zeros_like(m_scr)

    rr = lax.broadcasted_iota(I32, (L, L), 0)
    cc = lax.broadcasted_iota(I32, (L, L), 1)
    causal = cc <= rr
    ltri = causal.astype(BF16)
    utri = (rr <= cc).astype(BF16)
    eye = (rr == cc).astype(BF16)
    one_col = (lax.broadcasted_iota(I32, (L, dh), 1) == 0).astype(BF16)
    grow = lax.broadcasted_iota(I32, (16, 1), 0)
    scale = 1.0 / math.sqrt(dh)

    for b in range(bsz):
        xqk = qk_ref[b].astype(F32)
        ext = jnp.concatenate([tail_scr[b], xqk], axis=0)
        acc = jnp.broadcast_to(cb_ref[...], xqk.shape)
        for j in range(CONV_WIDTH):
            back = CONV_WIDTH - 1 - j
            sh = ext if back == 0 else pltpu.roll(ext, back, 0)
            acc = acc + cw_ref[j:j + 1, :] * sh[SUBLANES:]
        tail_scr[b] = xqk[L - SUBLANES:]
        qkc = acc * _sigmoid(acc)
        q_all = qkc[:, :width].astype(BF16)
        k_all = (qkc[:, width:] * scale).astype(BF16)

        g = gt_ref[b]
        g2 = jnp.where(grow >= H, _log_sigmoid(g), g)
        p1, p2, p3 = _split3(g2)
        brow = _dot(p1, utri) + _dot(p2, utri) + _dot(p3, utri)
        bcol = _dot_nt(ltri, p1) + _dot_nt(ltri, p2) + _dot_nt(ltri, p3)
        gcol = _dot_nt(eye, p1) + _dot_nt(eye, p2) + _dot_nt(eye, p3)

        outs = []
        for hd in range(H):
            idx = b * H + hd
            sl = slice(hd * dh, (hd + 1) * dh)
            b_r = brow[H + hd:H + hd + 1, :]
            i_r = g[hd:hd + 1, :]
            b_c = bcol[:, H + hd:H + hd + 1]
            i_c = gcol[:, hd:hd + 1]
            b_last = b_r[:, L - 1:L]
            m_prev = m_scr[idx][0:1, 0:1]
            caug = c_scr[idx]

            log_d = jnp.where(causal, b_c - b_r + i_r, -jnp.inf)
            log_inter = b_c + m_prev
            m = jnp.maximum(log_inter, jnp.max(log_d, axis=-1, keepdims=True))
            w_intra = jnp.exp(log_d - m)
            w_inter = jnp.exp(log_inter - m)
            q_h = q_all[:, sl]
            k_h = k_all[:, sl]
            vaug = jnp.concatenate([v_ref[b][:, sl], one_col], axis=1)
            s = _dot_nt(q_h, k_h) * w_intra
            tot = w_inter * _dot(q_h, caug.astype(BF16)) + _dot(s.astype(BF16), vaug)
            num = tot[:, :dh]
            den = tot[:, dh:dh + 1]
            hh = num / jnp.maximum(jnp.abs(den), jnp.exp(-m))

            lws_r = b_last - b_r + i_r
            m_next = jnp.maximum(b_last + m_prev, jnp.max(lws_r, axis=-1, keepdims=True))
            decay = jnp.exp(b_last + m_prev - m_next)
            ws_c = jnp.exp(b_last - b_c + i_c - m_next)
            kw = (k_h.astype(F32) * ws_c).astype(BF16)
            c_scr[idx] = decay * caug + _dot_tn(kw, vaug)
            m_scr[idx] = jnp.broadcast_to(m_next, (SUBLANES, LANES))

            mu = jnp.mean(hh, axis=-1, keepdims=True)
            ctr = hh - mu
            var = jnp.mean(ctr * ctr, axis=-1, keepdims=True)
            hn = ctr * lax.rsqrt(var + LN_EPS) * ng_ref[:, sl]
            outs.append(hn * _sigmoid(o_ref[b][:, sl].astype(F32)))
        y_ref[b] = jnp.concatenate(outs, axis=1).astype(BF16)


def _mlstm(qk, v, o, gt, conv_w, conv_b, norm_g, bsz, seq):
    width = v.shape[-1]
    L = min(ML_CHUNK, seq)
    dh = width // ML_HEADS
    qk3 = qk.reshape(bsz, seq, 2 * width)
    v3 = v.reshape(bsz, seq, width)
    o3 = o.reshape(bsz, seq, width)
    full = lambda c: (0, 0)
    return pl.pallas_call(
        functools.partial(_mlstm_kernel, bsz=bsz, chunk=L, width=width),
        grid=(seq // L,),
        in_specs=[
            pl.BlockSpec((bsz, L, 2 * width), lambda c: (0, c, 0)),
            pl.BlockSpec((bsz, L, width), lambda c: (0, c, 0)),
            pl.BlockSpec((bsz, L, width), lambda c: (0, c, 0)),
            pl.BlockSpec((bsz, 16, L), lambda c: (0, 0, c)),
            pl.BlockSpec((CONV_WIDTH, 2 * width), full),
            pl.BlockSpec((1, 2 * width), full),
            pl.BlockSpec((1, width), full),
        ],
        out_specs=pl.BlockSpec((bsz, L, width), lambda c: (0, c, 0)),
        out_shape=jax.ShapeDtypeStruct((bsz, seq, width), BF16),
        scratch_shapes=[
            pltpu.VMEM((bsz, SUBLANES, 2 * width), F32),
            pltpu.VMEM((bsz * ML_HEADS, dh, 2 * dh), F32),
            pltpu.VMEM((bsz * ML_HEADS, SUBLANES, LANES), F32),
        ],
        compiler_params=pltpu.CompilerParams(
            dimension_semantics=("arbitrary",), vmem_limit_bytes=VMEM_LIMIT),
        name="mlstm",
    )(qk3, v3, o3, gt, conv_w.astype(F32), conv_b.reshape(1, -1).astype(F32),
      norm_g.reshape(1, -1).astype(F32))


def _post_kernel(x_ref, g_ref, yml_ref, wglu_ref, bglu_ref, wout_ref, nffn_ref, wrt_ref, br_ref,
                 h2_ref, xn_ref, eidx_ref, gate_ref, rank_ref, cnt_ref, carry_scr):
    step = pl.program_id(0)

    @pl.when(step == 0)
    def _():
        carry_scr[...] = jnp.zeros_like(carry_scr)

    g = g_ref[...]
    z = _dot(g, wglu_ref[...]) + bglu_ref[...]
    s5 = g.astype(F32) * _sigmoid(z)
    mix = jnp.concatenate([s5.astype(BF16), yml_ref[...]], axis=1)
    h2 = x_ref[...] + _dot(mix, wout_ref[...])
    h2_ref[...] = h2
    ms = jnp.mean(h2 * h2, axis=-1, keepdims=True)
    xn = h2 * lax.rsqrt(ms + RMS_EPS) * nffn_ref[...]
    xn_ref[...] = xn

    tm = xn.shape[0]
    logits = _dot_nt(wrt_ref[...], xn.astype(BF16)) + br_ref[...]
    eio = lax.broadcasted_iota(I32, (N_EXPERTS, tm), 0).astype(F32)
    vals = logits
    onehot = jnp.zeros((N_EXPERTS, tm), F32)
    idxs, tops = [], []
    for _ in range(TOP_K):
        mx = jnp.max(vals, axis=0, keepdims=True)
        idx = jnp.min(jnp.where(vals == mx, eio, float(N_EXPERTS)), axis=0, keepdims=True)
        sel = eio == idx
        onehot = onehot + sel.astype(F32)
        vals = jnp.where(sel, -jnp.inf, vals)
        idxs.append(idx)
        tops.append(mx)
    exps = [jnp.exp(t - tops[0]) for t in tops]
    tot = exps[0] + exps[1] + exps[2] + exps[3]
    pad_f = jnp.zeros((SUBLANES - TOP_K, tm), F32)
    eidx_ref[...] = jnp.concatenate(idxs + [pad_f], axis=0).astype(I32)
    gate_ref[...] = jnp.concatenate([e / tot for e in exps] + [pad_f], axis=0)

    rr = lax.broadcasted_iota(I32, (tm, tm), 0)
    cc = lax.broadcasted_iota(I32, (tm, tm), 1)
    before = (rr < cc).astype(BF16)
    carry = carry_scr[:, 0:1]
    rank_ex = _dot(onehot.astype(BF16), before) + carry
    ranks = [jnp.sum(jnp.where(eio == i, rank_ex, 0.0), axis=0, keepdims=True) for i in idxs]
    rank_ref[...] = jnp.concatenate(ranks + [pad_f], axis=0).astype(I32)
    carry_scr[...] = jnp.broadcast_to(carry + jnp.sum(onehot, axis=1, keepdims=True),
                                      carry_scr.shape)
    cnt_ref[...] = carry_scr[...]


def _post(x2d, g_nat, y_ml, w_glu, b_glu, w_out, norm_g, w_router, b_router):
    n_tok, d = x2d.shape
    s5w = g_nat.shape[-1]
    mlw = y_ml.shape[-1]
    tm = min(PROJ_TILE, n_tok)
    full = lambda i: (0, 0)
    row = lambda i: (i, 0)
    colb = lambda i: (0, i)
    return pl.pallas_call(
        _post_kernel,
        grid=(n_tok // tm,),
        in_specs=[
            pl.BlockSpec((tm, d), row),
            pl.BlockSpec((tm, s5w), row),
            pl.BlockSpec((tm, mlw), row),
            pl.BlockSpec((s5w, s5w), full),
            pl.BlockSpec((1, s5w), full),
            pl.BlockSpec((d, d), full),
            pl.BlockSpec((1, d), full),
            pl.BlockSpec((N_EXPERTS, d), full),
            pl.BlockSpec((N_EXPERTS, 1), full),
        ],
        out_specs=[
            pl.BlockSpec((tm, d), row),
            pl.BlockSpec((tm, d), row),
            pl.BlockSpec((SUBLANES, tm), colb),
            pl.BlockSpec((SUBLANES, tm), colb),
            pl.BlockSpec((SUBLANES, tm), colb),
            pl.BlockSpec((N_EXPERTS, LANES), full),
        ],
        out_shape=[
            jax.ShapeDtypeStruct((n_tok, d), F32),
            jax.ShapeDtypeStruct((n_tok, d), F32),
            jax.ShapeDtypeStruct((SUBLANES, n_tok), I32),
            jax.ShapeDtypeStruct((SUBLANES, n_tok), F32),
            jax.ShapeDtypeStruct((SUBLANES, n_tok), I32),
            jax.ShapeDtypeStruct((N_EXPERTS, LANES), F32),
        ],
        scratch_shapes=[pltpu.VMEM((N_EXPERTS, LANES), F32)],
        compiler_params=pltpu.CompilerParams(
            dimension_semantics=("arbitrary",), vmem_limit_bytes=VMEM_LIMIT),
        name="post_router",
    )(x2d, g_nat, y_ml, w_glu.astype(BF16), b_glu.reshape(1, -1).astype(F32), w_out.astype(BF16),
      norm_g.reshape(1, -1).astype(F32), w_router.T.astype(BF16),
      b_router.reshape(-1, 1).astype(F32))


def _gather_rows(idx_ref, n_rows, src_hbm, dst_vmem, sem):
    def body(r, carry):
        pltpu.make_async_copy(src_hbm.at[pl.ds(idx_ref[r], 1), :],
                              dst_vmem.at[pl.ds(r, 1), :], sem).start()
        return carry

    lax.fori_loop(0, n_rows, body, 0, unroll=8)


def _wait_rows(n_rows, src_hbm, dst_vmem, sem):
    pltpu.make_async_copy(src_hbm.at[pl.ds(0, n_rows), :], dst_vmem, sem).wait()


def _expert_kernel(te_ref, nu_ref, tok_ref, tokn_ref, xn_hbm, wup_ref, bup_ref, wdn_ref, bdn_ref,
                   y_ref, xbuf, sem, *, d_ff):
    i = pl.program_id(0)
    n_used = nu_ref[0]
    tm = xbuf.shape[1]
    slot = i % 2

    @pl.when(i == 0)
    def _():
        _gather_rows(tok_ref, tm, xn_hbm, xbuf.at[0], sem.at[0])

    @pl.when(i + 1 < n_used)
    def _():
        _gather_rows(tokn_ref, tm, xn_hbm, xbuf.at[1 - slot], sem.at[1 - slot])

    @pl.when(jnp.logical_or(i < n_used, i == 0))
    def _():
        _wait_rows(tm, xn_hbm, xbuf.at[slot], sem.at[slot])

    @pl.when(i < n_used)
    def _():
        xb = xbuf[slot].astype(BF16)
        h = _dot(xb, wup_ref[0].astype(BF16)) + bup_ref[0]
        gl = jnp.minimum(h[:, :d_ff], SWIGLU_LIMIT)
        lin = jnp.clip(h[:, d_ff:], -SWIGLU_LIMIT, SWIGLU_LIMIT)
        act = gl * _sigmoid(SWIGLU_ALPHA * gl) * (lin + 1.0)
        y_ref[...] = _dot(act.astype(BF16), wdn_ref[0].astype(BF16)) + bdn_ref[0]

    @pl.when(i >= n_used)
    def _():
        y_ref[...] = jnp.zeros_like(y_ref)


def _experts(xn, slot_tok, tile_e, n_used, w_up, b_up, w_down, b_down):
    n_tok, d = xn.shape
    n_slots = slot_tok.shape[0]
    tm = EXPERT_TILE
    n_tiles = n_slots // tm
    d_ff = w_down.shape[1]
    return pl.pallas_call(
        functools.partial(_expert_kernel, d_ff=d_ff),
        grid_spec=pltpu.PrefetchScalarGridSpec(
            num_scalar_prefetch=2,
            grid=(n_tiles,),
            in_specs=[
                pl.BlockSpec((tm,), lambda i, te, nu: (i,), memory_space=pltpu.SMEM),
                pl.BlockSpec((tm,), lambda i, te, nu: (jnp.minimum(i + 1, n_tiles - 1),),
                             memory_space=pltpu.SMEM),
                pl.BlockSpec(memory_space=pl.ANY),
                pl.BlockSpec((1, d, 2 * d_ff), lambda i, te, nu: (te[i], 0, 0)),
                pl.BlockSpec((1, 1, 2 * d_ff), lambda i, te, nu: (te[i], 0, 0)),
                pl.BlockSpec((1, d_ff, d), lambda i, te, nu: (te[i], 0, 0)),
                pl.BlockSpec((1, 1, d), lambda i, te, nu: (te[i], 0, 0)),
            ],
            out_specs=pl.BlockSpec((tm, d), lambda i, te, nu: (i, 0)),
            scratch_shapes=[
                pltpu.VMEM((2, tm, d), F32),
                pltpu.SemaphoreType.DMA((2,)),
            ],
        ),
        out_shape=jax.ShapeDtypeStruct((n_slots, d), F32),
        compiler_params=pltpu.CompilerParams(
            dimension_semantics=("arbitrary",), vmem_limit_bytes=VMEM_LIMIT),
        name="experts",
    )(tile_e, n_used, slot_tok, slot_tok, xn, w_up, b_up.reshape(N_EXPERTS, 1, -1),
      w_down, b_down.reshape(N_EXPERTS, 1, -1))


def _combine_kernel(dst_ref, dstn_ref, ys_hbm, h2_ref, gate_ref, ng_ref, out_ref, ybuf, sem,
                    *, n_steps):
    i = pl.program_id(0)
    tc = h2_ref.shape[0]
    rows = TOP_K * tc
    slot = i % 2

    @pl.when(i == 0)
    def _():
        _gather_rows(dst_ref, rows, ys_hbm, ybuf.at[0], sem.at[0])

    @pl.when(i + 1 < n_steps)
    def _():
        _gather_rows(dstn_ref, rows, ys_hbm, ybuf.at[1 - slot], sem.at[1 - slot])

    _wait_rows(rows, ys_hbm, ybuf.at[slot], sem.at[slot])
    acc = h2_ref[...]
    for k in range(TOP_K):
        acc = acc + gate_ref[:, k:k + 1] * ybuf[slot, k * tc:(k + 1) * tc, :]
    ms = jnp.mean(acc * acc, axis=-1, keepdims=True)
    out_ref[...] = acc * lax.rsqrt(ms + RMS_EPS) * ng_ref[...]


def _combine(y_slots, dest_km, h2, gate_cols, norm_g):
    n_tok, d = h2.shape
    tc = min(COMBINE_TILE, n_tok)
    n_steps = n_tok // tc
    rows = TOP_K * tc
    return pl.pallas_call(
        functools.partial(_combine_kernel, n_steps=n_steps),
        grid=(n_steps,),
        in_specs=[
            pl.BlockSpec((rows,), lambda i: (i,), memory_space=pltpu.SMEM),
            pl.BlockSpec((rows,), lambda i: (jnp.minimum(i + 1, n_steps - 1),),
                         memory_space=pltpu.SMEM),
            pl.BlockSpec(memory_space=pl.ANY),
            pl.BlockSpec((tc, d), lambda i: (i, 0)),
            pl.BlockSpec((tc, SUBLANES), lambda i: (i, 0)),
            pl.BlockSpec((1, d), lambda i: (0, 0)),
        ],
        out_specs=pl.BlockSpec((tc, d), lambda i: (i, 0)),
        out_shape=jax.ShapeDtypeStruct((n_tok, d), F32),
        scratch_shapes=[
            pltpu.VMEM((2, rows, d), F32),
            pltpu.SemaphoreType.DMA((2,)),
        ],
        compiler_params=pltpu.CompilerParams(
            dimension_semantics=("arbitrary",), vmem_limit_bytes=VMEM_LIMIT),
        name="combine",
    )(dest_km, dest_km, y_slots, h2, gate_cols, norm_g.reshape(1, -1).astype(F32))


def _moe(h2, xn, eidx, gate, rank, cnt, w_up, b_up, w_down, b_down, norm_final_g):
    n_tok, d = h2.shape
    tm = EXPERT_TILE
    n_assign = n_tok * TOP_K
    n_slots = n_assign + N_EXPERTS * tm
    n_tiles = n_slots // tm
    counts = cnt[:, 0].astype(I32)
    padded = (counts + tm - 1) // tm * tm
    pad_end = jnp.cumsum(padded)
    pad_start = pad_end - padded
    dest = pad_start[eidx[:TOP_K]] + rank[:TOP_K]
    n_used = (pad_end[-1] // tm).astype(I32).reshape(1)
    tile_start = jnp.arange(n_tiles, dtype=I32) * tm
    tile_e = jnp.minimum(jnp.sum(pad_end[None, :] <= tile_start[:, None], axis=-1),
                         N_EXPERTS - 1).astype(I32)
    tok_ids = jnp.broadcast_to(jnp.arange(n_tok, dtype=I32)[None, :], (TOP_K, n_tok))
    slot_tok = jnp.zeros((n_slots,), I32).at[dest.reshape(-1)].set(tok_ids.reshape(-1))
    y_slots = _experts(xn, slot_tok, tile_e, n_used, w_up, b_up, w_down, b_down)
    tc = min(COMBINE_TILE, n_tok)
    dest_km = dest.reshape(TOP_K, n_tok // tc, tc).transpose(1, 0, 2).reshape(-1)
    return _combine(y_slots, dest_km, h2, gate.T, norm_final_g)


def kernel(x, norm_mix_g, w_in, s5_log_dt, s5_a_re, s5_a_im, s5_b_re, s5_b_im, s5_c_re, s5_c_im,
           s5_d, s5_w_glu, s5_b_glu, ml_conv_w, ml_conv_b, ml_b_gates, ml_norm_g, w_out,
           norm_ffn_g, w_router, b_router, w_up, b_up, w_down, b_down, norm_final_g):
    bsz, seq, d = x.shape
    depth = w_in.shape[0]
    assert depth == 1, "single-layer block"
    l = 0
    x2d = x.reshape(bsz * seq, d)
    qk, v, o, ut, gt = _in_proj(x2d, norm_mix_g[l], w_in[l], ml_b_gates[l], bsz, seq)
    g_t = _s5(ut, s5_log_dt[l], s5_a_re[l], s5_a_im[l], s5_b_re[l], s5_b_im[l],
              s5_c_re[l], s5_c_im[l], s5_d[l])
    g_nat = jnp.swapaxes(g_t, 1, 2).reshape(bsz * seq, -1)
    y_ml = _mlstm(qk, v, o, gt, ml_conv_w[l], ml_conv_b[l], ml_norm_g[l], bsz, seq)
    y_ml = y_ml.reshape(bsz * seq, -1)
    h2, xn, eidx, gate, rank, cnt = _post(x2d, g_nat, y_ml, s5_w_glu[l], s5_b_glu[l], w_out[l],
                                          norm_ffn_g[l], w_router[l], b_router[l])
    out = _moe(h2, xn, eidx, gate, rank, cnt, w_up[l], b_up[l], w_down[l], b_down[l],
               norm_final_g)
    return out.reshape(bsz, seq, d)
```

```python
import functools
import math

import jax
import jax.numpy as jnp
from jax import lax
from jax.experimental import pallas as pl
from jax.experimental.pallas import tpu as pltpu

F32 = jnp.float32
BF16 = jnp.bfloat16
I32 = jnp.int32

S5_GROUP = 16
S5_STATE = 64
ML_HEADS = 4
CONV_WIDTH = 4
N_EXPERTS = 32
TOP_K = 4
SWIGLU_LIMIT = 7.0
SWIGLU_ALPHA = 1.702
RMS_EPS = 1e-5
LN_EPS = 1e-6

LANES = 128
SUBLANES = 8
S5_CHUNK = LANES
ML_CHUNK = 128
PROJ_TILE = 512
EXPERT_TILE = 256
COMBINE_TILE = 256
VMEM_LIMIT = 56 * 1024 * 1024

_NT = (((1,), (1,)), ((), ()))
_TN = (((0,), (0,)), ((), ()))


def _dot(a, b):
    return jnp.dot(a, b, preferred_element_type=F32)


def _dot_nt(a, b):
    return lax.dot_general(a, b, _NT, preferred_element_type=F32)


def _dot_tn(a, b):
    return lax.dot_general(a, b, _TN, preferred_element_type=F32)


def _split3(x):
    p1 = x.astype(BF16)
    r1 = x - p1.astype(F32)
    p2 = r1.astype(BF16)
    r2 = r1 - p2.astype(F32)
    return p1, p2, r2.astype(BF16)


def _sigmoid(x):
    return 1.0 / (1.0 + jnp.exp(-x))


def _inproj_kernel(x_ref, g_ref, wnat_ref, wut_ref, wgt_ref, bg_ref,
                   qk_ref, v_ref, o_ref, ut_ref, gt_ref, *, width):
    x = x_ref[...]
    ms = jnp.mean(x * x, axis=-1, keepdims=True)
    hn = (x * lax.rsqrt(ms + RMS_EPS) * g_ref[...]).astype(BF16)
    nat = _dot(hn, wnat_ref[...])
    qk_ref[...] = nat[:, :2 * width].astype(BF16)
    v_ref[...] = nat[:, 2 * width:3 * width].astype(BF16)
    o_ref[...] = nat[:, 3 * width:].astype(BF16)
    ut_ref[0] = _dot_nt(wut_ref[...], hn).astype(BF16)
    gt_ref[0] = _dot_nt(wgt_ref[...], hn) + bg_ref[...]


def _in_proj(x2d, norm_g, w_in, b_gates, bsz, seq):
    n_tok, d = x2d.shape
    s5w = d // 2
    mlw = d - s5w
    tm = min(PROJ_TILE, seq)
    tpb = seq // tm
    w_bf = w_in.astype(BF16)
    w_nat = w_bf[:, s5w:s5w + 4 * mlw]
    w_ut = w_bf[:, :s5w].T
    n_gate = 2 * ML_HEADS
    w_gt = jnp.zeros((16, d), BF16).at[:n_gate].set(w_bf[:, s5w + 4 * mlw:].T)
    b_g = jnp.zeros((16, 1), F32).at[:n_gate, 0].set(b_gates.astype(F32))
    grid = (n_tok // tm,)
    full = lambda i: (0, 0)
    return pl.pallas_call(
        functools.partial(_inproj_kernel, width=mlw),
        grid=grid,
        in_specs=[
            pl.BlockSpec((tm, d), lambda i: (i, 0)),
            pl.BlockSpec((1, d), full),
            pl.BlockSpec((d, 4 * mlw), full),
            pl.BlockSpec((s5w, d), full),
            pl.BlockSpec((16, d), full),
            pl.BlockSpec((16, 1), full),
        ],
        out_specs=[
            pl.BlockSpec((tm, 2 * mlw), lambda i: (i, 0)),
            pl.BlockSpec((tm, mlw), lambda i: (i, 0)),
            pl.BlockSpec((tm, mlw), lambda i: (i, 0)),
            pl.BlockSpec((1, s5w, tm), lambda i: (i // tpb, 0, i % tpb)),
            pl.BlockSpec((1, 16, tm), lambda i: (i // tpb, 0, i % tpb)),
        ],
        out_shape=[
            jax.ShapeDtypeStruct((n_tok, 2 * mlw), BF16),
            jax.ShapeDtypeStruct((n_tok, mlw), BF16),
            jax.ShapeDtypeStruct((n_tok, mlw), BF16),
            jax.ShapeDtypeStruct((bsz, s5w, seq), BF16),
            jax.ShapeDtypeStruct((bsz, 16, seq), F32),
        ],
        compiler_params=pltpu.CompilerParams(
            dimension_semantics=("parallel",), vmem_limit_bytes=VMEM_LIMIT),
        name="in_proj",
    )(x2d, norm_g.reshape(1, d).astype(F32), w_nat, w_ut, w_gt, b_g)


def _s5_kernel(d_ref, x_ref, prm_ref, c_ref, bt_ref, out_ref, kmat_scr, rhs_scr, *, bsz, nc):
    L = S5_CHUNK
    P = S5_GROUP
    N = S5_STATE
    grp = pl.program_id(0)
    lane = lax.broadcasted_iota(I32, (1, 2 * N), 1)
    lo = lane < N
    a_re = jnp.minimum(prm_ref[0, 0:1, :], -1e-4)
    a_im = prm_ref[0, 1:2, :]
    dt = jnp.exp(prm_ref[0, 2:3, :])
    zr = dt * a_re
    zi = dt * a_im

    def powtab(e, swap=False):
        ang = e * zi
        pick_cos = (lane >= N) if swap else lo
        return jnp.exp(e * zr) * jnp.where(pick_cos, jnp.cos(ang), jnp.sin(ang))

    er = jnp.exp(zr) * jnp.cos(zi)
    ei = jnp.exp(zr) * jnp.sin(zi)
    den = a_re * a_re + a_im * a_im
    coef_r = ((er - 1.0) * a_re + ei * a_im) / den
    coef_i = (ei * a_re - (er - 1.0) * a_im) / den
    c_r = c_ref[0, 0]
    c_i = c_ref[0, 1]
    bb_r = coef_r * bt_ref[0, 0] - coef_i * bt_ref[0, 1]
    bb_i = coef_r * bt_ref[0, 1] + coef_i * bt_ref[0, 0]

    tau = lax.broadcasted_iota(I32, (L, 1), 0).astype(F32)

    cb_rows = []
    for q in range(P):
        cbr = c_r * bb_r[q:q + 1] - c_i * bb_i[q:q + 1]
        cbi = c_r * bb_i[q:q + 1] + c_i * bb_r[q:q + 1]
        cb_rows.append(jnp.where(lo, cbr, -cbi))
    cb = jnp.concatenate(cb_rows, axis=0)
    pt0 = powtab(tau)
    c1, c2, c3 = _split3(cb)
    t1, t2, t3 = _split3(pt0)
    kmat_scr[...] = (_dot_nt(c1, t1) + _dot_nt(c1, t2) + _dot_nt(c2, t1)
                     + _dot_nt(c2, t2) + _dot_nt(c1, t3) + _dot_nt(c3, t1))

    rr = lax.broadcasted_iota(I32, (L, L), 0)
    cc = lax.broadcasted_iota(I32, (L, L), 1)
    causal = cc >= rr

    def build_q(q, carry):
        for p in range(P):
            krow = kmat_scr[pl.ds(q * P + p, 1), :]
            kb = jnp.broadcast_to(krow, (L, L))
            toe = pltpu.roll(kb, 0, 1, stride=1, stride_axis=0)
            rhs_scr[pl.ds(pl.multiple_of(q * L, L), L), p * L:(p + 1) * L] = (
                jnp.where(causal, toe, 0.0).astype(BF16))
        return carry

    lax.fori_loop(0, P, build_q, 0)

    pt_rev = powtab(L - 1.0 - tau)
    pt_rev_sw = powtab(L - 1.0 - tau, swap=True)
    f_rows = []
    for q in range(P):
        a1 = bb_r[q:q + 1]
        a2 = jnp.where(lo, -bb_i[q:q + 1], bb_i[q:q + 1])
        f_rows.append((a1 * pt_rev + a2 * pt_rev_sw).astype(BF16))
    fmat = jnp.concatenate(f_rows, axis=0)
    pt1 = powtab(tau + 1.0)
    pt1_sw = powtab(tau + 1.0, swap=True)
    e_rows = []
    for p in range(P):
        b1 = jnp.where(lo, c_r[p:p + 1], -c_r[p:p + 1])
        b2 = -c_i[p:p + 1]
        e_rows.append((b1 * pt1 + b2 * pt1_sw).astype(BF16))
    emat_t = jnp.concatenate(e_rows, axis=0)

    lhs = jnp.concatenate(
        [jnp.concatenate([x_ref[b, q] for q in range(P)], axis=1) for b in range(bsz)], axis=0)
    y = _dot(lhs, rhs_scr[...])
    s_end = _dot(lhs, fmat)

    m_rows = bsz * nc
    ridx = lax.broadcasted_iota(I32, (m_rows, 1), 0)
    cidx = ridx % nc
    h = jnp.where(cidx >= 1, pltpu.roll(s_end, 1, 0), 0.0)
    d = 1
    while d < nc:
        e = float(d * L)
        pr = jnp.exp(e * zr) * jnp.cos(e * zi)
        pi = jnp.exp(e * zr) * jnp.sin(e * zi)
        sh = jnp.where(cidx >= d, pltpu.roll(h, d, 0), 0.0)
        h = h + pr * sh + jnp.where(lo, -pi, pi) * pltpu.roll(sh, N, 1)
        d *= 2
    y = y + _dot_nt(h.astype(BF16), emat_t)

    for b in range(bsz):
        for p in range(P):
            yp = (y[b * nc:(b + 1) * nc, p * L:(p + 1) * L]
                  + d_ref[grp * P + p] * x_ref[b, p].astype(F32))
            out_ref[b, p] = jax.nn.gelu(yp).astype(BF16)


def _s5(ut, log_dt, a_re, a_im, b_re, b_im, c_re, c_im, d_skip):
    bsz, s5w, seq = ut.shape
    L = S5_CHUNK
    nc = seq // L
    groups = s5w // S5_GROUP
    n = S5_STATE
    x4 = ut.reshape(bsz, s5w, nc, L)
    dup = lambda t: jnp.concatenate([t, t], axis=-1).astype(F32)
    prm = jnp.zeros((groups, SUBLANES, 2 * n), F32)
    prm = prm.at[:, 0].set(dup(a_re)).at[:, 1].set(dup(a_im))
    prm = prm.at[:, 2].set(jnp.broadcast_to(log_dt.astype(F32)[:, None], (groups, 2 * n)))
    cpar = jnp.stack([dup(c_re), dup(c_im)], axis=1)
    btpar = jnp.stack([dup(jnp.swapaxes(b_re, 1, 2)), dup(jnp.swapaxes(b_im, 1, 2))], axis=1)
    out = pl.pallas_call(
        functools.partial(_s5_kernel, bsz=bsz, nc=nc),
        grid_spec=pltpu.PrefetchScalarGridSpec(
            num_scalar_prefetch=1,
            grid=(groups,),
            in_specs=[
                pl.BlockSpec((bsz, S5_GROUP, nc, L), lambda g, d: (0, g, 0, 0)),
                pl.BlockSpec((1, SUBLANES, 2 * n), lambda g, d: (g, 0, 0)),
                pl.BlockSpec((1, 2, S5_GROUP, 2 * n), lambda g, d: (g, 0, 0, 0)),
                pl.BlockSpec((1, 2, S5_GROUP, 2 * n), lambda g, d: (g, 0, 0, 0)),
            ],
            out_specs=pl.BlockSpec((bsz, S5_GROUP, nc, L), lambda g, d: (0, g, 0, 0)),
            scratch_shapes=[
                pltpu.VMEM((S5_GROUP * S5_GROUP, L), F32),
                pltpu.VMEM((S5_GROUP * L, S5_GROUP * L), BF16),
            ],
        ),
        out_shape=jax.ShapeDtypeStruct((bsz, s5w, nc, L), BF16),
        compiler_params=pltpu.CompilerParams(
            dimension_semantics=("parallel",), vmem_limit_bytes=VMEM_LIMIT),
        name="s5",
    )(d_skip.astype(F32), x4, prm, cpar, btpar)
    return out.reshape(bsz, s5w, seq)


def _log_sigmoid(x):
    return jnp.minimum(x, 0.0) - jnp.log(1.0 + jnp.exp(-jnp.abs(x)))


def _mlstm_kernel(qk_ref, v_ref, o_ref, gt_ref, cw_ref, cb_ref, ng_ref, y_ref,
                  tail_scr, c_scr, m_scr, *, bsz, chunk, width):
    L = chunk
    H = ML_HEADS
    dh = width // H
    step = pl.program_id(0)

    @pl.when(step == 0)
    def _():
        tail_scr[...] = jnp.zeros_like(tail_scr)
        c_scr[...] = jnp.zeros_like(c_scr)
        m_scr[...] = jnp.zeros_like(m_scr)

    assert L == LANES and dh == LANES, "column replication below uses one 128x128 tile per head"
    rr = lax.broadcasted_iota(I32, (L, L), 0)
    cc = lax.broadcasted_iota(I32, (L, L), 1)
    causal = cc <= rr
    ltri = causal.astype(BF16)
    utri = (rr <= cc).astype(BF16)
    eye = (rr == cc).astype(BF16)
    ones_blk = jnp.ones((L, dh), BF16)
    grow = lax.broadcasted_iota(I32, (16, 1), 0)
    lane_row = lax.broadcasted_iota(I32, (SUBLANES, L), 1)
    scale = 1.0 / math.sqrt(dh)
    inv_dh = 1.0 / dh

    def replicate(mat, rows):
        p = jnp.concatenate([jnp.broadcast_to(r, (LANES, L)) for r in rows], axis=0)
        p1 = p.astype(BF16)
        p2 = (p - p1.astype(F32)).astype(BF16)
        return _dot_nt(mat, p1) + _dot_nt(mat, p2)

    def rowsum(x):
        hi = x.astype(BF16)
        lo = (x - hi.astype(F32)).astype(BF16)
        return _dot(hi, ones_blk) + _dot(lo, ones_blk)

    n_str = bsz * H
    streams = [(b, hd) for b in range(bsz) for hd in range(H)]
    tails = [tail_scr[b] for b in range(bsz)]
    caugs = [c_scr[i] for i in range(n_str)]
    m_prevs = [m_scr[i][0:1, 0:1] for i in range(n_str)]

    q_all, k_all, new_tails, i_rows, lf_rows, b_rows = [], [], [], [], [], []
    for b in range(bsz):
        xqk = qk_ref[b].astype(F32)
        ext = jnp.concatenate([tails[b], xqk], axis=0)
        acc = jnp.broadcast_to(cb_ref[...], xqk.shape)
        for j in range(CONV_WIDTH):
            back = CONV_WIDTH - 1 - j
            sh = ext if back == 0 else pltpu.roll(ext, back, 0)
            acc = acc + cw_ref[j:j + 1, :] * sh[SUBLANES:]
        new_tails.append(xqk[L - SUBLANES:])
        qkc = acc * _sigmoid(acc)
        q_all.append(qkc[:, :width].astype(BF16))
        k_all.append((qkc[:, width:] * scale).astype(BF16))
        g = gt_ref[b]
        g2 = jnp.where(grow >= H, _log_sigmoid(g), g)
        p1, p2, p3 = _split3(g2)
        brow = _dot(p1, utri) + _dot(p2, utri) + _dot(p3, utri)
        i_rows.append(g[0:H])
        lf_rows.append(g2[H:2 * H])
        b_rows.append(brow[H:2 * H])

    cm_rows = []
    for b in range(bsz):
        x = i_rows[b] - b_rows[b]
        x = jnp.concatenate([x, x], axis=0)
        sft = 1
        while sft < L:
            x = jnp.maximum(x, jnp.where(lane_row >= sft, pltpu.roll(x, sft, 1), -jnp.inf))
            sft *= 2
        cm_rows.append(x[0:H])

    bc_rep = [replicate(ltri, [lf_rows[b][h:h + 1] for h in range(H)]) for b in range(bsz)]
    ic_rep = [replicate(eye, [i_rows[b][h:h + 1] for h in range(H)]) for b in range(bsz)]
    cm_rep = [replicate(eye, [cm_rows[b][h:h + 1] for h in range(H)]) for b in range(bsz)]

    q_h = [q_all[b][:, hd * dh:(hd + 1) * dh] for b, hd in streams]
    k_h = [k_all[b][:, hd * dh:(hd + 1) * dh] for b, hd in streams]
    vaug = [jnp.concatenate([v_ref[b][:, hd * dh:(hd + 1) * dh], ones_blk], axis=1)
            for b, hd in streams]
    scores = [_dot_nt(q_h[i], k_h[i]) for i in range(n_str)]
    inter = [_dot(q_h[i], caugs[i].astype(BF16)) for i in range(n_str)]

    m_reps, w_inter, s_bf, kw, decays, new_ms = [], [], [], [], [], []
    for i, (b, hd) in enumerate(streams):
        sl = slice(hd * LANES, (hd + 1) * LANES)
        b_r = b_rows[b][hd:hd + 1]
        i_r = i_rows[b][hd:hd + 1]
        b_last = b_r[:, L - 1:L]
        bc = bc_rep[b][:, sl]
        m_prev = m_prevs[i]
        m = bc + jnp.maximum(cm_rep[b][:, sl], m_prev)
        m_reps.append(m)
        w_inter.append(jnp.exp(bc + m_prev - m))
        log_d = jnp.where(causal, bc - b_r + i_r, -jnp.inf)
        s_bf.append((scores[i] * jnp.exp(log_d - m)).astype(BF16))
        m_next = jnp.maximum(b_last + m_prev, b_last + cm_rows[b][hd:hd + 1, L - 1:L])
        decays.append(jnp.exp(b_last + m_prev - m_next))
        ws = jnp.exp(b_last - bc + ic_rep[b][:, sl] - m_next)
        kw.append((k_h[i].astype(F32) * ws).astype(BF16))
        new_ms.append(jnp.broadcast_to(m_next, (SUBLANES, LANES)))

    intra = [_dot(s_bf[i], vaug[i]) for i in range(n_str)]
    upd = [_dot_tn(kw[i], vaug[i]) for i in range(n_str)]

    hh = []
    for i in range(n_str):
        num = w_inter[i] * inter[i][:, :dh] + intra[i][:, :dh]
        den = w_inter[i] * inter[i][:, dh:] + intra[i][:, dh:]
        hh.append(num / jnp.maximum(jnp.abs(den), jnp.exp(-m_reps[i])))
    mu = [rowsum(h) * inv_dh for h in hh]
    ctr = [hh[i] - mu[i] for i in range(n_str)]
    var = [rowsum(c * c) * inv_dh for c in ctr]
    outs = []
    for i, (b, hd) in enumerate(streams):
        sl = slice(hd * dh, (hd + 1) * dh)
        hn = ctr[i] * lax.rsqrt(var[i] + LN_EPS) * ng_ref[:, sl]
        outs.append(hn * _sigmoid(o_ref[b][:, sl].astype(F32)))

    for b in range(bsz):
        tail_scr[b] = new_tails[b]
        y_ref[b] = jnp.concatenate(outs[b * H:(b + 1) * H], axis=1).astype(BF16)
    for i in range(n_str):
        c_scr[i] = decays[i] * caugs[i] + upd[i]
        m_scr[i] = new_ms[i]


def _mlstm(qk, v, o, gt, conv_w, conv_b, norm_g, bsz, seq):
    width = v.shape[-1]
    L = min(ML_CHUNK, seq)
    dh = width // ML_HEADS
    qk3 = qk.reshape(bsz, seq, 2 * width)
    v3 = v.reshape(bsz, seq, width)
    o3 = o.reshape(bsz, seq, width)
    full = lambda c: (0, 0)
    return pl.pallas_call(
        functools.partial(_mlstm_kernel, bsz=bsz, chunk=L, width=width),
        grid=(seq // L,),
        in_specs=[
            pl.BlockSpec((bsz, L, 2 * width), lambda c: (0, c, 0)),
            pl.BlockSpec((bsz, L, width), lambda c: (0, c, 0)),
            pl.BlockSpec((bsz, L, width), lambda c: (0, c, 0)),
            pl.BlockSpec((bsz, 16, L), lambda c: (0, 0, c)),
            pl.BlockSpec((CONV_WIDTH, 2 * width), full),
            pl.BlockSpec((1, 2 * width), full),
            pl.BlockSpec((1, width), full),
        ],
        out_specs=pl.BlockSpec((bsz, L, width), lambda c: (0, c, 0)),
        out_shape=jax.ShapeDtypeStruct((bsz, seq, width), BF16),
        scratch_shapes=[
            pltpu.VMEM((bsz, SUBLANES, 2 * width), F32),
            pltpu.VMEM((bsz * ML_HEADS, dh, 2 * dh), F32),
            pltpu.VMEM((bsz * ML_HEADS, SUBLANES, LANES), F32),
        ],
        compiler_params=pltpu.CompilerParams(
            dimension_semantics=("arbitrary",), vmem_limit_bytes=VMEM_LIMIT),
        name="mlstm",
    )(qk3, v3, o3, gt, conv_w.astype(F32), conv_b.reshape(1, -1).astype(F32),
      norm_g.reshape(1, -1).astype(F32))


def _post_kernel(x_ref, g_ref, yml_ref, wglu_ref, bglu_ref, wout_ref, nffn_ref, wrt_ref, br_ref,
                 h2_ref, xn_ref, eidx_ref, gate_ref, rank_ref, cnt_ref, carry_scr):
    step = pl.program_id(0)

    @pl.when(step == 0)
    def _():
        carry_scr[...] = jnp.zeros_like(carry_scr)

    g = g_ref[...]
    z = _dot(g, wglu_ref[...]) + bglu_ref[...]
    s5 = g.astype(F32) * _sigmoid(z)
    mix = jnp.concatenate([s5.astype(BF16), yml_ref[...]], axis=1)
    h2 = x_ref[...] + _dot(mix, wout_ref[...])
    h2_ref[...] = h2
    ms = jnp.mean(h2 * h2, axis=-1, keepdims=True)
    xn = h2 * lax.rsqrt(ms + RMS_EPS) * nffn_ref[...]
    xn_ref[...] = xn

    tm = xn.shape[0]
    logits = _dot_nt(wrt_ref[...], xn.astype(BF16)) + br_ref[...]
    eio = lax.broadcasted_iota(I32, (N_EXPERTS, tm), 0).astype(F32)
    vals = logits
    onehot = jnp.zeros((N_EXPERTS, tm), F32)
    idxs, tops = [], []
    for _ in range(TOP_K):
        mx = jnp.max(vals, axis=0, keepdims=True)
        idx = jnp.min(jnp.where(vals == mx, eio, float(N_EXPERTS)), axis=0, keepdims=True)
        sel = eio == idx
        onehot = onehot + sel.astype(F32)
        vals = jnp.where(sel, -jnp.inf, vals)
        idxs.append(idx)
        tops.append(mx)
    exps = [jnp.exp(t - tops[0]) for t in tops]
    tot = exps[0] + exps[1] + exps[2] + exps[3]
    pad_f = jnp.zeros((SUBLANES - TOP_K, tm), F32)
    eidx_ref[...] = jnp.concatenate(idxs + [pad_f], axis=0).astype(I32)
    gate_ref[...] = jnp.concatenate([e / tot for e in exps] + [pad_f], axis=0)

    rr = lax.broadcasted_iota(I32, (tm, tm), 0)
    cc = lax.broadcasted_iota(I32, (tm, tm), 1)
    before = (rr < cc).astype(BF16)
    carry = carry_scr[:, 0:1]
    rank_ex = _dot(onehot.astype(BF16), before) + carry
    ranks = [jnp.sum(jnp.where(eio == i, rank_ex, 0.0), axis=0, keepdims=True) for i in idxs]
    rank_ref[...] = jnp.concatenate(ranks + [pad_f], axis=0).astype(I32)
    carry_scr[...] = jnp.broadcast_to(carry + jnp.sum(onehot, axis=1, keepdims=True),
                                      carry_scr.shape)
    cnt_ref[...] = carry_scr[...]


def _post(x2d, g_nat, y_ml, w_glu, b_glu, w_out, norm_g, w_router, b_router):
    n_tok, d = x2d.shape
    s5w = g_nat.shape[-1]
    mlw = y_ml.shape[-1]
    tm = min(PROJ_TILE, n_tok)
    full = lambda i: (0, 0)
    row = lambda i: (i, 0)
    colb = lambda i: (0, i)
    return pl.pallas_call(
        _post_kernel,
        grid=(n_tok // tm,),
        in_specs=[
            pl.BlockSpec((tm, d), row),
            pl.BlockSpec((tm, s5w), row),
            pl.BlockSpec((tm, mlw), row),
            pl.BlockSpec((s5w, s5w), full),
            pl.BlockSpec((1, s5w), full),
            pl.BlockSpec((d, d), full),
            pl.BlockSpec((1, d), full),
            pl.BlockSpec((N_EXPERTS, d), full),
            pl.BlockSpec((N_EXPERTS, 1), full),
        ],
        out_specs=[
            pl.BlockSpec((tm, d), row),
            pl.BlockSpec((tm, d), row),
            pl.BlockSpec((SUBLANES, tm), colb),
            pl.BlockSpec((SUBLANES, tm), colb),
            pl.BlockSpec((SUBLANES, tm), colb),
            pl.BlockSpec((N_EXPERTS, LANES), full),
        ],
        out_shape=[
            jax.ShapeDtypeStruct((n_tok, d), F32),
            jax.ShapeDtypeStruct((n_tok, d), F32),
            jax.ShapeDtypeStruct((SUBLANES, n_tok), I32),
            jax.ShapeDtypeStruct((SUBLANES, n_tok), F32),
            jax.ShapeDtypeStruct((SUBLANES, n_tok), I32),
            jax.ShapeDtypeStruct((N_EXPERTS, LANES), F32),
        ],
        scratch_shapes=[pltpu.VMEM((N_EXPERTS, LANES), F32)],
        compiler_params=pltpu.CompilerParams(
            dimension_semantics=("arbitrary",), vmem_limit_bytes=VMEM_LIMIT),
        name="post_router",
    )(x2d, g_nat, y_ml, w_glu.astype(BF16), b_glu.reshape(1, -1).astype(F32), w_out.astype(BF16),
      norm_g.reshape(1, -1).astype(F32), w_router.T.astype(BF16),
      b_router.reshape(-1, 1).astype(F32))


def _gather_rows(idx_ref, n_rows, src_hbm, dst_vmem, sem):
    def body(r, carry):
        pltpu.make_async_copy(src_hbm.at[pl.ds(idx_ref[r], 1), :],
                              dst_vmem.at[pl.ds(r, 1), :], sem).start()
        return carry

    lax.fori_loop(0, n_rows, body, 0, unroll=8)


def _wait_rows(n_rows, src_hbm, dst_vmem, sem):
    pltpu.make_async_copy(src_hbm.at[pl.ds(0, n_rows), :], dst_vmem, sem).wait()


def _expert_kernel(te_ref, nu_ref, tok_ref, tokn_ref, xn_hbm, wup_ref, bup_ref, wdn_ref, bdn_ref,
                   y_ref, xbuf, sem, *, d_ff):
    i = pl.program_id(0)
    n_used = nu_ref[0]
    tm = xbuf.shape[1]
    slot = i % 2

    @pl.when(i == 0)
    def _():
        _gather_rows(tok_ref, tm, xn_hbm, xbuf.at[0], sem.at[0])

    @pl.when(i + 1 < n_used)
    def _():
        _gather_rows(tokn_ref, tm, xn_hbm, xbuf.at[1 - slot], sem.at[1 - slot])

    @pl.when(jnp.logical_or(i < n_used, i == 0))
    def _():
        _wait_rows(tm, xn_hbm, xbuf.at[slot], sem.at[slot])

    @pl.when(i < n_used)
    def _():
        xb = xbuf[slot].astype(BF16)
        h = _dot(xb, wup_ref[0].astype(BF16)) + bup_ref[0]
        gl = jnp.minimum(h[:, :d_ff], SWIGLU_LIMIT)
        lin = jnp.clip(h[:, d_ff:], -SWIGLU_LIMIT, SWIGLU_LIMIT)
        act = gl * _sigmoid(SWIGLU_ALPHA * gl) * (lin + 1.0)
        y_ref[...] = _dot(act.astype(BF16), wdn_ref[0].astype(BF16)) + bdn_ref[0]

    @pl.when(i >= n_used)
    def _():
        y_ref[...] = jnp.zeros_like(y_ref)


def _experts(xn, slot_tok, tile_e, n_used, w_up, b_up, w_down, b_down):
    n_tok, d = xn.shape
    n_slots = slot_tok.shape[0]
    tm = EXPERT_TILE
    n_tiles = n_slots // tm
    d_ff = w_down.shape[1]
    return pl.pallas_call(
        functools.partial(_expert_kernel, d_ff=d_ff),
        grid_spec=pltpu.PrefetchScalarGridSpec(
            num_scalar_prefetch=2,
            grid=(n_tiles,),
            in_specs=[
                pl.BlockSpec((tm,), lambda i, te, nu: (i,), memory_space=pltpu.SMEM),
                pl.BlockSpec((tm,), lambda i, te, nu: (jnp.minimum(i + 1, n_tiles - 1),),
                             memory_space=pltpu.SMEM),
                pl.BlockSpec(memory_space=pl.ANY),
                pl.BlockSpec((1, d, 2 * d_ff), lambda i, te, nu: (te[i], 0, 0)),
                pl.BlockSpec((1, 1, 2 * d_ff), lambda i, te, nu: (te[i], 0, 0)),
                pl.BlockSpec((1, d_ff, d), lambda i, te, nu: (te[i], 0, 0)),
                pl.BlockSpec((1, 1, d), lambda i, te, nu: (te[i], 0, 0)),
            ],
            out_specs=pl.BlockSpec((tm, d), lambda i, te, nu: (i, 0)),
            scratch_shapes=[
                pltpu.VMEM((2, tm, d), F32),
                pltpu.SemaphoreType.DMA((2,)),
            ],
        ),
        out_shape=jax.ShapeDtypeStruct((n_slots, d), F32),
        compiler_params=pltpu.CompilerParams(
            dimension_semantics=("arbitrary",), vmem_limit_bytes=VMEM_LIMIT),
        name="experts",
    )(tile_e, n_used, slot_tok, slot_tok, xn, w_up, b_up.reshape(N_EXPERTS, 1, -1),
      w_down, b_down.reshape(N_EXPERTS, 1, -1))


def _combine_kernel(dst_ref, dstn_ref, ys_hbm, h2_ref, gate_ref, ng_ref, out_ref, ybuf, sem,
                    *, n_steps):
    i = pl.program_id(0)
    tc = h2_ref.shape[0]
    rows = TOP_K * tc
    slot = i % 2

    @pl.when(i == 0)
    def _():
        _gather_rows(dst_ref, rows, ys_hbm, ybuf.at[0], sem.at[0])

    @pl.when(i + 1 < n_steps)
    def _():
        _gather_rows(dstn_ref, rows, ys_hbm, ybuf.at[1 - slot], sem.at[1 - slot])

    _wait_rows(rows, ys_hbm, ybuf.at[slot], sem.at[slot])
    acc = h2_ref[...]
    for k in range(TOP_K):
        acc = acc + gate_ref[:, k:k + 1] * ybuf[slot, k * tc:(k + 1) * tc, :]
    ms = jnp.mean(acc * acc, axis=-1, keepdims=True)
    out_ref[...] = acc * lax.rsqrt(ms + RMS_EPS) * ng_ref[...]


def _combine(y_slots, dest_km, h2, gate_cols, norm_g):
    n_tok, d = h2.shape
    tc = min(COMBINE_TILE, n_tok)
    n_steps = n_tok // tc
    rows = TOP_K * tc
    return pl.pallas_call(
        functools.partial(_combine_kernel, n_steps=n_steps),
        grid=(n_steps,),
        in_specs=[
            pl.BlockSpec((rows,), lambda i: (i,), memory_space=pltpu.SMEM),
            pl.BlockSpec((rows,), lambda i: (jnp.minimum(i + 1, n_steps - 1),),
                         memory_space=pltpu.SMEM),
            pl.BlockSpec(memory_space=pl.ANY),
            pl.BlockSpec((tc, d), lambda i: (i, 0)),
            pl.BlockSpec((tc, SUBLANES), lambda i: (i, 0)),
            pl.BlockSpec((1, d), lambda i: (0, 0)),
        ],
        out_specs=pl.BlockSpec((tc, d), lambda i: (i, 0)),
        out_shape=jax.ShapeDtypeStruct((n_tok, d), F32),
        scratch_shapes=[
            pltpu.VMEM((2, rows, d), F32),
            pltpu.SemaphoreType.DMA((2,)),
        ],
        compiler_params=pltpu.CompilerParams(
            dimension_semantics=("arbitrary",), vmem_limit_bytes=VMEM_LIMIT),
        name="combine",
    )(dest_km, dest_km, y_slots, h2, gate_cols, norm_g.reshape(1, -1).astype(F32))


def _moe(h2, xn, eidx, gate, rank, cnt, w_up, b_up, w_down, b_down, norm_final_g):
    n_tok, d = h2.shape
    tm = EXPERT_TILE
    n_assign = n_tok * TOP_K
    n_slots = n_assign + N_EXPERTS * tm
    n_tiles = n_slots // tm
    counts = cnt[:, 0].astype(I32)
    padded = (counts + tm - 1) // tm * tm
    pad_end = jnp.cumsum(padded)
    pad_start = pad_end - padded
    e_sel = eidx[:TOP_K][None] == jnp.arange(N_EXPERTS, dtype=I32)[:, None, None]
    dest = jnp.sum(jnp.where(e_sel, pad_start[:, None, None], 0), axis=0) + rank[:TOP_K]
    n_used = (pad_end[-1] // tm).astype(I32).reshape(1)
    tile_start = jnp.arange(n_tiles, dtype=I32) * tm
    tile_e = jnp.minimum(jnp.sum(pad_end[None, :] <= tile_start[:, None], axis=-1),
                         N_EXPERTS - 1).astype(I32)
    tok_ids = jnp.broadcast_to(jnp.arange(n_tok, dtype=I32)[None, :], (TOP_K, n_tok))
    slot_tok = jnp.zeros((n_slots,), I32).at[dest.reshape(-1)].set(tok_ids.reshape(-1))
    y_slots = _experts(xn, slot_tok, tile_e, n_used, w_up, b_up, w_down, b_down)
    tc = min(COMBINE_TILE, n_tok)
    dest_km = dest.reshape(TOP_K, n_tok // tc, tc).transpose(1, 0, 2).reshape(-1)
    return _combine(y_slots, dest_km, h2, gate.T, norm_final_g)


def kernel(x, norm_mix_g, w_in, s5_log_dt, s5_a_re, s5_a_im, s5_b_re, s5_b_im, s5_c_re, s5_c_im,
           s5_d, s5_w_glu, s5_b_glu, ml_conv_w, ml_conv_b, ml_b_gates, ml_norm_g, w_out,
           norm_ffn_g, w_router, b_router, w_up, b_up, w_down, b_down, norm_final_g):
    bsz, seq, d = x.shape
    depth = w_in.shape[0]
    assert depth == 1, "single-layer block"
    l = 0
    x2d = x.reshape(bsz * seq, d)
    qk, v, o, ut, gt = _in_proj(x2d, norm_mix_g[l], w_in[l], ml_b_gates[l], bsz, seq)
    g_t = _s5(ut, s5_log_dt[l], s5_a_re[l], s5_a_im[l], s5_b_re[l], s5_b_im[l],
              s5_c_re[l], s5_c_im[l], s5_d[l])
    g_nat = jnp.swapaxes(g_t, 1, 2).reshape(bsz * seq, -1)
    y_ml = _mlstm(qk, v, o, gt, ml_conv_w[l], ml_conv_b[l], ml_norm_g[l], bsz, seq)
    y_ml = y_ml.reshape(bsz * seq, -1)
    h2, xn, eidx, gate, rank, cnt = _post(x2d, g_nat, y_ml, s5_w_glu[l], s5_b_glu[l], w_out[l],
                                          norm_ffn_g[l], w_router[l], b_router[l])
    out = _moe(h2, xn, eidx, gate, rank, cnt, w_up[l], b_up[l], w_down[l], b_down[l],
               norm_final_g)
    return out.reshape(bsz, seq, d)
```

```python
import functools
import math

import jax
import jax.numpy as jnp
from jax import lax
from jax.experimental import pallas as pl
from jax.experimental.pallas import tpu as pltpu

F32 = jnp.float32
BF16 = jnp.bfloat16
I32 = jnp.int32
U32 = jnp.uint32

S5_GROUP = 16
S5_STATE = 64
ML_HEADS = 4
CONV_WIDTH = 4
N_EXPERTS = 32
TOP_K = 4
SWIGLU_LIMIT = 7.0
SWIGLU_ALPHA = 1.702
RMS_EPS = 1e-5
LN_EPS = 1e-6

LANES = 128
SUBLANES = 8
S5_CHUNK = LANES
ML_CHUNK = 128
PROJ_TILE = 512
EXPERT_TILE = 256
DISPATCH_TILE = PROJ_TILE
SLAB_ALIGN = SUBLANES
BIG_CHUNK = 32
STAGE_ROWS = -(-(DISPATCH_TILE * TOP_K + N_EXPERTS * SLAB_ALIGN) // 512) * 512
VMEM_LIMIT = 56 * 1024 * 1024

_NT = (((1,), (1,)), ((), ()))
_TN = (((0,), (0,)), ((), ()))


def _dot(a, b):
    return jnp.dot(a, b, preferred_element_type=F32)


def _dot_nt(a, b):
    return lax.dot_general(a, b, _NT, preferred_element_type=F32)


def _dot_tn(a, b):
    return lax.dot_general(a, b, _TN, preferred_element_type=F32)


def _split3(x):
    p1 = x.astype(BF16)
    r1 = x - p1.astype(F32)
    p2 = r1.astype(BF16)
    r2 = r1 - p2.astype(F32)
    return p1, p2, r2.astype(BF16)


def _sigmoid(x):
    return 1.0 / (1.0 + jnp.exp(-x))


def _inproj_kernel(x_ref, g_ref, wnat_ref, wut_ref, wgt_ref, bg_ref,
                   qk_ref, v_ref, o_ref, ut_ref, gt_ref, *, width):
    x = x_ref[...]
    ms = jnp.mean(x * x, axis=-1, keepdims=True)
    hn = (x * lax.rsqrt(ms + RMS_EPS) * g_ref[...]).astype(BF16)
    nat = _dot(hn, wnat_ref[...])
    qk_ref[...] = nat[:, :2 * width].astype(BF16)
    v_ref[...] = nat[:, 2 * width:3 * width].astype(BF16)
    o_ref[...] = nat[:, 3 * width:].astype(BF16)
    ut_ref[0] = _dot_nt(wut_ref[...], hn).astype(BF16)
    gt_ref[0] = _dot_nt(wgt_ref[...], hn) + bg_ref[...]


def _in_proj(x2d, norm_g, w_in, b_gates, bsz, seq):
    n_tok, d = x2d.shape
    s5w = d // 2
    mlw = d - s5w
    tm = min(PROJ_TILE, seq)
    tpb = seq // tm
    w_bf = w_in.astype(BF16)
    w_nat = w_bf[:, s5w:s5w + 4 * mlw]
    w_ut = w_bf[:, :s5w].T
    n_gate = 2 * ML_HEADS
    w_gt = jnp.zeros((16, d), BF16).at[:n_gate].set(w_bf[:, s5w + 4 * mlw:].T)
    b_g = jnp.zeros((16, 1), F32).at[:n_gate, 0].set(b_gates.astype(F32))
    grid = (n_tok // tm,)
    full = lambda i: (0, 0)
    return pl.pallas_call(
        functools.partial(_inproj_kernel, width=mlw),
        grid=grid,
        in_specs=[
            pl.BlockSpec((tm, d), lambda i: (i, 0)),
            pl.BlockSpec((1, d), full),
            pl.BlockSpec((d, 4 * mlw), full),
            pl.BlockSpec((s5w, d), full),
            pl.BlockSpec((16, d), full),
            pl.BlockSpec((16, 1), full),
        ],
        out_specs=[
            pl.BlockSpec((tm, 2 * mlw), lambda i: (i, 0)),
            pl.BlockSpec((tm, mlw), lambda i: (i, 0)),
            pl.BlockSpec((tm, mlw), lambda i: (i, 0)),
            pl.BlockSpec((1, s5w, tm), lambda i: (i // tpb, 0, i % tpb)),
            pl.BlockSpec((1, 16, tm), lambda i: (i // tpb, 0, i % tpb)),
        ],
        out_shape=[
            jax.ShapeDtypeStruct((n_tok, 2 * mlw), BF16),
            jax.ShapeDtypeStruct((n_tok, mlw), BF16),
            jax.ShapeDtypeStruct((n_tok, mlw), BF16),
            jax.ShapeDtypeStruct((bsz, s5w, seq), BF16),
            jax.ShapeDtypeStruct((bsz, 16, seq), F32),
        ],
        compiler_params=pltpu.CompilerParams(
            dimension_semantics=("parallel",), vmem_limit_bytes=VMEM_LIMIT),
        name="in_proj",
    )(x2d, norm_g.reshape(1, d).astype(F32), w_nat, w_ut, w_gt, b_g)


def _s5_kernel(d_ref, x_ref, prm_ref, c_ref, bt_ref, out_ref, kmat_scr, rhs_scr, *, bsz, nc):
    L = S5_CHUNK
    P = S5_GROUP
    N = S5_STATE
    grp = pl.program_id(0)
    lane = lax.broadcasted_iota(I32, (1, 2 * N), 1)
    lo = lane < N
    a_re = jnp.minimum(prm_ref[0, 0:1, :], -1e-4)
    a_im = prm_ref[0, 1:2, :]
    dt = jnp.exp(prm_ref[0, 2:3, :])
    zr = dt * a_re
    zi = dt * a_im

    def powtab(e, swap=False):
        ang = e * zi
        pick_cos = (lane >= N) if swap else lo
        return jnp.exp(e * zr) * jnp.where(pick_cos, jnp.cos(ang), jnp.sin(ang))

    er = jnp.exp(zr) * jnp.cos(zi)
    ei = jnp.exp(zr) * jnp.sin(zi)
    den = a_re * a_re + a_im * a_im
    coef_r = ((er - 1.0) * a_re + ei * a_im) / den
    coef_i = (ei * a_re - (er - 1.0) * a_im) / den
    c_r = c_ref[0, 0]
    c_i = c_ref[0, 1]
    bb_r = coef_r * bt_ref[0, 0] - coef_i * bt_ref[0, 1]
    bb_i = coef_r * bt_ref[0, 1] + coef_i * bt_ref[0, 0]

    tau = lax.broadcasted_iota(I32, (L, 1), 0).astype(F32)

    cb_rows = []
    for q in range(P):
        cbr = c_r * bb_r[q:q + 1] - c_i * bb_i[q:q + 1]
        cbi = c_r * bb_i[q:q + 1] + c_i * bb_r[q:q + 1]
        cb_rows.append(jnp.where(lo, cbr, -cbi))
    cb = jnp.concatenate(cb_rows, axis=0)
    pt0 = powtab(tau)
    c1, c2, c3 = _split3(cb)
    t1, t2, t3 = _split3(pt0)
    kmat_scr[...] = (_dot_nt(c1, t1) + _dot_nt(c1, t2) + _dot_nt(c2, t1)
                     + _dot_nt(c2, t2) + _dot_nt(c1, t3) + _dot_nt(c3, t1))

    rr = lax.broadcasted_iota(I32, (L, L), 0)
    cc = lax.broadcasted_iota(I32, (L, L), 1)
    causal = cc >= rr

    def build_q(q, carry):
        for p in range(P):
            krow = kmat_scr[pl.ds(q * P + p, 1), :]
            kb = jnp.broadcast_to(krow, (L, L))
            toe = pltpu.roll(kb, 0, 1, stride=1, stride_axis=0)
            rhs_scr[pl.ds(pl.multiple_of(q * L, L), L), p * L:(p + 1) * L] = (
                jnp.where(causal, toe, 0.0).astype(BF16))
        return carry

    lax.fori_loop(0, P, build_q, 0)

    pt_rev = powtab(L - 1.0 - tau)
    pt_rev_sw = powtab(L - 1.0 - tau, swap=True)
    f_rows = []
    for q in range(P):
        a1 = bb_r[q:q + 1]
        a2 = jnp.where(lo, -bb_i[q:q + 1], bb_i[q:q + 1])
        f_rows.append((a1 * pt_rev + a2 * pt_rev_sw).astype(BF16))
    fmat = jnp.concatenate(f_rows, axis=0)
    pt1 = powtab(tau + 1.0)
    pt1_sw = powtab(tau + 1.0, swap=True)
    e_rows = []
    for p in range(P):
        b1 = jnp.where(lo, c_r[p:p + 1], -c_r[p:p + 1])
        b2 = -c_i[p:p + 1]
        e_rows.append((b1 * pt1 + b2 * pt1_sw).astype(BF16))
    emat_t = jnp.concatenate(e_rows, axis=0)

    lhs = jnp.concatenate(
        [jnp.concatenate([x_ref[b, q] for q in range(P)], axis=1) for b in range(bsz)], axis=0)
    y = _dot(lhs, rhs_scr[...])
    s_end = _dot(lhs, fmat)

    m_rows = bsz * nc
    ridx = lax.broadcasted_iota(I32, (m_rows, 1), 0)
    cidx = ridx % nc
    h = jnp.where(cidx >= 1, pltpu.roll(s_end, 1, 0), 0.0)
    d = 1
    while d < nc:
        e = float(d * L)
        pr = jnp.exp(e * zr) * jnp.cos(e * zi)
        pi = jnp.exp(e * zr) * jnp.sin(e * zi)
        sh = jnp.where(cidx >= d, pltpu.roll(h, d, 0), 0.0)
        h = h + pr * sh + jnp.where(lo, -pi, pi) * pltpu.roll(sh, N, 1)
        d *= 2
    y = y + _dot_nt(h.astype(BF16), emat_t)

    for b in range(bsz):
        for p in range(P):
            yp = (y[b * nc:(b + 1) * nc, p * L:(p + 1) * L]
                  + d_ref[grp * P + p] * x_ref[b, p].astype(F32))
            out_ref[b, p] = jax.nn.gelu(yp).astype(BF16)


def _s5(ut, log_dt, a_re, a_im, b_re, b_im, c_re, c_im, d_skip):
    bsz, s5w, seq = ut.shape
    L = S5_CHUNK
    nc = seq // L
    groups = s5w // S5_GROUP
    n = S5_STATE
    x4 = ut.reshape(bsz, s5w, nc, L)
    dup = lambda t: jnp.concatenate([t, t], axis=-1).astype(F32)
    prm = jnp.zeros((groups, SUBLANES, 2 * n), F32)
    prm = prm.at[:, 0].set(dup(a_re)).at[:, 1].set(dup(a_im))
    prm = prm.at[:, 2].set(jnp.broadcast_to(log_dt.astype(F32)[:, None], (groups, 2 * n)))
    cpar = jnp.stack([dup(c_re), dup(c_im)], axis=1)
    btpar = jnp.stack([dup(jnp.swapaxes(b_re, 1, 2)), dup(jnp.swapaxes(b_im, 1, 2))], axis=1)
    out = pl.pallas_call(
        functools.partial(_s5_kernel, bsz=bsz, nc=nc),
        grid_spec=pltpu.PrefetchScalarGridSpec(
            num_scalar_prefetch=1,
            grid=(groups,),
            in_specs=[
                pl.BlockSpec((bsz, S5_GROUP, nc, L), lambda g, d: (0, g, 0, 0)),
                pl.BlockSpec((1, SUBLANES, 2 * n), lambda g, d: (g, 0, 0)),
                pl.BlockSpec((1, 2, S5_GROUP, 2 * n), lambda g, d: (g, 0, 0, 0)),
                pl.BlockSpec((1, 2, S5_GROUP, 2 * n), lambda g, d: (g, 0, 0, 0)),
            ],
            out_specs=pl.BlockSpec((bsz, S5_GROUP, nc, L), lambda g, d: (0, g, 0, 0)),
            scratch_shapes=[
                pltpu.VMEM((S5_GROUP * S5_GROUP, L), F32),
                pltpu.VMEM((S5_GROUP * L, S5_GROUP * L), BF16),
            ],
        ),
        out_shape=jax.ShapeDtypeStruct((bsz, s5w, nc, L), BF16),
        compiler_params=pltpu.CompilerParams(
            dimension_semantics=("parallel",), vmem_limit_bytes=VMEM_LIMIT),
        name="s5",
    )(d_skip.astype(F32), x4, prm, cpar, btpar)
    return out.reshape(bsz, s5w, seq)


def _log_sigmoid(x):
    return jnp.minimum(x, 0.0) - jnp.log(1.0 + jnp.exp(-jnp.abs(x)))


def _mlstm_kernel(qk_ref, v_ref, o_ref, gt_ref, cw_ref, cb_ref, ng_ref, y_ref,
                  tail_scr, c_scr, m_scr, *, bsz, chunk, width):
    L = chunk
    H = ML_HEADS
    dh = width // H
    step = pl.program_id(0)

    @pl.when(step == 0)
    def _():
        tail_scr[...] = jnp.zeros_like(tail_scr)
        c_scr[...] = jnp.zeros_like(c_scr)
        m_scr[...] = jnp.zeros_like(m_scr)

    assert L == LANES and dh == LANES, "column replication below uses one 128x128 tile per head"
    rr = lax.broadcasted_iota(I32, (L, L), 0)
    cc = lax.broadcasted_iota(I32, (L, L), 1)
    causal = cc <= rr
    ltri = causal.astype(BF16)
    utri = (rr <= cc).astype(BF16)
    eye = (rr == cc).astype(BF16)
    ones_blk = jnp.ones((L, dh), BF16)
    grow = lax.broadcasted_iota(I32, (16, 1), 0)
    lane_row = lax.broadcasted_iota(I32, (SUBLANES, L), 1)
    scale = 1.0 / math.sqrt(dh)
    inv_dh = 1.0 / dh

    def replicate(mat, rows):
        p = jnp.concatenate([jnp.broadcast_to(r, (LANES, L)) for r in rows], axis=0)
        p1 = p.astype(BF16)
        p2 = (p - p1.astype(F32)).astype(BF16)
        return _dot_nt(mat, p1) + _dot_nt(mat, p2)

    def rowsum(x):
        hi = x.astype(BF16)
        lo = (x - hi.astype(F32)).astype(BF16)
        return _dot(hi, ones_blk) + _dot(lo, ones_blk)

    n_str = bsz * H
    streams = [(b, hd) for b in range(bsz) for hd in range(H)]
    tails = [tail_scr[b] for b in range(bsz)]
    caugs = [c_scr[i] for i in range(n_str)]
    m_prevs = [m_scr[i][0:1, 0:1] for i in range(n_str)]

    q_all, k_all, new_tails, i_rows, lf_rows, b_rows = [], [], [], [], [], []
    for b in range(bsz):
        xqk = qk_ref[b].astype(F32)
        ext = jnp.concatenate([tails[b], xqk], axis=0)
        acc = jnp.broadcast_to(cb_ref[...], xqk.shape)
        for j in range(CONV_WIDTH):
            back = CONV_WIDTH - 1 - j
            sh = ext if back == 0 else pltpu.roll(ext, back, 0)
            acc = acc + cw_ref[j:j + 1, :] * sh[SUBLANES:]
        new_tails.append(xqk[L - SUBLANES:])
        qkc = acc * _sigmoid(acc)
        q_all.append(qkc[:, :width].astype(BF16))
        k_all.append((qkc[:, width:] * scale).astype(BF16))
        g = gt_ref[b]
        g2 = jnp.where(grow >= H, _log_sigmoid(g), g)
        p1, p2, p3 = _split3(g2)
        brow = _dot(p1, utri) + _dot(p2, utri) + _dot(p3, utri)
        i_rows.append(g[0:H])
        lf_rows.append(g2[H:2 * H])
        b_rows.append(brow[H:2 * H])

    cm_rows = []
    for b in range(bsz):
        x = i_rows[b] - b_rows[b]
        x = jnp.concatenate([x, x], axis=0)
        sft = 1
        while sft < L:
            x = jnp.maximum(x, jnp.where(lane_row >= sft, pltpu.roll(x, sft, 1), -jnp.inf))
            sft *= 2
        cm_rows.append(x[0:H])

    bc_rep = [replicate(ltri, [lf_rows[b][h:h + 1] for h in range(H)]) for b in range(bsz)]
    ic_rep = [replicate(eye, [i_rows[b][h:h + 1] for h in range(H)]) for b in range(bsz)]
    cm_rep = [replicate(eye, [cm_rows[b][h:h + 1] for h in range(H)]) for b in range(bsz)]

    q_h = [q_all[b][:, hd * dh:(hd + 1) * dh] for b, hd in streams]
    k_h = [k_all[b][:, hd * dh:(hd + 1) * dh] for b, hd in streams]
    vaug = [jnp.concatenate([v_ref[b][:, hd * dh:(hd + 1) * dh], ones_blk], axis=1)
            for b, hd in streams]
    scores = [_dot_nt(q_h[i], k_h[i]) for i in range(n_str)]
    inter = [_dot(q_h[i], caugs[i].astype(BF16)) for i in range(n_str)]

    m_reps, w_inter, s_bf, kw, decays, new_ms = [], [], [], [], [], []
    for i, (b, hd) in enumerate(streams):
        sl = slice(hd * LANES, (hd + 1) * LANES)
        b_r = b_rows[b][hd:hd + 1]
        i_r = i_rows[b][hd:hd + 1]
        b_last = b_r[:, L - 1:L]
        bc = bc_rep[b][:, sl]
        m_prev = m_prevs[i]
        m = bc + jnp.maximum(cm_rep[b][:, sl], m_prev)
        m_reps.append(m)
        w_inter.append(jnp.exp(bc + m_prev - m))
        log_d = jnp.where(causal, bc - b_r + i_r, -jnp.inf)
        s_bf.append((scores[i] * jnp.exp(log_d - m)).astype(BF16))
        m_next = jnp.maximum(b_last + m_prev, b_last + cm_rows[b][hd:hd + 1, L - 1:L])
        decays.append(jnp.exp(b_last + m_prev - m_next))
        ws = jnp.exp(b_last - bc + ic_rep[b][:, sl] - m_next)
        kw.append((k_h[i].astype(F32) * ws).astype(BF16))
        new_ms.append(jnp.broadcast_to(m_next, (SUBLANES, LANES)))

    intra = [_dot(s_bf[i], vaug[i]) for i in range(n_str)]
    upd = [_dot_tn(kw[i], vaug[i]) for i in range(n_str)]

    hh = []
    for i in range(n_str):
        num = w_inter[i] * inter[i][:, :dh] + intra[i][:, :dh]
        den = w_inter[i] * inter[i][:, dh:] + intra[i][:, dh:]
        hh.append(num / jnp.maximum(jnp.abs(den), jnp.exp(-m_reps[i])))
    mu = [rowsum(h) * inv_dh for h in hh]
    ctr = [hh[i] - mu[i] for i in range(n_str)]
    var = [rowsum(c * c) * inv_dh for c in ctr]
    outs = []
    for i, (b, hd) in enumerate(streams):
        sl = slice(hd * dh, (hd + 1) * dh)
        hn = ctr[i] * lax.rsqrt(var[i] + LN_EPS) * ng_ref[:, sl]
        outs.append(hn * _sigmoid(o_ref[b][:, sl].astype(F32)))

    for b in range(bsz):
        tail_scr[b] = new_tails[b]
        y_ref[b] = jnp.concatenate(outs[b * H:(b + 1) * H], axis=1).astype(BF16)
    for i in range(n_str):
        c_scr[i] = decays[i] * caugs[i] + upd[i]
        m_scr[i] = new_ms[i]


def _mlstm(qk, v, o, gt, conv_w, conv_b, norm_g, bsz, seq):
    width = v.shape[-1]
    L = min(ML_CHUNK, seq)
    dh = width // ML_HEADS
    qk3 = qk.reshape(bsz, seq, 2 * width)
    v3 = v.reshape(bsz, seq, width)
    o3 = o.reshape(bsz, seq, width)
    full = lambda c: (0, 0)
    return pl.pallas_call(
        functools.partial(_mlstm_kernel, bsz=bsz, chunk=L, width=width),
        grid=(seq // L,),
        in_specs=[
            pl.BlockSpec((bsz, L, 2 * width), lambda c: (0, c, 0)),
            pl.BlockSpec((bsz, L, width), lambda c: (0, c, 0)),
            pl.BlockSpec((bsz, L, width), lambda c: (0, c, 0)),
            pl.BlockSpec((bsz, 16, L), lambda c: (0, 0, c)),
            pl.BlockSpec((CONV_WIDTH, 2 * width), full),
            pl.BlockSpec((1, 2 * width), full),
            pl.BlockSpec((1, width), full),
        ],
        out_specs=pl.BlockSpec((bsz, L, width), lambda c: (0, c, 0)),
        out_shape=jax.ShapeDtypeStruct((bsz, seq, width), BF16),
        scratch_shapes=[
            pltpu.VMEM((bsz, SUBLANES, 2 * width), F32),
            pltpu.VMEM((bsz * ML_HEADS, dh, 2 * dh), F32),
            pltpu.VMEM((bsz * ML_HEADS, SUBLANES, LANES), F32),
        ],
        compiler_params=pltpu.CompilerParams(
            dimension_semantics=("arbitrary",), vmem_limit_bytes=VMEM_LIMIT),
        name="mlstm",
    )(qk3, v3, o3, gt, conv_w.astype(F32), conv_b.reshape(1, -1).astype(F32),
      norm_g.reshape(1, -1).astype(F32))


def _post_kernel(x_ref, g_ref, yml_ref, wglu_ref, bglu_ref, wout_ref, nffn_ref, wrt_ref, br_ref,
                 h2_ref, xn_ref, eidx_ref, gate_ref, rank_ref, tcnt_ref, carry_scr):
    step = pl.program_id(0)

    @pl.when(step == 0)
    def _():
        carry_scr[...] = jnp.zeros_like(carry_scr)

    g = g_ref[...]
    z = _dot(g, wglu_ref[...]) + bglu_ref[...]
    s5 = g.astype(F32) * _sigmoid(z)
    mix = jnp.concatenate([s5.astype(BF16), yml_ref[...]], axis=1)
    h2 = x_ref[...] + _dot(mix, wout_ref[...])
    h2_ref[...] = h2
    ms = jnp.mean(h2 * h2, axis=-1, keepdims=True)
    xn = h2 * lax.rsqrt(ms + RMS_EPS) * nffn_ref[...]
    xb = xn.astype(BF16)
    xn_ref[...] = xb

    tm = xn.shape[0]
    logits = _dot_nt(wrt_ref[...], xb) + br_ref[...]
    eio = lax.broadcasted_iota(I32, (N_EXPERTS, tm), 0).astype(F32)
    vals = logits
    onehot = jnp.zeros((N_EXPERTS, tm), F32)
    idxs, tops = [], []
    for _ in range(TOP_K):
        mx = jnp.max(vals, axis=0, keepdims=True)
        idx = jnp.min(jnp.where(vals == mx, eio, float(N_EXPERTS)), axis=0, keepdims=True)
        sel = eio == idx
        onehot = onehot + sel.astype(F32)
        vals = jnp.where(sel, -jnp.inf, vals)
        idxs.append(idx)
        tops.append(mx)
    exps = [jnp.exp(t - tops[0]) for t in tops]
    tot = exps[0] + exps[1] + exps[2] + exps[3]
    pad_f = jnp.zeros((SUBLANES - TOP_K, tm), F32)
    eidx_ref[...] = jnp.concatenate(idxs + [pad_f], axis=0).astype(I32)
    gate_ref[...] = jnp.concatenate([e / tot for e in exps] + [pad_f], axis=0)

    rr = lax.broadcasted_iota(I32, (tm, tm), 0)
    cc = lax.broadcasted_iota(I32, (tm, tm), 1)
    before = (rr < cc).astype(BF16)
    carry = carry_scr[:, 0:1]
    rank_ex = _dot(onehot.astype(BF16), before) + carry
    ranks = [jnp.sum(jnp.where(eio == i, rank_ex, 0.0), axis=0, keepdims=True) for i in idxs]
    rank_ref[...] = jnp.concatenate(ranks + [pad_f], axis=0).astype(I32)
    tile_cnt = jnp.sum(onehot, axis=1, keepdims=True)
    carry_scr[...] = jnp.broadcast_to(carry + tile_cnt, carry_scr.shape)
    tcnt_ref[0] = jnp.broadcast_to(tile_cnt, carry_scr.shape)


def _post(x2d, g_nat, y_ml, w_glu, b_glu, w_out, norm_g, w_router, b_router):
    n_tok, d = x2d.shape
    s5w = g_nat.shape[-1]
    mlw = y_ml.shape[-1]
    tm = min(PROJ_TILE, n_tok)
    full = lambda i: (0, 0)
    row = lambda i: (i, 0)
    colb = lambda i: (0, i)
    return pl.pallas_call(
        _post_kernel,
        grid=(n_tok // tm,),
        in_specs=[
            pl.BlockSpec((tm, d), row),
            pl.BlockSpec((tm, s5w), row),
            pl.BlockSpec((tm, mlw), row),
            pl.BlockSpec((s5w, s5w), full),
            pl.BlockSpec((1, s5w), full),
            pl.BlockSpec((d, d), full),
            pl.BlockSpec((1, d), full),
            pl.BlockSpec((N_EXPERTS, d), full),
            pl.BlockSpec((N_EXPERTS, 1), full),
        ],
        out_specs=[
            pl.BlockSpec((tm, d), row),
            pl.BlockSpec((tm, d), row),
            pl.BlockSpec((SUBLANES, tm), colb),
            pl.BlockSpec((SUBLANES, tm), colb),
            pl.BlockSpec((SUBLANES, tm), colb),
            pl.BlockSpec((1, N_EXPERTS, LANES), lambda i: (i, 0, 0)),
        ],
        out_shape=[
            jax.ShapeDtypeStruct((n_tok, d), F32),
            jax.ShapeDtypeStruct((n_tok, d), BF16),
            jax.ShapeDtypeStruct((SUBLANES, n_tok), I32),
            jax.ShapeDtypeStruct((SUBLANES, n_tok), F32),
            jax.ShapeDtypeStruct((SUBLANES, n_tok), I32),
            jax.ShapeDtypeStruct((n_tok // tm, N_EXPERTS, LANES), F32),
        ],
        scratch_shapes=[pltpu.VMEM((N_EXPERTS, LANES), F32)],
        compiler_params=pltpu.CompilerParams(
            dimension_semantics=("arbitrary",), vmem_limit_bytes=VMEM_LIMIT),
        name="post_router",
    )(x2d, g_nat, y_ml, w_glu.astype(BF16), b_glu.reshape(1, -1).astype(F32), w_out.astype(BF16),
      norm_g.reshape(1, -1).astype(F32), w_router.T.astype(BF16),
      b_router.reshape(-1, 1).astype(F32))


def _slab_loop(base, nbig_ref, nsmall_ref, fn, n_slabs=N_EXPERTS):
    def per_expert(e, carry):
        idx = base + e
        nbig = nbig_ref[idx]

        def big(j, c2):
            fn(idx, j * BIG_CHUNK, BIG_CHUNK)
            return c2

        def small(j, c2):
            fn(idx, nbig * BIG_CHUNK + j * SLAB_ALIGN, SLAB_ALIGN)
            return c2

        lax.fori_loop(0, nbig, big, 0)
        lax.fori_loop(0, nsmall_ref[idx], small, 0)
        return carry

    lax.fori_loop(0, n_slabs, per_expert, 0)


def _unpack_pairs(words):
    lo = lax.bitcast_convert_type(words << 16, F32).astype(BF16)
    hi = lax.bitcast_convert_type(words & jnp.uint32(0xFFFF0000), F32).astype(BF16)
    return lo, hi


def _pack_pairs(lo_f32, hi_f32):
    lo = lax.bitcast_convert_type(lo_f32, U32) >> 16
    hi = lax.bitcast_convert_type(hi_f32, U32) & jnp.uint32(0xFFFF0000)
    return hi | lo


def _dispatch_kernel(nbig_ref, nsmall_ref, off_ref, row_ref, znbig_ref, znsmall_ref, zrow_ref,
                     xn_ref, pos_ref, xs_hbm, stage, zbuf, sem, zsem, *, n_steps):
    i = pl.program_id(0)
    slot = i % 2
    tw, d = xn_ref.shape
    n_rows = stage.shape[1]
    half = d // 2

    def slab(s, idx, r0, rows):
        src = stage.at[s, pl.ds(pl.multiple_of(off_ref[idx] + r0, SLAB_ALIGN), rows), :]
        dst = xs_hbm.at[pl.ds(pl.multiple_of(row_ref[idx] + r0, SLAB_ALIGN), rows), :]
        return pltpu.make_async_copy(src, dst, sem)

    @pl.when(i == 0)
    def _():
        zbuf[...] = jnp.zeros_like(zbuf)

        def tail(idx, r0, rows):
            dst = xs_hbm.at[pl.ds(pl.multiple_of(zrow_ref[idx] + r0, SLAB_ALIGN), rows), :]
            return pltpu.make_async_copy(zbuf.at[pl.ds(0, rows), :], dst, zsem)

        _slab_loop(0, znbig_ref, znsmall_ref, lambda *a: tail(*a).start(), N_EXPERTS + 1)
        _slab_loop(0, znbig_ref, znsmall_ref, lambda *a: tail(*a).wait(), N_EXPERTS + 1)

    xt = xn_ref[...]
    blk = 512
    for r0 in range(0, n_rows, blk):
        rid = lax.broadcasted_iota(I32, (blk, tw), 0) + r0
        hit = rid == pos_ref[0:1, :]
        for k in range(1, TOP_K):
            hit = jnp.logical_or(hit, rid == pos_ref[k:k + 1, :])
        srt = _dot(jnp.where(hit, 1.0, 0.0).astype(BF16), xt)
        stage[slot, r0:r0 + blk, :] = _pack_pairs(srt[:, :half], srt[:, half:])

    @pl.when(i > 0)
    def _():
        _slab_loop((i - 1) * N_EXPERTS, nbig_ref, nsmall_ref,
                   lambda *a: slab(1 - slot, *a).wait())

    _slab_loop(i * N_EXPERTS, nbig_ref, nsmall_ref, lambda *a: slab(slot, *a).start())

    @pl.when(i == n_steps - 1)
    def _():
        _slab_loop(i * N_EXPERTS, nbig_ref, nsmall_ref, lambda *a: slab(slot, *a).wait())


def _dispatch(xn, pos8, slabs, tails, n_slots):
    n_tok, d = xn.shape
    tw = DISPATCH_TILE
    n_steps = n_tok // tw
    return pl.pallas_call(
        functools.partial(_dispatch_kernel, n_steps=n_steps),
        grid_spec=pltpu.PrefetchScalarGridSpec(
            num_scalar_prefetch=7,
            grid=(n_steps,),
            in_specs=[
                pl.BlockSpec((tw, d), lambda i, *_: (i, 0)),
                pl.BlockSpec((SUBLANES, tw), lambda i, *_: (0, i)),
            ],
            out_specs=pl.BlockSpec(memory_space=pl.ANY),
            scratch_shapes=[
                pltpu.VMEM((2, STAGE_ROWS, d // 2), U32),
                pltpu.VMEM((BIG_CHUNK, d // 2), U32),
                pltpu.SemaphoreType.DMA(()),
                pltpu.SemaphoreType.DMA(()),
            ],
        ),
        out_shape=jax.ShapeDtypeStruct((n_slots, d // 2), U32),
        compiler_params=pltpu.CompilerParams(
            dimension_semantics=("arbitrary",), vmem_limit_bytes=VMEM_LIMIT),
        name="dispatch",
    )(*slabs, *tails, xn, pos8)


def _expert_kernel(te_ref, nu_ref, x_ref, wup_ref, bup_ref, wdn_ref, bdn_ref, y_ref, *, d_ff):
    i = pl.program_id(0)
    n_used = nu_ref[0]

    @pl.when(i < n_used)
    def _():
        lo, hi = _unpack_pairs(x_ref[...])
        half = lo.shape[1]
        h = (_dot(lo, wup_ref[0, :half, :].astype(BF16))
             + _dot(hi, wup_ref[0, half:, :].astype(BF16)) + bup_ref[0])
        gl = jnp.minimum(h[:, :d_ff], SWIGLU_LIMIT)
        lin = jnp.clip(h[:, d_ff:], -SWIGLU_LIMIT, SWIGLU_LIMIT)
        act = gl * _sigmoid(SWIGLU_ALPHA * gl) * (lin + 1.0)
        y = _dot(act.astype(BF16), wdn_ref[0].astype(BF16)) + bdn_ref[0]
        yb = y.astype(BF16).astype(F32)
        y_ref[...] = _pack_pairs(yb[:, :half], yb[:, half:])

    @pl.when(i >= n_used)
    def _():
        y_ref[...] = jnp.zeros_like(y_ref)


def _experts(xs, tile_e, n_used, w_up, b_up, w_down, b_down):
    n_slots, half = xs.shape
    d = 2 * half
    tm = EXPERT_TILE
    n_tiles = n_slots // tm
    d_ff = w_down.shape[1]
    return pl.pallas_call(
        functools.partial(_expert_kernel, d_ff=d_ff),
        grid_spec=pltpu.PrefetchScalarGridSpec(
            num_scalar_prefetch=2,
            grid=(n_tiles,),
            in_specs=[
                pl.BlockSpec((tm, half), lambda i, te, nu: (jnp.minimum(i, nu[0] - 1), 0)),
                pl.BlockSpec((1, d, 2 * d_ff), lambda i, te, nu: (te[i], 0, 0)),
                pl.BlockSpec((1, 1, 2 * d_ff), lambda i, te, nu: (te[i], 0, 0)),
                pl.BlockSpec((1, d_ff, d), lambda i, te, nu: (te[i], 0, 0)),
                pl.BlockSpec((1, 1, d), lambda i, te, nu: (te[i], 0, 0)),
            ],
            out_specs=pl.BlockSpec((tm, half), lambda i, te, nu: (i, 0)),
        ),
        out_shape=jax.ShapeDtypeStruct((n_slots, half), U32),
        compiler_params=pltpu.CompilerParams(
            dimension_semantics=("arbitrary",), vmem_limit_bytes=VMEM_LIMIT),
        name="experts",
    )(tile_e, n_used, xs, w_up, b_up.reshape(N_EXPERTS, 1, -1),
      w_down, b_down.reshape(N_EXPERTS, 1, -1))


def _combine_kernel(nbig_ref, nsmall_ref, off_ref, row_ref, ys_hbm, h2_ref, posc_ref, gatec_ref,
                    ng_ref, out_ref, stage, sem, *, n_steps):
    i = pl.program_id(0)
    slot = i % 2
    tw, d = h2_ref.shape
    n_rows = stage.shape[1]

    def slab(s, idx, r0, rows):
        src = ys_hbm.at[pl.ds(pl.multiple_of(row_ref[idx] + r0, SLAB_ALIGN), rows), :]
        dst = stage.at[s, pl.ds(pl.multiple_of(off_ref[idx] + r0, SLAB_ALIGN), rows), :]
        return pltpu.make_async_copy(src, dst, sem.at[s])

    @pl.when(i == 0)
    def _():
        stage[...] = jnp.zeros_like(stage)
        _slab_loop(0, nbig_ref, nsmall_ref, lambda *a: slab(0, *a).start())

    @pl.when(i + 1 < n_steps)
    def _():
        _slab_loop((i + 1) * N_EXPERTS, nbig_ref, nsmall_ref,
                   lambda *a: slab(1 - slot, *a).start())

    _slab_loop(i * N_EXPERTS, nbig_ref, nsmall_ref, lambda *a: slab(slot, *a).wait())

    y_lo, y_hi = _unpack_pairs(stage[slot])
    blk = 128
    for t0 in range(0, tw, blk):
        cid = lax.broadcasted_iota(I32, (blk, n_rows), 1)
        pg = jnp.zeros((blk, n_rows), F32)
        for k in range(TOP_K):
            pg = pg + jnp.where(cid == posc_ref[t0:t0 + blk, k:k + 1],
                                gatec_ref[t0:t0 + blk, k:k + 1], 0.0)
        pgb = pg.astype(BF16)
        moe = jnp.concatenate([_dot(pgb, y_lo), _dot(pgb, y_hi)], axis=1)
        acc = h2_ref[t0:t0 + blk, :] + moe
        ms = jnp.mean(acc * acc, axis=-1, keepdims=True)
        out_ref[t0:t0 + blk, :] = acc * lax.rsqrt(ms + RMS_EPS) * ng_ref[...]


def _combine(ys, h2, pos_cols, gate_cols, slabs, norm_g):
    n_tok, d = h2.shape
    tw = DISPATCH_TILE
    n_steps = n_tok // tw
    return pl.pallas_call(
        functools.partial(_combine_kernel, n_steps=n_steps),
        grid_spec=pltpu.PrefetchScalarGridSpec(
            num_scalar_prefetch=4,
            grid=(n_steps,),
            in_specs=[
                pl.BlockSpec(memory_space=pl.ANY),
                pl.BlockSpec((tw, d), lambda i, *_: (i, 0)),
                pl.BlockSpec((tw, SUBLANES), lambda i, *_: (i, 0)),
                pl.BlockSpec((tw, SUBLANES), lambda i, *_: (i, 0)),
                pl.BlockSpec((1, d), lambda i, *_: (0, 0)),
            ],
            out_specs=pl.BlockSpec((tw, d), lambda i, *_: (i, 0)),
            scratch_shapes=[
                pltpu.VMEM((2, STAGE_ROWS, d // 2), U32),
                pltpu.SemaphoreType.DMA((2,)),
            ],
        ),
        out_shape=jax.ShapeDtypeStruct((n_tok, d), F32),
        compiler_params=pltpu.CompilerParams(
            dimension_semantics=("arbitrary",), vmem_limit_bytes=VMEM_LIMIT),
        name="combine",
    )(*slabs, ys, h2, pos_cols, gate_cols, norm_g.reshape(1, -1).astype(F32))


def _moe(h2, xn, eidx, gate, rank, tcnt, w_up, b_up, w_down, b_down, norm_final_g):
    n_tok, d = h2.shape
    tm = EXPERT_TILE
    tw = DISPATCH_TILE
    n_steps = n_tok // tw
    n_exp = N_EXPERTS
    e_ids = jnp.arange(n_exp, dtype=I32)
    al = SLAB_ALIGN
    tile_cnt = tcnt[:, :, 0].astype(I32)
    slab_rows = (tile_cnt + al - 1) // al * al
    exp_rows = jnp.sum(slab_rows, axis=0)
    region = (exp_rows + tm - 1) // tm * tm
    pad_end = jnp.cumsum(region)
    pad_start = pad_end - region
    n_slots = (n_tok * TOP_K + n_steps * n_exp * (al - 1) + n_exp * (tm - 1) + tm - 1) // tm * tm
    n_tiles = n_slots // tm

    tile_carry = jnp.cumsum(tile_cnt, axis=0) - tile_cnt
    tile_off = jnp.cumsum(slab_rows, axis=1) - slab_rows
    slab_row0 = pad_start[None, :] + jnp.cumsum(slab_rows, axis=0) - slab_rows

    def chunks(rows):
        return (rows // BIG_CHUNK).reshape(-1), (rows % BIG_CHUNK // al).reshape(-1)

    slabs = (*chunks(slab_rows), tile_off.reshape(-1), slab_row0.reshape(-1))
    tail_rows = jnp.concatenate([region - exp_rows, n_slots - pad_end[-1:]])
    tails = (*chunks(tail_rows), jnp.concatenate([pad_start + exp_rows, pad_end[-1:]]))

    base_t = jnp.repeat((tile_off - tile_carry).T, tw, axis=1)
    e_sel = eidx[:TOP_K][None] == e_ids[:, None, None]
    pos = jnp.sum(jnp.where(e_sel, base_t[:, None, :], 0), axis=0) + rank[:TOP_K]
    pos8 = jnp.concatenate([pos, jnp.full((SUBLANES - TOP_K, n_tok), -1, I32)], axis=0)

    tile_start = jnp.arange(n_tiles, dtype=I32) * tm
    tile_e = jnp.minimum(jnp.sum(pad_end[None, :] <= tile_start[:, None], axis=-1),
                         n_exp - 1).astype(I32)
    n_used = (pad_end[-1] // tm).astype(I32).reshape(1)

    xs = _dispatch(xn, pos8, slabs, tails, n_slots)
    ys = _experts(xs, tile_e, n_used, w_up, b_up, w_down, b_down)
    return _combine(ys, h2, pos8.T, gate.T, slabs, norm_final_g)


def kernel(x, norm_mix_g, w_in, s5_log_dt, s5_a_re, s5_a_im, s5_b_re, s5_b_im, s5_c_re, s5_c_im,
           s5_d, s5_w_glu, s5_b_glu, ml_conv_w, ml_conv_b, ml_b_gates, ml_norm_g, w_out,
           norm_ffn_g, w_router, b_router, w_up, b_up, w_down, b_down, norm_final_g):
    bsz, seq, d = x.shape
    depth = w_in.shape[0]
    assert depth == 1, "single-layer block"
    l = 0
    x2d = x.reshape(bsz * seq, d)
    qk, v, o, ut, gt = _in_proj(x2d, norm_mix_g[l], w_in[l], ml_b_gates[l], bsz, seq)
    g_t = _s5(ut, s5_log_dt[l], s5_a_re[l], s5_a_im[l], s5_b_re[l], s5_b_im[l],
              s5_c_re[l], s5_c_im[l], s5_d[l])
    g_nat = jnp.swapaxes(g_t, 1, 2).reshape(bsz * seq, -1)
    y_ml = _mlstm(qk, v, o, gt, ml_conv_w[l], ml_conv_b[l], ml_norm_g[l], bsz, seq)
    y_ml = y_ml.reshape(bsz * seq, -1)
    h2, xn, eidx, gate, rank, tcnt = _post(x2d, g_nat, y_ml, s5_w_glu[l], s5_b_glu[l], w_out[l],
                                           norm_ffn_g[l], w_router[l], b_router[l])
    out = _moe(h2, xn, eidx, gate, rank, tcnt, w_up[l], b_up[l], w_down[l], b_down[l],
               norm_final_g)
    return out.reshape(bsz, seq, d)
```

```python
import functools
import math

import jax
import jax.numpy as jnp
from jax import lax
from jax.experimental import pallas as pl
from jax.experimental.pallas import tpu as pltpu

F32 = jnp.float32
BF16 = jnp.bfloat16
I32 = jnp.int32
U32 = jnp.uint32

S5_GROUP = 16
S5_STATE = 64
ML_HEADS = 4
CONV_WIDTH = 4
N_EXPERTS = 32
TOP_K = 4
SWIGLU_LIMIT = 7.0
SWIGLU_ALPHA = 1.702
RMS_EPS = 1e-5
LN_EPS = 1e-6

LANES = 128
SUBLANES = 8
S5_CHUNK = LANES
ML_CHUNK = 128
PROJ_TILE = 512
EXPERT_TILE = 512
DISPATCH_TILE = PROJ_TILE
SLAB_ALIGN = SUBLANES
BIG_CHUNK = 32
STAGE_ROWS = DISPATCH_TILE * TOP_K + N_EXPERTS * SLAB_ALIGN
STAGE_BLOCK = STAGE_ROWS // 3
MAX_BIG = STAGE_ROWS // BIG_CHUNK
MAX_SMALL = N_EXPERTS * (BIG_CHUNK // SLAB_ALIGN - 1)
VMEM_LIMIT = 56 * 1024 * 1024

_NT = (((1,), (1,)), ((), ()))
_TN = (((0,), (0,)), ((), ()))


def _dot(a, b):
    return jnp.dot(a, b, preferred_element_type=F32)


def _dot_nt(a, b):
    return lax.dot_general(a, b, _NT, preferred_element_type=F32)


def _dot_tn(a, b):
    return lax.dot_general(a, b, _TN, preferred_element_type=F32)


def _split3(x):
    p1 = x.astype(BF16)
    r1 = x - p1.astype(F32)
    p2 = r1.astype(BF16)
    r2 = r1 - p2.astype(F32)
    return p1, p2, r2.astype(BF16)


def _sigmoid(x):
    return 1.0 / (1.0 + jnp.exp(-x))


def _inproj_kernel(x_ref, g_ref, wnat_ref, wut_ref, wgt_ref, bg_ref,
                   qk_ref, v_ref, o_ref, ut_ref, gt_ref, *, width):
    x = x_ref[...]
    ms = jnp.mean(x * x, axis=-1, keepdims=True)
    hn = (x * lax.rsqrt(ms + RMS_EPS) * g_ref[...]).astype(BF16)
    nat = _dot(hn, wnat_ref[...])
    qk_ref[...] = nat[:, :2 * width].astype(BF16)
    v_ref[...] = nat[:, 2 * width:3 * width].astype(BF16)
    o_ref[...] = nat[:, 3 * width:].astype(BF16)
    ut_ref[0] = _dot_nt(wut_ref[...], hn).astype(BF16)
    gt_ref[0] = _dot_nt(wgt_ref[...], hn) + bg_ref[...]


def _in_proj(x2d, norm_g, w_in, b_gates, bsz, seq):
    n_tok, d = x2d.shape
    s5w = d // 2
    mlw = d - s5w
    tm = min(PROJ_TILE, seq)
    tpb = seq // tm
    w_bf = w_in.astype(BF16)
    w_nat = w_bf[:, s5w:s5w + 4 * mlw]
    w_ut = w_bf[:, :s5w].T
    n_gate = 2 * ML_HEADS
    w_gt = jnp.zeros((16, d), BF16).at[:n_gate].set(w_bf[:, s5w + 4 * mlw:].T)
    b_g = jnp.zeros((16, 1), F32).at[:n_gate, 0].set(b_gates.astype(F32))
    grid = (n_tok // tm,)
    full = lambda i: (0, 0)
    return pl.pallas_call(
        functools.partial(_inproj_kernel, width=mlw),
        grid=grid,
        in_specs=[
            pl.BlockSpec((tm, d), lambda i: (i, 0)),
            pl.BlockSpec((1, d), full),
            pl.BlockSpec((d, 4 * mlw), full),
            pl.BlockSpec((s5w, d), full),
            pl.BlockSpec((16, d), full),
            pl.BlockSpec((16, 1), full),
        ],
        out_specs=[
            pl.BlockSpec((tm, 2 * mlw), lambda i: (i, 0)),
            pl.BlockSpec((tm, mlw), lambda i: (i, 0)),
            pl.BlockSpec((tm, mlw), lambda i: (i, 0)),
            pl.BlockSpec((1, s5w, tm), lambda i: (i // tpb, 0, i % tpb)),
            pl.BlockSpec((1, 16, tm), lambda i: (i // tpb, 0, i % tpb)),
        ],
        out_shape=[
            jax.ShapeDtypeStruct((n_tok, 2 * mlw), BF16),
            jax.ShapeDtypeStruct((n_tok, mlw), BF16),
            jax.ShapeDtypeStruct((n_tok, mlw), BF16),
            jax.ShapeDtypeStruct((bsz, s5w, seq), BF16),
            jax.ShapeDtypeStruct((bsz, 16, seq), F32),
        ],
        compiler_params=pltpu.CompilerParams(
            dimension_semantics=("parallel",), vmem_limit_bytes=VMEM_LIMIT),
        name="in_proj",
    )(x2d, norm_g.reshape(1, d).astype(F32), w_nat, w_ut, w_gt, b_g)


def _s5_kernel(d_ref, x_ref, prm_ref, c_ref, bt_ref, out_ref, kmat_scr, rhs_scr, *, bsz, nc):
    L = S5_CHUNK
    P = S5_GROUP
    N = S5_STATE
    grp = pl.program_id(0)
    lane = lax.broadcasted_iota(I32, (1, 2 * N), 1)
    lo = lane < N
    a_re = jnp.minimum(prm_ref[0, 0:1, :], -1e-4)
    a_im = prm_ref[0, 1:2, :]
    dt = jnp.exp(prm_ref[0, 2:3, :])
    zr = dt * a_re
    zi = dt * a_im

    def powtab(e, swap=False):
        ang = e * zi
        pick_cos = (lane >= N) if swap else lo
        return jnp.exp(e * zr) * jnp.where(pick_cos, jnp.cos(ang), jnp.sin(ang))

    er = jnp.exp(zr) * jnp.cos(zi)
    ei = jnp.exp(zr) * jnp.sin(zi)
    den = a_re * a_re + a_im * a_im
    coef_r = ((er - 1.0) * a_re + ei * a_im) / den
    coef_i = (ei * a_re - (er - 1.0) * a_im) / den
    c_r = c_ref[0, 0]
    c_i = c_ref[0, 1]
    bb_r = coef_r * bt_ref[0, 0] - coef_i * bt_ref[0, 1]
    bb_i = coef_r * bt_ref[0, 1] + coef_i * bt_ref[0, 0]

    tau = lax.broadcasted_iota(I32, (L, 1), 0).astype(F32)

    cb_rows = []
    for q in range(P):
        cbr = c_r * bb_r[q:q + 1] - c_i * bb_i[q:q + 1]
        cbi = c_r * bb_i[q:q + 1] + c_i * bb_r[q:q + 1]
        cb_rows.append(jnp.where(lo, cbr, -cbi))
    cb = jnp.concatenate(cb_rows, axis=0)
    pt0 = powtab(tau)
    c1, c2, c3 = _split3(cb)
    t1, t2, t3 = _split3(pt0)
    kmat_scr[...] = (_dot_nt(c1, t1) + _dot_nt(c1, t2) + _dot_nt(c2, t1)
                     + _dot_nt(c2, t2) + _dot_nt(c1, t3) + _dot_nt(c3, t1))

    rr = lax.broadcasted_iota(I32, (L, L), 0)
    cc = lax.broadcasted_iota(I32, (L, L), 1)
    causal = cc >= rr

    def build_q(q, carry):
        for p in range(P):
            krow = kmat_scr[pl.ds(q * P + p, 1), :]
            kb = jnp.broadcast_to(krow, (L, L))
            toe = pltpu.roll(kb, 0, 1, stride=1, stride_axis=0)
            rhs_scr[pl.ds(pl.multiple_of(q * L, L), L), p * L:(p + 1) * L] = (
                jnp.where(causal, toe, 0.0).astype(BF16))
        return carry

    lax.fori_loop(0, P, build_q, 0)

    pt_rev = powtab(L - 1.0 - tau)
    pt_rev_sw = powtab(L - 1.0 - tau, swap=True)
    f_rows = []
    for q in range(P):
        a1 = bb_r[q:q + 1]
        a2 = jnp.where(lo, -bb_i[q:q + 1], bb_i[q:q + 1])
        f_rows.append((a1 * pt_rev + a2 * pt_rev_sw).astype(BF16))
    fmat = jnp.concatenate(f_rows, axis=0)
    pt1 = powtab(tau + 1.0)
    pt1_sw = powtab(tau + 1.0, swap=True)
    e_rows = []
    for p in range(P):
        b1 = jnp.where(lo, c_r[p:p + 1], -c_r[p:p + 1])
        b2 = -c_i[p:p + 1]
        e_rows.append((b1 * pt1 + b2 * pt1_sw).astype(BF16))
    emat_t = jnp.concatenate(e_rows, axis=0)

    lhs = jnp.concatenate(
        [jnp.concatenate([x_ref[b, q] for q in range(P)], axis=1) for b in range(bsz)], axis=0)
    y = _dot(lhs, rhs_scr[...])
    s_end = _dot(lhs, fmat)

    m_rows = bsz * nc
    ridx = lax.broadcasted_iota(I32, (m_rows, 1), 0)
    cidx = ridx % nc
    h = jnp.where(cidx >= 1, pltpu.roll(s_end, 1, 0), 0.0)
    d = 1
    while d < nc:
        e = float(d * L)
        pr = jnp.exp(e * zr) * jnp.cos(e * zi)
        pi = jnp.exp(e * zr) * jnp.sin(e * zi)
        sh = jnp.where(cidx >= d, pltpu.roll(h, d, 0), 0.0)
        h = h + pr * sh + jnp.where(lo, -pi, pi) * pltpu.roll(sh, N, 1)
        d *= 2
    y = y + _dot_nt(h.astype(BF16), emat_t)

    for b in range(bsz):
        for p in range(P):
            yp = (y[b * nc:(b + 1) * nc, p * L:(p + 1) * L]
                  + d_ref[grp * P + p] * x_ref[b, p].astype(F32))
            out_ref[b, p] = jax.nn.gelu(yp).astype(BF16)


def _s5(ut, log_dt, a_re, a_im, b_re, b_im, c_re, c_im, d_skip):
    bsz, s5w, seq = ut.shape
    L = S5_CHUNK
    nc = seq // L
    groups = s5w // S5_GROUP
    n = S5_STATE
    x4 = ut.reshape(bsz, s5w, nc, L)
    dup = lambda t: jnp.concatenate([t, t], axis=-1).astype(F32)
    prm = jnp.zeros((groups, SUBLANES, 2 * n), F32)
    prm = prm.at[:, 0].set(dup(a_re)).at[:, 1].set(dup(a_im))
    prm = prm.at[:, 2].set(jnp.broadcast_to(log_dt.astype(F32)[:, None], (groups, 2 * n)))
    cpar = jnp.stack([dup(c_re), dup(c_im)], axis=1)
    btpar = jnp.stack([dup(jnp.swapaxes(b_re, 1, 2)), dup(jnp.swapaxes(b_im, 1, 2))], axis=1)
    out = pl.pallas_call(
        functools.partial(_s5_kernel, bsz=bsz, nc=nc),
        grid_spec=pltpu.PrefetchScalarGridSpec(
            num_scalar_prefetch=1,
            grid=(groups,),
            in_specs=[
                pl.BlockSpec((bsz, S5_GROUP, nc, L), lambda g, d: (0, g, 0, 0)),
                pl.BlockSpec((1, SUBLANES, 2 * n), lambda g, d: (g, 0, 0)),
                pl.BlockSpec((1, 2, S5_GROUP, 2 * n), lambda g, d: (g, 0, 0, 0)),
                pl.BlockSpec((1, 2, S5_GROUP, 2 * n), lambda g, d: (g, 0, 0, 0)),
            ],
            out_specs=pl.BlockSpec((bsz, S5_GROUP, nc, L), lambda g, d: (0, g, 0, 0)),
            scratch_shapes=[
                pltpu.VMEM((S5_GROUP * S5_GROUP, L), F32),
                pltpu.VMEM((S5_GROUP * L, S5_GROUP * L), BF16),
            ],
        ),
        out_shape=jax.ShapeDtypeStruct((bsz, s5w, nc, L), BF16),
        compiler_params=pltpu.CompilerParams(
            dimension_semantics=("parallel",), vmem_limit_bytes=VMEM_LIMIT),
        name="s5",
    )(d_skip.astype(F32), x4, prm, cpar, btpar)
    return out.reshape(bsz, s5w, seq)


def _log_sigmoid(x):
    return jnp.minimum(x, 0.0) - jnp.log(1.0 + jnp.exp(-jnp.abs(x)))


def _mlstm_kernel(qk_ref, v_ref, o_ref, gt_ref, cw_ref, cb_ref, ng_ref, y_ref,
                  tail_scr, c_scr, m_scr, *, bsz, chunk, width):
    L = chunk
    H = ML_HEADS
    dh = width // H
    step = pl.program_id(0)

    @pl.when(step == 0)
    def _():
        tail_scr[...] = jnp.zeros_like(tail_scr)
        c_scr[...] = jnp.zeros_like(c_scr)
        m_scr[...] = jnp.zeros_like(m_scr)

    assert L == LANES and dh == LANES, "column replication below uses one 128x128 tile per head"
    rr = lax.broadcasted_iota(I32, (L, L), 0)
    cc = lax.broadcasted_iota(I32, (L, L), 1)
    causal = cc <= rr
    ltri = causal.astype(BF16)
    utri = (rr <= cc).astype(BF16)
    eye = (rr == cc).astype(BF16)
    ones_blk = jnp.ones((L, dh), BF16)
    grow = lax.broadcasted_iota(I32, (16, 1), 0)
    lane_row = lax.broadcasted_iota(I32, (SUBLANES, L), 1)
    scale = 1.0 / math.sqrt(dh)
    inv_dh = 1.0 / dh

    def replicate(mat, rows):
        p = jnp.concatenate([jnp.broadcast_to(r, (LANES, L)) for r in rows], axis=0)
        p1 = p.astype(BF16)
        p2 = (p - p1.astype(F32)).astype(BF16)
        return _dot_nt(mat, p1) + _dot_nt(mat, p2)

    def rowsum(x):
        hi = x.astype(BF16)
        lo = (x - hi.astype(F32)).astype(BF16)
        return _dot(hi, ones_blk) + _dot(lo, ones_blk)

    n_str = bsz * H
    streams = [(b, hd) for b in range(bsz) for hd in range(H)]
    tails = [tail_scr[b] for b in range(bsz)]
    caugs = [c_scr[i] for i in range(n_str)]
    m_prevs = [m_scr[i][0:1, 0:1] for i in range(n_str)]

    q_all, k_all, new_tails, i_rows, lf_rows, b_rows = [], [], [], [], [], []
    for b in range(bsz):
        xqk = qk_ref[b].astype(F32)
        ext = jnp.concatenate([tails[b], xqk], axis=0)
        acc = jnp.broadcast_to(cb_ref[...], xqk.shape)
        for j in range(CONV_WIDTH):
            back = CONV_WIDTH - 1 - j
            sh = ext if back == 0 else pltpu.roll(ext, back, 0)
            acc = acc + cw_ref[j:j + 1, :] * sh[SUBLANES:]
        new_tails.append(xqk[L - SUBLANES:])
        qkc = acc * _sigmoid(acc)
        q_all.append(qkc[:, :width].astype(BF16))
        k_all.append((qkc[:, width:] * scale).astype(BF16))
        g = gt_ref[b]
        g2 = jnp.where(grow >= H, _log_sigmoid(g), g)
        p1, p2, p3 = _split3(g2)
        brow = _dot(p1, utri) + _dot(p2, utri) + _dot(p3, utri)
        i_rows.append(g[0:H])
        lf_rows.append(g2[H:2 * H])
        b_rows.append(brow[H:2 * H])

    cm_rows = []
    for b in range(bsz):
        x = i_rows[b] - b_rows[b]
        x = jnp.concatenate([x, x], axis=0)
        sft = 1
        while sft < L:
            x = jnp.maximum(x, jnp.where(lane_row >= sft, pltpu.roll(x, sft, 1), -jnp.inf))
            sft *= 2
        cm_rows.append(x[0:H])

    bc_rep = [replicate(ltri, [lf_rows[b][h:h + 1] for h in range(H)]) for b in range(bsz)]
    ic_rep = [replicate(eye, [i_rows[b][h:h + 1] for h in range(H)]) for b in range(bsz)]
    cm_rep = [replicate(eye, [cm_rows[b][h:h + 1] for h in range(H)]) for b in range(bsz)]

    q_h = [q_all[b][:, hd * dh:(hd + 1) * dh] for b, hd in streams]
    k_h = [k_all[b][:, hd * dh:(hd + 1) * dh] for b, hd in streams]
    vaug = [jnp.concatenate([v_ref[b][:, hd * dh:(hd + 1) * dh], ones_blk], axis=1)
            for b, hd in streams]
    scores = [_dot_nt(q_h[i], k_h[i]) for i in range(n_str)]
    inter = [_dot(q_h[i], caugs[i].astype(BF16)) for i in range(n_str)]

    m_reps, w_inter, s_bf, kw, decays, new_ms = [], [], [], [], [], []
    for i, (b, hd) in enumerate(streams):
        sl = slice(hd * LANES, (hd + 1) * LANES)
        b_r = b_rows[b][hd:hd + 1]
        i_r = i_rows[b][hd:hd + 1]
        b_last = b_r[:, L - 1:L]
        bc = bc_rep[b][:, sl]
        m_prev = m_prevs[i]
        m = bc + jnp.maximum(cm_rep[b][:, sl], m_prev)
        m_reps.append(m)
        w_inter.append(jnp.exp(bc + m_prev - m))
        log_d = jnp.where(causal, bc - b_r + i_r, -jnp.inf)
        s_bf.append((scores[i] * jnp.exp(log_d - m)).astype(BF16))
        m_next = jnp.maximum(b_last + m_prev, b_last + cm_rows[b][hd:hd + 1, L - 1:L])
        decays.append(jnp.exp(b_last + m_prev - m_next))
        ws = jnp.exp(b_last - bc + ic_rep[b][:, sl] - m_next)
        kw.append((k_h[i].astype(F32) * ws).astype(BF16))
        new_ms.append(jnp.broadcast_to(m_next, (SUBLANES, LANES)))

    intra = [_dot(s_bf[i], vaug[i]) for i in range(n_str)]
    upd = [_dot_tn(kw[i], vaug[i]) for i in range(n_str)]

    hh = []
    for i in range(n_str):
        num = w_inter[i] * inter[i][:, :dh] + intra[i][:, :dh]
        den = w_inter[i] * inter[i][:, dh:] + intra[i][:, dh:]
        hh.append(num / jnp.maximum(jnp.abs(den), jnp.exp(-m_reps[i])))
    mu = [rowsum(h) * inv_dh for h in hh]
    ctr = [hh[i] - mu[i] for i in range(n_str)]
    var = [rowsum(c * c) * inv_dh for c in ctr]
    outs = []
    for i, (b, hd) in enumerate(streams):
        sl = slice(hd * dh, (hd + 1) * dh)
        hn = ctr[i] * lax.rsqrt(var[i] + LN_EPS) * ng_ref[:, sl]
        outs.append(hn * _sigmoid(o_ref[b][:, sl].astype(F32)))

    for b in range(bsz):
        tail_scr[b] = new_tails[b]
        y_ref[b] = jnp.concatenate(outs[b * H:(b + 1) * H], axis=1).astype(BF16)
    for i in range(n_str):
        c_scr[i] = decays[i] * caugs[i] + upd[i]
        m_scr[i] = new_ms[i]


def _mlstm(qk, v, o, gt, conv_w, conv_b, norm_g, bsz, seq):
    width = v.shape[-1]
    L = min(ML_CHUNK, seq)
    dh = width // ML_HEADS
    qk3 = qk.reshape(bsz, seq, 2 * width)
    v3 = v.reshape(bsz, seq, width)
    o3 = o.reshape(bsz, seq, width)
    full = lambda c: (0, 0)
    return pl.pallas_call(
        functools.partial(_mlstm_kernel, bsz=bsz, chunk=L, width=width),
        grid=(seq // L,),
        in_specs=[
            pl.BlockSpec((bsz, L, 2 * width), lambda c: (0, c, 0)),
            pl.BlockSpec((bsz, L, width), lambda c: (0, c, 0)),
            pl.BlockSpec((bsz, L, width), lambda c: (0, c, 0)),
            pl.BlockSpec((bsz, 16, L), lambda c: (0, 0, c)),
            pl.BlockSpec((CONV_WIDTH, 2 * width), full),
            pl.BlockSpec((1, 2 * width), full),
            pl.BlockSpec((1, width), full),
        ],
        out_specs=pl.BlockSpec((bsz, L, width), lambda c: (0, c, 0)),
        out_shape=jax.ShapeDtypeStruct((bsz, seq, width), BF16),
        scratch_shapes=[
            pltpu.VMEM((bsz, SUBLANES, 2 * width), F32),
            pltpu.VMEM((bsz * ML_HEADS, dh, 2 * dh), F32),
            pltpu.VMEM((bsz * ML_HEADS, SUBLANES, LANES), F32),
        ],
        compiler_params=pltpu.CompilerParams(
            dimension_semantics=("arbitrary",), vmem_limit_bytes=VMEM_LIMIT),
        name="mlstm",
    )(qk3, v3, o3, gt, conv_w.astype(F32), conv_b.reshape(1, -1).astype(F32),
      norm_g.reshape(1, -1).astype(F32))


def _post_kernel(x_ref, g_ref, yml_ref, wglu_ref, bglu_ref, wout_ref, nffn_ref, wrt_ref, br_ref,
                 h2_ref, xn_ref, eidx_ref, gate_ref, rank_ref, tcnt_ref, carry_scr):
    step = pl.program_id(0)

    @pl.when(step == 0)
    def _():
        carry_scr[...] = jnp.zeros_like(carry_scr)

    g = g_ref[...]
    z = _dot(g, wglu_ref[...]) + bglu_ref[...]
    s5 = g.astype(F32) * _sigmoid(z)
    mix = jnp.concatenate([s5.astype(BF16), yml_ref[...]], axis=1)
    h2 = x_ref[...] + _dot(mix, wout_ref[...])
    h2_ref[...] = h2
    ms = jnp.mean(h2 * h2, axis=-1, keepdims=True)
    xn = h2 * lax.rsqrt(ms + RMS_EPS) * nffn_ref[...]
    xb = xn.astype(BF16)
    xn_ref[...] = xb

    tm = xn.shape[0]
    logits = _dot_nt(wrt_ref[...], xb) + br_ref[...]
    eio = lax.broadcasted_iota(I32, (N_EXPERTS, tm), 0).astype(F32)
    vals = logits
    onehot = jnp.zeros((N_EXPERTS, tm), F32)
    idxs, tops = [], []
    for _ in range(TOP_K):
        mx = jnp.max(vals, axis=0, keepdims=True)
        idx = jnp.min(jnp.where(vals == mx, eio, float(N_EXPERTS)), axis=0, keepdims=True)
        sel = eio == idx
        onehot = onehot + sel.astype(F32)
        vals = jnp.where(sel, -jnp.inf, vals)
        idxs.append(idx)
        tops.append(mx)
    exps = [jnp.exp(t - tops[0]) for t in tops]
    tot = exps[0] + exps[1] + exps[2] + exps[3]
    pad_f = jnp.zeros((SUBLANES - TOP_K, tm), F32)
    eidx_ref[...] = jnp.concatenate(idxs + [pad_f], axis=0).astype(I32)
    gate_ref[...] = jnp.concatenate([e / tot for e in exps] + [pad_f], axis=0)

    rr = lax.broadcasted_iota(I32, (tm, tm), 0)
    cc = lax.broadcasted_iota(I32, (tm, tm), 1)
    before = (rr < cc).astype(BF16)
    carry = carry_scr[:, 0:1]
    rank_ex = _dot(onehot.astype(BF16), before) + carry
    ranks = [jnp.sum(jnp.where(eio == i, rank_ex, 0.0), axis=0, keepdims=True) for i in idxs]
    rank_ref[...] = jnp.concatenate(ranks + [pad_f], axis=0).astype(I32)
    tile_cnt = jnp.sum(onehot, axis=1, keepdims=True)
    carry_scr[...] = jnp.broadcast_to(carry + tile_cnt, carry_scr.shape)
    tcnt_ref[0] = jnp.broadcast_to(tile_cnt, carry_scr.shape)


def _post(x2d, g_nat, y_ml, w_glu, b_glu, w_out, norm_g, w_router, b_router):
    n_tok, d = x2d.shape
    s5w = g_nat.shape[-1]
    mlw = y_ml.shape[-1]
    tm = min(PROJ_TILE, n_tok)
    full = lambda i: (0, 0)
    row = lambda i: (i, 0)
    colb = lambda i: (0, i)
    return pl.pallas_call(
        _post_kernel,
        grid=(n_tok // tm,),
        in_specs=[
            pl.BlockSpec((tm, d), row),
            pl.BlockSpec((tm, s5w), row),
            pl.BlockSpec((tm, mlw), row),
            pl.BlockSpec((s5w, s5w), full),
            pl.BlockSpec((1, s5w), full),
            pl.BlockSpec((d, d), full),
            pl.BlockSpec((1, d), full),
            pl.BlockSpec((N_EXPERTS, d), full),
            pl.BlockSpec((N_EXPERTS, 1), full),
        ],
        out_specs=[
            pl.BlockSpec((tm, d), row),
            pl.BlockSpec((tm, d), row),
            pl.BlockSpec((SUBLANES, tm), colb),
            pl.BlockSpec((SUBLANES, tm), colb),
            pl.BlockSpec((SUBLANES, tm), colb),
            pl.BlockSpec((1, N_EXPERTS, LANES), lambda i: (i, 0, 0)),
        ],
        out_shape=[
            jax.ShapeDtypeStruct((n_tok, d), F32),
            jax.ShapeDtypeStruct((n_tok, d), BF16),
            jax.ShapeDtypeStruct((SUBLANES, n_tok), I32),
            jax.ShapeDtypeStruct((SUBLANES, n_tok), F32),
            jax.ShapeDtypeStruct((SUBLANES, n_tok), I32),
            jax.ShapeDtypeStruct((n_tok // tm, N_EXPERTS, LANES), F32),
        ],
        scratch_shapes=[pltpu.VMEM((N_EXPERTS, LANES), F32)],
        compiler_params=pltpu.CompilerParams(
            dimension_semantics=("arbitrary",), vmem_limit_bytes=VMEM_LIMIT),
        name="post_router",
    )(x2d, g_nat, y_ml, w_glu.astype(BF16), b_glu.reshape(1, -1).astype(F32), w_out.astype(BF16),
      norm_g.reshape(1, -1).astype(F32), w_router.T.astype(BF16),
      b_router.reshape(-1, 1).astype(F32))


def _slab_loop(base, nbig_ref, nsmall_ref, fn, n_slabs=N_EXPERTS):
    def per_expert(e, carry):
        idx = base + e
        nbig = nbig_ref[idx]

        def big(j, c2):
            fn(idx, j * BIG_CHUNK, BIG_CHUNK)
            return c2

        def small(j, c2):
            fn(idx, nbig * BIG_CHUNK + j * SLAB_ALIGN, SLAB_ALIGN)
            return c2

        lax.fori_loop(0, nbig, big, 0)
        lax.fori_loop(0, nsmall_ref[idx], small, 0)
        return carry

    lax.fori_loop(0, n_slabs, per_expert, 0)


def _plan_copies(step, plan, fn):
    nbig_ref, nsmall_ref, _, bsrc_ref, bdst_ref, ssrc_ref, sdst_ref = plan

    def big(j, carry):
        k = step * MAX_BIG + j
        fn(bsrc_ref[k], bdst_ref[k], BIG_CHUNK)
        return carry

    def small(j, carry):
        k = step * MAX_SMALL + j
        fn(ssrc_ref[k], sdst_ref[k], SLAB_ALIGN)
        return carry

    lax.fori_loop(0, nbig_ref[step], big, 0)
    lax.fori_loop(0, nsmall_ref[step], small, 0)


def _plan_wait(step, plan, copy):
    copy(0, 0, DISPATCH_TILE * TOP_K).wait()

    def one(j, carry):
        copy(0, 0, SLAB_ALIGN).wait()
        return carry

    lax.fori_loop(0, plan[2][step], one, 0)


def _unpack_pairs(words):
    lo = lax.bitcast_convert_type(words << 16, F32).astype(BF16)
    hi = lax.bitcast_convert_type(words & jnp.uint32(0xFFFF0000), F32).astype(BF16)
    return lo, hi


def _pack_pairs(lo_f32, hi_f32):
    lo = lax.bitcast_convert_type(lo_f32, U32) >> 16
    hi = lax.bitcast_convert_type(hi_f32, U32) & jnp.uint32(0xFFFF0000)
    return hi | lo


def _dispatch_kernel(*refs, n_steps):
    plan = refs[:7]
    znbig_ref, znsmall_ref, zrow_ref, xn_ref, pos_ref, xs_hbm, stage, zbuf, sem, zsem = refs[7:]
    i = pl.program_id(0)
    slot = i % 2
    tw, d = xn_ref.shape
    n_rows = stage.shape[1]
    half = d // 2

    def copy(s, stage_row, buf_row, rows):
        src = stage.at[s, pl.ds(pl.multiple_of(stage_row, SLAB_ALIGN), rows), :]
        dst = xs_hbm.at[pl.ds(pl.multiple_of(buf_row, SLAB_ALIGN), rows), :]
        return pltpu.make_async_copy(src, dst, sem)

    @pl.when(i == 0)
    def _():
        zbuf[...] = jnp.zeros_like(zbuf)

        def tail(idx, r0, rows):
            dst = xs_hbm.at[pl.ds(pl.multiple_of(zrow_ref[idx] + r0, SLAB_ALIGN), rows), :]
            return pltpu.make_async_copy(zbuf.at[pl.ds(0, rows), :], dst, zsem)

        _slab_loop(0, znbig_ref, znsmall_ref, lambda *a: tail(*a).start(), N_EXPERTS + 1)
        _slab_loop(0, znbig_ref, znsmall_ref, lambda *a: tail(*a).wait(), N_EXPERTS + 1)

    xt = xn_ref[...]
    blk = STAGE_BLOCK
    for r0 in range(0, n_rows, blk):
        rid = lax.broadcasted_iota(I32, (blk, tw), 0) + r0
        hit = rid == pos_ref[0:1, :]
        for k in range(1, TOP_K):
            hit = jnp.logical_or(hit, rid == pos_ref[k:k + 1, :])
        srt = _dot(jnp.where(hit, 1.0, 0.0).astype(BF16), xt)
        stage[slot, r0:r0 + blk, :] = _pack_pairs(srt[:, :half], srt[:, half:])

    @pl.when(i > 0)
    def _():
        _plan_wait(i - 1, plan, functools.partial(copy, 1 - slot))

    _plan_copies(i, plan, lambda *a: copy(slot, *a).start())

    @pl.when(i == n_steps - 1)
    def _():
        _plan_wait(i, plan, functools.partial(copy, slot))


def _dispatch(xn, pos8, plan, tails, n_slots):
    n_tok, d = xn.shape
    tw = DISPATCH_TILE
    n_steps = n_tok // tw
    return pl.pallas_call(
        functools.partial(_dispatch_kernel, n_steps=n_steps),
        grid_spec=pltpu.PrefetchScalarGridSpec(
            num_scalar_prefetch=10,
            grid=(n_steps,),
            in_specs=[
                pl.BlockSpec((tw, d), lambda i, *_: (i, 0)),
                pl.BlockSpec((SUBLANES, tw), lambda i, *_: (0, i)),
            ],
            out_specs=pl.BlockSpec(memory_space=pl.ANY),
            scratch_shapes=[
                pltpu.VMEM((2, STAGE_ROWS, d // 2), U32),
                pltpu.VMEM((BIG_CHUNK, d // 2), U32),
                pltpu.SemaphoreType.DMA(()),
                pltpu.SemaphoreType.DMA(()),
            ],
        ),
        out_shape=jax.ShapeDtypeStruct((n_slots, d // 2), U32),
        compiler_params=pltpu.CompilerParams(
            dimension_semantics=("arbitrary",), vmem_limit_bytes=VMEM_LIMIT),
        name="dispatch",
    )(*plan, *tails, xn, pos8)


def _expert_kernel(te_ref, nu_ref, x_ref, wup_ref, bup_ref, wdn_ref, bdn_ref, y_ref, *, d_ff):
    i = pl.program_id(0)
    n_used = nu_ref[0]

    @pl.when(i < n_used)
    def _():
        lo, hi = _unpack_pairs(x_ref[...])
        half = lo.shape[1]
        h = (_dot(lo, wup_ref[0, :half, :].astype(BF16))
             + _dot(hi, wup_ref[0, half:, :].astype(BF16)) + bup_ref[0])
        gl = jnp.minimum(h[:, :d_ff], SWIGLU_LIMIT)
        lin = jnp.clip(h[:, d_ff:], -SWIGLU_LIMIT, SWIGLU_LIMIT)
        act = gl * _sigmoid(SWIGLU_ALPHA * gl) * (lin + 1.0)
        y = _dot(act.astype(BF16), wdn_ref[0].astype(BF16)) + bdn_ref[0]
        yb = y.astype(BF16).astype(F32)
        y_ref[...] = _pack_pairs(yb[:, :half], yb[:, half:])

    @pl.when(i >= n_used)
    def _():
        y_ref[...] = jnp.zeros_like(y_ref)


def _experts(xs, tile_e, n_used, w_up, b_up, w_down, b_down):
    n_slots, half = xs.shape
    d = 2 * half
    tm = EXPERT_TILE
    n_tiles = n_slots // tm
    d_ff = w_down.shape[1]
    return pl.pallas_call(
        functools.partial(_expert_kernel, d_ff=d_ff),
        grid_spec=pltpu.PrefetchScalarGridSpec(
            num_scalar_prefetch=2,
            grid=(n_tiles,),
            in_specs=[
                pl.BlockSpec((tm, half), lambda i, te, nu: (jnp.minimum(i, nu[0] - 1), 0)),
                pl.BlockSpec((1, d, 2 * d_ff), lambda i, te, nu: (te[i], 0, 0)),
                pl.BlockSpec((1, 1, 2 * d_ff), lambda i, te, nu: (te[i], 0, 0)),
                pl.BlockSpec((1, d_ff, d), lambda i, te, nu: (te[i], 0, 0)),
                pl.BlockSpec((1, 1, d), lambda i, te, nu: (te[i], 0, 0)),
            ],
            out_specs=pl.BlockSpec((tm, half), lambda i, te, nu: (i, 0)),
        ),
        out_shape=jax.ShapeDtypeStruct((n_slots, half), U32),
        compiler_params=pltpu.CompilerParams(
            dimension_semantics=("arbitrary",), vmem_limit_bytes=VMEM_LIMIT),
        name="experts",
    )(tile_e, n_used, xs, w_up, b_up.reshape(N_EXPERTS, 1, -1),
      w_down, b_down.reshape(N_EXPERTS, 1, -1))


def _combine_kernel(*refs, n_steps):
    plan = refs[:7]
    ys_hbm, h2_ref, posc_ref, gatec_ref, ng_ref, out_ref, stage, sem = refs[7:]
    i = pl.program_id(0)
    slot = i % 2
    tw, d = h2_ref.shape
    n_rows = stage.shape[1]

    def copy(s, stage_row, buf_row, rows):
        src = ys_hbm.at[pl.ds(pl.multiple_of(buf_row, SLAB_ALIGN), rows), :]
        dst = stage.at[s, pl.ds(pl.multiple_of(stage_row, SLAB_ALIGN), rows), :]
        return pltpu.make_async_copy(src, dst, sem.at[s])

    @pl.when(i == 0)
    def _():
        stage[...] = jnp.zeros_like(stage)
        _plan_copies(0, plan, lambda *a: copy(0, *a).start())

    @pl.when(i + 1 < n_steps)
    def _():
        _plan_copies(i + 1, plan, lambda *a: copy(1 - slot, *a).start())

    _plan_wait(i, plan, functools.partial(copy, slot))

    y_lo, y_hi = _unpack_pairs(stage[slot])
    blk = 128
    for t0 in range(0, tw, blk):
        cid = lax.broadcasted_iota(I32, (blk, n_rows), 1)
        pg = jnp.zeros((blk, n_rows), F32)
        for k in range(TOP_K):
            pg = jnp.where(cid == posc_ref[t0:t0 + blk, k:k + 1],
                           gatec_ref[t0:t0 + blk, k:k + 1], pg)
        pgb = pg.astype(BF16)
        moe = jnp.concatenate([_dot(pgb, y_lo), _dot(pgb, y_hi)], axis=1)
        acc = h2_ref[t0:t0 + blk, :] + moe
        ms = jnp.mean(acc * acc, axis=-1, keepdims=True)
        out_ref[t0:t0 + blk, :] = acc * lax.rsqrt(ms + RMS_EPS) * ng_ref[...]


def _combine(ys, h2, pos_cols, gate_cols, plan, norm_g):
    n_tok, d = h2.shape
    tw = DISPATCH_TILE
    n_steps = n_tok // tw
    return pl.pallas_call(
        functools.partial(_combine_kernel, n_steps=n_steps),
        grid_spec=pltpu.PrefetchScalarGridSpec(
            num_scalar_prefetch=7,
            grid=(n_steps,),
            in_specs=[
                pl.BlockSpec(memory_space=pl.ANY),
                pl.BlockSpec((tw, d), lambda i, *_: (i, 0)),
                pl.BlockSpec((tw, SUBLANES), lambda i, *_: (i, 0)),
                pl.BlockSpec((tw, SUBLANES), lambda i, *_: (i, 0)),
                pl.BlockSpec((1, d), lambda i, *_: (0, 0)),
            ],
            out_specs=pl.BlockSpec((tw, d), lambda i, *_: (i, 0)),
            scratch_shapes=[
                pltpu.VMEM((2, STAGE_ROWS, d // 2), U32),
                pltpu.SemaphoreType.DMA((2,)),
            ],
        ),
        out_shape=jax.ShapeDtypeStruct((n_tok, d), F32),
        compiler_params=pltpu.CompilerParams(
            dimension_semantics=("arbitrary",), vmem_limit_bytes=VMEM_LIMIT),
        name="combine",
    )(*plan, ys, h2, pos_cols, gate_cols, norm_g.reshape(1, -1).astype(F32))


def _moe(h2, xn, eidx, gate, rank, tcnt, w_up, b_up, w_down, b_down, norm_final_g):
    n_tok, d = h2.shape
    tm = EXPERT_TILE
    tw = DISPATCH_TILE
    n_steps = n_tok // tw
    n_exp = N_EXPERTS
    e_ids = jnp.arange(n_exp, dtype=I32)
    al = SLAB_ALIGN
    tile_cnt = tcnt[:, :, 0].astype(I32)
    slab_rows = (tile_cnt + al - 1) // al * al
    exp_rows = jnp.sum(slab_rows, axis=0)
    region = (exp_rows + tm - 1) // tm * tm
    pad_end = jnp.cumsum(region)
    pad_start = pad_end - region
    n_slots = (n_tok * TOP_K + n_steps * n_exp * (al - 1) + n_exp * (tm - 1) + tm - 1) // tm * tm
    n_tiles = n_slots // tm

    tile_carry = jnp.cumsum(tile_cnt, axis=0) - tile_cnt
    tile_off = jnp.cumsum(slab_rows, axis=1) - slab_rows
    slab_row0 = pad_start[None, :] + jnp.cumsum(slab_rows, axis=0) - slab_rows

    def chunks(rows):
        return rows // BIG_CHUNK, rows % BIG_CHUNK // al

    def flat(cnt, src0, dst0, size, n_max):
        run = jnp.cumsum(cnt, axis=1)
        j = jnp.arange(n_max, dtype=I32)
        e_of = jnp.sum(run[:, None, :] <= j[None, :, None], axis=-1)
        sel = e_of[:, :, None] == e_ids[None, None, :]
        pick = lambda a: jnp.sum(jnp.where(sel, a[:, None, :], 0), axis=-1)
        local = (j[None, :] - pick(run - cnt)) * size
        return run[:, -1], (pick(src0) + local).reshape(-1), (pick(dst0) + local).reshape(-1)

    nbig, nsmall = chunks(slab_rows)
    big_tot, big_src, big_dst = flat(nbig, tile_off, slab_row0, BIG_CHUNK, MAX_BIG)
    small_tot, small_src, small_dst = flat(nsmall, tile_off + nbig * BIG_CHUNK,
                                           slab_row0 + nbig * BIG_CHUNK, al, MAX_SMALL)
    pad_groups = (jnp.sum(slab_rows, axis=1) - tw * TOP_K) // al
    plan = (big_tot, small_tot, pad_groups, big_src, big_dst, small_src, small_dst)
    tail_rows = jnp.concatenate([region - exp_rows, n_slots - pad_end[-1:]])
    tails = (*chunks(tail_rows), jnp.concatenate([pad_start + exp_rows, pad_end[-1:]]))

    base_t = jnp.repeat((tile_off - tile_carry).T, tw, axis=1)
    e_sel = eidx[:TOP_K][None] == e_ids[:, None, None]
    pos = jnp.sum(jnp.where(e_sel, base_t[:, None, :], 0), axis=0) + rank[:TOP_K]
    pos8 = jnp.concatenate([pos, jnp.full((SUBLANES - TOP_K, n_tok), -1, I32)], axis=0)

    tile_start = jnp.arange(n_tiles, dtype=I32) * tm
    tile_e = jnp.minimum(jnp.sum(pad_end[None, :] <= tile_start[:, None], axis=-1),
                         n_exp - 1).astype(I32)
    n_used = (pad_end[-1] // tm).astype(I32).reshape(1)

    xs = _dispatch(xn, pos8, plan, tails, n_slots)
    ys = _experts(xs, tile_e, n_used, w_up, b_up, w_down, b_down)
    return _combine(ys, h2, pos8.T, gate.T, plan, norm_final_g)


def kernel(x, norm_mix_g, w_in, s5_log_dt, s5_a_re, s5_a_im, s5_b_re, s5_b_im, s5_c_re, s5_c_im,
           s5_d, s5_w_glu, s5_b_glu, ml_conv_w, ml_conv_b, ml_b_gates, ml_norm_g, w_out,
           norm_ffn_g, w_router, b_router, w_up, b_up, w_down, b_down, norm_final_g):
    bsz, seq, d = x.shape
    depth = w_in.shape[0]
    assert depth == 1, "single-layer block"
    l = 0
    x2d = x.reshape(bsz * seq, d)
    qk, v, o, ut, gt = _in_proj(x2d, norm_mix_g[l], w_in[l], ml_b_gates[l], bsz, seq)
    g_t = _s5(ut, s5_log_dt[l], s5_a_re[l], s5_a_im[l], s5_b_re[l], s5_b_im[l],
              s5_c_re[l], s5_c_im[l], s5_d[l])
    g_nat = jnp.swapaxes(g_t, 1, 2).reshape(bsz * seq, -1)
    y_ml = _mlstm(qk, v, o, gt, ml_conv_w[l], ml_conv_b[l], ml_norm_g[l], bsz, seq)
    y_ml = y_ml.reshape(bsz * seq, -1)
    h2, xn, eidx, gate, rank, tcnt = _post(x2d, g_nat, y_ml, s5_w_glu[l], s5_b_glu[l], w_out[l],
                                           norm_ffn_g[l], w_router[l], b_router[l])
    out = _moe(h2, xn, eidx, gate, rank, tcnt, w_up[l], b_up[l], w_down[l], b_down[l],
               norm_final_g)
    return out.reshape(bsz, seq, d)
```

```python
import functools
import math

import jax
import jax.numpy as jnp
from jax import lax
from jax.experimental import pallas as pl
from jax.experimental.pallas import tpu as pltpu

F32 = jnp.float32
BF16 = jnp.bfloat16
I32 = jnp.int32
U32 = jnp.uint32

S5_GROUP = 16
S5_STATE = 64
ML_HEADS = 4
CONV_WIDTH = 4
N_EXPERTS = 32
TOP_K = 4
SWIGLU_LIMIT = 7.0
SWIGLU_ALPHA = 1.702
RMS_EPS = 1e-5
LN_EPS = 1e-6

LANES = 128
SUBLANES = 8
S5_CHUNK = LANES
ML_CHUNK = 128
PROJ_TILE = 512
EXPERT_TILE = 512
DISPATCH_TILE = PROJ_TILE
SLAB_ALIGN = SUBLANES
BIG_CHUNK = 32
STAGE_ROWS = DISPATCH_TILE * TOP_K + N_EXPERTS * SLAB_ALIGN
STAGE_BLOCK = STAGE_ROWS // 3
MAX_BIG = STAGE_ROWS // BIG_CHUNK
MAX_SMALL = N_EXPERTS * (BIG_CHUNK // SLAB_ALIGN - 1)
VMEM_LIMIT = 56 * 1024 * 1024

_NT = (((1,), (1,)), ((), ()))
_TN = (((0,), (0,)), ((), ()))


def _dot(a, b):
    return jnp.dot(a, b, preferred_element_type=F32)


def _dot_nt(a, b):
    return lax.dot_general(a, b, _NT, preferred_element_type=F32)


def _dot_tn(a, b):
    return lax.dot_general(a, b, _TN, preferred_element_type=F32)


def _split3(x):
    p1 = x.astype(BF16)
    r1 = x - p1.astype(F32)
    p2 = r1.astype(BF16)
    r2 = r1 - p2.astype(F32)
    return p1, p2, r2.astype(BF16)


def _sigmoid(x):
    return 1.0 / (1.0 + jnp.exp(-x))


def _inproj_kernel(x_ref, g_ref, wnat_ref, wut_ref, wgt_ref, bg_ref,
                   qk_ref, v_ref, o_ref, ut_ref, gt_ref, *, width):
    x = x_ref[...]
    ms = jnp.mean(x * x, axis=-1, keepdims=True)
    hn = (x * lax.rsqrt(ms + RMS_EPS) * g_ref[...]).astype(BF16)
    nat = _dot(hn, wnat_ref[...])
    qk_ref[...] = nat[:, :2 * width].astype(BF16)
    v_ref[...] = nat[:, 2 * width:3 * width].astype(BF16)
    o_ref[...] = nat[:, 3 * width:].astype(BF16)
    ut_ref[0] = _dot_nt(wut_ref[...], hn).astype(BF16)
    gt_ref[0] = _dot_nt(wgt_ref[...], hn) + bg_ref[...]


def _in_proj(x2d, norm_g, w_in, b_gates, bsz, seq):
    n_tok, d = x2d.shape
    s5w = d // 2
    mlw = d - s5w
    tm = min(PROJ_TILE, seq)
    tpb = seq // tm
    w_bf = w_in.astype(BF16)
    w_nat = w_bf[:, s5w:s5w + 4 * mlw]
    w_ut = w_bf[:, :s5w].T
    n_gate = 2 * ML_HEADS
    w_gt = jnp.zeros((16, d), BF16).at[:n_gate].set(w_bf[:, s5w + 4 * mlw:].T)
    b_g = jnp.zeros((16, 1), F32).at[:n_gate, 0].set(b_gates.astype(F32))
    grid = (n_tok // tm,)
    full = lambda i: (0, 0)
    return pl.pallas_call(
        functools.partial(_inproj_kernel, width=mlw),
        grid=grid,
        in_specs=[
            pl.BlockSpec((tm, d), lambda i: (i, 0)),
            pl.BlockSpec((1, d), full),
            pl.BlockSpec((d, 4 * mlw), full),
            pl.BlockSpec((s5w, d), full),
            pl.BlockSpec((16, d), full),
            pl.BlockSpec((16, 1), full),
        ],
        out_specs=[
            pl.BlockSpec((tm, 2 * mlw), lambda i: (i, 0)),
            pl.BlockSpec((tm, mlw), lambda i: (i, 0)),
            pl.BlockSpec((tm, mlw), lambda i: (i, 0)),
            pl.BlockSpec((1, s5w, tm), lambda i: (i // tpb, 0, i % tpb)),
            pl.BlockSpec((1, 16, tm), lambda i: (i // tpb, 0, i % tpb)),
        ],
        out_shape=[
            jax.ShapeDtypeStruct((n_tok, 2 * mlw), BF16),
            jax.ShapeDtypeStruct((n_tok, mlw), BF16),
            jax.ShapeDtypeStruct((n_tok, mlw), BF16),
            jax.ShapeDtypeStruct((bsz, s5w, seq), BF16),
            jax.ShapeDtypeStruct((bsz, 16, seq), F32),
        ],
        compiler_params=pltpu.CompilerParams(
            dimension_semantics=("parallel",), vmem_limit_bytes=VMEM_LIMIT),
        name="in_proj",
    )(x2d, norm_g.reshape(1, d).astype(F32), w_nat, w_ut, w_gt, b_g)


def _s5_kernel(d_ref, x_ref, prm_ref, c_ref, bt_ref, out_ref, *, bsz, nc):
    L = S5_CHUNK
    P = S5_GROUP
    N = S5_STATE
    grp = pl.program_id(0)
    lane = lax.broadcasted_iota(I32, (1, 2 * N), 1)
    lo = lane < N
    a_re = jnp.minimum(prm_ref[0, 0:1, :], -1e-4)
    a_im = prm_ref[0, 1:2, :]
    dt = jnp.exp(prm_ref[0, 2:3, :])
    zr = dt * a_re
    zi = dt * a_im

    quarter = jnp.where(lo, 0.0, 0.5 * math.pi)
    quarter_sw = jnp.where(lo, 0.5 * math.pi, 0.0)

    def powtab(e, swap=False):
        return jnp.exp(e * zr) * jnp.cos(e * zi - (quarter_sw if swap else quarter))

    er = jnp.exp(zr) * jnp.cos(zi)
    ei = jnp.exp(zr) * jnp.sin(zi)
    den = a_re * a_re + a_im * a_im
    coef_r = ((er - 1.0) * a_re + ei * a_im) / den
    coef_i = (ei * a_re - (er - 1.0) * a_im) / den
    c_r = c_ref[0, 0]
    c_i = c_ref[0, 1]
    bb_r = coef_r * bt_ref[0, 0] - coef_i * bt_ref[0, 1]
    bb_i = coef_r * bt_ref[0, 1] + coef_i * bt_ref[0, 0]

    tau = lax.broadcasted_iota(I32, (L, 1), 0).astype(F32)

    cb_rows = []
    for q in range(P):
        cbr = c_r * bb_r[q:q + 1] - c_i * bb_i[q:q + 1]
        cbi = c_r * bb_i[q:q + 1] + c_i * bb_r[q:q + 1]
        cb_rows.append(jnp.where(lo, cbr, -cbi))
    cb = jnp.concatenate(cb_rows, axis=0)
    pt0 = powtab(tau)
    c1, c2, c3 = _split3(cb)
    t1, t2, t3 = _split3(pt0)
    kmat = (_dot_nt(c1, t1) + _dot_nt(c1, t2) + _dot_nt(c2, t1)
            + _dot_nt(c2, t2) + _dot_nt(c1, t3) + _dot_nt(c3, t1))

    pt_rev = powtab(L - 1.0 - tau)
    pt_rev_sw = powtab(L - 1.0 - tau, swap=True)
    f_rows = []
    for q in range(P):
        a1 = bb_r[q:q + 1]
        a2 = jnp.where(lo, -bb_i[q:q + 1], bb_i[q:q + 1])
        f_rows.append((a1 * pt_rev + a2 * pt_rev_sw).astype(BF16))
    fmat = jnp.concatenate(f_rows, axis=0)
    pt1 = powtab(tau + 1.0)
    pt1_sw = powtab(tau + 1.0, swap=True)
    e_rows = []
    for p in range(P):
        b1 = jnp.where(lo, c_r[p:p + 1], -c_r[p:p + 1])
        b2 = -c_i[p:p + 1]
        e_rows.append((b1 * pt1 + b2 * pt1_sw).astype(BF16))
    emat_t = jnp.concatenate(e_rows, axis=0)

    lhs = jnp.concatenate(
        [jnp.concatenate([x_ref[b, q] for q in range(P)], axis=1) for b in range(bsz)], axis=0)
    s_end = _dot(lhs, fmat)

    rr = lax.broadcasted_iota(I32, (L, L), 0)
    cc = lax.broadcasted_iota(I32, (L, L), 1)
    causal = cc >= rr
    y = None
    for q0 in range(0, P, 2):
        rows = []
        for q in (q0, q0 + 1):
            tiles = []
            for p in range(P):
                kb = jnp.broadcast_to(kmat[q * P + p:q * P + p + 1, :], (L, L))
                toe = pltpu.roll(kb, 0, 1, stride=1, stride_axis=0)
                tiles.append(jnp.where(causal, toe, 0.0).astype(BF16))
            rows.append(jnp.concatenate(tiles, axis=1))
        part = _dot(lhs[:, q0 * L:(q0 + 2) * L], jnp.concatenate(rows, axis=0))
        y = part if y is None else y + part

    m_rows = bsz * nc
    ridx = lax.broadcasted_iota(I32, (m_rows, 1), 0)
    cidx = ridx % nc
    h = jnp.where(cidx >= 1, pltpu.roll(s_end, 1, 0), 0.0)
    d = 1
    while d < nc:
        e = float(d * L)
        pr = jnp.exp(e * zr) * jnp.cos(e * zi)
        pi = jnp.exp(e * zr) * jnp.sin(e * zi)
        sh = jnp.where(cidx >= d, pltpu.roll(h, d, 0), 0.0)
        h = h + pr * sh + jnp.where(lo, -pi, pi) * pltpu.roll(sh, N, 1)
        d *= 2
    y = y + _dot_nt(h.astype(BF16), emat_t)

    for b in range(bsz):
        for p in range(P):
            yp = (y[b * nc:(b + 1) * nc, p * L:(p + 1) * L]
                  + d_ref[grp * P + p] * x_ref[b, p].astype(F32))
            out_ref[b, p] = jax.nn.gelu(yp).astype(BF16)


def _s5(ut, log_dt, a_re, a_im, b_re, b_im, c_re, c_im, d_skip):
    bsz, s5w, seq = ut.shape
    L = S5_CHUNK
    nc = seq // L
    groups = s5w // S5_GROUP
    n = S5_STATE
    x4 = ut.reshape(bsz, s5w, nc, L)
    dup = lambda t: jnp.concatenate([t, t], axis=-1).astype(F32)
    prm = jnp.zeros((groups, SUBLANES, 2 * n), F32)
    prm = prm.at[:, 0].set(dup(a_re)).at[:, 1].set(dup(a_im))
    prm = prm.at[:, 2].set(jnp.broadcast_to(log_dt.astype(F32)[:, None], (groups, 2 * n)))
    cpar = jnp.stack([dup(c_re), dup(c_im)], axis=1)
    btpar = jnp.stack([dup(jnp.swapaxes(b_re, 1, 2)), dup(jnp.swapaxes(b_im, 1, 2))], axis=1)
    out = pl.pallas_call(
        functools.partial(_s5_kernel, bsz=bsz, nc=nc),
        grid_spec=pltpu.PrefetchScalarGridSpec(
            num_scalar_prefetch=1,
            grid=(groups,),
            in_specs=[
                pl.BlockSpec((bsz, S5_GROUP, nc, L), lambda g, d: (0, g, 0, 0)),
                pl.BlockSpec((1, SUBLANES, 2 * n), lambda g, d: (g, 0, 0)),
                pl.BlockSpec((1, 2, S5_GROUP, 2 * n), lambda g, d: (g, 0, 0, 0)),
                pl.BlockSpec((1, 2, S5_GROUP, 2 * n), lambda g, d: (g, 0, 0, 0)),
            ],
            out_specs=pl.BlockSpec((bsz, S5_GROUP, nc, L), lambda g, d: (0, g, 0, 0)),
        ),
        out_shape=jax.ShapeDtypeStruct((bsz, s5w, nc, L), BF16),
        compiler_params=pltpu.CompilerParams(
            dimension_semantics=("parallel",), vmem_limit_bytes=VMEM_LIMIT),
        name="s5",
    )(d_skip.astype(F32), x4, prm, cpar, btpar)
    return out.reshape(bsz, s5w, seq)


def _log_sigmoid(x):
    return jnp.minimum(x, 0.0) - jnp.log(1.0 + jnp.exp(-jnp.abs(x)))


def _mlstm_kernel(qk_ref, v_ref, o_ref, gt_ref, cw_ref, cb_ref, ng_ref, y_ref,
                  tail_scr, c_scr, m_scr, *, bsz, chunk, width):
    L = chunk
    H = ML_HEADS
    dh = width // H
    step = pl.program_id(0)

    @pl.when(step == 0)
    def _():
        tail_scr[...] = jnp.zeros_like(tail_scr)
        c_scr[...] = jnp.zeros_like(c_scr)
        m_scr[...] = jnp.zeros_like(m_scr)

    assert L == LANES and dh == LANES, "column replication below uses one 128x128 tile per head"
    rr = lax.broadcasted_iota(I32, (L, L), 0)
    cc = lax.broadcasted_iota(I32, (L, L), 1)
    causal = cc <= rr
    ltri = causal.astype(BF16)
    utri = (rr <= cc).astype(BF16)
    eye = (rr == cc).astype(BF16)
    ones_blk = jnp.ones((L, dh), BF16)
    grow = lax.broadcasted_iota(I32, (16, 1), 0)
    lane_row = lax.broadcasted_iota(I32, (SUBLANES, L), 1)
    scale = 1.0 / math.sqrt(dh)
    inv_dh = 1.0 / dh

    def replicate(mat, rows):
        p = jnp.concatenate([jnp.broadcast_to(r, (LANES, L)) for r in rows], axis=0)
        p1 = p.astype(BF16)
        p2 = (p - p1.astype(F32)).astype(BF16)
        return _dot_nt(mat, p1) + _dot_nt(mat, p2)

    def rowsum(x):
        hi = x.astype(BF16)
        lo = (x - hi.astype(F32)).astype(BF16)
        return _dot(hi, ones_blk) + _dot(lo, ones_blk)

    n_str = bsz * H
    streams = [(b, hd) for b in range(bsz) for hd in range(H)]
    tails = [tail_scr[b] for b in range(bsz)]
    caugs = [c_scr[i] for i in range(n_str)]
    m_prevs = [m_scr[i][0:1, 0:1] for i in range(n_str)]

    q_all, k_all, new_tails, i_rows, lf_rows, b_rows = [], [], [], [], [], []
    for b in range(bsz):
        xqk = qk_ref[b].astype(F32)
        ext = jnp.concatenate([tails[b], xqk], axis=0)
        acc = jnp.broadcast_to(cb_ref[...], xqk.shape)
        for j in range(CONV_WIDTH):
            back = CONV_WIDTH - 1 - j
            sh = ext if back == 0 else pltpu.roll(ext, back, 0)
            acc = acc + cw_ref[j:j + 1, :] * sh[SUBLANES:]
        new_tails.append(xqk[L - SUBLANES:])
        qkc = acc * _sigmoid(acc)
        q_all.append(qkc[:, :width].astype(BF16))
        k_all.append((qkc[:, width:] * scale).astype(BF16))
        g = gt_ref[b]
        g2 = jnp.where(grow >= H, _log_sigmoid(g), g)
        p1, p2, p3 = _split3(g2)
        brow = _dot(p1, utri) + _dot(p2, utri) + _dot(p3, utri)
        i_rows.append(g[0:H])
        lf_rows.append(g2[H:2 * H])
        b_rows.append(brow[H:2 * H])

    cm_rows = []
    for b in range(bsz):
        x = i_rows[b] - b_rows[b]
        x = jnp.concatenate([x, x], axis=0)
        sft = 1
        while sft < L:
            x = jnp.maximum(x, jnp.where(lane_row >= sft, pltpu.roll(x, sft, 1), -jnp.inf))
            sft *= 2
        cm_rows.append(x[0:H])

    bc_rep = [replicate(ltri, [lf_rows[b][h:h + 1] for h in range(H)]) for b in range(bsz)]
    ic_rep = [replicate(eye, [i_rows[b][h:h + 1] for h in range(H)]) for b in range(bsz)]
    cm_rep = [replicate(eye, [cm_rows[b][h:h + 1] for h in range(H)]) for b in range(bsz)]

    q_h = [q_all[b][:, hd * dh:(hd + 1) * dh] for b, hd in streams]
    k_h = [k_all[b][:, hd * dh:(hd + 1) * dh] for b, hd in streams]
    vaug = [jnp.concatenate([v_ref[b][:, hd * dh:(hd + 1) * dh], ones_blk], axis=1)
            for b, hd in streams]
    scores = [_dot_nt(q_h[i], k_h[i]) for i in range(n_str)]
    inter = [_dot(q_h[i], caugs[i].astype(BF16)) for i in range(n_str)]

    m_reps, w_inter, s_bf, kw, decays, new_ms = [], [], [], [], [], []
    for i, (b, hd) in enumerate(streams):
        sl = slice(hd * LANES, (hd + 1) * LANES)
        b_r = b_rows[b][hd:hd + 1]
        i_r = i_rows[b][hd:hd + 1]
        b_last = b_r[:, L - 1:L]
        bc = bc_rep[b][:, sl]
        m_prev = m_prevs[i]
        m = bc + jnp.maximum(cm_rep[b][:, sl], m_prev)
        m_reps.append(m)
        w_inter.append(jnp.exp(bc + m_prev - m))
        log_d = jnp.where(causal, bc - b_r + i_r, -jnp.inf)
        s_bf.append((scores[i] * jnp.exp(log_d - m)).astype(BF16))
        m_next = jnp.maximum(b_last + m_prev, b_last + cm_rows[b][hd:hd + 1, L - 1:L])
        decays.append(jnp.exp(b_last + m_prev - m_next))
        ws = jnp.exp(b_last - bc + ic_rep[b][:, sl] - m_next)
        kw.append((k_h[i].astype(F32) * ws).astype(BF16))
        new_ms.append(jnp.broadcast_to(m_next, (SUBLANES, LANES)))

    intra = [_dot(s_bf[i], vaug[i]) for i in range(n_str)]
    upd = [_dot_tn(kw[i], vaug[i]) for i in range(n_str)]

    hh = []
    for i in range(n_str):
        num = w_inter[i] * inter[i][:, :dh] + intra[i][:, :dh]
        den = w_inter[i] * inter[i][:, dh:] + intra[i][:, dh:]
        hh.append(num / jnp.maximum(jnp.abs(den), jnp.exp(-m_reps[i])))
    mu = [rowsum(h) * inv_dh for h in hh]
    ctr = [hh[i] - mu[i] for i in range(n_str)]
    var = [rowsum(c * c) * inv_dh for c in ctr]
    outs = []
    for i, (b, hd) in enumerate(streams):
        sl = slice(hd * dh, (hd + 1) * dh)
        hn = ctr[i] * lax.rsqrt(var[i] + LN_EPS) * ng_ref[:, sl]
        outs.append(hn * _sigmoid(o_ref[b][:, sl].astype(F32)))

    for b in range(bsz):
        tail_scr[b] = new_tails[b]
        y_ref[b] = jnp.concatenate(outs[b * H:(b + 1) * H], axis=1).astype(BF16)
    for i in range(n_str):
        c_scr[i] = decays[i] * caugs[i] + upd[i]
        m_scr[i] = new_ms[i]


def _mlstm(qk, v, o, gt, conv_w, conv_b, norm_g, bsz, seq):
    width = v.shape[-1]
    L = min(ML_CHUNK, seq)
    dh = width // ML_HEADS
    qk3 = qk.reshape(bsz, seq, 2 * width)
    v3 = v.reshape(bsz, seq, width)
    o3 = o.reshape(bsz, seq, width)
    full = lambda c: (0, 0)
    return pl.pallas_call(
        functools.partial(_mlstm_kernel, bsz=bsz, chunk=L, width=width),
        grid=(seq // L,),
        in_specs=[
            pl.BlockSpec((bsz, L, 2 * width), lambda c: (0, c, 0)),
            pl.BlockSpec((bsz, L, width), lambda c: (0, c, 0)),
            pl.BlockSpec((bsz, L, width), lambda c: (0, c, 0)),
            pl.BlockSpec((bsz, 16, L), lambda c: (0, 0, c)),
            pl.BlockSpec((CONV_WIDTH, 2 * width), full),
            pl.BlockSpec((1, 2 * width), full),
            pl.BlockSpec((1, width), full),
        ],
        out_specs=pl.BlockSpec((bsz, L, width), lambda c: (0, c, 0)),
        out_shape=jax.ShapeDtypeStruct((bsz, seq, width), BF16),
        scratch_shapes=[
            pltpu.VMEM((bsz, SUBLANES, 2 * width), F32),
            pltpu.VMEM((bsz * ML_HEADS, dh, 2 * dh), F32),
            pltpu.VMEM((bsz * ML_HEADS, SUBLANES, LANES), F32),
        ],
        compiler_params=pltpu.CompilerParams(
            dimension_semantics=("arbitrary",), vmem_limit_bytes=VMEM_LIMIT),
        name="mlstm",
    )(qk3, v3, o3, gt, conv_w.astype(F32), conv_b.reshape(1, -1).astype(F32),
      norm_g.reshape(1, -1).astype(F32))


def _post_kernel(x_ref, g_ref, yml_ref, wglu_ref, bglu_ref, wout_ref, nffn_ref, wrt_ref, br_ref,
                 h2_ref, xn_ref, eidx_ref, gate_ref, rank_ref, tcnt_ref, carry_scr):
    step = pl.program_id(0)

    @pl.when(step == 0)
    def _():
        carry_scr[...] = jnp.zeros_like(carry_scr)

    g = g_ref[...]
    z = _dot(g, wglu_ref[...]) + bglu_ref[...]
    s5 = g.astype(F32) * _sigmoid(z)
    mix = jnp.concatenate([s5.astype(BF16), yml_ref[...]], axis=1)
    h2 = x_ref[...] + _dot(mix, wout_ref[...])
    h2_ref[...] = h2
    ms = jnp.mean(h2 * h2, axis=-1, keepdims=True)
    xn = h2 * lax.rsqrt(ms + RMS_EPS) * nffn_ref[...]
    xb = xn.astype(BF16)
    xn_ref[...] = xb

    tm = xn.shape[0]
    logits = _dot_nt(wrt_ref[...], xb) + br_ref[...]
    eio = lax.broadcasted_iota(I32, (N_EXPERTS, tm), 0).astype(F32)
    vals = logits
    onehot = jnp.zeros((N_EXPERTS, tm), F32)
    idxs, tops = [], []
    for _ in range(TOP_K):
        mx = jnp.max(vals, axis=0, keepdims=True)
        idx = jnp.min(jnp.where(vals == mx, eio, float(N_EXPERTS)), axis=0, keepdims=True)
        sel = eio == idx
        onehot = onehot + sel.astype(F32)
        vals = jnp.where(sel, -jnp.inf, vals)
        idxs.append(idx)
        tops.append(mx)
    exps = [jnp.exp(t - tops[0]) for t in tops]
    tot = exps[0] + exps[1] + exps[2] + exps[3]
    pad_f = jnp.zeros((SUBLANES - TOP_K, tm), F32)
    eidx_ref[...] = jnp.concatenate(idxs + [pad_f], axis=0).astype(I32)
    gate_ref[...] = jnp.concatenate([e / tot for e in exps] + [pad_f], axis=0)

    rr = lax.broadcasted_iota(I32, (tm, tm), 0)
    cc = lax.broadcasted_iota(I32, (tm, tm), 1)
    before = (rr < cc).astype(BF16)
    carry = carry_scr[:, 0:1]
    rank_ex = _dot(onehot.astype(BF16), before) + carry
    ranks = [jnp.sum(jnp.where(eio == i, rank_ex, 0.0), axis=0, keepdims=True) for i in idxs]
    rank_ref[...] = jnp.concatenate(ranks + [pad_f], axis=0).astype(I32)
    tile_cnt = jnp.sum(onehot, axis=1, keepdims=True)
    carry_scr[...] = jnp.broadcast_to(carry + tile_cnt, carry_scr.shape)
    tcnt_ref[0] = jnp.broadcast_to(tile_cnt, carry_scr.shape)


def _post(x2d, g_nat, y_ml, w_glu, b_glu, w_out, norm_g, w_router, b_router):
    n_tok, d = x2d.shape
    s5w = g_nat.shape[-1]
    mlw = y_ml.shape[-1]
    tm = min(PROJ_TILE, n_tok)
    full = lambda i: (0, 0)
    row = lambda i: (i, 0)
    colb = lambda i: (0, i)
    return pl.pallas_call(
        _post_kernel,
        grid=(n_tok // tm,),
        in_specs=[
            pl.BlockSpec((tm, d), row),
            pl.BlockSpec((tm, s5w), row),
            pl.BlockSpec((tm, mlw), row),
            pl.BlockSpec((s5w, s5w), full),
            pl.BlockSpec((1, s5w), full),
            pl.BlockSpec((d, d), full),
            pl.BlockSpec((1, d), full),
            pl.BlockSpec((N_EXPERTS, d), full),
            pl.BlockSpec((N_EXPERTS, 1), full),
        ],
        out_specs=[
            pl.BlockSpec((tm, d), row),
            pl.BlockSpec((tm, d), row),
            pl.BlockSpec((SUBLANES, tm), colb),
            pl.BlockSpec((SUBLANES, tm), colb),
            pl.BlockSpec((SUBLANES, tm), colb),
            pl.BlockSpec((1, N_EXPERTS, LANES), lambda i: (i, 0, 0)),
        ],
        out_shape=[
            jax.ShapeDtypeStruct((n_tok, d), F32),
            jax.ShapeDtypeStruct((n_tok, d), BF16),
            jax.ShapeDtypeStruct((SUBLANES, n_tok), I32),
            jax.ShapeDtypeStruct((SUBLANES, n_tok), F32),
            jax.ShapeDtypeStruct((SUBLANES, n_tok), I32),
            jax.ShapeDtypeStruct((n_tok // tm, N_EXPERTS, LANES), F32),
        ],
        scratch_shapes=[pltpu.VMEM((N_EXPERTS, LANES), F32)],
        compiler_params=pltpu.CompilerParams(
            dimension_semantics=("arbitrary",), vmem_limit_bytes=VMEM_LIMIT),
        name="post_router",
    )(x2d, g_nat, y_ml, w_glu.astype(BF16), b_glu.reshape(1, -1).astype(F32), w_out.astype(BF16),
      norm_g.reshape(1, -1).astype(F32), w_router.T.astype(BF16),
      b_router.reshape(-1, 1).astype(F32))


def _slab_loop(base, nbig_ref, nsmall_ref, fn, n_slabs=N_EXPERTS):
    def per_expert(e, carry):
        idx = base + e
        nbig = nbig_ref[idx]

        def big(j, c2):
            fn(idx, j * BIG_CHUNK, BIG_CHUNK)
            return c2

        def small(j, c2):
            fn(idx, nbig * BIG_CHUNK + j * SLAB_ALIGN, SLAB_ALIGN)
            return c2

        lax.fori_loop(0, nbig, big, 0)
        lax.fori_loop(0, nsmall_ref[idx], small, 0)
        return carry

    lax.fori_loop(0, n_slabs, per_expert, 0)


def _plan_copies(step, plan, fn):
    nbig_ref, nsmall_ref, _, bsrc_ref, bdst_ref, ssrc_ref, sdst_ref = plan

    def big(j, carry):
        k = step * MAX_BIG + j
        fn(bsrc_ref[k], bdst_ref[k], BIG_CHUNK)
        return carry

    def small(j, carry):
        k = step * MAX_SMALL + j
        fn(ssrc_ref[k], sdst_ref[k], SLAB_ALIGN)
        return carry

    lax.fori_loop(0, nbig_ref[step], big, 0)
    lax.fori_loop(0, nsmall_ref[step], small, 0)


def _plan_wait(step, plan, copy):
    copy(0, 0, DISPATCH_TILE * TOP_K).wait()

    def one(j, carry):
        copy(0, 0, SLAB_ALIGN).wait()
        return carry

    lax.fori_loop(0, plan[2][step], one, 0)


def _unpack_pairs(words):
    lo = lax.bitcast_convert_type(words << 16, F32).astype(BF16)
    hi = lax.bitcast_convert_type(words & jnp.uint32(0xFFFF0000), F32).astype(BF16)
    return lo, hi


def _pack_pairs(lo_f32, hi_f32):
    lo = lax.bitcast_convert_type(lo_f32, U32) >> 16
    hi = lax.bitcast_convert_type(hi_f32, U32) & jnp.uint32(0xFFFF0000)
    return hi | lo


def _dispatch_kernel(*refs, n_steps):
    plan = refs[:7]
    znbig_ref, znsmall_ref, zrow_ref, xn_ref, pos_ref, xs_hbm, stage, zbuf, sem, zsem = refs[7:]
    i = pl.program_id(0)
    slot = i % 2
    tw, d = xn_ref.shape
    n_rows = stage.shape[1]
    half = d // 2

    def copy(s, stage_row, buf_row, rows):
        src = stage.at[s, pl.ds(pl.multiple_of(stage_row, SLAB_ALIGN), rows), :]
        dst = xs_hbm.at[pl.ds(pl.multiple_of(buf_row, SLAB_ALIGN), rows), :]
        return pltpu.make_async_copy(src, dst, sem)

    @pl.when(i == 0)
    def _():
        zbuf[...] = jnp.zeros_like(zbuf)

        def tail(idx, r0, rows):
            dst = xs_hbm.at[pl.ds(pl.multiple_of(zrow_ref[idx] + r0, SLAB_ALIGN), rows), :]
            return pltpu.make_async_copy(zbuf.at[pl.ds(0, rows), :], dst, zsem)

        _slab_loop(0, znbig_ref, znsmall_ref, lambda *a: tail(*a).start(), N_EXPERTS + 1)
        _slab_loop(0, znbig_ref, znsmall_ref, lambda *a: tail(*a).wait(), N_EXPERTS + 1)

    xt = xn_ref[...]
    blk = STAGE_BLOCK
    for r0 in range(0, n_rows, blk):
        rid = lax.broadcasted_iota(I32, (blk, tw), 0) + r0
        hit = rid == pos_ref[0:1, :]
        for k in range(1, TOP_K):
            hit = jnp.logical_or(hit, rid == pos_ref[k:k + 1, :])
        srt = _dot(jnp.where(hit, 1.0, 0.0).astype(BF16), xt)
        stage[slot, r0:r0 + blk, :] = _pack_pairs(srt[:, :half], srt[:, half:])

    @pl.when(i > 0)
    def _():
        _plan_wait(i - 1, plan, functools.partial(copy, 1 - slot))

    _plan_copies(i, plan, lambda *a: copy(slot, *a).start())

    @pl.when(i == n_steps - 1)
    def _():
        _plan_wait(i, plan, functools.partial(copy, slot))


def _dispatch(xn, pos8, plan, tails, n_slots):
    n_tok, d = xn.shape
    tw = DISPATCH_TILE
    n_steps = n_tok // tw
    return pl.pallas_call(
        functools.partial(_dispatch_kernel, n_steps=n_steps),
        grid_spec=pltpu.PrefetchScalarGridSpec(
            num_scalar_prefetch=10,
            grid=(n_steps,),
            in_specs=[
                pl.BlockSpec((tw, d), lambda i, *_: (i, 0)),
                pl.BlockSpec((SUBLANES, tw), lambda i, *_: (0, i)),
            ],
            out_specs=pl.BlockSpec(memory_space=pl.ANY),
            scratch_shapes=[
                pltpu.VMEM((2, STAGE_ROWS, d // 2), U32),
                pltpu.VMEM((BIG_CHUNK, d // 2), U32),
                pltpu.SemaphoreType.DMA(()),
                pltpu.SemaphoreType.DMA(()),
            ],
        ),
        out_shape=jax.ShapeDtypeStruct((n_slots, d // 2), U32),
        compiler_params=pltpu.CompilerParams(
            dimension_semantics=("arbitrary",), vmem_limit_bytes=VMEM_LIMIT),
        name="dispatch",
    )(*plan, *tails, xn, pos8)


def _expert_kernel(te_ref, nu_ref, x_ref, wup_ref, bup_ref, wdn_ref, bdn_ref, y_ref,
                   wup_bf, wdn_bf, *, d_ff):
    i = pl.program_id(0)
    n_used = nu_ref[0]

    @pl.when(jnp.logical_and(i < n_used,
                             jnp.logical_or(i == 0, te_ref[i] != te_ref[jnp.maximum(i - 1, 0)])))
    def _():
        wup_bf[...] = wup_ref[0].astype(BF16)
        wdn_bf[...] = wdn_ref[0].astype(BF16)

    @pl.when(i < n_used)
    def _():
        lo, hi = _unpack_pairs(x_ref[...])
        half = lo.shape[1]
        h = _dot(lo, wup_bf[:half, :]) + _dot(hi, wup_bf[half:, :]) + bup_ref[0]
        gl = jnp.minimum(h[:, :d_ff], SWIGLU_LIMIT)
        lin = jnp.clip(h[:, d_ff:], -SWIGLU_LIMIT, SWIGLU_LIMIT)
        act = gl * _sigmoid(SWIGLU_ALPHA * gl) * (lin + 1.0)
        y = _dot(act.astype(BF16), wdn_bf[...]) + bdn_ref[0]
        yb = y.astype(BF16).astype(F32)
        y_ref[...] = _pack_pairs(yb[:, :half], yb[:, half:])

    @pl.when(i >= n_used)
    def _():
        y_ref[...] = jnp.zeros_like(y_ref)


def _experts(xs, tile_e, n_used, w_up, b_up, w_down, b_down):
    n_slots, half = xs.shape
    d = 2 * half
    tm = EXPERT_TILE
    n_tiles = n_slots // tm
    d_ff = w_down.shape[1]
    return pl.pallas_call(
        functools.partial(_expert_kernel, d_ff=d_ff),
        grid_spec=pltpu.PrefetchScalarGridSpec(
            num_scalar_prefetch=2,
            grid=(n_tiles,),
            in_specs=[
                pl.BlockSpec((tm, half), lambda i, te, nu: (jnp.minimum(i, nu[0] - 1), 0)),
                pl.BlockSpec((1, d, 2 * d_ff), lambda i, te, nu: (te[i], 0, 0)),
                pl.BlockSpec((1, 1, 2 * d_ff), lambda i, te, nu: (te[i], 0, 0)),
                pl.BlockSpec((1, d_ff, d), lambda i, te, nu: (te[i], 0, 0)),
                pl.BlockSpec((1, 1, d), lambda i, te, nu: (te[i], 0, 0)),
            ],
            out_specs=pl.BlockSpec((tm, half), lambda i, te, nu: (i, 0)),
            scratch_shapes=[
                pltpu.VMEM((d, 2 * d_ff), BF16),
                pltpu.VMEM((d_ff, d), BF16),
            ],
        ),
        out_shape=jax.ShapeDtypeStruct((n_slots, half), U32),
        compiler_params=pltpu.CompilerParams(
            dimension_semantics=("arbitrary",), vmem_limit_bytes=VMEM_LIMIT),
        name="experts",
    )(tile_e, n_used, xs, w_up, b_up.reshape(N_EXPERTS, 1, -1),
      w_down, b_down.reshape(N_EXPERTS, 1, -1))


def _combine_kernel(*refs, n_steps):
    plan = refs[:7]
    ys_hbm, h2_ref, posc_ref, gatec_ref, ng_ref, out_ref, stage, sem = refs[7:]
    i = pl.program_id(0)
    slot = i % 2
    tw, d = h2_ref.shape
    n_rows = stage.shape[1]

    def copy(s, stage_row, buf_row, rows):
        src = ys_hbm.at[pl.ds(pl.multiple_of(buf_row, SLAB_ALIGN), rows), :]
        dst = stage.at[s, pl.ds(pl.multiple_of(stage_row, SLAB_ALIGN), rows), :]
        return pltpu.make_async_copy(src, dst, sem.at[s])

    @pl.when(i == 0)
    def _():
        stage[...] = jnp.zeros_like(stage)
        _plan_copies(0, plan, lambda *a: copy(0, *a).start())

    @pl.when(i + 1 < n_steps)
    def _():
        _plan_copies(i + 1, plan, lambda *a: copy(1 - slot, *a).start())

    _plan_wait(i, plan, functools.partial(copy, slot))

    y_lo, y_hi = _unpack_pairs(stage[slot])
    blk = 128
    for t0 in range(0, tw, blk):
        cid = lax.broadcasted_iota(I32, (blk, n_rows), 1)
        pg = jnp.zeros((blk, n_rows), F32)
        for k in range(TOP_K):
            pg = jnp.where(cid == posc_ref[t0:t0 + blk, k:k + 1],
                           gatec_ref[t0:t0 + blk, k:k + 1], pg)
        pgb = pg.astype(BF16)
        moe = jnp.concatenate([_dot(pgb, y_lo), _dot(pgb, y_hi)], axis=1)
        acc = h2_ref[t0:t0 + blk, :] + moe
        ms = jnp.mean(acc * acc, axis=-1, keepdims=True)
        out_ref[t0:t0 + blk, :] = acc * lax.rsqrt(ms + RMS_EPS) * ng_ref[...]


def _combine(ys, h2, pos_cols, gate_cols, plan, norm_g):
    n_tok, d = h2.shape
    tw = DISPATCH_TILE
    n_steps = n_tok // tw
    return pl.pallas_call(
        functools.partial(_combine_kernel, n_steps=n_steps),
        grid_spec=pltpu.PrefetchScalarGridSpec(
            num_scalar_prefetch=7,
            grid=(n_steps,),
            in_specs=[
                pl.BlockSpec(memory_space=pl.ANY),
                pl.BlockSpec((tw, d), lambda i, *_: (i, 0)),
                pl.BlockSpec((tw, SUBLANES), lambda i, *_: (i, 0)),
                pl.BlockSpec((tw, SUBLANES), lambda i, *_: (i, 0)),
                pl.BlockSpec((1, d), lambda i, *_: (0, 0)),
            ],
            out_specs=pl.BlockSpec((tw, d), lambda i, *_: (i, 0)),
            scratch_shapes=[
                pltpu.VMEM((2, STAGE_ROWS, d // 2), U32),
                pltpu.SemaphoreType.DMA((2,)),
            ],
        ),
        out_shape=jax.ShapeDtypeStruct((n_tok, d), F32),
        compiler_params=pltpu.CompilerParams(
            dimension_semantics=("arbitrary",), vmem_limit_bytes=VMEM_LIMIT),
        name="combine",
    )(*plan, ys, h2, pos_cols, gate_cols, norm_g.reshape(1, -1).astype(F32))


def _moe(h2, xn, eidx, gate, rank, tcnt, w_up, b_up, w_down, b_down, norm_final_g):
    n_tok, d = h2.shape
    tm = EXPERT_TILE
    tw = DISPATCH_TILE
    n_steps = n_tok // tw
    n_exp = N_EXPERTS
    e_ids = jnp.arange(n_exp, dtype=I32)
    al = SLAB_ALIGN
    tile_cnt = tcnt[:, :, 0].astype(I32)
    slab_rows = (tile_cnt + al - 1) // al * al
    exp_rows = jnp.sum(slab_rows, axis=0)
    region = (exp_rows + tm - 1) // tm * tm
    pad_end = jnp.cumsum(region)
    pad_start = pad_end - region
    n_slots = (n_tok * TOP_K + n_steps * n_exp * (al - 1) + n_exp * (tm - 1) + tm - 1) // tm * tm
    n_tiles = n_slots // tm

    tile_carry = jnp.cumsum(tile_cnt, axis=0) - tile_cnt
    tile_off = jnp.cumsum(slab_rows, axis=1) - slab_rows
    slab_row0 = pad_start[None, :] + jnp.cumsum(slab_rows, axis=0) - slab_rows

    def chunks(rows):
        return rows // BIG_CHUNK, rows % BIG_CHUNK // al

    def flat(cnt, src0, dst0, size, n_max):
        run = jnp.cumsum(cnt, axis=1)
        j = jnp.arange(n_max, dtype=I32)
        e_of = jnp.sum(run[:, None, :] <= j[None, :, None], axis=-1)
        sel = e_of[:, :, None] == e_ids[None, None, :]
        pick = lambda a: jnp.sum(jnp.where(sel, a[:, None, :], 0), axis=-1)
        local = (j[None, :] - pick(run - cnt)) * size
        return run[:, -1], (pick(src0) + local).reshape(-1), (pick(dst0) + local).reshape(-1)

    nbig, nsmall = chunks(slab_rows)
    big_tot, big_src, big_dst = flat(nbig, tile_off, slab_row0, BIG_CHUNK, MAX_BIG)
    small_tot, small_src, small_dst = flat(nsmall, tile_off + nbig * BIG_CHUNK,
                                           slab_row0 + nbig * BIG_CHUNK, al, MAX_SMALL)
    pad_groups = (jnp.sum(slab_rows, axis=1) - tw * TOP_K) // al
    plan = (big_tot, small_tot, pad_groups, big_src, big_dst, small_src, small_dst)
    tail_rows = jnp.concatenate([region - exp_rows, n_slots - pad_end[-1:]])
    tails = (*chunks(tail_rows), jnp.concatenate([pad_start + exp_rows, pad_end[-1:]]))

    base_t = jnp.repeat((tile_off - tile_carry).T, tw, axis=1)
    e_sel = eidx[:TOP_K][None] == e_ids[:, None, None]
    pos = jnp.sum(jnp.where(e_sel, base_t[:, None, :], 0), axis=0) + rank[:TOP_K]
    pos8 = jnp.concatenate([pos, jnp.full((SUBLANES - TOP_K, n_tok), -1, I32)], axis=0)

    tile_start = jnp.arange(n_tiles, dtype=I32) * tm
    tile_e = jnp.minimum(jnp.sum(pad_end[None, :] <= tile_start[:, None], axis=-1),
                         n_exp - 1).astype(I32)
    n_used = (pad_end[-1] // tm).astype(I32).reshape(1)

    xs = _dispatch(xn, pos8, plan, tails, n_slots)
    ys = _experts(xs, tile_e, n_used, w_up, b_up, w_down, b_down)
    return _combine(ys, h2, pos8.T, gate.T, plan, norm_final_g)


def kernel(x, norm_mix_g, w_in, s5_log_dt, s5_a_re, s5_a_im, s5_b_re, s5_b_im, s5_c_re, s5_c_im,
           s5_d, s5_w_glu, s5_b_glu, ml_conv_w, ml_conv_b, ml_b_gates, ml_norm_g, w_out,
           norm_ffn_g, w_router, b_router, w_up, b_up, w_down, b_down, norm_final_g):
    bsz, seq, d = x.shape
    depth = w_in.shape[0]
    assert depth == 1, "single-layer block"
    l = 0
    x2d = x.reshape(bsz * seq, d)
    qk, v, o, ut, gt = _in_proj(x2d, norm_mix_g[l], w_in[l], ml_b_gates[l], bsz, seq)
    g_t = _s5(ut, s5_log_dt[l], s5_a_re[l], s5_a_im[l], s5_b_re[l], s5_b_im[l],
              s5_c_re[l], s5_c_im[l], s5_d[l])
    g_nat = jnp.swapaxes(g_t, 1, 2).reshape(bsz * seq, -1)
    y_ml = _mlstm(qk, v, o, gt, ml_conv_w[l], ml_conv_b[l], ml_norm_g[l], bsz, seq)
    y_ml = y_ml.reshape(bsz * seq, -1)
    h2, xn, eidx, gate, rank, tcnt = _post(x2d, g_nat, y_ml, s5_w_glu[l], s5_b_glu[l], w_out[l],
                                           norm_ffn_g[l], w_router[l], b_router[l])
    out = _moe(h2, xn, eidx, gate, rank, tcnt, w_up[l], b_up[l], w_down[l], b_down[l],
               norm_final_g)
    return out.reshape(bsz, seq, d)
```

```python
import functools
import math

import jax
import jax.numpy as jnp
from jax import lax
from jax.experimental import pallas as pl
from jax.experimental.pallas import tpu as pltpu

F32 = jnp.float32
BF16 = jnp.bfloat16
I32 = jnp.int32
U32 = jnp.uint32

S5_GROUP = 16
S5_STATE = 64
ML_HEADS = 4
CONV_WIDTH = 4
N_EXPERTS = 32
TOP_K = 4
SWIGLU_LIMIT = 7.0
SWIGLU_ALPHA = 1.702
RMS_EPS = 1e-5
LN_EPS = 1e-6

LANES = 128
SUBLANES = 8
S5_CHUNK = LANES
ML_CHUNK = 128
PROJ_TILE = 512
EXPERT_TILE = 512
DISPATCH_TILE = 256
SLAB_ALIGN = SUBLANES
BIG_CHUNK = 32
STAGE_ROWS = DISPATCH_TILE * TOP_K + N_EXPERTS * SLAB_ALIGN
STAGE_BLOCK = STAGE_ROWS // 2
MAX_BIG = STAGE_ROWS // BIG_CHUNK
MAX_SMALL = N_EXPERTS * (BIG_CHUNK // SLAB_ALIGN - 1)
VMEM_LIMIT = 56 * 1024 * 1024

_NT = (((1,), (1,)), ((), ()))
_TN = (((0,), (0,)), ((), ()))


def _dot(a, b):
    return jnp.dot(a, b, preferred_element_type=F32)


def _dot_nt(a, b):
    return lax.dot_general(a, b, _NT, preferred_element_type=F32)


def _dot_tn(a, b):
    return lax.dot_general(a, b, _TN, preferred_element_type=F32)


def _split3(x):
    p1 = x.astype(BF16)
    r1 = x - p1.astype(F32)
    p2 = r1.astype(BF16)
    r2 = r1 - p2.astype(F32)
    return p1, p2, r2.astype(BF16)


def _sigmoid(x):
    return 0.5 * jnp.tanh(0.5 * x) + 0.5


def _inproj_kernel(x_ref, g_ref, wnat_ref, wut_ref, wgt_ref, bg_ref,
                   qk_ref, v_ref, o_ref, ut_ref, gt_ref, *, width):
    x = x_ref[...]
    ms = jnp.mean(x * x, axis=-1, keepdims=True)
    hn = (x * lax.rsqrt(ms + RMS_EPS) * g_ref[...]).astype(BF16)
    nat = _dot(hn, wnat_ref[...])
    qk_ref[...] = nat[:, :2 * width].astype(BF16)
    v_ref[...] = nat[:, 2 * width:3 * width].astype(BF16)
    o_ref[...] = nat[:, 3 * width:].astype(BF16)
    ut_ref[0] = _dot_nt(wut_ref[...], hn).astype(BF16)
    gt_ref[0] = _dot_nt(wgt_ref[...], hn) + bg_ref[...]


def _in_proj(x2d, norm_g, w_in, b_gates, bsz, seq):
    n_tok, d = x2d.shape
    s5w = d // 2
    mlw = d - s5w
    tm = min(PROJ_TILE, seq)
    tpb = seq // tm
    w_bf = w_in.astype(BF16)
    w_nat = w_bf[:, s5w:s5w + 4 * mlw]
    w_ut = w_bf[:, :s5w].T
    n_gate = 2 * ML_HEADS
    w_gt = jnp.zeros((16, d), BF16).at[:n_gate].set(w_bf[:, s5w + 4 * mlw:].T)
    b_g = jnp.zeros((16, 1), F32).at[:n_gate, 0].set(b_gates.astype(F32))
    grid = (n_tok // tm,)
    full = lambda i: (0, 0)
    return pl.pallas_call(
        functools.partial(_inproj_kernel, width=mlw),
        grid=grid,
        in_specs=[
            pl.BlockSpec((tm, d), lambda i: (i, 0)),
            pl.BlockSpec((1, d), full),
            pl.BlockSpec((d, 4 * mlw), full),
            pl.BlockSpec((s5w, d), full),
            pl.BlockSpec((16, d), full),
            pl.BlockSpec((16, 1), full),
        ],
        out_specs=[
            pl.BlockSpec((tm, 2 * mlw), lambda i: (i, 0)),
            pl.BlockSpec((tm, mlw), lambda i: (i, 0)),
            pl.BlockSpec((tm, mlw), lambda i: (i, 0)),
            pl.BlockSpec((1, s5w, tm), lambda i: (i // tpb, 0, i % tpb)),
            pl.BlockSpec((1, 16, tm), lambda i: (i // tpb, 0, i % tpb)),
        ],
        out_shape=[
            jax.ShapeDtypeStruct((n_tok, 2 * mlw), BF16),
            jax.ShapeDtypeStruct((n_tok, mlw), BF16),
            jax.ShapeDtypeStruct((n_tok, mlw), BF16),
            jax.ShapeDtypeStruct((bsz, s5w, seq), BF16),
            jax.ShapeDtypeStruct((bsz, 16, seq), F32),
        ],
        compiler_params=pltpu.CompilerParams(
            dimension_semantics=("parallel",), vmem_limit_bytes=VMEM_LIMIT),
        name="in_proj",
    )(x2d, norm_g.reshape(1, d).astype(F32), w_nat, w_ut, w_gt, b_g)


def _s5_kernel(d_ref, x_ref, prm_ref, c_ref, bt_ref, out_ref, *, bsz, nc):
    L = S5_CHUNK
    P = S5_GROUP
    N = S5_STATE
    grp = pl.program_id(0)
    lane = lax.broadcasted_iota(I32, (1, 2 * N), 1)
    lo = lane < N
    a_re = jnp.minimum(prm_ref[0, 0:1, :], -1e-4)
    a_im = prm_ref[0, 1:2, :]
    dt = jnp.exp(prm_ref[0, 2:3, :])
    zr = dt * a_re
    zi = dt * a_im

    quarter = jnp.where(lo, 0.0, 0.5 * math.pi)
    quarter_sw = jnp.where(lo, 0.5 * math.pi, 0.0)

    def powtab(e, swap=False):
        return jnp.exp(e * zr) * jnp.cos(e * zi - (quarter_sw if swap else quarter))

    er = jnp.exp(zr) * jnp.cos(zi)
    ei = jnp.exp(zr) * jnp.sin(zi)
    den = a_re * a_re + a_im * a_im
    coef_r = ((er - 1.0) * a_re + ei * a_im) / den
    coef_i = (ei * a_re - (er - 1.0) * a_im) / den
    c_r = c_ref[0, 0]
    c_i = c_ref[0, 1]
    bb_r = coef_r * bt_ref[0, 0] - coef_i * bt_ref[0, 1]
    bb_i = coef_r * bt_ref[0, 1] + coef_i * bt_ref[0, 0]

    tau = lax.broadcasted_iota(I32, (L, 1), 0).astype(F32)

    cb_rows = []
    for q in range(P):
        cbr = c_r * bb_r[q:q + 1] - c_i * bb_i[q:q + 1]
        cbi = c_r * bb_i[q:q + 1] + c_i * bb_r[q:q + 1]
        cb_rows.append(jnp.where(lo, cbr, -cbi))
    cb = jnp.concatenate(cb_rows, axis=0)
    pt0 = powtab(tau)
    c1, c2, c3 = _split3(cb)
    t1, t2, t3 = _split3(pt0)
    kmat = (_dot_nt(c1, t1) + _dot_nt(c1, t2) + _dot_nt(c2, t1)
            + _dot_nt(c2, t2) + _dot_nt(c1, t3) + _dot_nt(c3, t1))

    pt_rev = powtab(L - 1.0 - tau)
    pt_rev_sw = powtab(L - 1.0 - tau, swap=True)
    f_rows = []
    for q in range(P):
        a1 = bb_r[q:q + 1]
        a2 = jnp.where(lo, -bb_i[q:q + 1], bb_i[q:q + 1])
        f_rows.append((a1 * pt_rev + a2 * pt_rev_sw).astype(BF16))
    fmat = jnp.concatenate(f_rows, axis=0)
    pt1 = powtab(tau + 1.0)
    pt1_sw = powtab(tau + 1.0, swap=True)
    e_rows = []
    for p in range(P):
        b1 = jnp.where(lo, c_r[p:p + 1], -c_r[p:p + 1])
        b2 = -c_i[p:p + 1]
        e_rows.append((b1 * pt1 + b2 * pt1_sw).astype(BF16))
    emat_t = jnp.concatenate(e_rows, axis=0)

    lhs = jnp.concatenate(
        [jnp.concatenate([x_ref[b, q] for q in range(P)], axis=1) for b in range(bsz)], axis=0)
    s_end = _dot(lhs, fmat)

    rr = lax.broadcasted_iota(I32, (L, L), 0)
    cc = lax.broadcasted_iota(I32, (L, L), 1)
    causal = cc >= rr
    y = None
    for q0 in range(0, P, 2):
        rows = []
        for q in (q0, q0 + 1):
            tiles = []
            for p in range(P):
                kb = jnp.broadcast_to(kmat[q * P + p:q * P + p + 1, :], (L, L))
                toe = pltpu.roll(kb, 0, 1, stride=1, stride_axis=0)
                tiles.append(jnp.where(causal, toe, 0.0).astype(BF16))
            rows.append(jnp.concatenate(tiles, axis=1))
        part = _dot(lhs[:, q0 * L:(q0 + 2) * L], jnp.concatenate(rows, axis=0))
        y = part if y is None else y + part

    m_rows = bsz * nc
    ridx = lax.broadcasted_iota(I32, (m_rows, 1), 0)
    cidx = ridx % nc
    h = jnp.where(cidx >= 1, pltpu.roll(s_end, 1, 0), 0.0)
    d = 1
    while d < nc:
        e = float(d * L)
        pr = jnp.exp(e * zr) * jnp.cos(e * zi)
        pi = jnp.exp(e * zr) * jnp.sin(e * zi)
        sh = jnp.where(cidx >= d, pltpu.roll(h, d, 0), 0.0)
        h = h + pr * sh + jnp.where(lo, -pi, pi) * pltpu.roll(sh, N, 1)
        d *= 2
    y = y + _dot_nt(h.astype(BF16), emat_t)

    for b in range(bsz):
        for p in range(P):
            yp = (y[b * nc:(b + 1) * nc, p * L:(p + 1) * L]
                  + d_ref[grp * P + p] * x_ref[b, p].astype(F32))
            out_ref[b, p] = jax.nn.gelu(yp).astype(BF16)


def _s5(ut, log_dt, a_re, a_im, b_re, b_im, c_re, c_im, d_skip):
    bsz, s5w, seq = ut.shape
    L = S5_CHUNK
    nc = seq // L
    groups = s5w // S5_GROUP
    n = S5_STATE
    x4 = ut.reshape(bsz, s5w, nc, L)
    dup = lambda t: jnp.concatenate([t, t], axis=-1).astype(F32)
    prm = jnp.zeros((groups, SUBLANES, 2 * n), F32)
    prm = prm.at[:, 0].set(dup(a_re)).at[:, 1].set(dup(a_im))
    prm = prm.at[:, 2].set(jnp.broadcast_to(log_dt.astype(F32)[:, None], (groups, 2 * n)))
    cpar = jnp.stack([dup(c_re), dup(c_im)], axis=1)
    btpar = jnp.stack([dup(jnp.swapaxes(b_re, 1, 2)), dup(jnp.swapaxes(b_im, 1, 2))], axis=1)
    out = pl.pallas_call(
        functools.partial(_s5_kernel, bsz=bsz, nc=nc),
        grid_spec=pltpu.PrefetchScalarGridSpec(
            num_scalar_prefetch=1,
            grid=(groups,),
            in_specs=[
                pl.BlockSpec((bsz, S5_GROUP, nc, L), lambda g, d: (0, g, 0, 0)),
                pl.BlockSpec((1, SUBLANES, 2 * n), lambda g, d: (g, 0, 0)),
                pl.BlockSpec((1, 2, S5_GROUP, 2 * n), lambda g, d: (g, 0, 0, 0)),
                pl.BlockSpec((1, 2, S5_GROUP, 2 * n), lambda g, d: (g, 0, 0, 0)),
            ],
            out_specs=pl.BlockSpec((bsz, S5_GROUP, nc, L), lambda g, d: (0, g, 0, 0)),
        ),
        out_shape=jax.ShapeDtypeStruct((bsz, s5w, nc, L), BF16),
        compiler_params=pltpu.CompilerParams(
            dimension_semantics=("parallel",), vmem_limit_bytes=VMEM_LIMIT),
        name="s5",
    )(d_skip.astype(F32), x4, prm, cpar, btpar)
    return out.reshape(bsz, s5w, seq)


def _log_sigmoid(x):
    return jnp.minimum(x, 0.0) - jnp.log(1.0 + jnp.exp(-jnp.abs(x)))


def _mlstm_kernel(qk_ref, v_ref, o_ref, gt_ref, cw_ref, cb_ref, ng_ref, y_ref,
                  tail_scr, c_scr, m_scr, *, bsz, chunk, width):
    L = chunk
    H = ML_HEADS
    dh = width // H
    step = pl.program_id(0)

    @pl.when(step == 0)
    def _():
        tail_scr[...] = jnp.zeros_like(tail_scr)
        c_scr[...] = jnp.zeros_like(c_scr)
        m_scr[...] = jnp.zeros_like(m_scr)

    assert L == LANES and dh == LANES, "column replication below uses one 128x128 tile per head"
    rr = lax.broadcasted_iota(I32, (L, L), 0)
    cc = lax.broadcasted_iota(I32, (L, L), 1)
    causal = cc <= rr
    ltri = causal.astype(BF16)
    utri = (rr <= cc).astype(BF16)
    eye = (rr == cc).astype(BF16)
    ones_blk = jnp.ones((L, dh), BF16)
    grow = lax.broadcasted_iota(I32, (16, 1), 0)
    lane_row = lax.broadcasted_iota(I32, (SUBLANES, L), 1)
    scale = 1.0 / math.sqrt(dh)
    inv_dh = 1.0 / dh

    def replicate(mat, rows):
        p = jnp.concatenate([jnp.broadcast_to(r, (LANES, L)) for r in rows], axis=0)
        p1 = p.astype(BF16)
        p2 = (p - p1.astype(F32)).astype(BF16)
        return _dot_nt(mat, p1) + _dot_nt(mat, p2)

    def rowsum(x):
        hi = x.astype(BF16)
        lo = (x - hi.astype(F32)).astype(BF16)
        return _dot(hi, ones_blk) + _dot(lo, ones_blk)

    n_str = bsz * H
    streams = [(b, hd) for b in range(bsz) for hd in range(H)]
    tails = [tail_scr[b] for b in range(bsz)]
    caugs = [c_scr[i] for i in range(n_str)]
    m_prevs = [m_scr[i][0:1, 0:1] for i in range(n_str)]

    q_all, k_all, new_tails, i_rows, lf_rows, b_rows = [], [], [], [], [], []
    for b in range(bsz):
        xqk = qk_ref[b].astype(F32)
        ext = jnp.concatenate([tails[b], xqk], axis=0)
        acc = jnp.broadcast_to(cb_ref[...], xqk.shape)
        for j in range(CONV_WIDTH):
            back = CONV_WIDTH - 1 - j
            sh = ext if back == 0 else pltpu.roll(ext, back, 0)
            acc = acc + cw_ref[j:j + 1, :] * sh[SUBLANES:]
        new_tails.append(xqk[L - SUBLANES:])
        qkc = acc * _sigmoid(acc)
        q_all.append(qkc[:, :width].astype(BF16))
        k_all.append((qkc[:, width:] * scale).astype(BF16))
        g = gt_ref[b]
        g2 = jnp.where(grow >= H, _log_sigmoid(g), g)
        p1, p2, p3 = _split3(g2)
        brow = _dot(p1, utri) + _dot(p2, utri) + _dot(p3, utri)
        i_rows.append(g[0:H])
        lf_rows.append(g2[H:2 * H])
        b_rows.append(brow[H:2 * H])

    cm_rows = []
    for b in range(bsz):
        x = i_rows[b] - b_rows[b]
        x = jnp.concatenate([x, x], axis=0)
        sft = 1
        while sft < L:
            x = jnp.maximum(x, jnp.where(lane_row >= sft, pltpu.roll(x, sft, 1), -jnp.inf))
            sft *= 2
        cm_rows.append(x[0:H])

    bc_rep = [replicate(ltri, [lf_rows[b][h:h + 1] for h in range(H)]) for b in range(bsz)]
    ic_rep = [replicate(eye, [i_rows[b][h:h + 1] for h in range(H)]) for b in range(bsz)]
    cm_rep = [replicate(eye, [cm_rows[b][h:h + 1] for h in range(H)]) for b in range(bsz)]

    q_h = [q_all[b][:, hd * dh:(hd + 1) * dh] for b, hd in streams]
    k_h = [k_all[b][:, hd * dh:(hd + 1) * dh] for b, hd in streams]
    vaug = [jnp.concatenate([v_ref[b][:, hd * dh:(hd + 1) * dh], ones_blk], axis=1)
            for b, hd in streams]
    scores = [_dot_nt(q_h[i], k_h[i]) for i in range(n_str)]
    inter = [_dot(q_h[i], caugs[i].astype(BF16)) for i in range(n_str)]

    m_reps, w_inter, s_bf, kw, decays, new_ms = [], [], [], [], [], []
    for i, (b, hd) in enumerate(streams):
        sl = slice(hd * LANES, (hd + 1) * LANES)
        b_r = b_rows[b][hd:hd + 1]
        i_r = i_rows[b][hd:hd + 1]
        b_last = b_r[:, L - 1:L]
        bc = bc_rep[b][:, sl]
        m_prev = m_prevs[i]
        m = bc + jnp.maximum(cm_rep[b][:, sl], m_prev)
        m_reps.append(m)
        w_inter.append(jnp.exp(bc + m_prev - m))
        log_d = jnp.where(causal, bc - b_r + i_r, -jnp.inf)
        s_bf.append((scores[i] * jnp.exp(log_d - m)).astype(BF16))
        m_next = jnp.maximum(b_last + m_prev, b_last + cm_rows[b][hd:hd + 1, L - 1:L])
        decays.append(jnp.exp(b_last + m_prev - m_next))
        ws = jnp.exp(b_last - bc + ic_rep[b][:, sl] - m_next)
        kw.append((k_h[i].astype(F32) * ws).astype(BF16))
        new_ms.append(jnp.broadcast_to(m_next, (SUBLANES, LANES)))

    intra = [_dot(s_bf[i], vaug[i]) for i in range(n_str)]
    upd = [_dot_tn(kw[i], vaug[i]) for i in range(n_str)]

    hh = []
    for i in range(n_str):
        num = w_inter[i] * inter[i][:, :dh] + intra[i][:, :dh]
        den = w_inter[i] * inter[i][:, dh:] + intra[i][:, dh:]
        hh.append(num / jnp.maximum(jnp.abs(den), jnp.exp(-m_reps[i])))
    mu = [rowsum(h) * inv_dh for h in hh]
    ctr = [hh[i] - mu[i] for i in range(n_str)]
    var = [rowsum(c * c) * inv_dh for c in ctr]
    outs = []
    for i, (b, hd) in enumerate(streams):
        sl = slice(hd * dh, (hd + 1) * dh)
        hn = ctr[i] * lax.rsqrt(var[i] + LN_EPS) * ng_ref[:, sl]
        outs.append(hn * _sigmoid(o_ref[b][:, sl].astype(F32)))

    for b in range(bsz):
        tail_scr[b] = new_tails[b]
        y_ref[b] = jnp.concatenate(outs[b * H:(b + 1) * H], axis=1).astype(BF16)
    for i in range(n_str):
        c_scr[i] = decays[i] * caugs[i] + upd[i]
        m_scr[i] = new_ms[i]


def _mlstm(qk, v, o, gt, conv_w, conv_b, norm_g, bsz, seq):
    width = v.shape[-1]
    L = min(ML_CHUNK, seq)
    dh = width // ML_HEADS
    qk3 = qk.reshape(bsz, seq, 2 * width)
    v3 = v.reshape(bsz, seq, width)
    o3 = o.reshape(bsz, seq, width)
    full = lambda c: (0, 0)
    return pl.pallas_call(
        functools.partial(_mlstm_kernel, bsz=bsz, chunk=L, width=width),
        grid=(seq // L,),
        in_specs=[
            pl.BlockSpec((bsz, L, 2 * width), lambda c: (0, c, 0)),
            pl.BlockSpec((bsz, L, width), lambda c: (0, c, 0)),
            pl.BlockSpec((bsz, L, width), lambda c: (0, c, 0)),
            pl.BlockSpec((bsz, 16, L), lambda c: (0, 0, c)),
            pl.BlockSpec((CONV_WIDTH, 2 * width), full),
            pl.BlockSpec((1, 2 * width), full),
            pl.BlockSpec((1, width), full),
        ],
        out_specs=pl.BlockSpec((bsz, L, width), lambda c: (0, c, 0)),
        out_shape=jax.ShapeDtypeStruct((bsz, seq, width), BF16),
        scratch_shapes=[
            pltpu.VMEM((bsz, SUBLANES, 2 * width), F32),
            pltpu.VMEM((bsz * ML_HEADS, dh, 2 * dh), F32),
            pltpu.VMEM((bsz * ML_HEADS, SUBLANES, LANES), F32),
        ],
        compiler_params=pltpu.CompilerParams(
            dimension_semantics=("arbitrary",), vmem_limit_bytes=VMEM_LIMIT),
        name="mlstm",
    )(qk3, v3, o3, gt, conv_w.astype(F32), conv_b.reshape(1, -1).astype(F32),
      norm_g.reshape(1, -1).astype(F32))


def _post_kernel(x_ref, g_ref, yml_ref, wglu_ref, bglu_ref, wout_ref, nffn_ref, wrt_ref, br_ref,
                 h2_ref, xn_ref, eidx_ref, gate_ref, rank_ref, tcnt_ref, carry_scr):
    step = pl.program_id(0)

    @pl.when(step == 0)
    def _():
        carry_scr[...] = jnp.zeros_like(carry_scr)

    g = g_ref[...]
    z = _dot(g, wglu_ref[...]) + bglu_ref[...]
    s5 = g.astype(F32) * _sigmoid(z)
    mix = jnp.concatenate([s5.astype(BF16), yml_ref[...]], axis=1)
    h2 = x_ref[...] + _dot(mix, wout_ref[...])
    h2_ref[...] = h2
    ms = jnp.mean(h2 * h2, axis=-1, keepdims=True)
    xn = h2 * lax.rsqrt(ms + RMS_EPS) * nffn_ref[...]
    xb = xn.astype(BF16)
    xn_ref[...] = xb

    tm = xn.shape[0]
    logits = _dot_nt(wrt_ref[...], xb) + br_ref[...]
    eio = lax.broadcasted_iota(I32, (N_EXPERTS, tm), 0).astype(F32)
    vals = logits
    onehot = jnp.zeros((N_EXPERTS, tm), F32)
    idxs, tops = [], []
    for _ in range(TOP_K):
        mx = jnp.max(vals, axis=0, keepdims=True)
        idx = jnp.min(jnp.where(vals == mx, eio, float(N_EXPERTS)), axis=0, keepdims=True)
        sel = eio == idx
        onehot = onehot + sel.astype(F32)
        vals = jnp.where(sel, -jnp.inf, vals)
        idxs.append(idx)
        tops.append(mx)
    exps = [jnp.exp(t - tops[0]) for t in tops]
    tot = exps[0] + exps[1] + exps[2] + exps[3]
    pad_f = jnp.zeros((SUBLANES - TOP_K, tm), F32)
    eidx_ref[...] = jnp.concatenate(idxs + [pad_f], axis=0).astype(I32)
    gate_ref[...] = jnp.concatenate([e / tot for e in exps] + [pad_f], axis=0)

    rr = lax.broadcasted_iota(I32, (tm, tm), 0)
    cc = lax.broadcasted_iota(I32, (tm, tm), 1)
    before = (rr < cc).astype(BF16)
    carry = carry_scr[:, 0:1]
    rank_ex = _dot(onehot.astype(BF16), before) + carry
    ranks = [jnp.sum(jnp.where(eio == i, rank_ex, 0.0), axis=0, keepdims=True) for i in idxs]
    rank_ref[...] = jnp.concatenate(ranks + [pad_f], axis=0).astype(I32)
    tile_cnt = jnp.sum(onehot, axis=1, keepdims=True)
    carry_scr[...] = jnp.broadcast_to(carry + tile_cnt, carry_scr.shape)
    for s in range(tm // DISPATCH_TILE):
        sub = onehot[:, s * DISPATCH_TILE:(s + 1) * DISPATCH_TILE]
        tcnt_ref[0, s] = jnp.broadcast_to(jnp.sum(sub, axis=1, keepdims=True), carry_scr.shape)


def _post(x2d, g_nat, y_ml, w_glu, b_glu, w_out, norm_g, w_router, b_router):
    n_tok, d = x2d.shape
    s5w = g_nat.shape[-1]
    mlw = y_ml.shape[-1]
    tm = min(PROJ_TILE, n_tok)
    full = lambda i: (0, 0)
    row = lambda i: (i, 0)
    colb = lambda i: (0, i)
    return pl.pallas_call(
        _post_kernel,
        grid=(n_tok // tm,),
        in_specs=[
            pl.BlockSpec((tm, d), row),
            pl.BlockSpec((tm, s5w), row),
            pl.BlockSpec((tm, mlw), row),
            pl.BlockSpec((s5w, s5w), full),
            pl.BlockSpec((1, s5w), full),
            pl.BlockSpec((d, d), full),
            pl.BlockSpec((1, d), full),
            pl.BlockSpec((N_EXPERTS, d), full),
            pl.BlockSpec((N_EXPERTS, 1), full),
        ],
        out_specs=[
            pl.BlockSpec((tm, d), row),
            pl.BlockSpec((tm, d), row),
            pl.BlockSpec((SUBLANES, tm), colb),
            pl.BlockSpec((SUBLANES, tm), colb),
            pl.BlockSpec((SUBLANES, tm), colb),
            pl.BlockSpec((1, tm // DISPATCH_TILE, N_EXPERTS, LANES), lambda i: (i, 0, 0, 0)),
        ],
        out_shape=[
            jax.ShapeDtypeStruct((n_tok, d), F32),
            jax.ShapeDtypeStruct((n_tok, d), BF16),
            jax.ShapeDtypeStruct((SUBLANES, n_tok), I32),
            jax.ShapeDtypeStruct((SUBLANES, n_tok), F32),
            jax.ShapeDtypeStruct((SUBLANES, n_tok), I32),
            jax.ShapeDtypeStruct((n_tok // tm, tm // DISPATCH_TILE, N_EXPERTS, LANES), F32),
        ],
        scratch_shapes=[pltpu.VMEM((N_EXPERTS, LANES), F32)],
        compiler_params=pltpu.CompilerParams(
            dimension_semantics=("arbitrary",), vmem_limit_bytes=VMEM_LIMIT),
        name="post_router",
    )(x2d, g_nat, y_ml, w_glu.astype(BF16), b_glu.reshape(1, -1).astype(F32), w_out.astype(BF16),
      norm_g.reshape(1, -1).astype(F32), w_router.T.astype(BF16),
      b_router.reshape(-1, 1).astype(F32))


def _slab_loop(base, nbig_ref, nsmall_ref, fn, n_slabs=N_EXPERTS):
    def per_expert(e, carry):
        idx = base + e
        nbig = nbig_ref[idx]

        def big(j, c2):
            fn(idx, j * BIG_CHUNK, BIG_CHUNK)
            return c2

        def small(j, c2):
            fn(idx, nbig * BIG_CHUNK + j * SLAB_ALIGN, SLAB_ALIGN)
            return c2

        lax.fori_loop(0, nbig, big, 0)
        lax.fori_loop(0, nsmall_ref[idx], small, 0)
        return carry

    lax.fori_loop(0, n_slabs, per_expert, 0)


def _plan_copies(step, plan, fn):
    nbig_ref, nsmall_ref, _, bsrc_ref, bdst_ref, ssrc_ref, sdst_ref = plan

    def big(j, carry):
        k = step * MAX_BIG + j
        fn(bsrc_ref[k], bdst_ref[k], BIG_CHUNK)
        return carry

    def small(j, carry):
        k = step * MAX_SMALL + j
        fn(ssrc_ref[k], sdst_ref[k], SLAB_ALIGN)
        return carry

    lax.fori_loop(0, nbig_ref[step], big, 0)
    lax.fori_loop(0, nsmall_ref[step], small, 0)


def _plan_wait(step, plan, copy):
    copy(0, 0, DISPATCH_TILE * TOP_K).wait()

    def one(j, carry):
        copy(0, 0, SLAB_ALIGN).wait()
        return carry

    lax.fori_loop(0, plan[2][step], one, 0)


def _unpack_pairs(words):
    lo = lax.bitcast_convert_type(words << 16, F32).astype(BF16)
    hi = lax.bitcast_convert_type(words & jnp.uint32(0xFFFF0000), F32).astype(BF16)
    return lo, hi


def _pack_pairs(lo_f32, hi_f32):
    lo = lax.bitcast_convert_type(lo_f32, U32) >> 16
    hi = lax.bitcast_convert_type(hi_f32, U32) & jnp.uint32(0xFFFF0000)
    return hi | lo


def _dispatch_kernel(*refs, n_steps):
    plan = refs[:7]
    znbig_ref, znsmall_ref, zrow_ref, xn_ref, pos_ref, xs_hbm, stage, zbuf, sem, zsem = refs[7:]
    i = pl.program_id(0)
    slot = i % 2
    tw, d = xn_ref.shape
    n_rows = stage.shape[1]
    half = d // 2

    def copy(s, stage_row, buf_row, rows):
        src = stage.at[s, pl.ds(pl.multiple_of(stage_row, SLAB_ALIGN), rows), :]
        dst = xs_hbm.at[pl.ds(pl.multiple_of(buf_row, SLAB_ALIGN), rows), :]
        return pltpu.make_async_copy(src, dst, sem)

    @pl.when(i == 0)
    def _():
        zbuf[...] = jnp.zeros_like(zbuf)

        def tail(idx, r0, rows):
            dst = xs_hbm.at[pl.ds(pl.multiple_of(zrow_ref[idx] + r0, SLAB_ALIGN), rows), :]
            return pltpu.make_async_copy(zbuf.at[pl.ds(0, rows), :], dst, zsem)

        _slab_loop(0, znbig_ref, znsmall_ref, lambda *a: tail(*a).start(), N_EXPERTS + 1)
        _slab_loop(0, znbig_ref, znsmall_ref, lambda *a: tail(*a).wait(), N_EXPERTS + 1)

    xt = xn_ref[...]
    blk = STAGE_BLOCK
    for r0 in range(0, n_rows, blk):
        rid = lax.broadcasted_iota(I32, (blk, tw), 0) + r0
        hit = rid == pos_ref[0:1, :]
        for k in range(1, TOP_K):
            hit = jnp.logical_or(hit, rid == pos_ref[k:k + 1, :])
        srt = _dot(jnp.where(hit, 1.0, 0.0).astype(BF16), xt)
        stage[slot, r0:r0 + blk, :] = _pack_pairs(srt[:, :half], srt[:, half:])

    @pl.when(i > 0)
    def _():
        _plan_wait(i - 1, plan, functools.partial(copy, 1 - slot))

    _plan_copies(i, plan, lambda *a: copy(slot, *a).start())

    @pl.when(i == n_steps - 1)
    def _():
        _plan_wait(i, plan, functools.partial(copy, slot))


def _dispatch(xn, pos8, plan, tails, n_slots):
    n_tok, d = xn.shape
    tw = DISPATCH_TILE
    n_steps = n_tok // tw
    return pl.pallas_call(
        functools.partial(_dispatch_kernel, n_steps=n_steps),
        grid_spec=pltpu.PrefetchScalarGridSpec(
            num_scalar_prefetch=10,
            grid=(n_steps,),
            in_specs=[
                pl.BlockSpec((tw, d), lambda i, *_: (i, 0)),
                pl.BlockSpec((SUBLANES, tw), lambda i, *_: (0, i)),
            ],
            out_specs=pl.BlockSpec(memory_space=pl.ANY),
            scratch_shapes=[
                pltpu.VMEM((2, STAGE_ROWS, d // 2), U32),
                pltpu.VMEM((BIG_CHUNK, d // 2), U32),
                pltpu.SemaphoreType.DMA(()),
                pltpu.SemaphoreType.DMA(()),
            ],
        ),
        out_shape=jax.ShapeDtypeStruct((n_slots, d // 2), U32),
        compiler_params=pltpu.CompilerParams(
            dimension_semantics=("arbitrary",), vmem_limit_bytes=VMEM_LIMIT),
        name="dispatch",
    )(*plan, *tails, xn, pos8)


def _expert_kernel(te_ref, nu_ref, x_ref, wup_ref, bup_ref, wdn_ref, bdn_ref, y_ref,
                   wup_bf, wdn_bf, *, d_ff):
    i = pl.program_id(0)
    n_used = nu_ref[0]

    @pl.when(jnp.logical_and(i < n_used,
                             jnp.logical_or(i == 0, te_ref[i] != te_ref[jnp.maximum(i - 1, 0)])))
    def _():
        wup_bf[...] = wup_ref[0].astype(BF16)
        wdn_bf[...] = wdn_ref[0].astype(BF16)

    @pl.when(i < n_used)
    def _():
        lo, hi = _unpack_pairs(x_ref[...])
        half = lo.shape[1]
        h = _dot(lo, wup_bf[:half, :]) + _dot(hi, wup_bf[half:, :]) + bup_ref[0]
        gl = jnp.minimum(h[:, :d_ff], SWIGLU_LIMIT)
        lin = jnp.clip(h[:, d_ff:], -SWIGLU_LIMIT, SWIGLU_LIMIT)
        act = gl * _sigmoid(SWIGLU_ALPHA * gl) * (lin + 1.0)
        y = _dot(act.astype(BF16), wdn_bf[...]) + bdn_ref[0]
        yb = y.astype(BF16).astype(F32)
        y_ref[...] = _pack_pairs(yb[:, :half], yb[:, half:])

    @pl.when(i >= n_used)
    def _():
        y_ref[...] = jnp.zeros_like(y_ref)


def _experts(xs, tile_e, n_used, w_up, b_up, w_down, b_down):
    n_slots, half = xs.shape
    d = 2 * half
    tm = EXPERT_TILE
    n_tiles = n_slots // tm
    d_ff = w_down.shape[1]
    return pl.pallas_call(
        functools.partial(_expert_kernel, d_ff=d_ff),
        grid_spec=pltpu.PrefetchScalarGridSpec(
            num_scalar_prefetch=2,
            grid=(n_tiles,),
            in_specs=[
                pl.BlockSpec((tm, half), lambda i, te, nu: (jnp.minimum(i, nu[0] - 1), 0)),
                pl.BlockSpec((1, d, 2 * d_ff), lambda i, te, nu: (te[i], 0, 0)),
                pl.BlockSpec((1, 1, 2 * d_ff), lambda i, te, nu: (te[i], 0, 0)),
                pl.BlockSpec((1, d_ff, d), lambda i, te, nu: (te[i], 0, 0)),
                pl.BlockSpec((1, 1, d), lambda i, te, nu: (te[i], 0, 0)),
            ],
            out_specs=pl.BlockSpec((tm, half), lambda i, te, nu: (i, 0)),
            scratch_shapes=[
                pltpu.VMEM((d, 2 * d_ff), BF16),
                pltpu.VMEM((d_ff, d), BF16),
            ],
        ),
        out_shape=jax.ShapeDtypeStruct((n_slots, half), U32),
        compiler_params=pltpu.CompilerParams(
            dimension_semantics=("arbitrary",), vmem_limit_bytes=VMEM_LIMIT),
        name="experts",
    )(tile_e, n_used, xs, w_up, b_up.reshape(N_EXPERTS, 1, -1),
      w_down, b_down.reshape(N_EXPERTS, 1, -1))


def _combine_kernel(*refs, n_steps):
    plan = refs[:7]
    ys_hbm, h2_ref, posc_ref, gatec_ref, ng_ref, out_ref, stage, sem = refs[7:]
    i = pl.program_id(0)
    slot = i % 2
    tw, d = h2_ref.shape
    n_rows = stage.shape[1]

    def copy(s, stage_row, buf_row, rows):
        src = ys_hbm.at[pl.ds(pl.multiple_of(buf_row, SLAB_ALIGN), rows), :]
        dst = stage.at[s, pl.ds(pl.multiple_of(stage_row, SLAB_ALIGN), rows), :]
        return pltpu.make_async_copy(src, dst, sem.at[s])

    @pl.when(i == 0)
    def _():
        stage[...] = jnp.zeros_like(stage)
        _plan_copies(0, plan, lambda *a: copy(0, *a).start())

    @pl.when(i + 1 < n_steps)
    def _():
        _plan_copies(i + 1, plan, lambda *a: copy(1 - slot, *a).start())

    _plan_wait(i, plan, functools.partial(copy, slot))

    y_lo, y_hi = _unpack_pairs(stage[slot])
    blk = 128
    for t0 in range(0, tw, blk):
        cid = lax.broadcasted_iota(I32, (blk, n_rows), 1)
        pg = jnp.zeros((blk, n_rows), F32)
        for k in range(TOP_K):
            pg = jnp.where(cid == posc_ref[t0:t0 + blk, k:k + 1],
                           gatec_ref[t0:t0 + blk, k:k + 1], pg)
        pgb = pg.astype(BF16)
        moe = jnp.concatenate([_dot(pgb, y_lo), _dot(pgb, y_hi)], axis=1)
        acc = h2_ref[t0:t0 + blk, :] + moe
        ms = jnp.mean(acc * acc, axis=-1, keepdims=True)
        out_ref[t0:t0 + blk, :] = acc * lax.rsqrt(ms + RMS_EPS) * ng_ref[...]


def _combine(ys, h2, pos_cols, gate_cols, plan, norm_g):
    n_tok, d = h2.shape
    tw = DISPATCH_TILE
    n_steps = n_tok // tw
    return pl.pallas_call(
        functools.partial(_combine_kernel, n_steps=n_steps),
        grid_spec=pltpu.PrefetchScalarGridSpec(
            num_scalar_prefetch=7,
            grid=(n_steps,),
            in_specs=[
                pl.BlockSpec(memory_space=pl.ANY),
                pl.BlockSpec((tw, d), lambda i, *_: (i, 0)),
                pl.BlockSpec((tw, SUBLANES), lambda i, *_: (i, 0)),
                pl.BlockSpec((tw, SUBLANES), lambda i, *_: (i, 0)),
                pl.BlockSpec((1, d), lambda i, *_: (0, 0)),
            ],
            out_specs=pl.BlockSpec((tw, d), lambda i, *_: (i, 0)),
            scratch_shapes=[
                pltpu.VMEM((2, STAGE_ROWS, d // 2), U32),
                pltpu.SemaphoreType.DMA((2,)),
            ],
        ),
        out_shape=jax.ShapeDtypeStruct((n_tok, d), F32),
        compiler_params=pltpu.CompilerParams(
            dimension_semantics=("arbitrary",), vmem_limit_bytes=VMEM_LIMIT),
        name="combine",
    )(*plan, ys, h2, pos_cols, gate_cols, norm_g.reshape(1, -1).astype(F32))


def _moe(h2, xn, eidx, gate, rank, tcnt, w_up, b_up, w_down, b_down, norm_final_g):
    n_tok, d = h2.shape
    tm = EXPERT_TILE
    tw = DISPATCH_TILE
    n_steps = n_tok // tw
    n_exp = N_EXPERTS
    e_ids = jnp.arange(n_exp, dtype=I32)
    al = SLAB_ALIGN
    tile_cnt = tcnt.reshape(n_steps, n_exp, LANES)[:, :, 0].astype(I32)
    slab_rows = (tile_cnt + al - 1) // al * al
    exp_rows = jnp.sum(slab_rows, axis=0)
    region = (exp_rows + tm - 1) // tm * tm
    pad_end = jnp.cumsum(region)
    pad_start = pad_end - region
    n_slots = (n_tok * TOP_K + n_steps * n_exp * (al - 1) + n_exp * (tm - 1) + tm - 1) // tm * tm
    n_tiles = n_slots // tm

    tile_carry = jnp.cumsum(tile_cnt, axis=0) - tile_cnt
    tile_off = jnp.cumsum(slab_rows, axis=1) - slab_rows
    slab_row0 = pad_start[None, :] + jnp.cumsum(slab_rows, axis=0) - slab_rows

    def chunks(rows):
        return rows // BIG_CHUNK, rows % BIG_CHUNK // al

    def flat(cnt, src0, dst0, size, n_max):
        run = jnp.cumsum(cnt, axis=1)
        j = jnp.arange(n_max, dtype=I32)
        e_of = jnp.sum(run[:, None, :] <= j[None, :, None], axis=-1)
        sel = e_of[:, :, None] == e_ids[None, None, :]
        pick = lambda a: jnp.sum(jnp.where(sel, a[:, None, :], 0), axis=-1)
        local = (j[None, :] - pick(run - cnt)) * size
        return run[:, -1], (pick(src0) + local).reshape(-1), (pick(dst0) + local).reshape(-1)

    nbig, nsmall = chunks(slab_rows)
    big_tot, big_src, big_dst = flat(nbig, tile_off, slab_row0, BIG_CHUNK, MAX_BIG)
    small_tot, small_src, small_dst = flat(nsmall, tile_off + nbig * BIG_CHUNK,
                                           slab_row0 + nbig * BIG_CHUNK, al, MAX_SMALL)
    pad_groups = (jnp.sum(slab_rows, axis=1) - tw * TOP_K) // al
    plan = (big_tot, small_tot, pad_groups, big_src, big_dst, small_src, small_dst)
    tail_rows = jnp.concatenate([region - exp_rows, n_slots - pad_end[-1:]])
    tails = (*chunks(tail_rows), jnp.concatenate([pad_start + exp_rows, pad_end[-1:]]))

    base_t = jnp.repeat((tile_off - tile_carry).T, tw, axis=1)
    e_sel = eidx[:TOP_K][None] == e_ids[:, None, None]
    pos = jnp.sum(jnp.where(e_sel, base_t[:, None, :], 0), axis=0) + rank[:TOP_K]
    pos8 = jnp.concatenate([pos, jnp.full((SUBLANES - TOP_K, n_tok), -1, I32)], axis=0)

    tile_start = jnp.arange(n_tiles, dtype=I32) * tm
    tile_e = jnp.minimum(jnp.sum(pad_end[None, :] <= tile_start[:, None], axis=-1),
                         n_exp - 1).astype(I32)
    n_used = (pad_end[-1] // tm).astype(I32).reshape(1)

    xs = _dispatch(xn, pos8, plan, tails, n_slots)
    ys = _experts(xs, tile_e, n_used, w_up, b_up, w_down, b_down)
    return _combine(ys, h2, pos8.T, gate.T, plan, norm_final_g)


def kernel(x, norm_mix_g, w_in, s5_log_dt, s5_a_re, s5_a_im, s5_b_re, s5_b_im, s5_c_re, s5_c_im,
           s5_d, s5_w_glu, s5_b_glu, ml_conv_w, ml_conv_b, ml_b_gates, ml_norm_g, w_out,
           norm_ffn_g, w_router, b_router, w_up, b_up, w_down, b_down, norm_final_g):
    bsz, seq, d = x.shape
    depth = w_in.shape[0]
    assert depth == 1, "single-layer block"
    l = 0
    x2d = x.reshape(bsz * seq, d)
    qk, v, o, ut, gt = _in_proj(x2d, norm_mix_g[l], w_in[l], ml_b_gates[l], bsz, seq)
    g_t = _s5(ut, s5_log_dt[l], s5_a_re[l], s5_a_im[l], s5_b_re[l], s5_b_im[l],
              s5_c_re[l], s5_c_im[l], s5_d[l])
    g_nat = jnp.swapaxes(g_t, 1, 2).reshape(bsz * seq, -1)
    y_ml = _mlstm(qk, v, o, gt, ml_conv_w[l], ml_conv_b[l], ml_norm_g[l], bsz, seq)
    y_ml = y_ml.reshape(bsz * seq, -1)
    h2, xn, eidx, gate, rank, tcnt = _post(x2d, g_nat, y_ml, s5_w_glu[l], s5_b_glu[l], w_out[l],
                                           norm_ffn_g[l], w_router[l], b_router[l])
    out = _moe(h2, xn, eidx, gate, rank, tcnt, w_up[l], b_up[l], w_down[l], b_down[l],
               norm_final_g)
    return out.reshape(bsz, seq, d)
```

```python
import functools
import math

import jax
import jax.numpy as jnp
from jax import lax
from jax.experimental import pallas as pl
from jax.experimental.pallas import tpu as pltpu

F32 = jnp.float32
BF16 = jnp.bfloat16
I32 = jnp.int32
U32 = jnp.uint32

S5_GROUP = 16
S5_STATE = 64
ML_HEADS = 4
CONV_WIDTH = 4
N_EXPERTS = 32
TOP_K = 4
SWIGLU_LIMIT = 7.0
SWIGLU_ALPHA = 1.702
RMS_EPS = 1e-5
LN_EPS = 1e-6

LANES = 128
SUBLANES = 8
S5_CHUNK = LANES
ML_CHUNK = 128
GATE_CHUNKS = 8
PROJ_TILE = 512
EXPERT_TILE = 512
DISPATCH_TILE = 256
SLAB_ALIGN = SUBLANES
BIG_CHUNK = 32
STAGE_ROWS = DISPATCH_TILE * TOP_K + N_EXPERTS * SLAB_ALIGN
STAGE_BLOCK = STAGE_ROWS // 2
MAX_BIG = STAGE_ROWS // BIG_CHUNK
MAX_SMALL = N_EXPERTS * (BIG_CHUNK // SLAB_ALIGN - 1)
VMEM_LIMIT = 56 * 1024 * 1024

_NT = (((1,), (1,)), ((), ()))
_TN = (((0,), (0,)), ((), ()))


def _dot(a, b):
    return jnp.dot(a, b, preferred_element_type=F32)


def _dot_nt(a, b):
    return lax.dot_general(a, b, _NT, preferred_element_type=F32)


def _dot_tn(a, b):
    return lax.dot_general(a, b, _TN, preferred_element_type=F32)


def _split3(x):
    p1 = x.astype(BF16)
    r1 = x - p1.astype(F32)
    p2 = r1.astype(BF16)
    r2 = r1 - p2.astype(F32)
    return p1, p2, r2.astype(BF16)


def _sigmoid(x):
    return 0.5 * jnp.tanh(0.5 * x) + 0.5


def _inproj_kernel(x_ref, g_ref, wnat_ref, wut_ref, wgt_ref, bg_ref,
                   qk_ref, v_ref, o_ref, ut_ref, gt_ref, *, width):
    x = x_ref[...]
    ms = jnp.mean(x * x, axis=-1, keepdims=True)
    hn = (x * lax.rsqrt(ms + RMS_EPS) * g_ref[...]).astype(BF16)
    nat = _dot(hn, wnat_ref[...])
    qk_ref[...] = nat[:, :2 * width].astype(BF16)
    v_ref[...] = nat[:, 2 * width:3 * width].astype(BF16)
    o_ref[...] = nat[:, 3 * width:].astype(BF16)
    ut_ref[0] = _dot_nt(wut_ref[...], hn).astype(BF16)
    gt_ref[0] = _dot_nt(wgt_ref[...], hn) + bg_ref[...]


def _in_proj(x2d, norm_g, w_in, b_gates, bsz, seq):
    n_tok, d = x2d.shape
    s5w = d // 2
    mlw = d - s5w
    tm = min(PROJ_TILE, seq)
    tpb = seq // tm
    w_bf = w_in.astype(BF16)
    w_nat = w_bf[:, s5w:s5w + 4 * mlw]
    w_ut = w_bf[:, :s5w].T
    n_gate = 2 * ML_HEADS
    w_gt = jnp.zeros((16, d), BF16).at[:n_gate].set(w_bf[:, s5w + 4 * mlw:].T)
    b_g = jnp.zeros((16, 1), F32).at[:n_gate, 0].set(b_gates.astype(F32))
    grid = (n_tok // tm,)
    full = lambda i: (0, 0)
    return pl.pallas_call(
        functools.partial(_inproj_kernel, width=mlw),
        grid=grid,
        in_specs=[
            pl.BlockSpec((tm, d), lambda i: (i, 0)),
            pl.BlockSpec((1, d), full),
            pl.BlockSpec((d, 4 * mlw), full),
            pl.BlockSpec((s5w, d), full),
            pl.BlockSpec((16, d), full),
            pl.BlockSpec((16, 1), full),
        ],
        out_specs=[
            pl.BlockSpec((tm, 2 * mlw), lambda i: (i, 0)),
            pl.BlockSpec((tm, mlw), lambda i: (i, 0)),
            pl.BlockSpec((tm, mlw), lambda i: (i, 0)),
            pl.BlockSpec((1, s5w, tm), lambda i: (i // tpb, 0, i % tpb)),
            pl.BlockSpec((1, 16, tm), lambda i: (i // tpb, 0, i % tpb)),
        ],
        out_shape=[
            jax.ShapeDtypeStruct((n_tok, 2 * mlw), BF16),
            jax.ShapeDtypeStruct((n_tok, mlw), BF16),
            jax.ShapeDtypeStruct((n_tok, mlw), BF16),
            jax.ShapeDtypeStruct((bsz, s5w, seq), BF16),
            jax.ShapeDtypeStruct((bsz, 16, seq), F32),
        ],
        compiler_params=pltpu.CompilerParams(
            dimension_semantics=("parallel",), vmem_limit_bytes=VMEM_LIMIT),
        name="in_proj",
    )(x2d, norm_g.reshape(1, d).astype(F32), w_nat, w_ut, w_gt, b_g)


def _s5_kernel(d_ref, x_ref, prm_ref, c_ref, bt_ref, out_ref, *, bsz, nc):
    L = S5_CHUNK
    P = S5_GROUP
    N = S5_STATE
    grp = pl.program_id(0)
    lane = lax.broadcasted_iota(I32, (1, 2 * N), 1)
    lo = lane < N
    a_re = jnp.minimum(prm_ref[0, 0:1, :], -1e-4)
    a_im = prm_ref[0, 1:2, :]
    dt = jnp.exp(prm_ref[0, 2:3, :])
    zr = dt * a_re
    zi = dt * a_im

    quarter = jnp.where(lo, 0.0, 0.5 * math.pi)
    quarter_sw = jnp.where(lo, 0.5 * math.pi, 0.0)

    def powtab(e, swap=False):
        return jnp.exp(e * zr) * jnp.cos(e * zi - (quarter_sw if swap else quarter))

    er = jnp.exp(zr) * jnp.cos(zi)
    ei = jnp.exp(zr) * jnp.sin(zi)
    den = a_re * a_re + a_im * a_im
    coef_r = ((er - 1.0) * a_re + ei * a_im) / den
    coef_i = (ei * a_re - (er - 1.0) * a_im) / den
    c_r = c_ref[0, 0]
    c_i = c_ref[0, 1]
    bb_r = coef_r * bt_ref[0, 0] - coef_i * bt_ref[0, 1]
    bb_i = coef_r * bt_ref[0, 1] + coef_i * bt_ref[0, 0]

    tau = lax.broadcasted_iota(I32, (L, 1), 0).astype(F32)

    cb_rows = []
    for q in range(P):
        cbr = c_r * bb_r[q:q + 1] - c_i * bb_i[q:q + 1]
        cbi = c_r * bb_i[q:q + 1] + c_i * bb_r[q:q + 1]
        cb_rows.append(jnp.where(lo, cbr, -cbi))
    cb = jnp.concatenate(cb_rows, axis=0)
    pt0 = powtab(tau)
    c1, c2, c3 = _split3(cb)
    t1, t2, t3 = _split3(pt0)
    kmat = (_dot_nt(c1, t1) + _dot_nt(c1, t2) + _dot_nt(c2, t1)
            + _dot_nt(c2, t2) + _dot_nt(c1, t3) + _dot_nt(c3, t1))

    pt_rev = powtab(L - 1.0 - tau)
    pt_rev_sw = powtab(L - 1.0 - tau, swap=True)
    f_rows = []
    for q in range(P):
        a1 = bb_r[q:q + 1]
        a2 = jnp.where(lo, -bb_i[q:q + 1], bb_i[q:q + 1])
        f_rows.append((a1 * pt_rev + a2 * pt_rev_sw).astype(BF16))
    fmat = jnp.concatenate(f_rows, axis=0)
    pt1 = powtab(tau + 1.0)
    pt1_sw = powtab(tau + 1.0, swap=True)
    e_rows = []
    for p in range(P):
        b1 = jnp.where(lo, c_r[p:p + 1], -c_r[p:p + 1])
        b2 = -c_i[p:p + 1]
        e_rows.append((b1 * pt1 + b2 * pt1_sw).astype(BF16))
    emat_t = jnp.concatenate(e_rows, axis=0)

    lhs = jnp.concatenate(
        [jnp.concatenate([x_ref[b, q] for q in range(P)], axis=1) for b in range(bsz)], axis=0)
    s_end = _dot(lhs, fmat)

    rr = lax.broadcasted_iota(I32, (L, L), 0)
    cc = lax.broadcasted_iota(I32, (L, L), 1)
    causal = cc >= rr
    y = None
    for q0 in range(0, P, 2):
        rows = []
        for q in (q0, q0 + 1):
            tiles = []
            for p in range(P):
                kb = jnp.broadcast_to(kmat[q * P + p:q * P + p + 1, :], (L, L))
                toe = pltpu.roll(kb, 0, 1, stride=1, stride_axis=0)
                tiles.append(jnp.where(causal, toe, 0.0).astype(BF16))
            rows.append(jnp.concatenate(tiles, axis=1))
        part = _dot(lhs[:, q0 * L:(q0 + 2) * L], jnp.concatenate(rows, axis=0))
        y = part if y is None else y + part

    m_rows = bsz * nc
    ridx = lax.broadcasted_iota(I32, (m_rows, 1), 0)
    cidx = ridx % nc
    h = jnp.where(cidx >= 1, pltpu.roll(s_end, 1, 0), 0.0)
    d = 1
    while d < nc:
        e = float(d * L)
        pr = jnp.exp(e * zr) * jnp.cos(e * zi)
        pi = jnp.exp(e * zr) * jnp.sin(e * zi)
        sh = jnp.where(cidx >= d, pltpu.roll(h, d, 0), 0.0)
        h = h + pr * sh + jnp.where(lo, -pi, pi) * pltpu.roll(sh, N, 1)
        d *= 2
    y = y + _dot_nt(h.astype(BF16), emat_t)

    for b in range(bsz):
        for p in range(P):
            yp = (y[b * nc:(b + 1) * nc, p * L:(p + 1) * L]
                  + d_ref[grp * P + p] * x_ref[b, p].astype(F32))
            out_ref[b, p] = jax.nn.gelu(yp).astype(BF16)


def _s5(ut, log_dt, a_re, a_im, b_re, b_im, c_re, c_im, d_skip):
    bsz, s5w, seq = ut.shape
    L = S5_CHUNK
    nc = seq // L
    groups = s5w // S5_GROUP
    n = S5_STATE
    x4 = ut.reshape(bsz, s5w, nc, L)
    dup = lambda t: jnp.concatenate([t, t], axis=-1).astype(F32)
    prm = jnp.zeros((groups, SUBLANES, 2 * n), F32)
    prm = prm.at[:, 0].set(dup(a_re)).at[:, 1].set(dup(a_im))
    prm = prm.at[:, 2].set(jnp.broadcast_to(log_dt.astype(F32)[:, None], (groups, 2 * n)))
    cpar = jnp.stack([dup(c_re), dup(c_im)], axis=1)
    btpar = jnp.stack([dup(jnp.swapaxes(b_re, 1, 2)), dup(jnp.swapaxes(b_im, 1, 2))], axis=1)
    out = pl.pallas_call(
        functools.partial(_s5_kernel, bsz=bsz, nc=nc),
        grid_spec=pltpu.PrefetchScalarGridSpec(
            num_scalar_prefetch=1,
            grid=(groups,),
            in_specs=[
                pl.BlockSpec((bsz, S5_GROUP, nc, L), lambda g, d: (0, g, 0, 0)),
                pl.BlockSpec((1, SUBLANES, 2 * n), lambda g, d: (g, 0, 0)),
                pl.BlockSpec((1, 2, S5_GROUP, 2 * n), lambda g, d: (g, 0, 0, 0)),
                pl.BlockSpec((1, 2, S5_GROUP, 2 * n), lambda g, d: (g, 0, 0, 0)),
            ],
            out_specs=pl.BlockSpec((bsz, S5_GROUP, nc, L), lambda g, d: (0, g, 0, 0)),
        ),
        out_shape=jax.ShapeDtypeStruct((bsz, s5w, nc, L), BF16),
        compiler_params=pltpu.CompilerParams(
            dimension_semantics=("parallel",), vmem_limit_bytes=VMEM_LIMIT),
        name="s5",
    )(d_skip.astype(F32), x4, prm, cpar, btpar)
    return out.reshape(bsz, s5w, seq)


def _log_sigmoid(x):
    return jnp.minimum(x, 0.0) - jnp.log(1.0 + jnp.exp(-jnp.abs(x)))


def _gates_kernel(gt_ref, out_ref, m_scr, *, bsz, chunk, n_sub):
    L = chunk
    H = ML_HEADS
    step = pl.program_id(0)

    @pl.when(step == 0)
    def _():
        m_scr[...] = jnp.zeros_like(m_scr)

    rr = lax.broadcasted_iota(I32, (L, L), 0)
    cc = lax.broadcasted_iota(I32, (L, L), 1)
    utri = (rr <= cc).astype(BF16)
    grow = lax.broadcasted_iota(I32, (16, 1), 0)
    lane_row = lax.broadcasted_iota(I32, (SUBLANES, L), 1)
    m_prev = [m_scr[b][0:H, 0:1] for b in range(bsz)]
    pairs = [(j, b) for j in range(n_sub) for b in range(bsz)]

    irow, brow, cm = {}, {}, {}
    for j, b in pairs:
        g = gt_ref[b, :, j * L:(j + 1) * L]
        g2 = jnp.where(grow >= H, _log_sigmoid(g), g)
        p1, p2, p3 = _split3(g2)
        brow[j, b] = (_dot(p1, utri) + _dot(p2, utri) + _dot(p3, utri))[H:2 * H]
        irow[j, b] = g[0:H]
    for j, b in pairs:
        ib = irow[j, b] - brow[j, b]
        x = jnp.concatenate([ib, ib], axis=0)
        sft = 1
        while sft < L:
            x = jnp.maximum(x, jnp.where(lane_row >= sft, pltpu.roll(x, sft, 1), -jnp.inf))
            sft *= 2
        cm[j, b] = x[0:H]
    for j, b in pairs:
        mp = m_prev[b]
        mm = jnp.maximum(cm[j, b], mp)
        b_last = brow[j, b][:, L - 1:L]
        m_next = b_last + jnp.maximum(mp, cm[j, b][:, L - 1:L])
        planes = (mm, jnp.exp(mp - mm), jnp.exp(-(brow[j, b] + mm)),
                  jnp.exp(b_last - brow[j, b] + irow[j, b] - m_next), irow[j, b] - brow[j, b],
                  jnp.broadcast_to(jnp.exp(b_last + mp - m_next), (H, L)))
        for k, rows in enumerate(planes):
            out_ref[k, b, :, j * L:(j + 1) * L] = jnp.concatenate([rows, rows], axis=0)
        m_prev[b] = m_next
    for b in range(bsz):
        m_scr[b] = jnp.broadcast_to(jnp.concatenate([m_prev[b], m_prev[b]], axis=0),
                                    (SUBLANES, LANES))


def _gates(gt, bsz, seq, chunk):
    n_sub = min(GATE_CHUNKS, seq // chunk)
    blk = n_sub * chunk
    return pl.pallas_call(
        functools.partial(_gates_kernel, bsz=bsz, chunk=chunk, n_sub=n_sub),
        grid=(seq // blk,),
        in_specs=[pl.BlockSpec((bsz, 16, blk), lambda c: (0, 0, c))],
        out_specs=pl.BlockSpec((6, bsz, SUBLANES, blk), lambda c: (0, 0, 0, c)),
        out_shape=jax.ShapeDtypeStruct((6, bsz, SUBLANES, seq), F32),
        scratch_shapes=[pltpu.VMEM((bsz, SUBLANES, LANES), F32)],
        compiler_params=pltpu.CompilerParams(
            dimension_semantics=("arbitrary",), vmem_limit_bytes=VMEM_LIMIT),
        name="gates",
    )(gt)


def _mlstm_kernel(qk_ref, v_ref, o_ref, gr_ref, cw_ref, cb_ref, ng_ref, y_ref,
                  tail_scr, c_scr, *, bsz, chunk, width):
    L = chunk
    H = ML_HEADS
    dh = width // H
    step = pl.program_id(0)
    assert L == LANES and dh == LANES, "column replication below uses one 128x128 tile per head"

    @pl.when(step == 0)
    def _():
        tail_scr[...] = jnp.zeros_like(tail_scr)
        c_scr[...] = jnp.zeros_like(c_scr)

    rr = lax.broadcasted_iota(I32, (L, L), 0)
    cc = lax.broadcasted_iota(I32, (L, L), 1)
    causal = cc <= rr
    eye = (rr == cc).astype(BF16)
    ones_blk = jnp.ones((L, dh), BF16)
    scale = 1.0 / math.sqrt(dh)
    inv_dh = 1.0 / dh
    n_str = bsz * H
    streams = [(b, hd) for b in range(bsz) for hd in range(H)]

    def replicate(rows):
        p = jnp.concatenate([jnp.broadcast_to(rows[h:h + 1], (LANES, L)) for h in range(H)], axis=0)
        p1 = p.astype(BF16)
        p2 = (p - p1.astype(F32)).astype(BF16)
        return _dot_nt(eye, p1) + _dot_nt(eye, p2)

    def rowsum(x):
        hi = x.astype(BF16)
        lo = (x - hi.astype(F32)).astype(BF16)
        return _dot(hi, ones_blk) + _dot(lo, ones_blk)

    tails = [tail_scr[b] for b in range(bsz)]
    caugs = [c_scr[i] for i in range(n_str)]
    reps = [[replicate(gr_ref[k, b][0:H]) for b in range(bsz)] for k in range(4)]
    col = lambda k: [reps[k][b][:, hd * LANES:(hd + 1) * LANES] for b, hd in streams]
    mm_rep, w_inter, enm, ws_rep = col(0), col(1), col(2), col(3)
    ib_row = [gr_ref[4, b][hd:hd + 1] for b, hd in streams]
    decays = [gr_ref[5, b][hd:hd + 1, 0:1] for b, hd in streams]

    q_all, k_all, new_tails = [], [], []
    for b in range(bsz):
        xqk = qk_ref[b].astype(F32)
        ext = jnp.concatenate([tails[b], xqk], axis=0)
        acc = jnp.broadcast_to(cb_ref[...], xqk.shape)
        for j in range(CONV_WIDTH):
            back = CONV_WIDTH - 1 - j
            sh = ext if back == 0 else pltpu.roll(ext, back, 0)
            acc = acc + cw_ref[j:j + 1, :] * sh[SUBLANES:]
        new_tails.append(xqk[L - SUBLANES:])
        qkc = acc * _sigmoid(acc)
        q_all.append(qkc[:, :width].astype(BF16))
        k_all.append((qkc[:, width:] * scale).astype(BF16))

    q_h = [q_all[b][:, hd * dh:(hd + 1) * dh] for b, hd in streams]
    k_h = [k_all[b][:, hd * dh:(hd + 1) * dh] for b, hd in streams]
    vaug = [jnp.concatenate([v_ref[b][:, hd * dh:(hd + 1) * dh], ones_blk], axis=1)
            for b, hd in streams]
    scores = [_dot_nt(q_h[i], k_h[i]) for i in range(n_str)]
    inter = [_dot(q_h[i], caugs[i].astype(BF16)) for i in range(n_str)]
    s_bf = [(scores[i] * jnp.exp(jnp.where(causal, ib_row[i] - mm_rep[i], -jnp.inf))).astype(BF16)
            for i in range(n_str)]
    kw = [(k_h[i].astype(F32) * ws_rep[i]).astype(BF16) for i in range(n_str)]
    intra = [_dot(s_bf[i], vaug[i]) for i in range(n_str)]
    upd = [_dot_tn(kw[i], vaug[i]) for i in range(n_str)]

    hh = []
    for i in range(n_str):
        num = w_inter[i] * inter[i][:, :dh] + intra[i][:, :dh]
        den = w_inter[i] * inter[i][:, dh:] + intra[i][:, dh:]
        hh.append(num / jnp.maximum(jnp.abs(den), enm[i]))
    mu = [rowsum(h) * inv_dh for h in hh]
    ctr = [hh[i] - mu[i] for i in range(n_str)]
    var = [rowsum(c * c) * inv_dh for c in ctr]
    outs = []
    for i, (b, hd) in enumerate(streams):
        sl = slice(hd * dh, (hd + 1) * dh)
        hn = ctr[i] * lax.rsqrt(var[i] + LN_EPS) * ng_ref[:, sl]
        outs.append(hn * _sigmoid(o_ref[b][:, sl].astype(F32)))

    for b in range(bsz):
        tail_scr[b] = new_tails[b]
        y_ref[b] = jnp.concatenate(outs[b * H:(b + 1) * H], axis=1).astype(BF16)
    for i in range(n_str):
        c_scr[i] = decays[i] * caugs[i] + upd[i]


def _mlstm(qk, v, o, gt, conv_w, conv_b, norm_g, bsz, seq):
    width = v.shape[-1]
    L = min(ML_CHUNK, seq)
    dh = width // ML_HEADS
    qk3 = qk.reshape(bsz, seq, 2 * width)
    v3 = v.reshape(bsz, seq, width)
    o3 = o.reshape(bsz, seq, width)
    gate_rows = _gates(gt, bsz, seq, L)
    full = lambda c: (0, 0)
    return pl.pallas_call(
        functools.partial(_mlstm_kernel, bsz=bsz, chunk=L, width=width),
        grid=(seq // L,),
        in_specs=[
            pl.BlockSpec((bsz, L, 2 * width), lambda c: (0, c, 0)),
            pl.BlockSpec((bsz, L, width), lambda c: (0, c, 0)),
            pl.BlockSpec((bsz, L, width), lambda c: (0, c, 0)),
            pl.BlockSpec((6, bsz, SUBLANES, L), lambda c: (0, 0, 0, c)),
            pl.BlockSpec((CONV_WIDTH, 2 * width), full),
            pl.BlockSpec((1, 2 * width), full),
            pl.BlockSpec((1, width), full),
        ],
        out_specs=pl.BlockSpec((bsz, L, width), lambda c: (0, c, 0)),
        out_shape=jax.ShapeDtypeStruct((bsz, seq, width), BF16),
        scratch_shapes=[
            pltpu.VMEM((bsz, SUBLANES, 2 * width), F32),
            pltpu.VMEM((bsz * ML_HEADS, dh, 2 * dh), F32),
        ],
        compiler_params=pltpu.CompilerParams(
            dimension_semantics=("arbitrary",), vmem_limit_bytes=VMEM_LIMIT),
        name="mlstm",
    )(qk3, v3, o3, gate_rows, conv_w.astype(F32), conv_b.reshape(1, -1).astype(F32),
      norm_g.reshape(1, -1).astype(F32))


def _post_kernel(x_ref, g_ref, yml_ref, wglu_ref, bglu_ref, wout_ref, nffn_ref, wrt_ref, br_ref,
                 h2_ref, xn_ref, eidx_ref, gate_ref, rank_ref, tcnt_ref, carry_scr):
    step = pl.program_id(0)

    @pl.when(step == 0)
    def _():
        carry_scr[...] = jnp.zeros_like(carry_scr)

    g = g_ref[...]
    z = _dot(g, wglu_ref[...]) + bglu_ref[...]
    s5 = g.astype(F32) * _sigmoid(z)
    mix = jnp.concatenate([s5.astype(BF16), yml_ref[...]], axis=1)
    h2 = x_ref[...] + _dot(mix, wout_ref[...])
    h2_ref[...] = h2
    ms = jnp.mean(h2 * h2, axis=-1, keepdims=True)
    xn = h2 * lax.rsqrt(ms + RMS_EPS) * nffn_ref[...]
    xb = xn.astype(BF16)
    xn_ref[...] = xb

    tm = xn.shape[0]
    logits = _dot_nt(wrt_ref[...], xb) + br_ref[...]
    eio = lax.broadcasted_iota(I32, (N_EXPERTS, tm), 0).astype(F32)
    vals = logits
    onehot = jnp.zeros((N_EXPERTS, tm), F32)
    idxs, tops = [], []
    for _ in range(TOP_K):
        mx = jnp.max(vals, axis=0, keepdims=True)
        idx = jnp.min(jnp.where(vals == mx, eio, float(N_EXPERTS)), axis=0, keepdims=True)
        sel = eio == idx
        onehot = onehot + sel.astype(F32)
        vals = jnp.where(sel, -jnp.inf, vals)
        idxs.append(idx)
        tops.append(mx)
    exps = [jnp.exp(t - tops[0]) for t in tops]
    tot = exps[0] + exps[1] + exps[2] + exps[3]
    pad_f = jnp.zeros((SUBLANES - TOP_K, tm), F32)
    eidx_ref[...] = jnp.concatenate(idxs + [pad_f], axis=0).astype(I32)
    gate_ref[...] = jnp.concatenate([e / tot for e in exps] + [pad_f], axis=0)

    rr = lax.broadcasted_iota(I32, (tm, tm), 0)
    cc = lax.broadcasted_iota(I32, (tm, tm), 1)
    before = (rr < cc).astype(BF16)
    carry = carry_scr[:, 0:1]
    rank_ex = _dot(onehot.astype(BF16), before) + carry
    ranks = [jnp.sum(jnp.where(eio == i, rank_ex, 0.0), axis=0, keepdims=True) for i in idxs]
    rank_ref[...] = jnp.concatenate(ranks + [pad_f], axis=0).astype(I32)
    tile_cnt = jnp.sum(onehot, axis=1, keepdims=True)
    carry_scr[...] = jnp.broadcast_to(carry + tile_cnt, carry_scr.shape)
    for s in range(tm // DISPATCH_TILE):
        sub = onehot[:, s * DISPATCH_TILE:(s + 1) * DISPATCH_TILE]
        tcnt_ref[0, s] = jnp.broadcast_to(jnp.sum(sub, axis=1, keepdims=True), carry_scr.shape)


def _post(x2d, g_nat, y_ml, w_glu, b_glu, w_out, norm_g, w_router, b_router):
    n_tok, d = x2d.shape
    s5w = g_nat.shape[-1]
    mlw = y_ml.shape[-1]
    tm = min(PROJ_TILE, n_tok)
    full = lambda i: (0, 0)
    row = lambda i: (i, 0)
    colb = lambda i: (0, i)
    return pl.pallas_call(
        _post_kernel,
        grid=(n_tok // tm,),
        in_specs=[
            pl.BlockSpec((tm, d), row),
            pl.BlockSpec((tm, s5w), row),
            pl.BlockSpec((tm, mlw), row),
            pl.BlockSpec((s5w, s5w), full),
            pl.BlockSpec((1, s5w), full),
            pl.BlockSpec((d, d), full),
            pl.BlockSpec((1, d), full),
            pl.BlockSpec((N_EXPERTS, d), full),
            pl.BlockSpec((N_EXPERTS, 1), full),
        ],
        out_specs=[
            pl.BlockSpec((tm, d), row),
            pl.BlockSpec((tm, d), row),
            pl.BlockSpec((SUBLANES, tm), colb),
            pl.BlockSpec((SUBLANES, tm), colb),
            pl.BlockSpec((SUBLANES, tm), colb),
            pl.BlockSpec((1, tm // DISPATCH_TILE, N_EXPERTS, LANES), lambda i: (i, 0, 0, 0)),
        ],
        out_shape=[
            jax.ShapeDtypeStruct((n_tok, d), F32),
            jax.ShapeDtypeStruct((n_tok, d), BF16),
            jax.ShapeDtypeStruct((SUBLANES, n_tok), I32),
            jax.ShapeDtypeStruct((SUBLANES, n_tok), F32),
            jax.ShapeDtypeStruct((SUBLANES, n_tok), I32),
            jax.ShapeDtypeStruct((n_tok // tm, tm // DISPATCH_TILE, N_EXPERTS, LANES), F32),
        ],
        scratch_shapes=[pltpu.VMEM((N_EXPERTS, LANES), F32)],
        compiler_params=pltpu.CompilerParams(
            dimension_semantics=("arbitrary",), vmem_limit_bytes=VMEM_LIMIT),
        name="post_router",
    )(x2d, g_nat, y_ml, w_glu.astype(BF16), b_glu.reshape(1, -1).astype(F32), w_out.astype(BF16),
      norm_g.reshape(1, -1).astype(F32), w_router.T.astype(BF16),
      b_router.reshape(-1, 1).astype(F32))


def _slab_loop(base, nbig_ref, nsmall_ref, fn, n_slabs=N_EXPERTS):
    def per_expert(e, carry):
        idx = base + e
        nbig = nbig_ref[idx]

        def big(j, c2):
            fn(idx, j * BIG_CHUNK, BIG_CHUNK)
            return c2

        def small(j, c2):
            fn(idx, nbig * BIG_CHUNK + j * SLAB_ALIGN, SLAB_ALIGN)
            return c2

        lax.fori_loop(0, nbig, big, 0)
        lax.fori_loop(0, nsmall_ref[idx], small, 0)
        return carry

    lax.fori_loop(0, n_slabs, per_expert, 0)


def _plan_copies(step, plan, fn):
    nbig_ref, nsmall_ref, _, bsrc_ref, bdst_ref, ssrc_ref, sdst_ref = plan

    def big(j, carry):
        k = step * MAX_BIG + j
        fn(bsrc_ref[k], bdst_ref[k], BIG_CHUNK)
        return carry

    def small(j, carry):
        k = step * MAX_SMALL + j
        fn(ssrc_ref[k], sdst_ref[k], SLAB_ALIGN)
        return carry

    lax.fori_loop(0, nbig_ref[step], big, 0)
    lax.fori_loop(0, nsmall_ref[step], small, 0)


def _plan_wait(step, plan, copy):
    copy(0, 0, DISPATCH_TILE * TOP_K).wait()

    def one(j, carry):
        copy(0, 0, SLAB_ALIGN).wait()
        return carry

    lax.fori_loop(0, plan[2][step], one, 0)


def _unpack_pairs(words):
    lo = lax.bitcast_convert_type(words << 16, F32).astype(BF16)
    hi = lax.bitcast_convert_type(words & jnp.uint32(0xFFFF0000), F32).astype(BF16)
    return lo, hi


def _pack_pairs(lo_f32, hi_f32):
    lo = lax.bitcast_convert_type(lo_f32, U32) >> 16
    hi = lax.bitcast_convert_type(hi_f32, U32) & jnp.uint32(0xFFFF0000)
    return hi | lo


def _dispatch_kernel(*refs, n_steps):
    plan = refs[:7]
    znbig_ref, znsmall_ref, zrow_ref, xn_ref, pos_ref, xs_hbm, stage, zbuf, sem, zsem = refs[7:]
    i = pl.program_id(0)
    slot = i % 2
    tw, d = xn_ref.shape
    n_rows = stage.shape[1]
    half = d // 2

    def copy(s, stage_row, buf_row, rows):
        src = stage.at[s, pl.ds(pl.multiple_of(stage_row, SLAB_ALIGN), rows), :]
        dst = xs_hbm.at[pl.ds(pl.multiple_of(buf_row, SLAB_ALIGN), rows), :]
        return pltpu.make_async_copy(src, dst, sem.at[s])

    @pl.when(i == 0)
    def _():
        zbuf[...] = jnp.zeros_like(zbuf)

        def tail(idx, r0, rows):
            dst = xs_hbm.at[pl.ds(pl.multiple_of(zrow_ref[idx] + r0, SLAB_ALIGN), rows), :]
            return pltpu.make_async_copy(zbuf.at[pl.ds(0, rows), :], dst, zsem)

        _slab_loop(0, znbig_ref, znsmall_ref, lambda *a: tail(*a).start(), N_EXPERTS + 1)
        _slab_loop(0, znbig_ref, znsmall_ref, lambda *a: tail(*a).wait(), N_EXPERTS + 1)

    @pl.when(i >= 2)
    def _():
        _plan_wait(i - 2, plan, functools.partial(copy, slot))

    xt = xn_ref[...]
    blk = STAGE_BLOCK
    for r0 in range(0, n_rows, blk):
        rid = lax.broadcasted_iota(I32, (blk, tw), 0) + r0
        hit = rid == pos_ref[0:1, :]
        for k in range(1, TOP_K):
            hit = jnp.logical_or(hit, rid == pos_ref[k:k + 1, :])
        srt = _dot(jnp.where(hit, 1.0, 0.0).astype(BF16), xt)
        stage[slot, r0:r0 + blk, :] = _pack_pairs(srt[:, :half], srt[:, half:])

    _plan_copies(i, plan, lambda *a: copy(slot, *a).start())

    @pl.when(i == n_steps - 1)
    def _():
        if n_steps > 1:
            _plan_wait(i - 1, plan, functools.partial(copy, 1 - slot))
        _plan_wait(i, plan, functools.partial(copy, slot))


def _dispatch(xn, pos8, plan, tails, n_slots):
    n_tok, d = xn.shape
    tw = DISPATCH_TILE
    n_steps = n_tok // tw
    return pl.pallas_call(
        functools.partial(_dispatch_kernel, n_steps=n_steps),
        grid_spec=pltpu.PrefetchScalarGridSpec(
            num_scalar_prefetch=10,
            grid=(n_steps,),
            in_specs=[
                pl.BlockSpec((tw, d), lambda i, *_: (i, 0)),
                pl.BlockSpec((SUBLANES, tw), lambda i, *_: (0, i)),
            ],
            out_specs=pl.BlockSpec(memory_space=pl.ANY),
            scratch_shapes=[
                pltpu.VMEM((2, STAGE_ROWS, d // 2), U32),
                pltpu.VMEM((BIG_CHUNK, d // 2), U32),
                pltpu.SemaphoreType.DMA((2,)),
                pltpu.SemaphoreType.DMA(()),
            ],
        ),
        out_shape=jax.ShapeDtypeStruct((n_slots, d // 2), U32),
        compiler_params=pltpu.CompilerParams(
            dimension_semantics=("arbitrary",), vmem_limit_bytes=VMEM_LIMIT),
        name="dispatch",
    )(*plan, *tails, xn, pos8)


def _expert_kernel(te_ref, nu_ref, x_ref, wup_ref, bup_ref, wdn_ref, bdn_ref, y_ref,
                   wup_bf, wdn_bf, *, d_ff):
    i = pl.program_id(0)
    n_used = nu_ref[0]

    @pl.when(jnp.logical_and(i < n_used,
                             jnp.logical_or(i == 0, te_ref[i] != te_ref[jnp.maximum(i - 1, 0)])))
    def _():
        wup_bf[...] = wup_ref[0].astype(BF16)
        wdn_bf[...] = wdn_ref[0].astype(BF16)

    @pl.when(i < n_used)
    def _():
        lo, hi = _unpack_pairs(x_ref[...])
        half = lo.shape[1]
        h = _dot(lo, wup_bf[:half, :]) + _dot(hi, wup_bf[half:, :]) + bup_ref[0]
        gl = jnp.minimum(h[:, :d_ff], SWIGLU_LIMIT)
        lin = jnp.clip(h[:, d_ff:], -SWIGLU_LIMIT, SWIGLU_LIMIT)
        act = gl * _sigmoid(SWIGLU_ALPHA * gl) * (lin + 1.0)
        y = _dot(act.astype(BF16), wdn_bf[...]) + bdn_ref[0]
        yb = y.astype(BF16).astype(F32)
        y_ref[...] = _pack_pairs(yb[:, :half], yb[:, half:])

    @pl.when(i >= n_used)
    def _():
        y_ref[...] = jnp.zeros_like(y_ref)


def _experts(xs, tile_e, n_used, w_up, b_up, w_down, b_down):
    n_slots, half = xs.shape
    d = 2 * half
    tm = EXPERT_TILE
    n_tiles = n_slots // tm
    d_ff = w_down.shape[1]
    return pl.pallas_call(
        functools.partial(_expert_kernel, d_ff=d_ff),
        grid_spec=pltpu.PrefetchScalarGridSpec(
            num_scalar_prefetch=2,
            grid=(n_tiles,),
            in_specs=[
                pl.BlockSpec((tm, half), lambda i, te, nu: (jnp.minimum(i, nu[0] - 1), 0)),
                pl.BlockSpec((1, d, 2 * d_ff), lambda i, te, nu: (te[i], 0, 0)),
                pl.BlockSpec((1, 1, 2 * d_ff), lambda i, te, nu: (te[i], 0, 0)),
                pl.BlockSpec((1, d_ff, d), lambda i, te, nu: (te[i], 0, 0)),
                pl.BlockSpec((1, 1, d), lambda i, te, nu: (te[i], 0, 0)),
            ],
            out_specs=pl.BlockSpec((tm, half), lambda i, te, nu: (i, 0)),
            scratch_shapes=[
                pltpu.VMEM((d, 2 * d_ff), BF16),
                pltpu.VMEM((d_ff, d), BF16),
            ],
        ),
        out_shape=jax.ShapeDtypeStruct((n_slots, half), U32),
        compiler_params=pltpu.CompilerParams(
            dimension_semantics=("arbitrary",), vmem_limit_bytes=VMEM_LIMIT),
        name="experts",
    )(tile_e, n_used, xs, w_up, b_up.reshape(N_EXPERTS, 1, -1),
      w_down, b_down.reshape(N_EXPERTS, 1, -1))


def _combine_kernel(*refs, n_steps):
    plan = refs[:7]
    ys_hbm, h2_ref, posc_ref, gatec_ref, ng_ref, out_ref, stage, sem = refs[7:]
    i = pl.program_id(0)
    n_slot = stage.shape[0]
    slot = i % n_slot
    tw, d = h2_ref.shape
    n_rows = stage.shape[1]

    def copy(s, stage_row, buf_row, rows):
        src = ys_hbm.at[pl.ds(pl.multiple_of(buf_row, SLAB_ALIGN), rows), :]
        dst = stage.at[s, pl.ds(pl.multiple_of(stage_row, SLAB_ALIGN), rows), :]
        return pltpu.make_async_copy(src, dst, sem.at[s])

    @pl.when(i == 0)
    def _():
        stage[...] = jnp.zeros_like(stage)
        for t in range(min(n_slot - 1, n_steps)):
            _plan_copies(t, plan, lambda *a, t=t: copy(t, *a).start())

    ahead = i + n_slot - 1

    @pl.when(ahead < n_steps)
    def _():
        _plan_copies(ahead, plan, lambda *a: copy(ahead % n_slot, *a).start())

    _plan_wait(i, plan, functools.partial(copy, slot))

    y_lo, y_hi = _unpack_pairs(stage[slot])
    blk = 128
    for t0 in range(0, tw, blk):
        cid = lax.broadcasted_iota(I32, (blk, n_rows), 1)
        pg = jnp.zeros((blk, n_rows), F32)
        for k in range(TOP_K):
            pg = jnp.where(cid == posc_ref[t0:t0 + blk, k:k + 1],
                           gatec_ref[t0:t0 + blk, k:k + 1], pg)
        pgb = pg.astype(BF16)
        moe = jnp.concatenate([_dot(pgb, y_lo), _dot(pgb, y_hi)], axis=1)
        acc = h2_ref[t0:t0 + blk, :] + moe
        ms = jnp.mean(acc * acc, axis=-1, keepdims=True)
        out_ref[t0:t0 + blk, :] = acc * lax.rsqrt(ms + RMS_EPS) * ng_ref[...]


def _combine(ys, h2, pos_cols, gate_cols, plan, norm_g):
    n_tok, d = h2.shape
    tw = DISPATCH_TILE
    n_steps = n_tok // tw
    return pl.pallas_call(
        functools.partial(_combine_kernel, n_steps=n_steps),
        grid_spec=pltpu.PrefetchScalarGridSpec(
            num_scalar_prefetch=7,
            grid=(n_steps,),
            in_specs=[
                pl.BlockSpec(memory_space=pl.ANY),
                pl.BlockSpec((tw, d), lambda i, *_: (i, 0)),
                pl.BlockSpec((tw, SUBLANES), lambda i, *_: (i, 0)),
                pl.BlockSpec((tw, SUBLANES), lambda i, *_: (i, 0)),
                pl.BlockSpec((1, d), lambda i, *_: (0, 0)),
            ],
            out_specs=pl.BlockSpec((tw, d), lambda i, *_: (i, 0)),
            scratch_shapes=[
                pltpu.VMEM((3, STAGE_ROWS, d // 2), U32),
                pltpu.SemaphoreType.DMA((3,)),
            ],
        ),
        out_shape=jax.ShapeDtypeStruct((n_tok, d), F32),
        compiler_params=pltpu.CompilerParams(
            dimension_semantics=("arbitrary",), vmem_limit_bytes=VMEM_LIMIT),
        name="combine",
    )(*plan, ys, h2, pos_cols, gate_cols, norm_g.reshape(1, -1).astype(F32))


def _moe(h2, xn, eidx, gate, rank, tcnt, w_up, b_up, w_down, b_down, norm_final_g):
    n_tok, d = h2.shape
    tm = EXPERT_TILE
    tw = DISPATCH_TILE
    n_steps = n_tok // tw
    n_exp = N_EXPERTS
    e_ids = jnp.arange(n_exp, dtype=I32)
    al = SLAB_ALIGN
    tile_cnt = tcnt.reshape(n_steps, n_exp, LANES)[:, :, 0].astype(I32)
    slab_rows = (tile_cnt + al - 1) // al * al
    exp_rows = jnp.sum(slab_rows, axis=0)
    region = (exp_rows + tm - 1) // tm * tm
    pad_end = jnp.cumsum(region)
    pad_start = pad_end - region
    n_slots = (n_tok * TOP_K + n_steps * n_exp * (al - 1) + n_exp * (tm - 1) + tm - 1) // tm * tm
    n_tiles = n_slots // tm

    tile_carry = jnp.cumsum(tile_cnt, axis=0) - tile_cnt
    tile_off = jnp.cumsum(slab_rows, axis=1) - slab_rows
    slab_row0 = pad_start[None, :] + jnp.cumsum(slab_rows, axis=0) - slab_rows

    def chunks(rows):
        return rows // BIG_CHUNK, rows % BIG_CHUNK // al

    def flat(cnt, src0, dst0, size, n_max):
        run = jnp.cumsum(cnt, axis=1)
        j = jnp.arange(n_max, dtype=I32)
        e_of = jnp.sum(run[:, None, :] <= j[None, :, None], axis=-1)
        sel = e_of[:, :, None] == e_ids[None, None, :]
        pick = lambda a: jnp.sum(jnp.where(sel, a[:, None, :], 0), axis=-1)
        local = (j[None, :] - pick(run - cnt)) * size
        return run[:, -1], (pick(src0) + local).reshape(-1), (pick(dst0) + local).reshape(-1)

    nbig, nsmall = chunks(slab_rows)
    big_tot, big_src, big_dst = flat(nbig, tile_off, slab_row0, BIG_CHUNK, MAX_BIG)
    small_tot, small_src, small_dst = flat(nsmall, tile_off + nbig * BIG_CHUNK,
                                           slab_row0 + nbig * BIG_CHUNK, al, MAX_SMALL)
    pad_groups = (jnp.sum(slab_rows, axis=1) - tw * TOP_K) // al
    plan = (big_tot, small_tot, pad_groups, big_src, big_dst, small_src, small_dst)
    tail_rows = jnp.concatenate([region - exp_rows, n_slots - pad_end[-1:]])
    tails = (*chunks(tail_rows), jnp.concatenate([pad_start + exp_rows, pad_end[-1:]]))

    base_t = jnp.repeat((tile_off - tile_carry).T, tw, axis=1)
    e_sel = eidx[:TOP_K][None] == e_ids[:, None, None]
    pos = jnp.sum(jnp.where(e_sel, base_t[:, None, :], 0), axis=0) + rank[:TOP_K]
    pos8 = jnp.concatenate([pos, jnp.full((SUBLANES - TOP_K, n_tok), -1, I32)], axis=0)

    tile_start = jnp.arange(n_tiles, dtype=I32) * tm
    tile_e = jnp.minimum(jnp.sum(pad_end[None, :] <= tile_start[:, None], axis=-1),
                         n_exp - 1).astype(I32)
    n_used = (pad_end[-1] // tm).astype(I32).reshape(1)

    xs = _dispatch(xn, pos8, plan, tails, n_slots)
    ys = _experts(xs, tile_e, n_used, w_up, b_up, w_down, b_down)
    return _combine(ys, h2, pos8.T, gate.T, plan, norm_final_g)


def kernel(x, norm_mix_g, w_in, s5_log_dt, s5_a_re, s5_a_im, s5_b_re, s5_b_im, s5_c_re, s5_c_im,
           s5_d, s5_w_glu, s5_b_glu, ml_conv_w, ml_conv_b, ml_b_gates, ml_norm_g, w_out,
           norm_ffn_g, w_router, b_router, w_up, b_up, w_down, b_down, norm_final_g):
    bsz, seq, d = x.shape
    depth = w_in.shape[0]
    assert depth == 1, "single-layer block"
    l = 0
    x2d = x.reshape(bsz * seq, d)
    qk, v, o, ut, gt = _in_proj(x2d, norm_mix_g[l], w_in[l], ml_b_gates[l], bsz, seq)
    g_t = _s5(ut, s5_log_dt[l], s5_a_re[l], s5_a_im[l], s5_b_re[l], s5_b_im[l],
              s5_c_re[l], s5_c_im[l], s5_d[l])
    g_nat = jnp.swapaxes(g_t, 1, 2).reshape(bsz * seq, -1)
    y_ml = _mlstm(qk, v, o, gt, ml_conv_w[l], ml_conv_b[l], ml_norm_g[l], bsz, seq)
    y_ml = y_ml.reshape(bsz * seq, -1)
    h2, xn, eidx, gate, rank, tcnt = _post(x2d, g_nat, y_ml, s5_w_glu[l], s5_b_glu[l], w_out[l],
                                           norm_ffn_g[l], w_router[l], b_router[l])
    out = _moe(h2, xn, eidx, gate, rank, tcnt, w_up[l], b_up[l], w_down[l], b_down[l],
               norm_final_g)
    return out.reshape(bsz, seq, d)
```

```python
import functools
import math

import jax
import jax.numpy as jnp
from jax import lax
from jax.experimental import pallas as pl
from jax.experimental.pallas import tpu as pltpu

F32 = jnp.float32
BF16 = jnp.bfloat16
I32 = jnp.int32
U32 = jnp.uint32

S5_GROUP = 16
S5_STATE = 64
ML_HEADS = 4
CONV_WIDTH = 4
N_EXPERTS = 32
TOP_K = 4
SWIGLU_LIMIT = 7.0
SWIGLU_ALPHA = 1.702
RMS_EPS = 1e-5
LN_EPS = 1e-6

LANES = 128
SUBLANES = 8
S5_CHUNK = LANES
ML_CHUNK = 128
GATE_CHUNKS = 8
PROJ_TILE = 512
EXPERT_TILE = 512
DISPATCH_TILE = 256
SLAB_ALIGN = SUBLANES
BIG_CHUNK = 32
STAGE_ROWS = DISPATCH_TILE * TOP_K + N_EXPERTS * SLAB_ALIGN
STAGE_BLOCK = STAGE_ROWS // 2
MAX_BIG = STAGE_ROWS // BIG_CHUNK
MAX_SMALL = N_EXPERTS * (BIG_CHUNK // SLAB_ALIGN - 1)
VMEM_LIMIT = 56 * 1024 * 1024

_NT = (((1,), (1,)), ((), ()))
_TN = (((0,), (0,)), ((), ()))


def _dot(a, b):
    return jnp.dot(a, b, preferred_element_type=F32)


def _dot_nt(a, b):
    return lax.dot_general(a, b, _NT, preferred_element_type=F32)


def _dot_tn(a, b):
    return lax.dot_general(a, b, _TN, preferred_element_type=F32)


def _split3(x):
    p1 = x.astype(BF16)
    r1 = x - p1.astype(F32)
    p2 = r1.astype(BF16)
    r2 = r1 - p2.astype(F32)
    return p1, p2, r2.astype(BF16)


def _sigmoid(x):
    return 0.5 * jnp.tanh(0.5 * x) + 0.5


def _inproj_kernel(x_ref, g_ref, wnat_ref, wut_ref, wgt_ref, bg_ref,
                   qk_ref, v_ref, o_ref, ut_ref, gt_ref, *, width):
    x = x_ref[...]
    ms = jnp.mean(x * x, axis=-1, keepdims=True)
    hn = (x * lax.rsqrt(ms + RMS_EPS) * g_ref[...]).astype(BF16)
    nat = _dot(hn, wnat_ref[...])
    qk_ref[...] = nat[:, :2 * width].astype(BF16)
    v_ref[...] = nat[:, 2 * width:3 * width].astype(BF16)
    o_ref[...] = nat[:, 3 * width:].astype(BF16)
    ut_ref[0] = _dot_nt(wut_ref[...], hn).astype(BF16)
    gt_ref[0] = _dot_nt(wgt_ref[...], hn) + bg_ref[...]


def _in_proj(x2d, norm_g, w_in, b_gates, bsz, seq):
    n_tok, d = x2d.shape
    s5w = d // 2
    mlw = d - s5w
    tm = min(PROJ_TILE, seq)
    tpb = seq // tm
    w_bf = w_in.astype(BF16)
    w_nat = w_bf[:, s5w:s5w + 4 * mlw]
    w_ut = w_bf[:, :s5w].T
    n_gate = 2 * ML_HEADS
    w_gt = jnp.zeros((16, d), BF16).at[:n_gate].set(w_bf[:, s5w + 4 * mlw:].T)
    b_g = jnp.zeros((16, 1), F32).at[:n_gate, 0].set(b_gates.astype(F32))
    grid = (n_tok // tm,)
    full = lambda i: (0, 0)
    return pl.pallas_call(
        functools.partial(_inproj_kernel, width=mlw),
        grid=grid,
        in_specs=[
            pl.BlockSpec((tm, d), lambda i: (i, 0)),
            pl.BlockSpec((1, d), full),
            pl.BlockSpec((d, 4 * mlw), full),
            pl.BlockSpec((s5w, d), full),
            pl.BlockSpec((16, d), full),
            pl.BlockSpec((16, 1), full),
        ],
        out_specs=[
            pl.BlockSpec((tm, 2 * mlw), lambda i: (i, 0)),
            pl.BlockSpec((tm, mlw), lambda i: (i, 0)),
            pl.BlockSpec((tm, mlw), lambda i: (i, 0)),
            pl.BlockSpec((1, s5w, tm), lambda i: (i // tpb, 0, i % tpb)),
            pl.BlockSpec((1, 16, tm), lambda i: (i // tpb, 0, i % tpb)),
        ],
        out_shape=[
            jax.ShapeDtypeStruct((n_tok, 2 * mlw), BF16),
            jax.ShapeDtypeStruct((n_tok, mlw), BF16),
            jax.ShapeDtypeStruct((n_tok, mlw), BF16),
            jax.ShapeDtypeStruct((bsz, s5w, seq), BF16),
            jax.ShapeDtypeStruct((bsz, 16, seq), F32),
        ],
        compiler_params=pltpu.CompilerParams(
            dimension_semantics=("parallel",), vmem_limit_bytes=VMEM_LIMIT),
        name="in_proj",
    )(x2d, norm_g.reshape(1, d).astype(F32), w_nat, w_ut, w_gt, b_g)


def _s5_kernel(d_ref, x_ref, prm_ref, c_ref, bt_ref, out_ref, *, bsz, nc):
    L = S5_CHUNK
    P = S5_GROUP
    N = S5_STATE
    grp = pl.program_id(0)
    lane = lax.broadcasted_iota(I32, (1, 2 * N), 1)
    lo = lane < N
    a_re = jnp.minimum(prm_ref[0, 0:1, :], -1e-4)
    a_im = prm_ref[0, 1:2, :]
    dt = jnp.exp(prm_ref[0, 2:3, :])
    zr = dt * a_re
    zi = dt * a_im

    quarter = jnp.where(lo, 0.0, 0.5 * math.pi)
    quarter_sw = jnp.where(lo, 0.5 * math.pi, 0.0)

    def powtab(e, swap=False):
        return jnp.exp(e * zr) * jnp.cos(e * zi - (quarter_sw if swap else quarter))

    er = jnp.exp(zr) * jnp.cos(zi)
    ei = jnp.exp(zr) * jnp.sin(zi)
    den = a_re * a_re + a_im * a_im
    coef_r = ((er - 1.0) * a_re + ei * a_im) / den
    coef_i = (ei * a_re - (er - 1.0) * a_im) / den
    c_r = c_ref[0, 0]
    c_i = c_ref[0, 1]
    bb_r = coef_r * bt_ref[0, 0] - coef_i * bt_ref[0, 1]
    bb_i = coef_r * bt_ref[0, 1] + coef_i * bt_ref[0, 0]

    tau = lax.broadcasted_iota(I32, (L, 1), 0).astype(F32)

    cb_rows = []
    for q in range(P):
        cbr = c_r * bb_r[q:q + 1] - c_i * bb_i[q:q + 1]
        cbi = c_r * bb_i[q:q + 1] + c_i * bb_r[q:q + 1]
        cb_rows.append(jnp.where(lo, cbr, -cbi))
    cb = jnp.concatenate(cb_rows, axis=0)
    pt0 = powtab(tau)
    c1, c2, c3 = _split3(cb)
    t1, t2, t3 = _split3(pt0)
    kmat = (_dot_nt(c1, t1) + _dot_nt(c1, t2) + _dot_nt(c2, t1)
            + _dot_nt(c2, t2) + _dot_nt(c1, t3) + _dot_nt(c3, t1))

    pt_rev = powtab(L - 1.0 - tau)
    pt_rev_sw = powtab(L - 1.0 - tau, swap=True)
    f_rows = []
    for q in range(P):
        a1 = bb_r[q:q + 1]
        a2 = jnp.where(lo, -bb_i[q:q + 1], bb_i[q:q + 1])
        f_rows.append((a1 * pt_rev + a2 * pt_rev_sw).astype(BF16))
    fmat = jnp.concatenate(f_rows, axis=0)
    pt1 = powtab(tau + 1.0)
    pt1_sw = powtab(tau + 1.0, swap=True)
    e_rows = []
    for p in range(P):
        b1 = jnp.where(lo, c_r[p:p + 1], -c_r[p:p + 1])
        b2 = -c_i[p:p + 1]
        e_rows.append((b1 * pt1 + b2 * pt1_sw).astype(BF16))
    emat_t = jnp.concatenate(e_rows, axis=0)

    lhs = jnp.concatenate(
        [jnp.concatenate([x_ref[b, q] for q in range(P)], axis=1) for b in range(bsz)], axis=0)
    s_end = _dot(lhs, fmat)

    rr = lax.broadcasted_iota(I32, (L, L), 0)
    cc = lax.broadcasted_iota(I32, (L, L), 1)
    causal = cc >= rr
    y = None
    for q0 in range(0, P, 2):
        rows = []
        for q in (q0, q0 + 1):
            tiles = []
            for p in range(P):
                kb = jnp.broadcast_to(kmat[q * P + p:q * P + p + 1, :], (L, L))
                toe = pltpu.roll(kb, 0, 1, stride=1, stride_axis=0)
                tiles.append(jnp.where(causal, toe, 0.0).astype(BF16))
            rows.append(jnp.concatenate(tiles, axis=1))
        part = _dot(lhs[:, q0 * L:(q0 + 2) * L], jnp.concatenate(rows, axis=0))
        y = part if y is None else y + part

    m_rows = bsz * nc
    ridx = lax.broadcasted_iota(I32, (m_rows, 1), 0)
    cidx = ridx % nc
    h = jnp.where(cidx >= 1, pltpu.roll(s_end, 1, 0), 0.0)
    d = 1
    while d < nc:
        e = float(d * L)
        pr = jnp.exp(e * zr) * jnp.cos(e * zi)
        pi = jnp.exp(e * zr) * jnp.sin(e * zi)
        sh = jnp.where(cidx >= d, pltpu.roll(h, d, 0), 0.0)
        h = h + pr * sh + jnp.where(lo, -pi, pi) * pltpu.roll(sh, N, 1)
        d *= 2
    y = y + _dot_nt(h.astype(BF16), emat_t)

    for b in range(bsz):
        for p in range(P):
            yp = (y[b * nc:(b + 1) * nc, p * L:(p + 1) * L]
                  + d_ref[grp * P + p] * x_ref[b, p].astype(F32))
            out_ref[b, p] = jax.nn.gelu(yp).astype(BF16)


def _s5(ut, log_dt, a_re, a_im, b_re, b_im, c_re, c_im, d_skip):
    bsz, s5w, seq = ut.shape
    L = S5_CHUNK
    nc = seq // L
    groups = s5w // S5_GROUP
    n = S5_STATE
    x4 = ut.reshape(bsz, s5w, nc, L)
    dup = lambda t: jnp.concatenate([t, t], axis=-1).astype(F32)
    prm = jnp.zeros((groups, SUBLANES, 2 * n), F32)
    prm = prm.at[:, 0].set(dup(a_re)).at[:, 1].set(dup(a_im))
    prm = prm.at[:, 2].set(jnp.broadcast_to(log_dt.astype(F32)[:, None], (groups, 2 * n)))
    cpar = jnp.stack([dup(c_re), dup(c_im)], axis=1)
    btpar = jnp.stack([dup(jnp.swapaxes(b_re, 1, 2)), dup(jnp.swapaxes(b_im, 1, 2))], axis=1)
    out = pl.pallas_call(
        functools.partial(_s5_kernel, bsz=bsz, nc=nc),
        grid_spec=pltpu.PrefetchScalarGridSpec(
            num_scalar_prefetch=1,
            grid=(groups,),
            in_specs=[
                pl.BlockSpec((bsz, S5_GROUP, nc, L), lambda g, d: (0, g, 0, 0)),
                pl.BlockSpec((1, SUBLANES, 2 * n), lambda g, d: (g, 0, 0)),
                pl.BlockSpec((1, 2, S5_GROUP, 2 * n), lambda g, d: (g, 0, 0, 0)),
                pl.BlockSpec((1, 2, S5_GROUP, 2 * n), lambda g, d: (g, 0, 0, 0)),
            ],
            out_specs=pl.BlockSpec((bsz, S5_GROUP, nc, L), lambda g, d: (0, g, 0, 0)),
        ),
        out_shape=jax.ShapeDtypeStruct((bsz, s5w, nc, L), BF16),
        compiler_params=pltpu.CompilerParams(
            dimension_semantics=("parallel",), vmem_limit_bytes=VMEM_LIMIT),
        name="s5",
    )(d_skip.astype(F32), x4, prm, cpar, btpar)
    return out.reshape(bsz, s5w, seq)


def _log_sigmoid(x):
    return jnp.minimum(x, 0.0) - jnp.log(1.0 + jnp.exp(-jnp.abs(x)))


def _gates_kernel(gt_ref, out_ref, m_scr, *, bsz, chunk, n_sub):
    L = chunk
    H = ML_HEADS
    step = pl.program_id(0)

    @pl.when(step == 0)
    def _():
        m_scr[...] = jnp.zeros_like(m_scr)

    rr = lax.broadcasted_iota(I32, (L, L), 0)
    cc = lax.broadcasted_iota(I32, (L, L), 1)
    utri = (rr <= cc).astype(BF16)
    grow = lax.broadcasted_iota(I32, (16, 1), 0)
    lane_row = lax.broadcasted_iota(I32, (SUBLANES, L), 1)
    m_prev = [m_scr[b][0:H, 0:1] for b in range(bsz)]
    pairs = [(j, b) for j in range(n_sub) for b in range(bsz)]

    irow, brow, cm = {}, {}, {}
    for j, b in pairs:
        g = gt_ref[b, :, j * L:(j + 1) * L]
        g2 = jnp.where(grow >= H, _log_sigmoid(g), g)
        p1, p2, p3 = _split3(g2)
        brow[j, b] = (_dot(p1, utri) + _dot(p2, utri) + _dot(p3, utri))[H:2 * H]
        irow[j, b] = g[0:H]
    for j, b in pairs:
        ib = irow[j, b] - brow[j, b]
        x = jnp.concatenate([ib, ib], axis=0)
        sft = 1
        while sft < L:
            x = jnp.maximum(x, jnp.where(lane_row >= sft, pltpu.roll(x, sft, 1), -jnp.inf))
            sft *= 2
        cm[j, b] = x[0:H]
    for j, b in pairs:
        mp = m_prev[b]
        mm = jnp.maximum(cm[j, b], mp)
        b_last = brow[j, b][:, L - 1:L]
        m_next = b_last + jnp.maximum(mp, cm[j, b][:, L - 1:L])
        planes = (mm, jnp.exp(mp - mm), jnp.exp(-(brow[j, b] + mm)),
                  jnp.exp(b_last - brow[j, b] + irow[j, b] - m_next), irow[j, b] - brow[j, b],
                  jnp.broadcast_to(jnp.exp(b_last + mp - m_next), (H, L)))
        for k, rows in enumerate(planes):
            out_ref[k, b, :, j * L:(j + 1) * L] = jnp.concatenate([rows, rows], axis=0)
        m_prev[b] = m_next
    for b in range(bsz):
        m_scr[b] = jnp.broadcast_to(jnp.concatenate([m_prev[b], m_prev[b]], axis=0),
                                    (SUBLANES, LANES))


def _gates(gt, bsz, seq, chunk):
    n_sub = min(GATE_CHUNKS, seq // chunk)
    blk = n_sub * chunk
    return pl.pallas_call(
        functools.partial(_gates_kernel, bsz=bsz, chunk=chunk, n_sub=n_sub),
        grid=(seq // blk,),
        in_specs=[pl.BlockSpec((bsz, 16, blk), lambda c: (0, 0, c))],
        out_specs=pl.BlockSpec((6, bsz, SUBLANES, blk), lambda c: (0, 0, 0, c)),
        out_shape=jax.ShapeDtypeStruct((6, bsz, SUBLANES, seq), F32),
        scratch_shapes=[pltpu.VMEM((bsz, SUBLANES, LANES), F32)],
        compiler_params=pltpu.CompilerParams(
            dimension_semantics=("arbitrary",), vmem_limit_bytes=VMEM_LIMIT),
        name="gates",
    )(gt)


def _mlstm_kernel(qk_ref, v_ref, o_ref, gr_ref, cw_ref, cb_ref, ng_ref, y_ref,
                  tail_scr, c_scr, *, bsz, chunk, width):
    L = chunk
    H = ML_HEADS
    dh = width // H
    step = pl.program_id(0)
    assert L == LANES and dh == LANES, "column replication below uses one 128x128 tile per head"

    @pl.when(step == 0)
    def _():
        tail_scr[...] = jnp.zeros_like(tail_scr)
        c_scr[...] = jnp.zeros_like(c_scr)

    rr = lax.broadcasted_iota(I32, (L, L), 0)
    cc = lax.broadcasted_iota(I32, (L, L), 1)
    causal = cc <= rr
    eye = (rr == cc).astype(BF16)
    ones_blk = jnp.ones((L, dh), BF16)
    scale = 1.0 / math.sqrt(dh)
    inv_dh = 1.0 / dh
    n_str = bsz * H
    streams = [(b, hd) for b in range(bsz) for hd in range(H)]

    def replicate(rows, two_terms):
        p = jnp.concatenate([jnp.broadcast_to(rows[h:h + 1], (LANES, L)) for h in range(H)], axis=0)
        p1 = p.astype(BF16)
        if not two_terms:
            return _dot_nt(eye, p1)
        return _dot_nt(eye, p1) + _dot_nt(eye, (p - p1.astype(F32)).astype(BF16))

    def rowsum(x):
        hi = x.astype(BF16)
        lo = (x - hi.astype(F32)).astype(BF16)
        return _dot(hi, ones_blk) + _dot(lo, ones_blk)

    tails = [tail_scr[b] for b in range(bsz)]
    caugs = [c_scr[i] for i in range(n_str)]
    reps = [[replicate(gr_ref[k, b][0:H], k == 0) for b in range(bsz)] for k in range(4)]
    col = lambda k: [reps[k][b][:, hd * LANES:(hd + 1) * LANES] for b, hd in streams]
    mm_rep, w_inter, enm, ws_rep = col(0), col(1), col(2), col(3)
    ib_row = [gr_ref[4, b][hd:hd + 1] for b, hd in streams]
    decays = [gr_ref[5, b][hd:hd + 1, 0:1] for b, hd in streams]

    q_all, k_all, new_tails = [], [], []
    for b in range(bsz):
        xqk = qk_ref[b].astype(F32)
        ext = jnp.concatenate([tails[b], xqk], axis=0)
        acc = jnp.broadcast_to(cb_ref[...], xqk.shape)
        for j in range(CONV_WIDTH):
            back = CONV_WIDTH - 1 - j
            sh = ext if back == 0 else pltpu.roll(ext, back, 0)
            acc = acc + cw_ref[j:j + 1, :] * sh[SUBLANES:]
        new_tails.append(xqk[L - SUBLANES:])
        qkc = acc * _sigmoid(acc)
        q_all.append(qkc[:, :width].astype(BF16))
        k_all.append((qkc[:, width:] * scale).astype(BF16))

    q_h = [q_all[b][:, hd * dh:(hd + 1) * dh] for b, hd in streams]
    k_h = [k_all[b][:, hd * dh:(hd + 1) * dh] for b, hd in streams]
    vaug = [jnp.concatenate([v_ref[b][:, hd * dh:(hd + 1) * dh], ones_blk], axis=1)
            for b, hd in streams]
    scores = [_dot_nt(q_h[i], k_h[i]) for i in range(n_str)]
    inter = [_dot(q_h[i], caugs[i].astype(BF16)) for i in range(n_str)]
    s_bf = [(scores[i] * jnp.exp(jnp.where(causal, ib_row[i] - mm_rep[i], -jnp.inf))).astype(BF16)
            for i in range(n_str)]
    kw = [(k_h[i].astype(F32) * ws_rep[i]).astype(BF16) for i in range(n_str)]
    intra = [_dot(s_bf[i], vaug[i]) for i in range(n_str)]
    upd = [_dot_tn(kw[i], vaug[i]) for i in range(n_str)]

    hh = []
    for i in range(n_str):
        num = w_inter[i] * inter[i][:, :dh] + intra[i][:, :dh]
        den = w_inter[i] * inter[i][:, dh:] + intra[i][:, dh:]
        hh.append(num / jnp.maximum(jnp.abs(den), enm[i]))
    mu = [rowsum(h) * inv_dh for h in hh]
    ctr = [hh[i] - mu[i] for i in range(n_str)]
    var = [rowsum(c * c) * inv_dh for c in ctr]
    outs = []
    for i, (b, hd) in enumerate(streams):
        sl = slice(hd * dh, (hd + 1) * dh)
        hn = ctr[i] * lax.rsqrt(var[i] + LN_EPS) * ng_ref[:, sl]
        outs.append(hn * _sigmoid(o_ref[b][:, sl].astype(F32)))

    for b in range(bsz):
        tail_scr[b] = new_tails[b]
        y_ref[b] = jnp.concatenate(outs[b * H:(b + 1) * H], axis=1).astype(BF16)
    for i in range(n_str):
        c_scr[i] = decays[i] * caugs[i] + upd[i]


def _mlstm(qk, v, o, gt, conv_w, conv_b, norm_g, bsz, seq):
    width = v.shape[-1]
    L = min(ML_CHUNK, seq)
    dh = width // ML_HEADS
    qk3 = qk.reshape(bsz, seq, 2 * width)
    v3 = v.reshape(bsz, seq, width)
    o3 = o.reshape(bsz, seq, width)
    gate_rows = _gates(gt, bsz, seq, L)
    full = lambda c: (0, 0)
    return pl.pallas_call(
        functools.partial(_mlstm_kernel, bsz=bsz, chunk=L, width=width),
        grid=(seq // L,),
        in_specs=[
            pl.BlockSpec((bsz, L, 2 * width), lambda c: (0, c, 0)),
            pl.BlockSpec((bsz, L, width), lambda c: (0, c, 0)),
            pl.BlockSpec((bsz, L, width), lambda c: (0, c, 0)),
            pl.BlockSpec((6, bsz, SUBLANES, L), lambda c: (0, 0, 0, c)),
            pl.BlockSpec((CONV_WIDTH, 2 * width), full),
            pl.BlockSpec((1, 2 * width), full),
            pl.BlockSpec((1, width), full),
        ],
        out_specs=pl.BlockSpec((bsz, L, width), lambda c: (0, c, 0)),
        out_shape=jax.ShapeDtypeStruct((bsz, seq, width), BF16),
        scratch_shapes=[
            pltpu.VMEM((bsz, SUBLANES, 2 * width), F32),
            pltpu.VMEM((bsz * ML_HEADS, dh, 2 * dh), F32),
        ],
        compiler_params=pltpu.CompilerParams(
            dimension_semantics=("arbitrary",), vmem_limit_bytes=VMEM_LIMIT),
        name="mlstm",
    )(qk3, v3, o3, gate_rows, conv_w.astype(F32), conv_b.reshape(1, -1).astype(F32),
      norm_g.reshape(1, -1).astype(F32))


def _post_kernel(x_ref, gt_ref, yml_ref, wglut_ref, bglu_ref, wout_ref, nffn_ref, wrt_ref, br_ref,
                 h2_ref, xn_ref, eidx_ref, gate_ref, rank_ref, tcnt_ref, carry_scr):
    step = pl.program_id(0)

    @pl.when(step == 0)
    def _():
        carry_scr[...] = jnp.zeros_like(carry_scr)

    gt = gt_ref[0]
    s5w = gt.shape[0]
    zt = _dot(wglut_ref[...], gt) + bglu_ref[...]
    s5t = (gt.astype(F32) * _sigmoid(zt)).astype(BF16)
    h2 = (x_ref[...] + _dot_tn(s5t, wout_ref[:s5w, :])
          + _dot(yml_ref[...], wout_ref[s5w:, :]))
    h2_ref[...] = h2
    ms = jnp.mean(h2 * h2, axis=-1, keepdims=True)
    xn = h2 * lax.rsqrt(ms + RMS_EPS) * nffn_ref[...]
    xb = xn.astype(BF16)
    xn_ref[...] = xb

    tm = xn.shape[0]
    logits = _dot_nt(wrt_ref[...], xb) + br_ref[...]
    eio = lax.broadcasted_iota(I32, (N_EXPERTS, tm), 0).astype(F32)
    vals = logits
    onehot = jnp.zeros((N_EXPERTS, tm), F32)
    idxs, tops = [], []
    for _ in range(TOP_K):
        mx = jnp.max(vals, axis=0, keepdims=True)
        idx = jnp.min(jnp.where(vals == mx, eio, float(N_EXPERTS)), axis=0, keepdims=True)
        sel = eio == idx
        onehot = onehot + sel.astype(F32)
        vals = jnp.where(sel, -jnp.inf, vals)
        idxs.append(idx)
        tops.append(mx)
    exps = [jnp.exp(t - tops[0]) for t in tops]
    tot = exps[0] + exps[1] + exps[2] + exps[3]
    pad_f = jnp.zeros((SUBLANES - TOP_K, tm), F32)
    eidx_ref[...] = jnp.concatenate(idxs + [pad_f], axis=0).astype(I32)
    gate_ref[...] = jnp.concatenate([e / tot for e in exps] + [pad_f], axis=0)

    rr = lax.broadcasted_iota(I32, (tm, tm), 0)
    cc = lax.broadcasted_iota(I32, (tm, tm), 1)
    before = (rr < cc).astype(BF16)
    carry = carry_scr[:, 0:1]
    rank_ex = _dot(onehot.astype(BF16), before) + carry
    ranks = [jnp.sum(jnp.where(eio == i, rank_ex, 0.0), axis=0, keepdims=True) for i in idxs]
    rank_ref[...] = jnp.concatenate(ranks + [pad_f], axis=0).astype(I32)
    tile_cnt = jnp.sum(onehot, axis=1, keepdims=True)
    carry_scr[...] = jnp.broadcast_to(carry + tile_cnt, carry_scr.shape)
    for s in range(tm // DISPATCH_TILE):
        sub = onehot[:, s * DISPATCH_TILE:(s + 1) * DISPATCH_TILE]
        tcnt_ref[0, s] = jnp.broadcast_to(jnp.sum(sub, axis=1, keepdims=True), carry_scr.shape)


def _post(x2d, g_t, y_ml, w_glu, b_glu, w_out, norm_g, w_router, b_router):
    n_tok, d = x2d.shape
    bsz, s5w, seq = g_t.shape
    mlw = y_ml.shape[-1]
    tm = min(PROJ_TILE, seq)
    tpb = seq // tm
    full = lambda i: (0, 0)
    row = lambda i: (i, 0)
    colb = lambda i: (0, i)
    return pl.pallas_call(
        _post_kernel,
        grid=(n_tok // tm,),
        in_specs=[
            pl.BlockSpec((tm, d), row),
            pl.BlockSpec((1, s5w, tm), lambda i: (i // tpb, 0, i % tpb)),
            pl.BlockSpec((tm, mlw), row),
            pl.BlockSpec((s5w, s5w), full),
            pl.BlockSpec((s5w, 1), full),
            pl.BlockSpec((d, d), full),
            pl.BlockSpec((1, d), full),
            pl.BlockSpec((N_EXPERTS, d), full),
            pl.BlockSpec((N_EXPERTS, 1), full),
        ],
        out_specs=[
            pl.BlockSpec((tm, d), row),
            pl.BlockSpec((tm, d), row),
            pl.BlockSpec((SUBLANES, tm), colb),
            pl.BlockSpec((SUBLANES, tm), colb),
            pl.BlockSpec((SUBLANES, tm), colb),
            pl.BlockSpec((1, tm // DISPATCH_TILE, N_EXPERTS, LANES), lambda i: (i, 0, 0, 0)),
        ],
        out_shape=[
            jax.ShapeDtypeStruct((n_tok, d), F32),
            jax.ShapeDtypeStruct((n_tok, d), BF16),
            jax.ShapeDtypeStruct((SUBLANES, n_tok), I32),
            jax.ShapeDtypeStruct((SUBLANES, n_tok), F32),
            jax.ShapeDtypeStruct((SUBLANES, n_tok), I32),
            jax.ShapeDtypeStruct((n_tok // tm, tm // DISPATCH_TILE, N_EXPERTS, LANES), F32),
        ],
        scratch_shapes=[pltpu.VMEM((N_EXPERTS, LANES), F32)],
        compiler_params=pltpu.CompilerParams(
            dimension_semantics=("arbitrary",), vmem_limit_bytes=VMEM_LIMIT),
        name="post_router",
    )(x2d, g_t, y_ml, w_glu.T.astype(BF16), b_glu.reshape(-1, 1).astype(F32), w_out.astype(BF16),
      norm_g.reshape(1, -1).astype(F32), w_router.T.astype(BF16),
      b_router.reshape(-1, 1).astype(F32))


def _slab_loop(base, nbig_ref, nsmall_ref, fn, n_slabs=N_EXPERTS):
    def per_expert(e, carry):
        idx = base + e
        nbig = nbig_ref[idx]

        def big(j, c2):
            fn(idx, j * BIG_CHUNK, BIG_CHUNK)
            return c2

        def small(j, c2):
            fn(idx, nbig * BIG_CHUNK + j * SLAB_ALIGN, SLAB_ALIGN)
            return c2

        lax.fori_loop(0, nbig, big, 0)
        lax.fori_loop(0, nsmall_ref[idx], small, 0)
        return carry

    lax.fori_loop(0, n_slabs, per_expert, 0)


def _plan_copies(step, plan, fn):
    nbig_ref, nsmall_ref, _, bsrc_ref, bdst_ref, ssrc_ref, sdst_ref = plan

    def big(j, carry):
        k = step * MAX_BIG + j
        fn(bsrc_ref[k], bdst_ref[k], BIG_CHUNK)
        return carry

    def small(j, carry):
        k = step * MAX_SMALL + j
        fn(ssrc_ref[k], sdst_ref[k], SLAB_ALIGN)
        return carry

    lax.fori_loop(0, nbig_ref[step], big, 0)
    lax.fori_loop(0, nsmall_ref[step], small, 0)


def _plan_wait(step, plan, copy):
    copy(0, 0, DISPATCH_TILE * TOP_K).wait()

    def one(j, carry):
        copy(0, 0, SLAB_ALIGN).wait()
        return carry

    lax.fori_loop(0, plan[2][step], one, 0)


def _unpack_pairs(words):
    lo = lax.bitcast_convert_type(words << 16, F32).astype(BF16)
    hi = lax.bitcast_convert_type(words & jnp.uint32(0xFFFF0000), F32).astype(BF16)
    return lo, hi


def _pack_pairs(lo_f32, hi_f32):
    lo = lax.bitcast_convert_type(lo_f32, U32) >> 16
    hi = lax.bitcast_convert_type(hi_f32, U32) & jnp.uint32(0xFFFF0000)
    return hi | lo


def _dispatch_kernel(*refs, n_steps):
    plan = refs[:7]
    znbig_ref, znsmall_ref, zrow_ref, xn_ref, pos_ref, xs_hbm, stage, zbuf, sem, zsem = refs[7:]
    i = pl.program_id(0)
    slot = i % 2
    tw, d = xn_ref.shape
    n_rows = stage.shape[1]
    half = d // 2

    def copy(s, stage_row, buf_row, rows):
        src = stage.at[s, pl.ds(pl.multiple_of(stage_row, SLAB_ALIGN), rows), :]
        dst = xs_hbm.at[pl.ds(pl.multiple_of(buf_row, SLAB_ALIGN), rows), :]
        return pltpu.make_async_copy(src, dst, sem)

    @pl.when(i == 0)
    def _():
        zbuf[...] = jnp.zeros_like(zbuf)

        def tail(idx, r0, rows):
            dst = xs_hbm.at[pl.ds(pl.multiple_of(zrow_ref[idx] + r0, SLAB_ALIGN), rows), :]
            return pltpu.make_async_copy(zbuf.at[pl.ds(0, rows), :], dst, zsem)

        _slab_loop(0, znbig_ref, znsmall_ref, lambda *a: tail(*a).start(), N_EXPERTS + 1)
        _slab_loop(0, znbig_ref, znsmall_ref, lambda *a: tail(*a).wait(), N_EXPERTS + 1)

    xt = xn_ref[...]
    blk = STAGE_BLOCK
    for r0 in range(0, n_rows, blk):
        rid = lax.broadcasted_iota(I32, (blk, tw), 0) + r0
        hit = rid == pos_ref[0:1, :]
        for k in range(1, TOP_K):
            hit = jnp.logical_or(hit, rid == pos_ref[k:k + 1, :])
        srt = _dot(jnp.where(hit, 1.0, 0.0).astype(BF16), xt)
        stage[slot, r0:r0 + blk, :] = _pack_pairs(srt[:, :half], srt[:, half:])

    @pl.when(i > 0)
    def _():
        _plan_wait(i - 1, plan, functools.partial(copy, 1 - slot))

    _plan_copies(i, plan, lambda *a: copy(slot, *a).start())

    @pl.when(i == n_steps - 1)
    def _():
        _plan_wait(i, plan, functools.partial(copy, slot))


def _dispatch(xn, pos8, plan, tails, n_slots):
    n_tok, d = xn.shape
    tw = DISPATCH_TILE
    n_steps = n_tok // tw
    return pl.pallas_call(
        functools.partial(_dispatch_kernel, n_steps=n_steps),
        grid_spec=pltpu.PrefetchScalarGridSpec(
            num_scalar_prefetch=10,
            grid=(n_steps,),
            in_specs=[
                pl.BlockSpec((tw, d), lambda i, *_: (i, 0)),
                pl.BlockSpec((SUBLANES, tw), lambda i, *_: (0, i)),
            ],
            out_specs=pl.BlockSpec(memory_space=pl.ANY),
            scratch_shapes=[
                pltpu.VMEM((2, STAGE_ROWS, d // 2), U32),
                pltpu.VMEM((BIG_CHUNK, d // 2), U32),
                pltpu.SemaphoreType.DMA(()),
                pltpu.SemaphoreType.DMA(()),
            ],
        ),
        out_shape=jax.ShapeDtypeStruct((n_slots, d // 2), U32),
        compiler_params=pltpu.CompilerParams(
            dimension_semantics=("arbitrary",), vmem_limit_bytes=VMEM_LIMIT),
        name="dispatch",
    )(*plan, *tails, xn, pos8)


def _expert_kernel(te_ref, nu_ref, x_ref, wup_ref, bup_ref, wdn_ref, bdn_ref, y_ref,
                   wup_bf, wdn_bf, *, d_ff):
    i = pl.program_id(0)
    n_used = nu_ref[0]

    @pl.when(jnp.logical_and(i < n_used,
                             jnp.logical_or(i == 0, te_ref[i] != te_ref[jnp.maximum(i - 1, 0)])))
    def _():
        wup_bf[...] = wup_ref[0].astype(BF16)
        wdn_bf[...] = wdn_ref[0].astype(BF16)

    @pl.when(i < n_used)
    def _():
        lo, hi = _unpack_pairs(x_ref[...])
        half = lo.shape[1]
        h = _dot(lo, wup_bf[:half, :]) + _dot(hi, wup_bf[half:, :]) + bup_ref[0]
        gl = jnp.minimum(h[:, :d_ff], SWIGLU_LIMIT)
        lin = jnp.clip(h[:, d_ff:], -SWIGLU_LIMIT, SWIGLU_LIMIT)
        act = gl * _sigmoid(SWIGLU_ALPHA * gl) * (lin + 1.0)
        y = _dot(act.astype(BF16), wdn_bf[...]) + bdn_ref[0]
        yb = y.astype(BF16).astype(F32)
        y_ref[...] = _pack_pairs(yb[:, :half], yb[:, half:])

    @pl.when(i >= n_used)
    def _():
        y_ref[...] = jnp.zeros_like(y_ref)


def _experts(xs, tile_e, n_used, w_up, b_up, w_down, b_down):
    n_slots, half = xs.shape
    d = 2 * half
    tm = EXPERT_TILE
    n_tiles = n_slots // tm
    d_ff = w_down.shape[1]
    return pl.pallas_call(
        functools.partial(_expert_kernel, d_ff=d_ff),
        grid_spec=pltpu.PrefetchScalarGridSpec(
            num_scalar_prefetch=2,
            grid=(n_tiles,),
            in_specs=[
                pl.BlockSpec((tm, half), lambda i, te, nu: (jnp.minimum(i, nu[0] - 1), 0)),
                pl.BlockSpec((1, d, 2 * d_ff), lambda i, te, nu: (te[i], 0, 0)),
                pl.BlockSpec((1, 1, 2 * d_ff), lambda i, te, nu: (te[i], 0, 0)),
                pl.BlockSpec((1, d_ff, d), lambda i, te, nu: (te[i], 0, 0)),
                pl.BlockSpec((1, 1, d), lambda i, te, nu: (te[i], 0, 0)),
            ],
            out_specs=pl.BlockSpec((tm, half), lambda i, te, nu: (i, 0)),
            scratch_shapes=[
                pltpu.VMEM((d, 2 * d_ff), BF16),
                pltpu.VMEM((d_ff, d), BF16),
            ],
        ),
        out_shape=jax.ShapeDtypeStruct((n_slots, half), U32),
        compiler_params=pltpu.CompilerParams(
            dimension_semantics=("arbitrary",), vmem_limit_bytes=VMEM_LIMIT),
        name="experts",
    )(tile_e, n_used, xs, w_up, b_up.reshape(N_EXPERTS, 1, -1),
      w_down, b_down.reshape(N_EXPERTS, 1, -1))


def _combine_kernel(*refs, n_steps):
    plan = refs[:7]
    ys_hbm, h2_ref, posc_ref, gatec_ref, ng_ref, out_ref, stage, sem = refs[7:]
    i = pl.program_id(0)
    slot = i % 2
    tw, d = h2_ref.shape
    n_rows = stage.shape[1]

    def copy(s, stage_row, buf_row, rows):
        src = ys_hbm.at[pl.ds(pl.multiple_of(buf_row, SLAB_ALIGN), rows), :]
        dst = stage.at[s, pl.ds(pl.multiple_of(stage_row, SLAB_ALIGN), rows), :]
        return pltpu.make_async_copy(src, dst, sem.at[s])

    @pl.when(i == 0)
    def _():
        stage[...] = jnp.zeros_like(stage)
        _plan_copies(0, plan, lambda *a: copy(0, *a).start())

    @pl.when(i + 1 < n_steps)
    def _():
        _plan_copies(i + 1, plan, lambda *a: copy(1 - slot, *a).start())

    _plan_wait(i, plan, functools.partial(copy, slot))

    y_lo, y_hi = _unpack_pairs(stage[slot])
    blk = 128
    for t0 in range(0, tw, blk):
        cid = lax.broadcasted_iota(I32, (blk, n_rows), 1)
        pg = jnp.zeros((blk, n_rows), F32)
        for k in range(TOP_K):
            pg = jnp.where(cid == posc_ref[t0:t0 + blk, k:k + 1],
                           gatec_ref[t0:t0 + blk, k:k + 1], pg)
        pgb = pg.astype(BF16)
        moe = jnp.concatenate([_dot(pgb, y_lo), _dot(pgb, y_hi)], axis=1)
        acc = h2_ref[t0:t0 + blk, :] + moe
        ms = jnp.mean(acc * acc, axis=-1, keepdims=True)
        out_ref[t0:t0 + blk, :] = acc * lax.rsqrt(ms + RMS_EPS) * ng_ref[...]


def _combine(ys, h2, pos_cols, gate_cols, plan, norm_g):
    n_tok, d = h2.shape
    tw = DISPATCH_TILE
    n_steps = n_tok // tw
    return pl.pallas_call(
        functools.partial(_combine_kernel, n_steps=n_steps),
        grid_spec=pltpu.PrefetchScalarGridSpec(
            num_scalar_prefetch=7,
            grid=(n_steps,),
            in_specs=[
                pl.BlockSpec(memory_space=pl.ANY),
                pl.BlockSpec((tw, d), lambda i, *_: (i, 0)),
                pl.BlockSpec((tw, SUBLANES), lambda i, *_: (i, 0)),
                pl.BlockSpec((tw, SUBLANES), lambda i, *_: (i, 0)),
                pl.BlockSpec((1, d), lambda i, *_: (0, 0)),
            ],
            out_specs=pl.BlockSpec((tw, d), lambda i, *_: (i, 0)),
            scratch_shapes=[
                pltpu.VMEM((2, STAGE_ROWS, d // 2), U32),
                pltpu.SemaphoreType.DMA((2,)),
            ],
        ),
        out_shape=jax.ShapeDtypeStruct((n_tok, d), F32),
        compiler_params=pltpu.CompilerParams(
            dimension_semantics=("arbitrary",), vmem_limit_bytes=VMEM_LIMIT),
        name="combine",
    )(*plan, ys, h2, pos_cols, gate_cols, norm_g.reshape(1, -1).astype(F32))


def _moe(h2, xn, eidx, gate, rank, tcnt, w_up, b_up, w_down, b_down, norm_final_g):
    n_tok, d = h2.shape
    tm = EXPERT_TILE
    tw = DISPATCH_TILE
    n_steps = n_tok // tw
    n_exp = N_EXPERTS
    e_ids = jnp.arange(n_exp, dtype=I32)
    al = SLAB_ALIGN
    tile_cnt = tcnt.reshape(n_steps, n_exp, LANES)[:, :, 0].astype(I32)
    slab_rows = (tile_cnt + al - 1) // al * al
    exp_rows = jnp.sum(slab_rows, axis=0)
    region = (exp_rows + tm - 1) // tm * tm
    pad_end = jnp.cumsum(region)
    pad_start = pad_end - region
    n_slots = (n_tok * TOP_K + n_steps * n_exp * (al - 1) + n_exp * (tm - 1) + tm - 1) // tm * tm
    n_tiles = n_slots // tm

    tile_carry = jnp.cumsum(tile_cnt, axis=0) - tile_cnt
    tile_off = jnp.cumsum(slab_rows, axis=1) - slab_rows
    slab_row0 = pad_start[None, :] + jnp.cumsum(slab_rows, axis=0) - slab_rows

    def chunks(rows):
        return rows // BIG_CHUNK, rows % BIG_CHUNK // al

    def flat(cnt, src0, dst0, size, n_max):
        run = jnp.cumsum(cnt, axis=1)
        j = jnp.arange(n_max, dtype=I32)
        e_of = jnp.sum(run[:, None, :] <= j[None, :, None], axis=-1)
        sel = e_of[:, :, None] == e_ids[None, None, :]
        pick = lambda a: jnp.sum(jnp.where(sel, a[:, None, :], 0), axis=-1)
        local = (j[None, :] - pick(run - cnt)) * size
        return run[:, -1], (pick(src0) + local).reshape(-1), (pick(dst0) + local).reshape(-1)

    nbig, nsmall = chunks(slab_rows)
    big_tot, big_src, big_dst = flat(nbig, tile_off, slab_row0, BIG_CHUNK, MAX_BIG)
    small_tot, small_src, small_dst = flat(nsmall, tile_off + nbig * BIG_CHUNK,
                                           slab_row0 + nbig * BIG_CHUNK, al, MAX_SMALL)
    pad_groups = (jnp.sum(slab_rows, axis=1) - tw * TOP_K) // al
    plan = (big_tot, small_tot, pad_groups, big_src, big_dst, small_src, small_dst)
    tail_rows = jnp.concatenate([region - exp_rows, n_slots - pad_end[-1:]])
    tails = (*chunks(tail_rows), jnp.concatenate([pad_start + exp_rows, pad_end[-1:]]))

    base_t = jnp.repeat((tile_off - tile_carry).T, tw, axis=1)
    e_sel = eidx[:TOP_K][None] == e_ids[:, None, None]
    pos = jnp.sum(jnp.where(e_sel, base_t[:, None, :], 0), axis=0) + rank[:TOP_K]
    pos8 = jnp.concatenate([pos, jnp.full((SUBLANES - TOP_K, n_tok), -1, I32)], axis=0)

    tile_start = jnp.arange(n_tiles, dtype=I32) * tm
    tile_e = jnp.minimum(jnp.sum(pad_end[None, :] <= tile_start[:, None], axis=-1),
                         n_exp - 1).astype(I32)
    n_used = (pad_end[-1] // tm).astype(I32).reshape(1)

    xs = _dispatch(xn, pos8, plan, tails, n_slots)
    ys = _experts(xs, tile_e, n_used, w_up, b_up, w_down, b_down)
    return _combine(ys, h2, pos8.T, gate.T, plan, norm_final_g)


def kernel(x, norm_mix_g, w_in, s5_log_dt, s5_a_re, s5_a_im, s5_b_re, s5_b_im, s5_c_re, s5_c_im,
           s5_d, s5_w_glu, s5_b_glu, ml_conv_w, ml_conv_b, ml_b_gates, ml_norm_g, w_out,
           norm_ffn_g, w_router, b_router, w_up, b_up, w_down, b_down, norm_final_g):
    bsz, seq, d = x.shape
    depth = w_in.shape[0]
    assert depth == 1, "single-layer block"
    l = 0
    x2d = x.reshape(bsz * seq, d)
    qk, v, o, ut, gt = _in_proj(x2d, norm_mix_g[l], w_in[l], ml_b_gates[l], bsz, seq)
    g_t = _s5(ut, s5_log_dt[l], s5_a_re[l], s5_a_im[l], s5_b_re[l], s5_b_im[l],
              s5_c_re[l], s5_c_im[l], s5_d[l])
    y_ml = _mlstm(qk, v, o, gt, ml_conv_w[l], ml_conv_b[l], ml_norm_g[l], bsz, seq)
    y_ml = y_ml.reshape(bsz * seq, -1)
    h2, xn, eidx, gate, rank, tcnt = _post(x2d, g_t, y_ml, s5_w_glu[l], s5_b_glu[l], w_out[l],
                                           norm_ffn_g[l], w_router[l], b_router[l])
    out = _moe(h2, xn, eidx, gate, rank, tcnt, w_up[l], b_up[l], w_down[l], b_down[l],
               norm_final_g)
    return out.reshape(bsz, seq, d)
```

```python
import functools
import math

import jax
import jax.numpy as jnp
from jax import lax
from jax.experimental import pallas as pl
from jax.experimental.pallas import tpu as pltpu

F32 = jnp.float32
BF16 = jnp.bfloat16
I32 = jnp.int32
U32 = jnp.uint32

S5_GROUP = 16
S5_STATE = 64
ML_HEADS = 4
CONV_WIDTH = 4
N_EXPERTS = 32
TOP_K = 4
SWIGLU_LIMIT = 7.0
SWIGLU_ALPHA = 1.702
RMS_EPS = 1e-5
LN_EPS = 1e-6

LANES = 128
SUBLANES = 8
S5_CHUNK = LANES
ML_CHUNK = 128
GATE_CHUNKS = 8
PROJ_TILE = 512
EXPERT_TILE = 512
DISPATCH_TILE = 256
SLAB_ALIGN = SUBLANES
BIG_CHUNK = 32
STAGE_ROWS = DISPATCH_TILE * TOP_K + N_EXPERTS * SLAB_ALIGN
STAGE_BLOCK = STAGE_ROWS // 2
MAX_BIG = STAGE_ROWS // BIG_CHUNK
MAX_SMALL = N_EXPERTS * (BIG_CHUNK // SLAB_ALIGN - 1)
VMEM_LIMIT = 56 * 1024 * 1024

_NT = (((1,), (1,)), ((), ()))
_TN = (((0,), (0,)), ((), ()))


def _dot(a, b):
    return jnp.dot(a, b, preferred_element_type=F32)


def _dot_nt(a, b):
    return lax.dot_general(a, b, _NT, preferred_element_type=F32)


def _dot_tn(a, b):
    return lax.dot_general(a, b, _TN, preferred_element_type=F32)


def _split3(x):
    p1 = x.astype(BF16)
    r1 = x - p1.astype(F32)
    p2 = r1.astype(BF16)
    r2 = r1 - p2.astype(F32)
    return p1, p2, r2.astype(BF16)


def _sigmoid(x):
    return 0.5 * jnp.tanh(0.5 * x) + 0.5


def _inproj_kernel(x_ref, g_ref, wnat_ref, wut_ref, wgt_ref, bg_ref,
                   qk_ref, v_ref, o_ref, ut_ref, gt_ref, *, width):
    x = x_ref[...]
    ms = jnp.mean(x * x, axis=-1, keepdims=True)
    hn = (x * lax.rsqrt(ms + RMS_EPS) * g_ref[...]).astype(BF16)
    nat = _dot(hn, wnat_ref[...])
    qk_ref[...] = nat[:, :2 * width].astype(BF16)
    v_ref[...] = nat[:, 2 * width:3 * width].astype(BF16)
    o_ref[...] = nat[:, 3 * width:].astype(BF16)
    ut_ref[0] = _dot_nt(wut_ref[...], hn).astype(BF16)
    gt_ref[0] = _dot_nt(wgt_ref[...], hn) + bg_ref[...]


def _in_proj(x2d, norm_g, w_in, b_gates, bsz, seq):
    n_tok, d = x2d.shape
    s5w = d // 2
    mlw = d - s5w
    tm = min(PROJ_TILE, seq)
    tpb = seq // tm
    w_bf = w_in.astype(BF16)
    w_nat = w_bf[:, s5w:s5w + 4 * mlw]
    w_ut = w_bf[:, :s5w].T
    n_gate = 2 * ML_HEADS
    w_gt = jnp.zeros((16, d), BF16).at[:n_gate].set(w_bf[:, s5w + 4 * mlw:].T)
    b_g = jnp.zeros((16, 1), F32).at[:n_gate, 0].set(b_gates.astype(F32))
    grid = (n_tok // tm,)
    full = lambda i: (0, 0)
    return pl.pallas_call(
        functools.partial(_inproj_kernel, width=mlw),
        grid=grid,
        in_specs=[
            pl.BlockSpec((tm, d), lambda i: (i, 0)),
            pl.BlockSpec((1, d), full),
            pl.BlockSpec((d, 4 * mlw), full),
            pl.BlockSpec((s5w, d), full),
            pl.BlockSpec((16, d), full),
            pl.BlockSpec((16, 1), full),
        ],
        out_specs=[
            pl.BlockSpec((tm, 2 * mlw), lambda i: (i, 0)),
            pl.BlockSpec((tm, mlw), lambda i: (i, 0)),
            pl.BlockSpec((tm, mlw), lambda i: (i, 0)),
            pl.BlockSpec((1, s5w, tm), lambda i: (i // tpb, 0, i % tpb)),
            pl.BlockSpec((1, 16, tm), lambda i: (i // tpb, 0, i % tpb)),
        ],
        out_shape=[
            jax.ShapeDtypeStruct((n_tok, 2 * mlw), BF16),
            jax.ShapeDtypeStruct((n_tok, mlw), BF16),
            jax.ShapeDtypeStruct((n_tok, mlw), BF16),
            jax.ShapeDtypeStruct((bsz, s5w, seq), BF16),
            jax.ShapeDtypeStruct((bsz, 16, seq), F32),
        ],
        compiler_params=pltpu.CompilerParams(
            dimension_semantics=("parallel",), vmem_limit_bytes=VMEM_LIMIT),
        name="in_proj",
    )(x2d, norm_g.reshape(1, d).astype(F32), w_nat, w_ut, w_gt, b_g)


def _s5_kernel(d_ref, x_ref, prm_ref, c_ref, bt_ref, out_ref, *, bsz, nc):
    L = S5_CHUNK
    P = S5_GROUP
    N = S5_STATE
    grp = pl.program_id(0)
    lane = lax.broadcasted_iota(I32, (1, 2 * N), 1)
    lo = lane < N
    a_re = jnp.minimum(prm_ref[0, 0:1, :], -1e-4)
    a_im = prm_ref[0, 1:2, :]
    dt = jnp.exp(prm_ref[0, 2:3, :])
    zr = dt * a_re
    zi = dt * a_im

    quarter = jnp.where(lo, 0.0, 0.5 * math.pi)

    def powrows(e):
        return jnp.exp(e * zr) * jnp.cos(e * zi - quarter)

    def swap(tab):
        return pltpu.roll(tab, N, 1)

    def cmul(tab, c):
        return c[0] * tab + jnp.where(lo, -c[1], c[1]) * swap(tab)

    n_dbl = max(nc - 1, 0).bit_length()
    exps = [1] + [SUBLANES << k for k in range(int(math.log2(L // SUBLANES)))] + [L << k for k in range(n_dbl)]
    e_col = jnp.concatenate([jnp.full((1, 1), float(e), F32) for e in exps]
                            + [jnp.zeros((-len(exps) % SUBLANES, 1), F32)], axis=0)
    mag = jnp.exp(e_col * zr)
    pw_r = mag * jnp.cos(e_col * zi)
    pw_i = mag * jnp.sin(e_col * zi)
    apow = {e: (pw_r[k:k + 1], pw_i[k:k + 1]) for k, e in enumerate(exps)}

    def powtab(descending):
        i8 = lax.broadcasted_iota(I32, (SUBLANES, 1), 0).astype(F32)
        tab = powrows(SUBLANES - 1.0 - i8 if descending else i8)
        rows = SUBLANES
        while rows < L:
            more = cmul(tab, apow[rows])
            tab = jnp.concatenate([more, tab] if descending else [tab, more], axis=0)
            rows *= 2
        return tab

    er, ei = apow[1]
    den = a_re * a_re + a_im * a_im
    coef_r = ((er - 1.0) * a_re + ei * a_im) / den
    coef_i = (ei * a_re - (er - 1.0) * a_im) / den
    c_r = c_ref[0, 0]
    c_i = c_ref[0, 1]
    bb_r = coef_r * bt_ref[0, 0] - coef_i * bt_ref[0, 1]
    bb_i = coef_r * bt_ref[0, 1] + coef_i * bt_ref[0, 0]

    cb_rows = []
    for q in range(P):
        cbr = c_r * bb_r[q:q + 1] - c_i * bb_i[q:q + 1]
        cbi = c_r * bb_i[q:q + 1] + c_i * bb_r[q:q + 1]
        cb_rows.append(jnp.where(lo, cbr, -cbi))
    cb = jnp.concatenate(cb_rows, axis=0)
    pt0 = powtab(False)
    c1, c2, c3 = _split3(cb)
    t1, t2, t3 = _split3(pt0)
    kmat = (_dot_nt(c1, t1) + _dot_nt(c1, t2) + _dot_nt(c2, t1)
            + _dot_nt(c2, t2) + _dot_nt(c1, t3) + _dot_nt(c3, t1))

    pt_rev = powtab(True)
    pt_rev_sw = swap(pt_rev)
    f_rows = []
    for q in range(P):
        a1 = bb_r[q:q + 1]
        a2 = jnp.where(lo, -bb_i[q:q + 1], bb_i[q:q + 1])
        f_rows.append((a1 * pt_rev + a2 * pt_rev_sw).astype(BF16))
    fmat = jnp.concatenate(f_rows, axis=0)
    pt1 = cmul(pt0, apow[1])
    pt1_sw = swap(pt1)
    e_rows = []
    for p in range(P):
        b1 = jnp.where(lo, c_r[p:p + 1], -c_r[p:p + 1])
        b2 = -c_i[p:p + 1]
        e_rows.append((b1 * pt1 + b2 * pt1_sw).astype(BF16))
    emat_t = jnp.concatenate(e_rows, axis=0)

    lhs = jnp.concatenate(
        [jnp.concatenate([x_ref[b, q] for q in range(P)], axis=1) for b in range(bsz)], axis=0)
    s_end = _dot(lhs, fmat)

    rr = lax.broadcasted_iota(I32, (L, L), 0)
    cc = lax.broadcasted_iota(I32, (L, L), 1)
    causal = cc >= rr
    y = None
    for q0 in range(0, P, 2):
        rows = []
        for q in (q0, q0 + 1):
            tiles = []
            for p in range(P):
                kb = jnp.broadcast_to(kmat[q * P + p:q * P + p + 1, :], (L, L))
                toe = pltpu.roll(kb, 0, 1, stride=1, stride_axis=0)
                tiles.append(jnp.where(causal, toe, 0.0).astype(BF16))
            rows.append(jnp.concatenate(tiles, axis=1))
        part = _dot(lhs[:, q0 * L:(q0 + 2) * L], jnp.concatenate(rows, axis=0))
        y = part if y is None else y + part

    m_rows = bsz * nc
    ridx = lax.broadcasted_iota(I32, (m_rows, 1), 0)
    cidx = ridx % nc
    h = jnp.where(cidx >= 1, pltpu.roll(s_end, 1, 0), 0.0)
    d = 1
    while d < nc:
        sh = jnp.where(cidx >= d, pltpu.roll(h, d, 0), 0.0)
        h = h + cmul(sh, apow[d * L])
        d *= 2
    y = y + _dot_nt(h.astype(BF16), emat_t)

    for b in range(bsz):
        for p in range(P):
            yp = (y[b * nc:(b + 1) * nc, p * L:(p + 1) * L]
                  + d_ref[grp * P + p] * x_ref[b, p].astype(F32))
            out_ref[b, p] = jax.nn.gelu(yp).astype(BF16)


def _s5(ut, log_dt, a_re, a_im, b_re, b_im, c_re, c_im, d_skip):
    bsz, s5w, seq = ut.shape
    L = S5_CHUNK
    nc = seq // L
    groups = s5w // S5_GROUP
    n = S5_STATE
    x4 = ut.reshape(bsz, s5w, nc, L)
    dup = lambda t: jnp.concatenate([t, t], axis=-1).astype(F32)
    prm = jnp.zeros((groups, SUBLANES, 2 * n), F32)
    prm = prm.at[:, 0].set(dup(a_re)).at[:, 1].set(dup(a_im))
    prm = prm.at[:, 2].set(jnp.broadcast_to(log_dt.astype(F32)[:, None], (groups, 2 * n)))
    cpar = jnp.stack([dup(c_re), dup(c_im)], axis=1)
    btpar = jnp.stack([dup(jnp.swapaxes(b_re, 1, 2)), dup(jnp.swapaxes(b_im, 1, 2))], axis=1)
    out = pl.pallas_call(
        functools.partial(_s5_kernel, bsz=bsz, nc=nc),
        grid_spec=pltpu.PrefetchScalarGridSpec(
            num_scalar_prefetch=1,
            grid=(groups,),
            in_specs=[
                pl.BlockSpec((bsz, S5_GROUP, nc, L), lambda g, d: (0, g, 0, 0)),
                pl.BlockSpec((1, SUBLANES, 2 * n), lambda g, d: (g, 0, 0)),
                pl.BlockSpec((1, 2, S5_GROUP, 2 * n), lambda g, d: (g, 0, 0, 0)),
                pl.BlockSpec((1, 2, S5_GROUP, 2 * n), lambda g, d: (g, 0, 0, 0)),
            ],
            out_specs=pl.BlockSpec((bsz, S5_GROUP, nc, L), lambda g, d: (0, g, 0, 0)),
        ),
        out_shape=jax.ShapeDtypeStruct((bsz, s5w, nc, L), BF16),
        compiler_params=pltpu.CompilerParams(
            dimension_semantics=("parallel",), vmem_limit_bytes=VMEM_LIMIT),
        name="s5",
    )(d_skip.astype(F32), x4, prm, cpar, btpar)
    return out.reshape(bsz, s5w, seq)


def _log_sigmoid(x):
    return jnp.minimum(x, 0.0) - jnp.log(1.0 + jnp.exp(-jnp.abs(x)))


def _gates_kernel(gt_ref, out_ref, m_scr, *, bsz, chunk, n_sub):
    L = chunk
    H = ML_HEADS
    step = pl.program_id(0)

    @pl.when(step == 0)
    def _():
        m_scr[...] = jnp.zeros_like(m_scr)

    rr = lax.broadcasted_iota(I32, (L, L), 0)
    cc = lax.broadcasted_iota(I32, (L, L), 1)
    utri = (rr <= cc).astype(BF16)
    grow = lax.broadcasted_iota(I32, (16, 1), 0)
    lane_row = lax.broadcasted_iota(I32, (SUBLANES, L), 1)
    m_prev = [m_scr[b][0:H, 0:1] for b in range(bsz)]
    pairs = [(j, b) for j in range(n_sub) for b in range(bsz)]

    irow, brow, cm = {}, {}, {}
    for j, b in pairs:
        g = gt_ref[b, :, j * L:(j + 1) * L]
        g2 = jnp.where(grow >= H, _log_sigmoid(g), g)
        p1, p2, p3 = _split3(g2)
        brow[j, b] = (_dot(p1, utri) + _dot(p2, utri) + _dot(p3, utri))[H:2 * H]
        irow[j, b] = g[0:H]
    for j, b in pairs:
        ib = irow[j, b] - brow[j, b]
        x = jnp.concatenate([ib, ib], axis=0)
        sft = 1
        while sft < L:
            x = jnp.maximum(x, jnp.where(lane_row >= sft, pltpu.roll(x, sft, 1), -jnp.inf))
            sft *= 2
        cm[j, b] = x[0:H]
    for j, b in pairs:
        mp = m_prev[b]
        mm = jnp.maximum(cm[j, b], mp)
        b_last = brow[j, b][:, L - 1:L]
        m_next = b_last + jnp.maximum(mp, cm[j, b][:, L - 1:L])
        planes = (mm, jnp.exp(mp - mm), jnp.exp(-(brow[j, b] + mm)),
                  jnp.exp(b_last - brow[j, b] + irow[j, b] - m_next), irow[j, b] - brow[j, b],
                  jnp.broadcast_to(jnp.exp(b_last + mp - m_next), (H, L)))
        for k, rows in enumerate(planes):
            out_ref[k, b, :, j * L:(j + 1) * L] = jnp.concatenate([rows, rows], axis=0)
        m_prev[b] = m_next
    for b in range(bsz):
        m_scr[b] = jnp.broadcast_to(jnp.concatenate([m_prev[b], m_prev[b]], axis=0),
                                    (SUBLANES, LANES))


def _gates(gt, bsz, seq, chunk):
    n_sub = min(GATE_CHUNKS, seq // chunk)
    blk = n_sub * chunk
    return pl.pallas_call(
        functools.partial(_gates_kernel, bsz=bsz, chunk=chunk, n_sub=n_sub),
        grid=(seq // blk,),
        in_specs=[pl.BlockSpec((bsz, 16, blk), lambda c: (0, 0, c))],
        out_specs=pl.BlockSpec((6, bsz, SUBLANES, blk), lambda c: (0, 0, 0, c)),
        out_shape=jax.ShapeDtypeStruct((6, bsz, SUBLANES, seq), F32),
        scratch_shapes=[pltpu.VMEM((bsz, SUBLANES, LANES), F32)],
        compiler_params=pltpu.CompilerParams(
            dimension_semantics=("arbitrary",), vmem_limit_bytes=VMEM_LIMIT),
        name="gates",
    )(gt)


def _mlstm_kernel(qk_ref, v_ref, o_ref, gr_ref, cw_ref, cb_ref, ng_ref, y_ref,
                  tail_scr, c_scr, *, bsz, chunk, width):
    L = chunk
    H = ML_HEADS
    dh = width // H
    step = pl.program_id(0)
    assert L == LANES and dh == LANES, "column replication below uses one 128x128 tile per head"

    @pl.when(step == 0)
    def _():
        tail_scr[...] = jnp.zeros_like(tail_scr)
        c_scr[...] = jnp.zeros_like(c_scr)

    rr = lax.broadcasted_iota(I32, (L, L), 0)
    cc = lax.broadcasted_iota(I32, (L, L), 1)
    causal = cc <= rr
    eye = (rr == cc).astype(BF16)
    ones_blk = jnp.ones((L, dh), BF16)
    scale = 1.0 / math.sqrt(dh)
    inv_dh = 1.0 / dh
    n_str = bsz * H
    streams = [(b, hd) for b in range(bsz) for hd in range(H)]

    def replicate(rows, two_terms):
        p = jnp.concatenate([jnp.broadcast_to(rows[h:h + 1], (LANES, L)) for h in range(H)], axis=0)
        p1 = p.astype(BF16)
        if not two_terms:
            return _dot_nt(eye, p1)
        return _dot_nt(eye, p1) + _dot_nt(eye, (p - p1.astype(F32)).astype(BF16))

    def rowsum(x):
        hi = x.astype(BF16)
        lo = (x - hi.astype(F32)).astype(BF16)
        return _dot(hi, ones_blk) + _dot(lo, ones_blk)

    tails = [tail_scr[b] for b in range(bsz)]
    caugs = [c_scr[i] for i in range(n_str)]
    reps = [[replicate(gr_ref[k, b][0:H], k == 0) for b in range(bsz)] for k in range(4)]
    col = lambda k: [reps[k][b][:, hd * LANES:(hd + 1) * LANES] for b, hd in streams]
    mm_rep, w_inter, enm, ws_rep = col(0), col(1), col(2), col(3)
    ib_row = [gr_ref[4, b][hd:hd + 1] for b, hd in streams]
    decays = [gr_ref[5, b][hd:hd + 1, 0:1] for b, hd in streams]

    q_all, k_all, new_tails = [], [], []
    for b in range(bsz):
        xqk = qk_ref[b].astype(F32)
        ext = jnp.concatenate([tails[b], xqk], axis=0)
        acc = jnp.broadcast_to(cb_ref[...], xqk.shape)
        for j in range(CONV_WIDTH):
            back = CONV_WIDTH - 1 - j
            sh = ext if back == 0 else pltpu.roll(ext, back, 0)
            acc = acc + cw_ref[j:j + 1, :] * sh[SUBLANES:]
        new_tails.append(xqk[L - SUBLANES:])
        qkc = acc * _sigmoid(acc)
        q_all.append(qkc[:, :width].astype(BF16))
        k_all.append((qkc[:, width:] * scale).astype(BF16))

    q_h = [q_all[b][:, hd * dh:(hd + 1) * dh] for b, hd in streams]
    k_h = [k_all[b][:, hd * dh:(hd + 1) * dh] for b, hd in streams]
    vaug = [jnp.concatenate([v_ref[b][:, hd * dh:(hd + 1) * dh], ones_blk], axis=1)
            for b, hd in streams]
    scores = [_dot_nt(q_h[i], k_h[i]) for i in range(n_str)]
    inter = [_dot(q_h[i], caugs[i].astype(BF16)) for i in range(n_str)]
    s_bf = [(scores[i] * jnp.exp(jnp.where(causal, ib_row[i] - mm_rep[i], -jnp.inf))).astype(BF16)
            for i in range(n_str)]
    kw = [(k_h[i].astype(F32) * ws_rep[i]).astype(BF16) for i in range(n_str)]
    intra = [_dot(s_bf[i], vaug[i]) for i in range(n_str)]
    upd = [_dot_tn(kw[i], vaug[i]) for i in range(n_str)]

    hh = []
    for i in range(n_str):
        num = w_inter[i] * inter[i][:, :dh] + intra[i][:, :dh]
        den = w_inter[i] * inter[i][:, dh:] + intra[i][:, dh:]
        hh.append(num / jnp.maximum(jnp.abs(den), enm[i]))
    mu = [rowsum(h) * inv_dh for h in hh]
    ctr = [hh[i] - mu[i] for i in range(n_str)]
    var = [rowsum(c * c) * inv_dh for c in ctr]
    outs = []
    for i, (b, hd) in enumerate(streams):
        sl = slice(hd * dh, (hd + 1) * dh)
        hn = ctr[i] * lax.rsqrt(var[i] + LN_EPS) * ng_ref[:, sl]
        outs.append(hn * _sigmoid(o_ref[b][:, sl].astype(F32)))

    for b in range(bsz):
        tail_scr[b] = new_tails[b]
        y_ref[b] = jnp.concatenate(outs[b * H:(b + 1) * H], axis=1).astype(BF16)
    for i in range(n_str):
        c_scr[i] = decays[i] * caugs[i] + upd[i]


def _mlstm(qk, v, o, gt, conv_w, conv_b, norm_g, bsz, seq):
    width = v.shape[-1]
    L = min(ML_CHUNK, seq)
    dh = width // ML_HEADS
    qk3 = qk.reshape(bsz, seq, 2 * width)
    v3 = v.reshape(bsz, seq, width)
    o3 = o.reshape(bsz, seq, width)
    gate_rows = _gates(gt, bsz, seq, L)
    full = lambda c: (0, 0)
    return pl.pallas_call(
        functools.partial(_mlstm_kernel, bsz=bsz, chunk=L, width=width),
        grid=(seq // L,),
        in_specs=[
            pl.BlockSpec((bsz, L, 2 * width), lambda c: (0, c, 0)),
            pl.BlockSpec((bsz, L, width), lambda c: (0, c, 0)),
            pl.BlockSpec((bsz, L, width), lambda c: (0, c, 0)),
            pl.BlockSpec((6, bsz, SUBLANES, L), lambda c: (0, 0, 0, c)),
            pl.BlockSpec((CONV_WIDTH, 2 * width), full),
            pl.BlockSpec((1, 2 * width), full),
            pl.BlockSpec((1, width), full),
        ],
        out_specs=pl.BlockSpec((bsz, L, width), lambda c: (0, c, 0)),
        out_shape=jax.ShapeDtypeStruct((bsz, seq, width), BF16),
        scratch_shapes=[
            pltpu.VMEM((bsz, SUBLANES, 2 * width), F32),
            pltpu.VMEM((bsz * ML_HEADS, dh, 2 * dh), F32),
        ],
        compiler_params=pltpu.CompilerParams(
            dimension_semantics=("arbitrary",), vmem_limit_bytes=VMEM_LIMIT),
        name="mlstm",
    )(qk3, v3, o3, gate_rows, conv_w.astype(F32), conv_b.reshape(1, -1).astype(F32),
      norm_g.reshape(1, -1).astype(F32))


def _post_kernel(x_ref, gt_ref, yml_ref, wglut_ref, bglu_ref, wout_ref, nffn_ref, wrt_ref, br_ref,
                 h2_ref, xn_ref, eidx_ref, gate_ref, rank_ref, tcnt_ref, carry_scr):
    step = pl.program_id(0)

    @pl.when(step == 0)
    def _():
        carry_scr[...] = jnp.zeros_like(carry_scr)

    gt = gt_ref[0]
    s5w = gt.shape[0]
    zt = _dot(wglut_ref[...], gt) + bglu_ref[...]
    s5t = (gt.astype(F32) * _sigmoid(zt)).astype(BF16)
    h2 = (x_ref[...] + _dot_tn(s5t, wout_ref[:s5w, :])
          + _dot(yml_ref[...], wout_ref[s5w:, :]))
    h2_ref[...] = h2.astype(BF16)
    ms = jnp.mean(h2 * h2, axis=-1, keepdims=True)
    xn = h2 * lax.rsqrt(ms + RMS_EPS) * nffn_ref[...]
    xb = xn.astype(BF16)
    xn_ref[...] = xb

    tm = xn.shape[0]
    logits = _dot_nt(wrt_ref[...], xb) + br_ref[...]
    eio = lax.broadcasted_iota(I32, (N_EXPERTS, tm), 0).astype(F32)
    vals = logits
    onehot = jnp.zeros((N_EXPERTS, tm), F32)
    idxs, tops = [], []
    for _ in range(TOP_K):
        mx = jnp.max(vals, axis=0, keepdims=True)
        idx = jnp.min(jnp.where(vals == mx, eio, float(N_EXPERTS)), axis=0, keepdims=True)
        sel = eio == idx
        onehot = onehot + sel.astype(F32)
        vals = jnp.where(sel, -jnp.inf, vals)
        idxs.append(idx)
        tops.append(mx)
    exps = [jnp.exp(t - tops[0]) for t in tops]
    tot = exps[0] + exps[1] + exps[2] + exps[3]
    pad_f = jnp.zeros((SUBLANES - TOP_K, tm), F32)
    eidx_ref[...] = jnp.concatenate(idxs + [pad_f], axis=0).astype(I32)
    gate_ref[...] = jnp.concatenate([e / tot for e in exps] + [pad_f], axis=0)

    rr = lax.broadcasted_iota(I32, (tm, tm), 0)
    cc = lax.broadcasted_iota(I32, (tm, tm), 1)
    before = (rr < cc).astype(BF16)
    carry = carry_scr[:, 0:1]
    rank_ex = _dot(onehot.astype(BF16), before) + carry
    ranks = [jnp.sum(jnp.where(eio == i, rank_ex, 0.0), axis=0, keepdims=True) for i in idxs]
    rank_ref[...] = jnp.concatenate(ranks + [pad_f], axis=0).astype(I32)
    tile_cnt = jnp.sum(onehot, axis=1, keepdims=True)
    carry_scr[...] = jnp.broadcast_to(carry + tile_cnt, carry_scr.shape)
    for s in range(tm // DISPATCH_TILE):
        sub = onehot[:, s * DISPATCH_TILE:(s + 1) * DISPATCH_TILE]
        tcnt_ref[0, s] = jnp.broadcast_to(jnp.sum(sub, axis=1, keepdims=True), carry_scr.shape)


def _post(x2d, g_t, y_ml, w_glu, b_glu, w_out, norm_g, w_router, b_router):
    n_tok, d = x2d.shape
    bsz, s5w, seq = g_t.shape
    mlw = y_ml.shape[-1]
    tm = min(PROJ_TILE, seq)
    tpb = seq // tm
    full = lambda i: (0, 0)
    row = lambda i: (i, 0)
    colb = lambda i: (0, i)
    return pl.pallas_call(
        _post_kernel,
        grid=(n_tok // tm,),
        in_specs=[
            pl.BlockSpec((tm, d), row),
            pl.BlockSpec((1, s5w, tm), lambda i: (i // tpb, 0, i % tpb)),
            pl.BlockSpec((tm, mlw), row),
            pl.BlockSpec((s5w, s5w), full),
            pl.BlockSpec((s5w, 1), full),
            pl.BlockSpec((d, d), full),
            pl.BlockSpec((1, d), full),
            pl.BlockSpec((N_EXPERTS, d), full),
            pl.BlockSpec((N_EXPERTS, 1), full),
        ],
        out_specs=[
            pl.BlockSpec((tm, d), row),
            pl.BlockSpec((tm, d), row),
            pl.BlockSpec((SUBLANES, tm), colb),
            pl.BlockSpec((SUBLANES, tm), colb),
            pl.BlockSpec((SUBLANES, tm), colb),
            pl.BlockSpec((1, tm // DISPATCH_TILE, N_EXPERTS, LANES), lambda i: (i, 0, 0, 0)),
        ],
        out_shape=[
            jax.ShapeDtypeStruct((n_tok, d), BF16),
            jax.ShapeDtypeStruct((n_tok, d), BF16),
            jax.ShapeDtypeStruct((SUBLANES, n_tok), I32),
            jax.ShapeDtypeStruct((SUBLANES, n_tok), F32),
            jax.ShapeDtypeStruct((SUBLANES, n_tok), I32),
            jax.ShapeDtypeStruct((n_tok // tm, tm // DISPATCH_TILE, N_EXPERTS, LANES), F32),
        ],
        scratch_shapes=[pltpu.VMEM((N_EXPERTS, LANES), F32)],
        compiler_params=pltpu.CompilerParams(
            dimension_semantics=("arbitrary",), vmem_limit_bytes=VMEM_LIMIT),
        name="post_router",
    )(x2d, g_t, y_ml, w_glu.T.astype(BF16), b_glu.reshape(-1, 1).astype(F32), w_out.astype(BF16),
      norm_g.reshape(1, -1).astype(F32), w_router.T.astype(BF16),
      b_router.reshape(-1, 1).astype(F32))


def _slab_loop(base, nbig_ref, nsmall_ref, fn, n_slabs=N_EXPERTS):
    def per_expert(e, carry):
        idx = base + e
        nbig = nbig_ref[idx]

        def big(j, c2):
            fn(idx, j * BIG_CHUNK, BIG_CHUNK)
            return c2

        def small(j, c2):
            fn(idx, nbig * BIG_CHUNK + j * SLAB_ALIGN, SLAB_ALIGN)
            return c2

        lax.fori_loop(0, nbig, big, 0)
        lax.fori_loop(0, nsmall_ref[idx], small, 0)
        return carry

    lax.fori_loop(0, n_slabs, per_expert, 0)


def _plan_copies(step, plan, fn):
    nbig_ref, nsmall_ref, _, bsrc_ref, bdst_ref, ssrc_ref, sdst_ref = plan

    def big(j, carry):
        k = step * MAX_BIG + j
        fn(bsrc_ref[k], bdst_ref[k], BIG_CHUNK)
        return carry

    def small(j, carry):
        k = step * MAX_SMALL + j
        fn(ssrc_ref[k], sdst_ref[k], SLAB_ALIGN)
        return carry

    lax.fori_loop(0, nbig_ref[step], big, 0)
    lax.fori_loop(0, nsmall_ref[step], small, 0)


def _plan_wait(step, plan, copy):
    copy(0, 0, DISPATCH_TILE * TOP_K).wait()

    def one(j, carry):
        copy(0, 0, SLAB_ALIGN).wait()
        return carry

    lax.fori_loop(0, plan[2][step], one, 0)


def _unpack_pairs(words):
    lo = lax.bitcast_convert_type(words << 16, F32).astype(BF16)
    hi = lax.bitcast_convert_type(words & jnp.uint32(0xFFFF0000), F32).astype(BF16)
    return lo, hi


def _pack_pairs(lo_f32, hi_f32):
    lo = lax.bitcast_convert_type(lo_f32, U32) >> 16
    hi = lax.bitcast_convert_type(hi_f32, U32) & jnp.uint32(0xFFFF0000)
    return hi | lo


def _dispatch_kernel(*refs, n_steps):
    plan = refs[:7]
    znbig_ref, znsmall_ref, zrow_ref, xn_ref, pos_ref, xs_hbm, stage, zbuf, sem, zsem = refs[7:]
    i = pl.program_id(0)
    slot = i % 2
    tw, d = xn_ref.shape
    n_rows = stage.shape[1]
    half = d // 2

    def copy(s, stage_row, buf_row, rows):
        src = stage.at[s, pl.ds(pl.multiple_of(stage_row, SLAB_ALIGN), rows), :]
        dst = xs_hbm.at[pl.ds(pl.multiple_of(buf_row, SLAB_ALIGN), rows), :]
        return pltpu.make_async_copy(src, dst, sem)

    @pl.when(i == 0)
    def _():
        zbuf[...] = jnp.zeros_like(zbuf)

        def tail(idx, r0, rows):
            dst = xs_hbm.at[pl.ds(pl.multiple_of(zrow_ref[idx] + r0, SLAB_ALIGN), rows), :]
            return pltpu.make_async_copy(zbuf.at[pl.ds(0, rows), :], dst, zsem)

        _slab_loop(0, znbig_ref, znsmall_ref, lambda *a: tail(*a).start(), N_EXPERTS + 1)
        _slab_loop(0, znbig_ref, znsmall_ref, lambda *a: tail(*a).wait(), N_EXPERTS + 1)

    xt = xn_ref[...]
    blk = STAGE_BLOCK
    for r0 in range(0, n_rows, blk):
        rid = lax.broadcasted_iota(I32, (blk, tw), 0) + r0
        hit = rid == pos_ref[0:1, :]
        for k in range(1, TOP_K):
            hit = jnp.logical_or(hit, rid == pos_ref[k:k + 1, :])
        srt = _dot(jnp.where(hit, 1.0, 0.0).astype(BF16), xt)
        stage[slot, r0:r0 + blk, :] = _pack_pairs(srt[:, :half], srt[:, half:])

    @pl.when(i > 0)
    def _():
        _plan_wait(i - 1, plan, functools.partial(copy, 1 - slot))

    _plan_copies(i, plan, lambda *a: copy(slot, *a).start())

    @pl.when(i == n_steps - 1)
    def _():
        _plan_wait(i, plan, functools.partial(copy, slot))


def _dispatch(xn, pos8, plan, tails, n_slots):
    n_tok, d = xn.shape
    tw = DISPATCH_TILE
    n_steps = n_tok // tw
    return pl.pallas_call(
        functools.partial(_dispatch_kernel, n_steps=n_steps),
        grid_spec=pltpu.PrefetchScalarGridSpec(
            num_scalar_prefetch=10,
            grid=(n_steps,),
            in_specs=[
                pl.BlockSpec((tw, d), lambda i, *_: (i, 0)),
                pl.BlockSpec((SUBLANES, tw), lambda i, *_: (0, i)),
            ],
            out_specs=pl.BlockSpec(memory_space=pl.ANY),
            scratch_shapes=[
                pltpu.VMEM((2, STAGE_ROWS, d // 2), U32),
                pltpu.VMEM((BIG_CHUNK, d // 2), U32),
                pltpu.SemaphoreType.DMA(()),
                pltpu.SemaphoreType.DMA(()),
            ],
        ),
        out_shape=jax.ShapeDtypeStruct((n_slots, d // 2), U32),
        compiler_params=pltpu.CompilerParams(
            dimension_semantics=("arbitrary",), vmem_limit_bytes=VMEM_LIMIT),
        name="dispatch",
    )(*plan, *tails, xn, pos8)


def _expert_kernel(te_ref, nu_ref, x_ref, wup_ref, bup_ref, wdn_ref, bdn_ref, y_ref,
                   wup_bf, wdn_bf, *, d_ff):
    i = pl.program_id(0)
    n_used = nu_ref[0]

    @pl.when(jnp.logical_and(i < n_used,
                             jnp.logical_or(i == 0, te_ref[i] != te_ref[jnp.maximum(i - 1, 0)])))
    def _():
        wup_bf[...] = wup_ref[0].astype(BF16)
        wdn_bf[...] = wdn_ref[0].astype(BF16)

    @pl.when(i < n_used)
    def _():
        lo, hi = _unpack_pairs(x_ref[...])
        half = lo.shape[1]
        h = _dot(lo, wup_bf[:half, :]) + _dot(hi, wup_bf[half:, :]) + bup_ref[0]
        gl = jnp.minimum(h[:, :d_ff], SWIGLU_LIMIT)
        lin = jnp.clip(h[:, d_ff:], -SWIGLU_LIMIT, SWIGLU_LIMIT)
        act = gl * _sigmoid(SWIGLU_ALPHA * gl) * (lin + 1.0)
        y = _dot(act.astype(BF16), wdn_bf[...]) + bdn_ref[0]
        yb = y.astype(BF16).astype(F32)
        y_ref[...] = _pack_pairs(yb[:, :half], yb[:, half:])

    @pl.when(i >= n_used)
    def _():
        y_ref[...] = jnp.zeros_like(y_ref)


def _experts(xs, tile_e, n_used, w_up, b_up, w_down, b_down):
    n_slots, half = xs.shape
    d = 2 * half
    tm = EXPERT_TILE
    n_tiles = n_slots // tm
    d_ff = w_down.shape[1]
    return pl.pallas_call(
        functools.partial(_expert_kernel, d_ff=d_ff),
        grid_spec=pltpu.PrefetchScalarGridSpec(
            num_scalar_prefetch=2,
            grid=(n_tiles,),
            in_specs=[
                pl.BlockSpec((tm, half), lambda i, te, nu: (jnp.minimum(i, nu[0] - 1), 0)),
                pl.BlockSpec((1, d, 2 * d_ff), lambda i, te, nu: (te[i], 0, 0)),
                pl.BlockSpec((1, 1, 2 * d_ff), lambda i, te, nu: (te[i], 0, 0)),
                pl.BlockSpec((1, d_ff, d), lambda i, te, nu: (te[i], 0, 0)),
                pl.BlockSpec((1, 1, d), lambda i, te, nu: (te[i], 0, 0)),
            ],
            out_specs=pl.BlockSpec((tm, half), lambda i, te, nu: (i, 0)),
            scratch_shapes=[
                pltpu.VMEM((d, 2 * d_ff), BF16),
                pltpu.VMEM((d_ff, d), BF16),
            ],
        ),
        out_shape=jax.ShapeDtypeStruct((n_slots, half), U32),
        compiler_params=pltpu.CompilerParams(
            dimension_semantics=("arbitrary",), vmem_limit_bytes=VMEM_LIMIT),
        name="experts",
    )(tile_e, n_used, xs, w_up, b_up.reshape(N_EXPERTS, 1, -1),
      w_down, b_down.reshape(N_EXPERTS, 1, -1))


def _combine_kernel(*refs, n_steps):
    plan = refs[:7]
    ys_hbm, h2_ref, posc_ref, gatec_ref, ng_ref, out_ref, stage, sem = refs[7:]
    i = pl.program_id(0)
    slot = i % 2
    tw, d = h2_ref.shape
    n_rows = stage.shape[1]

    def copy(s, stage_row, buf_row, rows):
        src = ys_hbm.at[pl.ds(pl.multiple_of(buf_row, SLAB_ALIGN), rows), :]
        dst = stage.at[s, pl.ds(pl.multiple_of(stage_row, SLAB_ALIGN), rows), :]
        return pltpu.make_async_copy(src, dst, sem.at[s])

    @pl.when(i == 0)
    def _():
        stage[...] = jnp.zeros_like(stage)
        _plan_copies(0, plan, lambda *a: copy(0, *a).start())

    @pl.when(i + 1 < n_steps)
    def _():
        _plan_copies(i + 1, plan, lambda *a: copy(1 - slot, *a).start())

    _plan_wait(i, plan, functools.partial(copy, slot))

    y_lo, y_hi = _unpack_pairs(stage[slot])
    blk = 128
    for t0 in range(0, tw, blk):
        cid = lax.broadcasted_iota(I32, (blk, n_rows), 1)
        pg = jnp.zeros((blk, n_rows), F32)
        for k in range(TOP_K):
            pg = jnp.where(cid == posc_ref[t0:t0 + blk, k:k + 1],
                           gatec_ref[t0:t0 + blk, k:k + 1], pg)
        pgb = pg.astype(BF16)
        moe = jnp.concatenate([_dot(pgb, y_lo), _dot(pgb, y_hi)], axis=1)
        acc = h2_ref[t0:t0 + blk, :].astype(F32) + moe
        ms = jnp.mean(acc * acc, axis=-1, keepdims=True)
        out_ref[t0:t0 + blk, :] = acc * lax.rsqrt(ms + RMS_EPS) * ng_ref[...]


def _combine(ys, h2, pos_cols, gate_cols, plan, norm_g):
    n_tok, d = h2.shape
    tw = DISPATCH_TILE
    n_steps = n_tok // tw
    return pl.pallas_call(
        functools.partial(_combine_kernel, n_steps=n_steps),
        grid_spec=pltpu.PrefetchScalarGridSpec(
            num_scalar_prefetch=7,
            grid=(n_steps,),
            in_specs=[
                pl.BlockSpec(memory_space=pl.ANY),
                pl.BlockSpec((tw, d), lambda i, *_: (i, 0)),
                pl.BlockSpec((tw, SUBLANES), lambda i, *_: (i, 0)),
                pl.BlockSpec((tw, SUBLANES), lambda i, *_: (i, 0)),
                pl.BlockSpec((1, d), lambda i, *_: (0, 0)),
            ],
            out_specs=pl.BlockSpec((tw, d), lambda i, *_: (i, 0)),
            scratch_shapes=[
                pltpu.VMEM((2, STAGE_ROWS, d // 2), U32),
                pltpu.SemaphoreType.DMA((2,)),
            ],
        ),
        out_shape=jax.ShapeDtypeStruct((n_tok, d), F32),
        compiler_params=pltpu.CompilerParams(
            dimension_semantics=("arbitrary",), vmem_limit_bytes=VMEM_LIMIT),
        name="combine",
    )(*plan, ys, h2, pos_cols, gate_cols, norm_g.reshape(1, -1).astype(F32))


def _moe(h2, xn, eidx, gate, rank, tcnt, w_up, b_up, w_down, b_down, norm_final_g):
    n_tok, d = h2.shape
    tm = EXPERT_TILE
    tw = DISPATCH_TILE
    n_steps = n_tok // tw
    n_exp = N_EXPERTS
    e_ids = jnp.arange(n_exp, dtype=I32)
    al = SLAB_ALIGN
    tile_cnt = tcnt.reshape(n_steps, n_exp, LANES)[:, :, 0].astype(I32)
    slab_rows = (tile_cnt + al - 1) // al * al
    exp_rows = jnp.sum(slab_rows, axis=0)
    region = (exp_rows + tm - 1) // tm * tm
    pad_end = jnp.cumsum(region)
    pad_start = pad_end - region
    n_slots = (n_tok * TOP_K + n_steps * n_exp * (al - 1) + n_exp * (tm - 1) + tm - 1) // tm * tm
    n_tiles = n_slots // tm

    tile_carry = jnp.cumsum(tile_cnt, axis=0) - tile_cnt
    tile_off = jnp.cumsum(slab_rows, axis=1) - slab_rows
    slab_row0 = pad_start[None, :] + jnp.cumsum(slab_rows, axis=0) - slab_rows

    def chunks(rows):
        return rows // BIG_CHUNK, rows % BIG_CHUNK // al

    def flat(cnt, src0, dst0, size, n_max):
        run = jnp.cumsum(cnt, axis=1)
        j = jnp.arange(n_max, dtype=I32)
        e_of = jnp.sum(run[:, None, :] <= j[None, :, None], axis=-1)
        sel = e_of[:, :, None] == e_ids[None, None, :]
        pick = lambda a: jnp.sum(jnp.where(sel, a[:, None, :], 0), axis=-1)
        local = (j[None, :] - pick(run - cnt)) * size
        return run[:, -1], (pick(src0) + local).reshape(-1), (pick(dst0) + local).reshape(-1)

    nbig, nsmall = chunks(slab_rows)
    big_tot, big_src, big_dst = flat(nbig, tile_off, slab_row0, BIG_CHUNK, MAX_BIG)
    small_tot, small_src, small_dst = flat(nsmall, tile_off + nbig * BIG_CHUNK,
                                           slab_row0 + nbig * BIG_CHUNK, al, MAX_SMALL)
    pad_groups = (jnp.sum(slab_rows, axis=1) - tw * TOP_K) // al
    plan = (big_tot, small_tot, pad_groups, big_src, big_dst, small_src, small_dst)
    tail_rows = jnp.concatenate([region - exp_rows, n_slots - pad_end[-1:]])
    tails = (*chunks(tail_rows), jnp.concatenate([pad_start + exp_rows, pad_end[-1:]]))

    base_t = jnp.repeat((tile_off - tile_carry).T, tw, axis=1)
    e_sel = eidx[:TOP_K][None] == e_ids[:, None, None]
    pos = jnp.sum(jnp.where(e_sel, base_t[:, None, :], 0), axis=0) + rank[:TOP_K]
    pos8 = jnp.concatenate([pos, jnp.full((SUBLANES - TOP_K, n_tok), -1, I32)], axis=0)

    tile_start = jnp.arange(n_tiles, dtype=I32) * tm
    tile_e = jnp.minimum(jnp.sum(pad_end[None, :] <= tile_start[:, None], axis=-1),
                         n_exp - 1).astype(I32)
    n_used = (pad_end[-1] // tm).astype(I32).reshape(1)

    xs = _dispatch(xn, pos8, plan, tails, n_slots)
    ys = _experts(xs, tile_e, n_used, w_up, b_up, w_down, b_down)
    return _combine(ys, h2, pos8.T, gate.T, plan, norm_final_g)


def kernel(x, norm_mix_g, w_in, s5_log_dt, s5_a_re, s5_a_im, s5_b_re, s5_b_im, s5_c_re, s5_c_im,
           s5_d, s5_w_glu, s5_b_glu, ml_conv_w, ml_conv_b, ml_b_gates, ml_norm_g, w_out,
           norm_ffn_g, w_router, b_router, w_up, b_up, w_down, b_down, norm_final_g):
    bsz, seq, d = x.shape
    depth = w_in.shape[0]
    assert depth == 1, "single-layer block"
    l = 0
    x2d = x.reshape(bsz * seq, d)
    qk, v, o, ut, gt = _in_proj(x2d, norm_mix_g[l], w_in[l], ml_b_gates[l], bsz, seq)
    g_t = _s5(ut, s5_log_dt[l], s5_a_re[l], s5_a_im[l], s5_b_re[l], s5_b_im[l],
              s5_c_re[l], s5_c_im[l], s5_d[l])
    y_ml = _mlstm(qk, v, o, gt, ml_conv_w[l], ml_conv_b[l], ml_norm_g[l], bsz, seq)
    y_ml = y_ml.reshape(bsz * seq, -1)
    h2, xn, eidx, gate, rank, tcnt = _post(x2d, g_t, y_ml, s5_w_glu[l], s5_b_glu[l], w_out[l],
                                           norm_ffn_g[l], w_router[l], b_router[l])
    out = _moe(h2, xn, eidx, gate, rank, tcnt, w_up[l], b_up[l], w_down[l], b_down[l],
               norm_final_g)
    return out.reshape(bsz, seq, d)
```

```python
import functools
import math

import jax
import jax.numpy as jnp
from jax import lax
from jax.experimental import pallas as pl
from jax.experimental.pallas import tpu as pltpu

F32 = jnp.float32
BF16 = jnp.bfloat16
I32 = jnp.int32
U32 = jnp.uint32

S5_GROUP = 16
S5_STATE = 64
ML_HEADS = 4
CONV_WIDTH = 4
N_EXPERTS = 32
TOP_K = 4
SWIGLU_LIMIT = 7.0
SWIGLU_ALPHA = 1.702
RMS_EPS = 1e-5
LN_EPS = 1e-6

LANES = 128
SUBLANES = 8
S5_CHUNK = LANES
ML_CHUNK = 128
GATE_CHUNKS = 8
EXP_CAP = 1e38
PROJ_TILE = 1024
EXPERT_TILE = 512
DISPATCH_TILE = 256
SLAB_ALIGN = SUBLANES
BIG_CHUNK = 32
STAGE_ROWS = DISPATCH_TILE * TOP_K + N_EXPERTS * SLAB_ALIGN
STAGE_BLOCK = STAGE_ROWS // 2
MAX_BIG = STAGE_ROWS // BIG_CHUNK
MAX_SMALL = N_EXPERTS * (BIG_CHUNK // SLAB_ALIGN - 1)
VMEM_LIMIT = 56 * 1024 * 1024

_NT = (((1,), (1,)), ((), ()))
_TN = (((0,), (0,)), ((), ()))


def _dot(a, b):
    return jnp.dot(a, b, preferred_element_type=F32)


def _dot_nt(a, b):
    return lax.dot_general(a, b, _NT, preferred_element_type=F32)


def _dot_tn(a, b):
    return lax.dot_general(a, b, _TN, preferred_element_type=F32)


def _split3(x):
    p1 = x.astype(BF16)
    r1 = x - p1.astype(F32)
    p2 = r1.astype(BF16)
    r2 = r1 - p2.astype(F32)
    return p1, p2, r2.astype(BF16)


def _sigmoid(x):
    return 0.5 * jnp.tanh(0.5 * x) + 0.5


def _inproj_kernel(x_ref, g_ref, wnat_ref, wut_ref, wgt_ref, bg_ref,
                   qk_ref, v_ref, o_ref, ut_ref, gt_ref, *, width):
    x = x_ref[...]
    ms = jnp.mean(x * x, axis=-1, keepdims=True)
    hn = (x * lax.rsqrt(ms + RMS_EPS) * g_ref[...]).astype(BF16)
    nat = _dot(hn, wnat_ref[...])
    qk_ref[...] = nat[:, :2 * width].astype(BF16)
    v_ref[...] = nat[:, 2 * width:3 * width].astype(BF16)
    o_ref[...] = nat[:, 3 * width:].astype(BF16)
    ut_ref[0] = _dot_nt(wut_ref[...], hn).astype(BF16)
    gt_ref[0] = _dot_nt(wgt_ref[...], hn) + bg_ref[...]


def _in_proj(x2d, norm_g, w_in, b_gates, bsz, seq):
    n_tok, d = x2d.shape
    s5w = d // 2
    mlw = d - s5w
    tm = min(PROJ_TILE, seq)
    tpb = seq // tm
    w_bf = w_in.astype(BF16)
    w_nat = w_bf[:, s5w:s5w + 4 * mlw]
    w_ut = w_bf[:, :s5w].T
    n_gate = 2 * ML_HEADS
    w_gt = jnp.zeros((16, d), BF16).at[:n_gate].set(w_bf[:, s5w + 4 * mlw:].T)
    b_g = jnp.zeros((16, 1), F32).at[:n_gate, 0].set(b_gates.astype(F32))
    grid = (n_tok // tm,)
    full = lambda i: (0, 0)
    return pl.pallas_call(
        functools.partial(_inproj_kernel, width=mlw),
        grid=grid,
        in_specs=[
            pl.BlockSpec((tm, d), lambda i: (i, 0)),
            pl.BlockSpec((1, d), full),
            pl.BlockSpec((d, 4 * mlw), full),
            pl.BlockSpec((s5w, d), full),
            pl.BlockSpec((16, d), full),
            pl.BlockSpec((16, 1), full),
        ],
        out_specs=[
            pl.BlockSpec((tm, 2 * mlw), lambda i: (i, 0)),
            pl.BlockSpec((tm, mlw), lambda i: (i, 0)),
            pl.BlockSpec((tm, mlw), lambda i: (i, 0)),
            pl.BlockSpec((1, s5w, tm), lambda i: (i // tpb, 0, i % tpb)),
            pl.BlockSpec((1, 16, tm), lambda i: (i // tpb, 0, i % tpb)),
        ],
        out_shape=[
            jax.ShapeDtypeStruct((n_tok, 2 * mlw), BF16),
            jax.ShapeDtypeStruct((n_tok, mlw), BF16),
            jax.ShapeDtypeStruct((n_tok, mlw), BF16),
            jax.ShapeDtypeStruct((bsz, s5w, seq), BF16),
            jax.ShapeDtypeStruct((bsz, 16, seq), F32),
        ],
        compiler_params=pltpu.CompilerParams(
            dimension_semantics=("parallel",), vmem_limit_bytes=VMEM_LIMIT),
        name="in_proj",
    )(x2d, norm_g.reshape(1, d).astype(F32), w_nat, w_ut, w_gt, b_g)


def _s5_kernel(d_ref, x_ref, prm_ref, c_ref, bt_ref, out_ref, *, bsz, nc):
    L = S5_CHUNK
    P = S5_GROUP
    N = S5_STATE
    grp = pl.program_id(0)
    lane = lax.broadcasted_iota(I32, (1, 2 * N), 1)
    lo = lane < N
    a_re = jnp.minimum(prm_ref[0, 0:1, :], -1e-4)
    a_im = prm_ref[0, 1:2, :]
    dt = jnp.exp(prm_ref[0, 2:3, :])
    zr = dt * a_re
    zi = dt * a_im

    quarter = jnp.where(lo, 0.0, 0.5 * math.pi)

    def powrows(e):
        return jnp.exp(e * zr) * jnp.cos(e * zi - quarter)

    def swap(tab):
        return pltpu.roll(tab, N, 1)

    def cmul(tab, c):
        return c[0] * tab + jnp.where(lo, -c[1], c[1]) * swap(tab)

    n_dbl = max(nc - 1, 0).bit_length()
    exps = [1] + [SUBLANES << k for k in range(int(math.log2(L // SUBLANES)))] + [L << k for k in range(n_dbl)]
    e_col = jnp.concatenate([jnp.full((1, 1), float(e), F32) for e in exps]
                            + [jnp.zeros((-len(exps) % SUBLANES, 1), F32)], axis=0)
    mag = jnp.exp(e_col * zr)
    pw_r = mag * jnp.cos(e_col * zi)
    pw_i = mag * jnp.sin(e_col * zi)
    apow = {e: (pw_r[k:k + 1], pw_i[k:k + 1]) for k, e in enumerate(exps)}

    def powtab(descending):
        i8 = lax.broadcasted_iota(I32, (SUBLANES, 1), 0).astype(F32)
        tab = powrows(SUBLANES - 1.0 - i8 if descending else i8)
        rows = SUBLANES
        while rows < L:
            more = cmul(tab, apow[rows])
            tab = jnp.concatenate([more, tab] if descending else [tab, more], axis=0)
            rows *= 2
        return tab

    er, ei = apow[1]
    den = a_re * a_re + a_im * a_im
    coef_r = ((er - 1.0) * a_re + ei * a_im) / den
    coef_i = (ei * a_re - (er - 1.0) * a_im) / den
    c_r = c_ref[0, 0]
    c_i = c_ref[0, 1]
    bb_r = coef_r * bt_ref[0, 0] - coef_i * bt_ref[0, 1]
    bb_i = coef_r * bt_ref[0, 1] + coef_i * bt_ref[0, 0]

    cb_rows = []
    for q in range(P):
        cbr = c_r * bb_r[q:q + 1] - c_i * bb_i[q:q + 1]
        cbi = c_r * bb_i[q:q + 1] + c_i * bb_r[q:q + 1]
        cb_rows.append(jnp.where(lo, cbr, -cbi))
    cb = jnp.concatenate(cb_rows, axis=0)
    pt0 = powtab(False)
    c1, c2, c3 = _split3(cb)
    t1, t2, t3 = _split3(pt0)
    kmat = (_dot_nt(c1, t1) + _dot_nt(c1, t2) + _dot_nt(c2, t1)
            + _dot_nt(c2, t2) + _dot_nt(c1, t3) + _dot_nt(c3, t1))

    pt_rev = powtab(True)
    pt_rev_sw = swap(pt_rev)
    f_rows = []
    for q in range(P):
        a1 = bb_r[q:q + 1]
        a2 = jnp.where(lo, -bb_i[q:q + 1], bb_i[q:q + 1])
        f_rows.append((a1 * pt_rev + a2 * pt_rev_sw).astype(BF16))
    fmat = jnp.concatenate(f_rows, axis=0)
    pt1 = cmul(pt0, apow[1])
    pt1_sw = swap(pt1)
    e_rows = []
    for p in range(P):
        b1 = jnp.where(lo, c_r[p:p + 1], -c_r[p:p + 1])
        b2 = -c_i[p:p + 1]
        e_rows.append((b1 * pt1 + b2 * pt1_sw).astype(BF16))
    emat_t = jnp.concatenate(e_rows, axis=0)

    lhs = jnp.concatenate(
        [jnp.concatenate([x_ref[b, q] for q in range(P)], axis=1) for b in range(bsz)], axis=0)
    s_end = _dot(lhs, fmat)

    rr = lax.broadcasted_iota(I32, (L, L), 0)
    cc = lax.broadcasted_iota(I32, (L, L), 1)
    causal = cc >= rr
    y = None
    for q0 in range(0, P, 2):
        rows = []
        for q in (q0, q0 + 1):
            tiles = []
            for p in range(P):
                kb = jnp.broadcast_to(kmat[q * P + p:q * P + p + 1, :], (L, L))
                toe = pltpu.roll(kb, 0, 1, stride=1, stride_axis=0)
                tiles.append(jnp.where(causal, toe, 0.0).astype(BF16))
            rows.append(jnp.concatenate(tiles, axis=1))
        part = _dot(lhs[:, q0 * L:(q0 + 2) * L], jnp.concatenate(rows, axis=0))
        y = part if y is None else y + part

    m_rows = bsz * nc
    ridx = lax.broadcasted_iota(I32, (m_rows, 1), 0)
    cidx = ridx % nc
    h = jnp.where(cidx >= 1, pltpu.roll(s_end, 1, 0), 0.0)
    d = 1
    while d < nc:
        sh = jnp.where(cidx >= d, pltpu.roll(h, d, 0), 0.0)
        h = h + cmul(sh, apow[d * L])
        d *= 2
    y = y + _dot_nt(h.astype(BF16), emat_t)

    for b in range(bsz):
        for p in range(P):
            yp = (y[b * nc:(b + 1) * nc, p * L:(p + 1) * L]
                  + d_ref[grp * P + p] * x_ref[b, p].astype(F32))
            out_ref[b, p] = jax.nn.gelu(yp).astype(BF16)


def _s5(ut, log_dt, a_re, a_im, b_re, b_im, c_re, c_im, d_skip):
    bsz, s5w, seq = ut.shape
    L = S5_CHUNK
    nc = seq // L
    groups = s5w // S5_GROUP
    n = S5_STATE
    x4 = ut.reshape(bsz, s5w, nc, L)
    dup = lambda t: jnp.concatenate([t, t], axis=-1).astype(F32)
    prm = jnp.zeros((groups, SUBLANES, 2 * n), F32)
    prm = prm.at[:, 0].set(dup(a_re)).at[:, 1].set(dup(a_im))
    prm = prm.at[:, 2].set(jnp.broadcast_to(log_dt.astype(F32)[:, None], (groups, 2 * n)))
    cpar = jnp.stack([dup(c_re), dup(c_im)], axis=1)
    btpar = jnp.stack([dup(jnp.swapaxes(b_re, 1, 2)), dup(jnp.swapaxes(b_im, 1, 2))], axis=1)
    out = pl.pallas_call(
        functools.partial(_s5_kernel, bsz=bsz, nc=nc),
        grid_spec=pltpu.PrefetchScalarGridSpec(
            num_scalar_prefetch=1,
            grid=(groups,),
            in_specs=[
                pl.BlockSpec((bsz, S5_GROUP, nc, L), lambda g, d: (0, g, 0, 0)),
                pl.BlockSpec((1, SUBLANES, 2 * n), lambda g, d: (g, 0, 0)),
                pl.BlockSpec((1, 2, S5_GROUP, 2 * n), lambda g, d: (g, 0, 0, 0)),
                pl.BlockSpec((1, 2, S5_GROUP, 2 * n), lambda g, d: (g, 0, 0, 0)),
            ],
            out_specs=pl.BlockSpec((bsz, S5_GROUP, nc, L), lambda g, d: (0, g, 0, 0)),
        ),
        out_shape=jax.ShapeDtypeStruct((bsz, s5w, nc, L), BF16),
        compiler_params=pltpu.CompilerParams(
            dimension_semantics=("parallel",), vmem_limit_bytes=VMEM_LIMIT),
        name="s5",
    )(d_skip.astype(F32), x4, prm, cpar, btpar)
    return out.reshape(bsz, s5w, seq)


def _log_sigmoid(x):
    return jnp.minimum(x, 0.0) - jnp.log(1.0 + jnp.exp(-jnp.abs(x)))


def _gates_kernel(gt_ref, out_ref, m_scr, *, bsz, chunk, n_sub):
    L = chunk
    H = ML_HEADS
    step = pl.program_id(0)

    @pl.when(step == 0)
    def _():
        m_scr[...] = jnp.zeros_like(m_scr)

    rr = lax.broadcasted_iota(I32, (L, L), 0)
    cc = lax.broadcasted_iota(I32, (L, L), 1)
    utri = (rr <= cc).astype(BF16)
    grow = lax.broadcasted_iota(I32, (16, 1), 0)
    lane_row = lax.broadcasted_iota(I32, (SUBLANES, L), 1)
    m_prev = [m_scr[b][0:H, 0:1] for b in range(bsz)]
    pairs = [(j, b) for j in range(n_sub) for b in range(bsz)]

    irow, brow, cm = {}, {}, {}
    for j, b in pairs:
        g = gt_ref[b, :, j * L:(j + 1) * L]
        g2 = jnp.where(grow >= H, _log_sigmoid(g), g)
        p1, p2, p3 = _split3(g2)
        brow[j, b] = (_dot(p1, utri) + _dot(p2, utri) + _dot(p3, utri))[H:2 * H]
        irow[j, b] = g[0:H]
    for j, b in pairs:
        ib = irow[j, b] - brow[j, b]
        x = jnp.concatenate([ib, ib], axis=0)
        sft = 1
        while sft < L:
            x = jnp.maximum(x, jnp.where(lane_row >= sft, pltpu.roll(x, sft, 1), -jnp.inf))
            sft *= 2
        cm[j, b] = x[0:H]
    for j, b in pairs:
        mp = m_prev[b]
        mm = jnp.maximum(cm[j, b], mp)
        b_last = brow[j, b][:, L - 1:L]
        m_next = b_last + jnp.maximum(mp, cm[j, b][:, L - 1:L])
        planes = (mm, jnp.exp(mp - mm), jnp.minimum(jnp.exp(-(brow[j, b] + mm)), EXP_CAP),
                  jnp.exp(b_last - brow[j, b] + irow[j, b] - m_next), irow[j, b] - brow[j, b],
                  jnp.broadcast_to(jnp.exp(b_last + mp - m_next), (H, L)))
        for k, rows in enumerate(planes):
            out_ref[k, b, :, j * L:(j + 1) * L] = jnp.concatenate([rows, rows], axis=0)
        m_prev[b] = m_next
    for b in range(bsz):
        m_scr[b] = jnp.broadcast_to(jnp.concatenate([m_prev[b], m_prev[b]], axis=0),
                                    (SUBLANES, LANES))


def _gates(gt, bsz, seq, chunk):
    n_sub = min(GATE_CHUNKS, seq // chunk)
    blk = n_sub * chunk
    return pl.pallas_call(
        functools.partial(_gates_kernel, bsz=bsz, chunk=chunk, n_sub=n_sub),
        grid=(seq // blk,),
        in_specs=[pl.BlockSpec((bsz, 16, blk), lambda c: (0, 0, c))],
        out_specs=pl.BlockSpec((6, bsz, SUBLANES, blk), lambda c: (0, 0, 0, c)),
        out_shape=jax.ShapeDtypeStruct((6, bsz, SUBLANES, seq), F32),
        scratch_shapes=[pltpu.VMEM((bsz, SUBLANES, LANES), F32)],
        compiler_params=pltpu.CompilerParams(
            dimension_semantics=("arbitrary",), vmem_limit_bytes=VMEM_LIMIT),
        name="gates",
    )(gt)


def _mlstm_kernel(qk_ref, v_ref, o_ref, gr_ref, cw_ref, cb_ref, ng_ref, y_ref,
                  tail_scr, c_scr, *, bsz, chunk, width):
    L = chunk
    H = ML_HEADS
    dh = width // H
    step = pl.program_id(0)
    assert L == LANES and dh == LANES, "column replication below uses one 128x128 tile per head"

    @pl.when(step == 0)
    def _():
        tail_scr[...] = jnp.zeros_like(tail_scr)
        c_scr[...] = jnp.zeros_like(c_scr)

    rr = lax.broadcasted_iota(I32, (L, L), 0)
    cc = lax.broadcasted_iota(I32, (L, L), 1)
    causal = cc <= rr
    eye = (rr == cc).astype(BF16)
    ones_blk = jnp.ones((L, dh), BF16)
    scale = 1.0 / math.sqrt(dh)
    inv_dh = 1.0 / dh
    n_str = bsz * H
    streams = [(b, hd) for b in range(bsz) for hd in range(H)]

    def replicate(rows, two_terms):
        p = jnp.concatenate([jnp.broadcast_to(rows[h:h + 1], (LANES, L)) for h in range(H)], axis=0)
        p1 = p.astype(BF16)
        if not two_terms:
            return _dot_nt(eye, p1)
        return _dot_nt(eye, p1) + _dot_nt(eye, (p - p1.astype(F32)).astype(BF16))

    def rowsum(x):
        hi = x.astype(BF16)
        lo = (x - hi.astype(F32)).astype(BF16)
        return _dot(hi, ones_blk) + _dot(lo, ones_blk)

    tails = [tail_scr[b] for b in range(bsz)]
    caugs = [c_scr[i] for i in range(n_str)]
    reps = [[replicate(gr_ref[k, b][0:H], k == 0) for b in range(bsz)] for k in range(4)]
    col = lambda k: [reps[k][b][:, hd * LANES:(hd + 1) * LANES] for b, hd in streams]
    mm_rep, w_inter, enm, ws_rep = col(0), col(1), col(2), col(3)
    ib_row = [gr_ref[4, b][hd:hd + 1] for b, hd in streams]
    decays = [gr_ref[5, b][hd:hd + 1, 0:1] for b, hd in streams]

    q_all, k_all, new_tails = [], [], []
    for b in range(bsz):
        xqk = qk_ref[b].astype(F32)
        ext = jnp.concatenate([tails[b], xqk], axis=0)
        acc = jnp.broadcast_to(cb_ref[...], xqk.shape)
        for j in range(CONV_WIDTH):
            back = CONV_WIDTH - 1 - j
            sh = ext if back == 0 else pltpu.roll(ext, back, 0)
            acc = acc + cw_ref[j:j + 1, :] * sh[SUBLANES:]
        new_tails.append(xqk[L - SUBLANES:])
        qkc = acc * _sigmoid(acc)
        q_all.append(qkc[:, :width].astype(BF16))
        k_all.append((qkc[:, width:] * scale).astype(BF16))

    q_h = [q_all[b][:, hd * dh:(hd + 1) * dh] for b, hd in streams]
    k_h = [k_all[b][:, hd * dh:(hd + 1) * dh] for b, hd in streams]
    vaug = [jnp.concatenate([v_ref[b][:, hd * dh:(hd + 1) * dh], ones_blk], axis=1)
            for b, hd in streams]
    scores = [_dot_nt(q_h[i], k_h[i]) for i in range(n_str)]
    inter = [_dot(q_h[i], caugs[i].astype(BF16)) for i in range(n_str)]
    s_bf = [(scores[i] * jnp.exp(jnp.where(causal, ib_row[i] - mm_rep[i], -jnp.inf))).astype(BF16)
            for i in range(n_str)]
    kw = [(k_h[i].astype(F32) * ws_rep[i]).astype(BF16) for i in range(n_str)]
    intra = [_dot(s_bf[i], vaug[i]) for i in range(n_str)]
    upd = [_dot_tn(kw[i], vaug[i]) for i in range(n_str)]

    hh = []
    for i in range(n_str):
        num = w_inter[i] * inter[i][:, :dh] + intra[i][:, :dh]
        den = w_inter[i] * inter[i][:, dh:] + intra[i][:, dh:]
        hh.append(num / jnp.maximum(jnp.abs(den), enm[i]))
    mu = [rowsum(h) * inv_dh for h in hh]
    ctr = [hh[i] - mu[i] for i in range(n_str)]
    var = [rowsum(c * c) * inv_dh for c in ctr]
    outs = []
    for i, (b, hd) in enumerate(streams):
        sl = slice(hd * dh, (hd + 1) * dh)
        hn = ctr[i] * lax.rsqrt(var[i] + LN_EPS) * ng_ref[:, sl]
        outs.append(hn * _sigmoid(o_ref[b][:, sl].astype(F32)))

    for b in range(bsz):
        tail_scr[b] = new_tails[b]
        y_ref[b] = jnp.concatenate(outs[b * H:(b + 1) * H], axis=1).astype(BF16)
    for i in range(n_str):
        c_scr[i] = decays[i] * caugs[i] + upd[i]


def _mlstm(qk, v, o, gt, conv_w, conv_b, norm_g, bsz, seq):
    width = v.shape[-1]
    L = min(ML_CHUNK, seq)
    dh = width // ML_HEADS
    qk3 = qk.reshape(bsz, seq, 2 * width)
    v3 = v.reshape(bsz, seq, width)
    o3 = o.reshape(bsz, seq, width)
    gate_rows = _gates(gt, bsz, seq, L)
    full = lambda c: (0, 0)
    return pl.pallas_call(
        functools.partial(_mlstm_kernel, bsz=bsz, chunk=L, width=width),
        grid=(seq // L,),
        in_specs=[
            pl.BlockSpec((bsz, L, 2 * width), lambda c: (0, c, 0)),
            pl.BlockSpec((bsz, L, width), lambda c: (0, c, 0)),
            pl.BlockSpec((bsz, L, width), lambda c: (0, c, 0)),
            pl.BlockSpec((6, bsz, SUBLANES, L), lambda c: (0, 0, 0, c)),
            pl.BlockSpec((CONV_WIDTH, 2 * width), full),
            pl.BlockSpec((1, 2 * width), full),
            pl.BlockSpec((1, width), full),
        ],
        out_specs=pl.BlockSpec((bsz, L, width), lambda c: (0, c, 0)),
        out_shape=jax.ShapeDtypeStruct((bsz, seq, width), BF16),
        scratch_shapes=[
            pltpu.VMEM((bsz, SUBLANES, 2 * width), F32),
            pltpu.VMEM((bsz * ML_HEADS, dh, 2 * dh), F32),
        ],
        compiler_params=pltpu.CompilerParams(
            dimension_semantics=("arbitrary",), vmem_limit_bytes=VMEM_LIMIT),
        name="mlstm",
    )(qk3, v3, o3, gate_rows, conv_w.astype(F32), conv_b.reshape(1, -1).astype(F32),
      norm_g.reshape(1, -1).astype(F32))


def _post_kernel(x_ref, gt_ref, yml_ref, wglut_ref, bglu_ref, wout_ref, nffn_ref, wrt_ref, br_ref,
                 h2_ref, xn_ref, eidx_ref, gate_ref, rank_ref, tcnt_ref, carry_scr):
    step = pl.program_id(0)

    @pl.when(step == 0)
    def _():
        carry_scr[...] = jnp.zeros_like(carry_scr)

    gt = gt_ref[0]
    s5w = gt.shape[0]
    zt = _dot(wglut_ref[...], gt) + bglu_ref[...]
    s5t = (gt.astype(F32) * _sigmoid(zt)).astype(BF16)
    h2 = (x_ref[...] + _dot_tn(s5t, wout_ref[:s5w, :])
          + _dot(yml_ref[...], wout_ref[s5w:, :]))
    h2_ref[...] = h2
    ms = jnp.mean(h2 * h2, axis=-1, keepdims=True)
    xn = h2 * lax.rsqrt(ms + RMS_EPS) * nffn_ref[...]
    xb = xn.astype(BF16)
    xn_ref[...] = xb

    tm = xn.shape[0]
    logits = _dot_nt(wrt_ref[...], xb) + br_ref[...]
    eio = lax.broadcasted_iota(I32, (N_EXPERTS, tm), 0).astype(F32)
    vals = logits
    onehot = jnp.zeros((N_EXPERTS, tm), F32)
    idxs, tops = [], []
    for _ in range(TOP_K):
        mx = jnp.max(vals, axis=0, keepdims=True)
        idx = jnp.min(jnp.where(vals == mx, eio, float(N_EXPERTS)), axis=0, keepdims=True)
        sel = eio == idx
        onehot = onehot + sel.astype(F32)
        vals = jnp.where(sel, -jnp.inf, vals)
        idxs.append(idx)
        tops.append(mx)
    exps = [jnp.exp(t - tops[0]) for t in tops]
    tot = exps[0] + exps[1] + exps[2] + exps[3]
    pad_f = jnp.zeros((SUBLANES - TOP_K, tm), F32)
    eidx_ref[...] = jnp.concatenate(idxs + [pad_f], axis=0).astype(I32)
    gate_ref[...] = jnp.concatenate([e / tot for e in exps] + [pad_f], axis=0)

    rr = lax.broadcasted_iota(I32, (tm, tm), 0)
    cc = lax.broadcasted_iota(I32, (tm, tm), 1)
    before = (rr < cc).astype(BF16)
    carry = carry_scr[:, 0:1]
    rank_ex = _dot(onehot.astype(BF16), before) + carry
    ranks = [jnp.sum(jnp.where(eio == i, rank_ex, 0.0), axis=0, keepdims=True) for i in idxs]
    rank_ref[...] = jnp.concatenate(ranks + [pad_f], axis=0).astype(I32)
    tile_cnt = jnp.sum(onehot, axis=1, keepdims=True)
    carry_scr[...] = jnp.broadcast_to(carry + tile_cnt, carry_scr.shape)
    for s in range(tm // DISPATCH_TILE):
        sub = onehot[:, s * DISPATCH_TILE:(s + 1) * DISPATCH_TILE]
        tcnt_ref[0, s] = jnp.broadcast_to(jnp.sum(sub, axis=1, keepdims=True), carry_scr.shape)


def _post(x2d, g_t, y_ml, w_glu, b_glu, w_out, norm_g, w_router, b_router):
    n_tok, d = x2d.shape
    bsz, s5w, seq = g_t.shape
    mlw = y_ml.shape[-1]
    tm = min(PROJ_TILE, seq)
    tpb = seq // tm
    full = lambda i: (0, 0)
    row = lambda i: (i, 0)
    colb = lambda i: (0, i)
    return pl.pallas_call(
        _post_kernel,
        grid=(n_tok // tm,),
        in_specs=[
            pl.BlockSpec((tm, d), row),
            pl.BlockSpec((1, s5w, tm), lambda i: (i // tpb, 0, i % tpb)),
            pl.BlockSpec((tm, mlw), row),
            pl.BlockSpec((s5w, s5w), full),
            pl.BlockSpec((s5w, 1), full),
            pl.BlockSpec((d, d), full),
            pl.BlockSpec((1, d), full),
            pl.BlockSpec((N_EXPERTS, d), full),
            pl.BlockSpec((N_EXPERTS, 1), full),
        ],
        out_specs=[
            pl.BlockSpec((tm, d), row),
            pl.BlockSpec((tm, d), row),
            pl.BlockSpec((SUBLANES, tm), colb),
            pl.BlockSpec((SUBLANES, tm), colb),
            pl.BlockSpec((SUBLANES, tm), colb),
            pl.BlockSpec((1, tm // DISPATCH_TILE, N_EXPERTS, LANES), lambda i: (i, 0, 0, 0)),
        ],
        out_shape=[
            jax.ShapeDtypeStruct((n_tok, d), F32),
            jax.ShapeDtypeStruct((n_tok, d), BF16),
            jax.ShapeDtypeStruct((SUBLANES, n_tok), I32),
            jax.ShapeDtypeStruct((SUBLANES, n_tok), F32),
            jax.ShapeDtypeStruct((SUBLANES, n_tok), I32),
            jax.ShapeDtypeStruct((n_tok // tm, tm // DISPATCH_TILE, N_EXPERTS, LANES), F32),
        ],
        scratch_shapes=[pltpu.VMEM((N_EXPERTS, LANES), F32)],
        compiler_params=pltpu.CompilerParams(
            dimension_semantics=("arbitrary",), vmem_limit_bytes=VMEM_LIMIT),
        name="post_router",
    )(x2d, g_t, y_ml, w_glu.T.astype(BF16), b_glu.reshape(-1, 1).astype(F32), w_out.astype(BF16),
      norm_g.reshape(1, -1).astype(F32), w_router.T.astype(BF16),
      b_router.reshape(-1, 1).astype(F32))


def _slab_loop(base, nbig_ref, nsmall_ref, fn, n_slabs=N_EXPERTS):
    def per_expert(e, carry):
        idx = base + e
        nbig = nbig_ref[idx]

        def big(j, c2):
            fn(idx, j * BIG_CHUNK, BIG_CHUNK)
            return c2

        def small(j, c2):
            fn(idx, nbig * BIG_CHUNK + j * SLAB_ALIGN, SLAB_ALIGN)
            return c2

        lax.fori_loop(0, nbig, big, 0)
        lax.fori_loop(0, nsmall_ref[idx], small, 0)
        return carry

    lax.fori_loop(0, n_slabs, per_expert, 0)


def _plan_copies(step, plan, fn):
    nbig_ref, nsmall_ref, _, bsrc_ref, bdst_ref, ssrc_ref, sdst_ref = plan

    def big(j, carry):
        k = step * MAX_BIG + j
        fn(bsrc_ref[k], bdst_ref[k], BIG_CHUNK)
        return carry

    def small(j, carry):
        k = step * MAX_SMALL + j
        fn(ssrc_ref[k], sdst_ref[k], SLAB_ALIGN)
        return carry

    lax.fori_loop(0, nbig_ref[step], big, 0)
    lax.fori_loop(0, nsmall_ref[step], small, 0)


def _plan_wait(step, plan, copy):
    copy(0, 0, DISPATCH_TILE * TOP_K).wait()

    def one(j, carry):
        copy(0, 0, SLAB_ALIGN).wait()
        return carry

    lax.fori_loop(0, plan[2][step], one, 0)


def _unpack_pairs(words):
    lo = lax.bitcast_convert_type(words << 16, F32).astype(BF16)
    hi = lax.bitcast_convert_type(words & jnp.uint32(0xFFFF0000), F32).astype(BF16)
    return lo, hi


def _pack_pairs(lo_f32, hi_f32):
    lo = lax.bitcast_convert_type(lo_f32, U32) >> 16
    hi = lax.bitcast_convert_type(hi_f32, U32) & jnp.uint32(0xFFFF0000)
    return hi | lo


def _dispatch_kernel(*refs, n_steps):
    plan = refs[:7]
    znbig_ref, znsmall_ref, zrow_ref, xn_ref, pos_ref, xs_hbm, stage, zbuf, sem, zsem = refs[7:]
    i = pl.program_id(0)
    slot = i % 2
    tw, d = xn_ref.shape
    n_rows = stage.shape[1]
    half = d // 2

    def copy(s, stage_row, buf_row, rows):
        src = stage.at[s, pl.ds(pl.multiple_of(stage_row, SLAB_ALIGN), rows), :]
        dst = xs_hbm.at[pl.ds(pl.multiple_of(buf_row, SLAB_ALIGN), rows), :]
        return pltpu.make_async_copy(src, dst, sem)

    @pl.when(i == 0)
    def _():
        zbuf[...] = jnp.zeros_like(zbuf)

        def tail(idx, r0, rows):
            dst = xs_hbm.at[pl.ds(pl.multiple_of(zrow_ref[idx] + r0, SLAB_ALIGN), rows), :]
            return pltpu.make_async_copy(zbuf.at[pl.ds(0, rows), :], dst, zsem)

        _slab_loop(0, znbig_ref, znsmall_ref, lambda *a: tail(*a).start(), N_EXPERTS + 1)
        _slab_loop(0, znbig_ref, znsmall_ref, lambda *a: tail(*a).wait(), N_EXPERTS + 1)

    xt = xn_ref[...]
    blk = STAGE_BLOCK
    for r0 in range(0, n_rows, blk):
        rid = lax.broadcasted_iota(I32, (blk, tw), 0) + r0
        hit = rid == pos_ref[0:1, :]
        for k in range(1, TOP_K):
            hit = jnp.logical_or(hit, rid == pos_ref[k:k + 1, :])
        srt = _dot(jnp.where(hit, 1.0, 0.0).astype(BF16), xt)
        stage[slot, r0:r0 + blk, :] = _pack_pairs(srt[:, :half], srt[:, half:])

    @pl.when(i > 0)
    def _():
        _plan_wait(i - 1, plan, functools.partial(copy, 1 - slot))

    _plan_copies(i, plan, lambda *a: copy(slot, *a).start())

    @pl.when(i == n_steps - 1)
    def _():
        _plan_wait(i, plan, functools.partial(copy, slot))


def _dispatch(xn, pos8, plan, tails, n_slots):
    n_tok, d = xn.shape
    tw = DISPATCH_TILE
    n_steps = n_tok // tw
    return pl.pallas_call(
        functools.partial(_dispatch_kernel, n_steps=n_steps),
        grid_spec=pltpu.PrefetchScalarGridSpec(
            num_scalar_prefetch=10,
            grid=(n_steps,),
            in_specs=[
                pl.BlockSpec((tw, d), lambda i, *_: (i, 0)),
                pl.BlockSpec((SUBLANES, tw), lambda i, *_: (0, i)),
            ],
            out_specs=pl.BlockSpec(memory_space=pl.ANY),
            scratch_shapes=[
                pltpu.VMEM((2, STAGE_ROWS, d // 2), U32),
                pltpu.VMEM((BIG_CHUNK, d // 2), U32),
                pltpu.SemaphoreType.DMA(()),
                pltpu.SemaphoreType.DMA(()),
            ],
        ),
        out_shape=jax.ShapeDtypeStruct((n_slots, d // 2), U32),
        compiler_params=pltpu.CompilerParams(
            dimension_semantics=("arbitrary",), vmem_limit_bytes=VMEM_LIMIT),
        name="dispatch",
    )(*plan, *tails, xn, pos8)


def _expert_kernel(te_ref, nu_ref, x_ref, wup_ref, bup_ref, wdn_ref, bdn_ref, y_ref,
                   wup_bf, wdn_bf, *, d_ff):
    i = pl.program_id(0)
    n_used = nu_ref[0]

    @pl.when(jnp.logical_and(i < n_used,
                             jnp.logical_or(i == 0, te_ref[i] != te_ref[jnp.maximum(i - 1, 0)])))
    def _():
        wup_bf[...] = wup_ref[0].astype(BF16)
        wdn_bf[...] = wdn_ref[0].astype(BF16)

    @pl.when(i < n_used)
    def _():
        lo, hi = _unpack_pairs(x_ref[...])
        half = lo.shape[1]
        h = _dot(lo, wup_bf[:half, :]) + _dot(hi, wup_bf[half:, :]) + bup_ref[0]
        gl = jnp.minimum(h[:, :d_ff], SWIGLU_LIMIT)
        lin = jnp.clip(h[:, d_ff:], -SWIGLU_LIMIT, SWIGLU_LIMIT)
        act = gl * _sigmoid(SWIGLU_ALPHA * gl) * (lin + 1.0)
        y = _dot(act.astype(BF16), wdn_bf[...]) + bdn_ref[0]
        yb = y.astype(BF16).astype(F32)
        y_ref[...] = _pack_pairs(yb[:, :half], yb[:, half:])

    @pl.when(i >= n_used)
    def _():
        y_ref[...] = jnp.zeros_like(y_ref)


def _experts(xs, tile_e, n_used, w_up, b_up, w_down, b_down):
    n_slots, half = xs.shape
    d = 2 * half
    tm = EXPERT_TILE
    n_tiles = n_slots // tm
    d_ff = w_down.shape[1]
    return pl.pallas_call(
        functools.partial(_expert_kernel, d_ff=d_ff),
        grid_spec=pltpu.PrefetchScalarGridSpec(
            num_scalar_prefetch=2,
            grid=(n_tiles,),
            in_specs=[
                pl.BlockSpec((tm, half), lambda i, te, nu: (jnp.minimum(i, nu[0] - 1), 0)),
                pl.BlockSpec((1, d, 2 * d_ff), lambda i, te, nu: (te[i], 0, 0)),
                pl.BlockSpec((1, 1, 2 * d_ff), lambda i, te, nu: (te[i], 0, 0)),
                pl.BlockSpec((1, d_ff, d), lambda i, te, nu: (te[i], 0, 0)),
                pl.BlockSpec((1, 1, d), lambda i, te, nu: (te[i], 0, 0)),
            ],
            out_specs=pl.BlockSpec((tm, half), lambda i, te, nu: (i, 0)),
            scratch_shapes=[
                pltpu.VMEM((d, 2 * d_ff), BF16),
                pltpu.VMEM((d_ff, d), BF16),
            ],
        ),
        out_shape=jax.ShapeDtypeStruct((n_slots, half), U32),
        compiler_params=pltpu.CompilerParams(
            dimension_semantics=("arbitrary",), vmem_limit_bytes=VMEM_LIMIT),
        name="experts",
    )(tile_e, n_used, xs, w_up, b_up.reshape(N_EXPERTS, 1, -1),
      w_down, b_down.reshape(N_EXPERTS, 1, -1))


def _combine_kernel(*refs, n_steps):
    plan = refs[:7]
    ys_hbm, h2_ref, posc_ref, gatec_ref, ng_ref, out_ref, stage, sem = refs[7:]
    i = pl.program_id(0)
    slot = i % 2
    tw, d = h2_ref.shape
    n_rows = stage.shape[1]

    def copy(s, stage_row, buf_row, rows):
        src = ys_hbm.at[pl.ds(pl.multiple_of(buf_row, SLAB_ALIGN), rows), :]
        dst = stage.at[s, pl.ds(pl.multiple_of(stage_row, SLAB_ALIGN), rows), :]
        return pltpu.make_async_copy(src, dst, sem.at[s])

    @pl.when(i == 0)
    def _():
        stage[...] = jnp.zeros_like(stage)
        _plan_copies(0, plan, lambda *a: copy(0, *a).start())

    @pl.when(i + 1 < n_steps)
    def _():
        _plan_copies(i + 1, plan, lambda *a: copy(1 - slot, *a).start())

    _plan_wait(i, plan, functools.partial(copy, slot))

    y_lo, y_hi = _unpack_pairs(stage[slot])
    blk = 128
    for t0 in range(0, tw, blk):
        cid = lax.broadcasted_iota(I32, (blk, n_rows), 1)
        pg = jnp.zeros((blk, n_rows), F32)
        for k in range(TOP_K):
            pg = jnp.where(cid == posc_ref[t0:t0 + blk, k:k + 1],
                           gatec_ref[t0:t0 + blk, k:k + 1], pg)
        pgb = pg.astype(BF16)
        moe = jnp.concatenate([_dot(pgb, y_lo), _dot(pgb, y_hi)], axis=1)
        acc = h2_ref[t0:t0 + blk, :] + moe
        ms = jnp.mean(acc * acc, axis=-1, keepdims=True)
        out_ref[t0:t0 + blk, :] = acc * lax.rsqrt(ms + RMS_EPS) * ng_ref[...]


def _combine(ys, h2, pos_cols, gate_cols, plan, norm_g):
    n_tok, d = h2.shape
    tw = DISPATCH_TILE
    n_steps = n_tok // tw
    return pl.pallas_call(
        functools.partial(_combine_kernel, n_steps=n_steps),
        grid_spec=pltpu.PrefetchScalarGridSpec(
            num_scalar_prefetch=7,
            grid=(n_steps,),
            in_specs=[
                pl.BlockSpec(memory_space=pl.ANY),
                pl.BlockSpec((tw, d), lambda i, *_: (i, 0)),
                pl.BlockSpec((tw, SUBLANES), lambda i, *_: (i, 0)),
                pl.BlockSpec((tw, SUBLANES), lambda i, *_: (i, 0)),
                pl.BlockSpec((1, d), lambda i, *_: (0, 0)),
            ],
            out_specs=pl.BlockSpec((tw, d), lambda i, *_: (i, 0)),
            scratch_shapes=[
                pltpu.VMEM((2, STAGE_ROWS, d // 2), U32),
                pltpu.SemaphoreType.DMA((2,)),
            ],
        ),
        out_shape=jax.ShapeDtypeStruct((n_tok, d), F32),
        compiler_params=pltpu.CompilerParams(
            dimension_semantics=("arbitrary",), vmem_limit_bytes=VMEM_LIMIT),
        name="combine",
    )(*plan, ys, h2, pos_cols, gate_cols, norm_g.reshape(1, -1).astype(F32))


def _moe(h2, xn, eidx, gate, rank, tcnt, w_up, b_up, w_down, b_down, norm_final_g):
    n_tok, d = h2.shape
    tm = EXPERT_TILE
    tw = DISPATCH_TILE
    n_steps = n_tok // tw
    n_exp = N_EXPERTS
    e_ids = jnp.arange(n_exp, dtype=I32)
    al = SLAB_ALIGN
    tile_cnt = tcnt.reshape(n_steps, n_exp, LANES)[:, :, 0].astype(I32)
    slab_rows = (tile_cnt + al - 1) // al * al
    exp_rows = jnp.sum(slab_rows, axis=0)
    region = (exp_rows + tm - 1) // tm * tm
    pad_end = jnp.cumsum(region)
    pad_start = pad_end - region
    n_slots = (n_tok * TOP_K + n_steps * n_exp * (al - 1) + n_exp * (tm - 1) + tm - 1) // tm * tm
    n_tiles = n_slots // tm

    tile_carry = jnp.cumsum(tile_cnt, axis=0) - tile_cnt
    tile_off = jnp.cumsum(slab_rows, axis=1) - slab_rows
    slab_row0 = pad_start[None, :] + jnp.cumsum(slab_rows, axis=0) - slab_rows

    def chunks(rows):
        return rows // BIG_CHUNK, rows % BIG_CHUNK // al

    def flat(cnt, src0, dst0, size, n_max):
        run = jnp.cumsum(cnt, axis=1)
        j = jnp.arange(n_max, dtype=I32)
        e_of = jnp.sum(run[:, None, :] <= j[None, :, None], axis=-1)
        sel = e_of[:, :, None] == e_ids[None, None, :]
        pick = lambda a: jnp.sum(jnp.where(sel, a[:, None, :], 0), axis=-1)
        local = (j[None, :] - pick(run - cnt)) * size
        return run[:, -1], (pick(src0) + local).reshape(-1), (pick(dst0) + local).reshape(-1)

    nbig, nsmall = chunks(slab_rows)
    big_tot, big_src, big_dst = flat(nbig, tile_off, slab_row0, BIG_CHUNK, MAX_BIG)
    small_tot, small_src, small_dst = flat(nsmall, tile_off + nbig * BIG_CHUNK,
                                           slab_row0 + nbig * BIG_CHUNK, al, MAX_SMALL)
    pad_groups = (jnp.sum(slab_rows, axis=1) - tw * TOP_K) // al
    plan = (big_tot, small_tot, pad_groups, big_src, big_dst, small_src, small_dst)
    tail_rows = jnp.concatenate([region - exp_rows, n_slots - pad_end[-1:]])
    tails = (*chunks(tail_rows), jnp.concatenate([pad_start + exp_rows, pad_end[-1:]]))

    base_t = jnp.repeat((tile_off - tile_carry).T, tw, axis=1)
    e_sel = eidx[:TOP_K][None] == e_ids[:, None, None]
    pos = jnp.sum(jnp.where(e_sel, base_t[:, None, :], 0), axis=0) + rank[:TOP_K]
    pos8 = jnp.concatenate([pos, jnp.full((SUBLANES - TOP_K, n_tok), -1, I32)], axis=0)

    tile_start = jnp.arange(n_tiles, dtype=I32) * tm
    tile_e = jnp.minimum(jnp.sum(pad_end[None, :] <= tile_start[:, None], axis=-1),
                         n_exp - 1).astype(I32)
    n_used = (pad_end[-1] // tm).astype(I32).reshape(1)

    xs = _dispatch(xn, pos8, plan, tails, n_slots)
    ys = _experts(xs, tile_e, n_used, w_up, b_up, w_down, b_down)
    return _combine(ys, h2, pos8.T, gate.T, plan, norm_final_g)


def kernel(x, norm_mix_g, w_in, s5_log_dt, s5_a_re, s5_a_im, s5_b_re, s5_b_im, s5_c_re, s5_c_im,
           s5_d, s5_w_glu, s5_b_glu, ml_conv_w, ml_conv_b, ml_b_gates, ml_norm_g, w_out,
           norm_ffn_g, w_router, b_router, w_up, b_up, w_down, b_down, norm_final_g):
    bsz, seq, d = x.shape
    depth = w_in.shape[0]
    assert depth == 1, "single-layer block"
    l = 0
    x2d = x.reshape(bsz * seq, d)
    qk, v, o, ut, gt = _in_proj(x2d, norm_mix_g[l], w_in[l], ml_b_gates[l], bsz, seq)
    g_t = _s5(ut, s5_log_dt[l], s5_a_re[l], s5_a_im[l], s5_b_re[l], s5_b_im[l],
              s5_c_re[l], s5_c_im[l], s5_d[l])
    y_ml = _mlstm(qk, v, o, gt, ml_conv_w[l], ml_conv_b[l], ml_norm_g[l], bsz, seq)
    y_ml = y_ml.reshape(bsz * seq, -1)
    h2, xn, eidx, gate, rank, tcnt = _post(x2d, g_t, y_ml, s5_w_glu[l], s5_b_glu[l], w_out[l],
                                           norm_ffn_g[l], w_router[l], b_router[l])
    out = _moe(h2, xn, eidx, gate, rank, tcnt, w_up[l], b_up[l], w_down[l], b_down[l],
               norm_final_g)
    return out.reshape(bsz, seq, d)
```

```python
import functools
import math

import jax
import jax.numpy as jnp
from jax import lax
from jax.experimental import pallas as pl
from jax.experimental.pallas import tpu as pltpu

F32 = jnp.float32
BF16 = jnp.bfloat16
I32 = jnp.int32
U32 = jnp.uint32

S5_GROUP = 16
S5_STATE = 64
ML_HEADS = 4
CONV_WIDTH = 4
N_EXPERTS = 32
TOP_K = 4
SWIGLU_LIMIT = 7.0
SWIGLU_ALPHA = 1.702
RMS_EPS = 1e-5
LN_EPS = 1e-6

LANES = 128
SUBLANES = 8
S5_CHUNK = LANES
ML_CHUNK = 128
GATE_CHUNKS = 8
EXP_CAP = 1e38
PROJ_TILE = 1024
EXPERT_TILE = 512
DISPATCH_TILE = 256
SLAB_ALIGN = SUBLANES
BIG_CHUNK = 32
STAGE_ROWS = DISPATCH_TILE * TOP_K + N_EXPERTS * SLAB_ALIGN
STAGE_BLOCK = STAGE_ROWS // 2
MAX_BIG = STAGE_ROWS // BIG_CHUNK
MAX_SMALL = N_EXPERTS * (BIG_CHUNK // SLAB_ALIGN - 1)
VMEM_LIMIT = 56 * 1024 * 1024

_NT = (((1,), (1,)), ((), ()))
_TN = (((0,), (0,)), ((), ()))


def _dot(a, b):
    return jnp.dot(a, b, preferred_element_type=F32)


def _dot_nt(a, b):
    return lax.dot_general(a, b, _NT, preferred_element_type=F32)


def _dot_tn(a, b):
    return lax.dot_general(a, b, _TN, preferred_element_type=F32)


def _split3(x):
    p1 = x.astype(BF16)
    r1 = x - p1.astype(F32)
    p2 = r1.astype(BF16)
    r2 = r1 - p2.astype(F32)
    return p1, p2, r2.astype(BF16)


def _sigmoid(x):
    return 0.5 * jnp.tanh(0.5 * x) + 0.5


def _inproj_kernel(x_ref, g_ref, wnat_ref, wut_ref, wgt_ref, bg_ref,
                   qk_ref, v_ref, o_ref, ut_ref, gt_ref, *, width):
    x = x_ref[...]
    ms = jnp.mean(x * x, axis=-1, keepdims=True)
    hn = (x * lax.rsqrt(ms + RMS_EPS) * g_ref[...]).astype(BF16)
    nat = _dot(hn, wnat_ref[...])
    qk_ref[...] = nat[:, :2 * width].astype(BF16)
    v_ref[...] = nat[:, 2 * width:3 * width].astype(BF16)
    o_ref[...] = nat[:, 3 * width:].astype(BF16)
    ut_ref[0] = _dot_nt(wut_ref[...], hn).astype(BF16)
    gt_ref[0] = _dot_nt(wgt_ref[...], hn) + bg_ref[...]


def _in_proj(x2d, norm_g, w_in, b_gates, bsz, seq):
    n_tok, d = x2d.shape
    s5w = d // 2
    mlw = d - s5w
    tm = min(PROJ_TILE, seq)
    tpb = seq // tm
    w_bf = w_in.astype(BF16)
    w_nat = w_bf[:, s5w:s5w + 4 * mlw]
    w_ut = w_bf[:, :s5w].T
    n_gate = 2 * ML_HEADS
    w_gt = jnp.zeros((16, d), BF16).at[:n_gate].set(w_bf[:, s5w + 4 * mlw:].T)
    b_g = jnp.zeros((16, 1), F32).at[:n_gate, 0].set(b_gates.astype(F32))
    grid = (n_tok // tm,)
    full = lambda i: (0, 0)
    return pl.pallas_call(
        functools.partial(_inproj_kernel, width=mlw),
        grid=grid,
        in_specs=[
            pl.BlockSpec((tm, d), lambda i: (i, 0)),
            pl.BlockSpec((1, d), full),
            pl.BlockSpec((d, 4 * mlw), full),
            pl.BlockSpec((s5w, d), full),
            pl.BlockSpec((16, d), full),
            pl.BlockSpec((16, 1), full),
        ],
        out_specs=[
            pl.BlockSpec((tm, 2 * mlw), lambda i: (i, 0)),
            pl.BlockSpec((tm, mlw), lambda i: (i, 0)),
            pl.BlockSpec((tm, mlw), lambda i: (i, 0)),
            pl.BlockSpec((1, s5w, tm), lambda i: (i // tpb, 0, i % tpb)),
            pl.BlockSpec((1, 16, tm), lambda i: (i // tpb, 0, i % tpb)),
        ],
        out_shape=[
            jax.ShapeDtypeStruct((n_tok, 2 * mlw), BF16),
            jax.ShapeDtypeStruct((n_tok, mlw), BF16),
            jax.ShapeDtypeStruct((n_tok, mlw), BF16),
            jax.ShapeDtypeStruct((bsz, s5w, seq), BF16),
            jax.ShapeDtypeStruct((bsz, 16, seq), F32),
        ],
        compiler_params=pltpu.CompilerParams(
            dimension_semantics=("parallel",), vmem_limit_bytes=VMEM_LIMIT),
        name="in_proj",
    )(x2d, norm_g.reshape(1, d).astype(F32), w_nat, w_ut, w_gt, b_g)


def _s5_kernel(d_ref, x_ref, prm_ref, c_ref, bt_ref, out_ref, *, bsz, nc):
    L = S5_CHUNK
    P = S5_GROUP
    N = S5_STATE
    grp = pl.program_id(0)
    lane = lax.broadcasted_iota(I32, (1, 2 * N), 1)
    lo = lane < N
    a_re = jnp.minimum(prm_ref[0, 0:1, :], -1e-4)
    a_im = prm_ref[0, 1:2, :]
    dt = jnp.exp(prm_ref[0, 2:3, :])
    zr = dt * a_re
    zi = dt * a_im

    quarter = jnp.where(lo, 0.0, 0.5 * math.pi)

    def powrows(e):
        return jnp.exp(e * zr) * jnp.cos(e * zi - quarter)

    def swap(tab):
        return pltpu.roll(tab, N, 1)

    def cmul(tab, c):
        return c[0] * tab + jnp.where(lo, -c[1], c[1]) * swap(tab)

    n_dbl = max(nc - 1, 0).bit_length()
    exps = [1] + [SUBLANES << k for k in range(int(math.log2(L // SUBLANES)))] + [L << k for k in range(n_dbl)]
    e_col = jnp.concatenate([jnp.full((1, 1), float(e), F32) for e in exps]
                            + [jnp.zeros((-len(exps) % SUBLANES, 1), F32)], axis=0)
    mag = jnp.exp(e_col * zr)
    pw_r = mag * jnp.cos(e_col * zi)
    pw_i = mag * jnp.sin(e_col * zi)
    apow = {e: (pw_r[k:k + 1], pw_i[k:k + 1]) for k, e in enumerate(exps)}

    def powtab(descending):
        i8 = lax.broadcasted_iota(I32, (SUBLANES, 1), 0).astype(F32)
        tab = powrows(SUBLANES - 1.0 - i8 if descending else i8)
        rows = SUBLANES
        while rows < L:
            more = cmul(tab, apow[rows])
            tab = jnp.concatenate([more, tab] if descending else [tab, more], axis=0)
            rows *= 2
        return tab

    er, ei = apow[1]
    den = a_re * a_re + a_im * a_im
    coef_r = ((er - 1.0) * a_re + ei * a_im) / den
    coef_i = (ei * a_re - (er - 1.0) * a_im) / den
    c_r = c_ref[0, 0]
    c_i = c_ref[0, 1]
    bb_r = coef_r * bt_ref[0, 0] - coef_i * bt_ref[0, 1]
    bb_i = coef_r * bt_ref[0, 1] + coef_i * bt_ref[0, 0]

    cb_rows = []
    for q in range(P):
        cbr = c_r * bb_r[q:q + 1] - c_i * bb_i[q:q + 1]
        cbi = c_r * bb_i[q:q + 1] + c_i * bb_r[q:q + 1]
        cb_rows.append(jnp.where(lo, cbr, -cbi))
    cb = jnp.concatenate(cb_rows, axis=0)
    pt0 = powtab(False)
    c1, c2, c3 = _split3(cb)
    t1, t2, t3 = _split3(pt0)
    kmat = (_dot_nt(c1, t1) + _dot_nt(c1, t2) + _dot_nt(c2, t1)
            + _dot_nt(c2, t2) + _dot_nt(c1, t3) + _dot_nt(c3, t1))

    pt_rev = powtab(True)
    pt_rev_sw = swap(pt_rev)
    f_rows = []
    for q in range(P):
        a1 = bb_r[q:q + 1]
        a2 = jnp.where(lo, -bb_i[q:q + 1], bb_i[q:q + 1])
        f_rows.append((a1 * pt_rev + a2 * pt_rev_sw).astype(BF16))
    fmat = jnp.concatenate(f_rows, axis=0)
    pt1 = cmul(pt0, apow[1])
    pt1_sw = swap(pt1)
    e_rows = []
    for p in range(P):
        b1 = jnp.where(lo, c_r[p:p + 1], -c_r[p:p + 1])
        b2 = -c_i[p:p + 1]
        e_rows.append((b1 * pt1 + b2 * pt1_sw).astype(BF16))
    emat_t = jnp.concatenate(e_rows, axis=0)

    lhs = jnp.concatenate(
        [jnp.concatenate([x_ref[b, q] for q in range(P)], axis=1) for b in range(bsz)], axis=0)
    s_end = _dot(lhs, fmat)

    rr = lax.broadcasted_iota(I32, (L, L), 0)
    cc = lax.broadcasted_iota(I32, (L, L), 1)
    causal = cc >= rr
    y = None
    for q0 in range(0, P, 2):
        rows = []
        for q in (q0, q0 + 1):
            tiles = []
            for p in range(P):
                kb = jnp.broadcast_to(kmat[q * P + p:q * P + p + 1, :], (L, L))
                toe = pltpu.roll(kb, 0, 1, stride=1, stride_axis=0)
                tiles.append(jnp.where(causal, toe, 0.0).astype(BF16))
            rows.append(jnp.concatenate(tiles, axis=1))
        part = _dot(lhs[:, q0 * L:(q0 + 2) * L], jnp.concatenate(rows, axis=0))
        y = part if y is None else y + part

    m_rows = bsz * nc
    ridx = lax.broadcasted_iota(I32, (m_rows, 1), 0)
    cidx = ridx % nc
    h = jnp.where(cidx >= 1, pltpu.roll(s_end, 1, 0), 0.0)
    d = 1
    while d < nc:
        sh = jnp.where(cidx >= d, pltpu.roll(h, d, 0), 0.0)
        h = h + cmul(sh, apow[d * L])
        d *= 2
    y = y + _dot_nt(h.astype(BF16), emat_t)

    for b in range(bsz):
        for p in range(P):
            yp = (y[b * nc:(b + 1) * nc, p * L:(p + 1) * L]
                  + d_ref[grp * P + p] * x_ref[b, p].astype(F32))
            out_ref[b, p] = jax.nn.gelu(yp).astype(BF16)


def _s5(ut, log_dt, a_re, a_im, b_re, b_im, c_re, c_im, d_skip):
    bsz, s5w, seq = ut.shape
    L = S5_CHUNK
    nc = seq // L
    groups = s5w // S5_GROUP
    n = S5_STATE
    x4 = ut.reshape(bsz, s5w, nc, L)
    dup = lambda t: jnp.concatenate([t, t], axis=-1).astype(F32)
    prm = jnp.zeros((groups, SUBLANES, 2 * n), F32)
    prm = prm.at[:, 0].set(dup(a_re)).at[:, 1].set(dup(a_im))
    prm = prm.at[:, 2].set(jnp.broadcast_to(log_dt.astype(F32)[:, None], (groups, 2 * n)))
    cpar = jnp.stack([dup(c_re), dup(c_im)], axis=1)
    btpar = jnp.stack([dup(jnp.swapaxes(b_re, 1, 2)), dup(jnp.swapaxes(b_im, 1, 2))], axis=1)
    out = pl.pallas_call(
        functools.partial(_s5_kernel, bsz=bsz, nc=nc),
        grid_spec=pltpu.PrefetchScalarGridSpec(
            num_scalar_prefetch=1,
            grid=(groups,),
            in_specs=[
                pl.BlockSpec((bsz, S5_GROUP, nc, L), lambda g, d: (0, g, 0, 0)),
                pl.BlockSpec((1, SUBLANES, 2 * n), lambda g, d: (g, 0, 0)),
                pl.BlockSpec((1, 2, S5_GROUP, 2 * n), lambda g, d: (g, 0, 0, 0)),
                pl.BlockSpec((1, 2, S5_GROUP, 2 * n), lambda g, d: (g, 0, 0, 0)),
            ],
            out_specs=pl.BlockSpec((bsz, S5_GROUP, nc, L), lambda g, d: (0, g, 0, 0)),
        ),
        out_shape=jax.ShapeDtypeStruct((bsz, s5w, nc, L), BF16),
        compiler_params=pltpu.CompilerParams(
            dimension_semantics=("parallel",), vmem_limit_bytes=VMEM_LIMIT),
        name="s5",
    )(d_skip.astype(F32), x4, prm, cpar, btpar)
    return out.reshape(bsz, s5w, seq)


def _log_sigmoid(x):
    return jnp.minimum(x, 0.0) - jnp.log(1.0 + jnp.exp(-jnp.abs(x)))


def _gates_kernel(gt_ref, out_ref, m_scr, *, bsz, chunk, n_sub):
    L = chunk
    H = ML_HEADS
    step = pl.program_id(0)

    @pl.when(step == 0)
    def _():
        m_scr[...] = jnp.zeros_like(m_scr)

    rr = lax.broadcasted_iota(I32, (L, L), 0)
    cc = lax.broadcasted_iota(I32, (L, L), 1)
    utri = (rr <= cc).astype(BF16)
    grow = lax.broadcasted_iota(I32, (16, 1), 0)
    lane_row = lax.broadcasted_iota(I32, (SUBLANES, L), 1)
    m_prev = [m_scr[b][0:H, 0:1] for b in range(bsz)]
    pairs = [(j, b) for j in range(n_sub) for b in range(bsz)]

    irow, brow, cm = {}, {}, {}
    for j, b in pairs:
        g = gt_ref[b, :, j * L:(j + 1) * L]
        g2 = jnp.where(grow >= H, _log_sigmoid(g), g)
        p1, p2, p3 = _split3(g2)
        brow[j, b] = (_dot(p1, utri) + _dot(p2, utri) + _dot(p3, utri))[H:2 * H]
        irow[j, b] = g[0:H]
    for j, b in pairs:
        ib = irow[j, b] - brow[j, b]
        x = jnp.concatenate([ib, ib], axis=0)
        sft = 1
        while sft < L:
            x = jnp.maximum(x, jnp.where(lane_row >= sft, pltpu.roll(x, sft, 1), -jnp.inf))
            sft *= 2
        cm[j, b] = x[0:H]
    for j, b in pairs:
        mp = m_prev[b]
        mm = jnp.maximum(cm[j, b], mp)
        b_last = brow[j, b][:, L - 1:L]
        m_next = b_last + jnp.maximum(mp, cm[j, b][:, L - 1:L])
        planes = (mm, jnp.exp(mp - mm), jnp.minimum(jnp.exp(-(brow[j, b] + mm)), EXP_CAP),
                  jnp.exp(b_last - brow[j, b] + irow[j, b] - m_next), irow[j, b] - brow[j, b],
                  jnp.broadcast_to(jnp.exp(b_last + mp - m_next), (H, L)))
        for k, rows in enumerate(planes):
            out_ref[k, b, :, j * L:(j + 1) * L] = jnp.concatenate([rows, rows], axis=0)
        m_prev[b] = m_next
    for b in range(bsz):
        m_scr[b] = jnp.broadcast_to(jnp.concatenate([m_prev[b], m_prev[b]], axis=0),
                                    (SUBLANES, LANES))


def _gates(gt, bsz, seq, chunk):
    n_sub = min(GATE_CHUNKS, seq // chunk)
    blk = n_sub * chunk
    return pl.pallas_call(
        functools.partial(_gates_kernel, bsz=bsz, chunk=chunk, n_sub=n_sub),
        grid=(seq // blk,),
        in_specs=[pl.BlockSpec((bsz, 16, blk), lambda c: (0, 0, c))],
        out_specs=pl.BlockSpec((6, bsz, SUBLANES, blk), lambda c: (0, 0, 0, c)),
        out_shape=jax.ShapeDtypeStruct((6, bsz, SUBLANES, seq), F32),
        scratch_shapes=[pltpu.VMEM((bsz, SUBLANES, LANES), F32)],
        compiler_params=pltpu.CompilerParams(
            dimension_semantics=("arbitrary",), vmem_limit_bytes=VMEM_LIMIT),
        name="gates",
    )(gt)


def _mlstm_kernel(qk_ref, v_ref, o_ref, gr_ref, cw_ref, cb_ref, ng_ref, y_ref,
                  tail_scr, c_scr, *, bsz, chunk, width):
    L = chunk
    H = ML_HEADS
    dh = width // H
    step = pl.program_id(0)
    assert L == LANES and dh == LANES, "column replication below uses one 128x128 tile per head"

    @pl.when(step == 0)
    def _():
        tail_scr[...] = jnp.zeros_like(tail_scr)
        c_scr[...] = jnp.zeros_like(c_scr)

    rr = lax.broadcasted_iota(I32, (L, L), 0)
    cc = lax.broadcasted_iota(I32, (L, L), 1)
    causal = cc <= rr
    eye = (rr == cc).astype(BF16)
    ones_blk = jnp.ones((L, dh), BF16)
    scale = 1.0 / math.sqrt(dh)
    inv_dh = 1.0 / dh
    n_str = bsz * H
    streams = [(b, hd) for b in range(bsz) for hd in range(H)]

    def replicate(rows, two_terms):
        p = jnp.concatenate([jnp.broadcast_to(rows[h:h + 1], (LANES, L)) for h in range(H)], axis=0)
        p1 = p.astype(BF16)
        if not two_terms:
            return _dot_nt(eye, p1)
        return _dot_nt(eye, p1) + _dot_nt(eye, (p - p1.astype(F32)).astype(BF16))

    def rowsum(x):
        hi = x.astype(BF16)
        lo = (x - hi.astype(F32)).astype(BF16)
        return _dot(hi, ones_blk) + _dot(lo, ones_blk)

    tails = [tail_scr[b] for b in range(bsz)]
    caugs = [c_scr[i] for i in range(n_str)]
    reps = [[replicate(gr_ref[k, b][0:H], k == 0) for b in range(bsz)] for k in range(4)]
    col = lambda k: [reps[k][b][:, hd * LANES:(hd + 1) * LANES] for b, hd in streams]
    mm_rep, w_inter, enm, ws_rep = col(0), col(1), col(2), col(3)
    ib_row = [gr_ref[4, b][hd:hd + 1] for b, hd in streams]
    decays = [gr_ref[5, b][hd:hd + 1, 0:1] for b, hd in streams]

    q_all, k_all, new_tails = [], [], []
    for b in range(bsz):
        xqk = qk_ref[b].astype(F32)
        ext = jnp.concatenate([tails[b], xqk], axis=0)
        acc = jnp.broadcast_to(cb_ref[...], xqk.shape)
        for j in range(CONV_WIDTH):
            back = CONV_WIDTH - 1 - j
            sh = ext if back == 0 else pltpu.roll(ext, back, 0)
            acc = acc + cw_ref[j:j + 1, :] * sh[SUBLANES:]
        new_tails.append(xqk[L - SUBLANES:])
        qkc = acc * _sigmoid(acc)
        q_all.append(qkc[:, :width].astype(BF16))
        k_all.append((qkc[:, width:] * scale).astype(BF16))

    q_h = [q_all[b][:, hd * dh:(hd + 1) * dh] for b, hd in streams]
    k_h = [k_all[b][:, hd * dh:(hd + 1) * dh] for b, hd in streams]
    vaug = [jnp.concatenate([v_ref[b][:, hd * dh:(hd + 1) * dh], ones_blk], axis=1)
            for b, hd in streams]
    scores = [_dot_nt(q_h[i], k_h[i]) for i in range(n_str)]
    inter = [_dot(q_h[i], caugs[i].astype(BF16)) for i in range(n_str)]
    s_bf = [(scores[i] * jnp.exp(jnp.where(causal, ib_row[i] - mm_rep[i], -jnp.inf))).astype(BF16)
            for i in range(n_str)]
    kw = [(k_h[i].astype(F32) * ws_rep[i]).astype(BF16) for i in range(n_str)]
    intra = [_dot(s_bf[i], vaug[i]) for i in range(n_str)]
    upd = [_dot_tn(kw[i], vaug[i]) for i in range(n_str)]

    hh = []
    for i in range(n_str):
        num = w_inter[i] * inter[i][:, :dh] + intra[i][:, :dh]
        den = w_inter[i] * inter[i][:, dh:] + intra[i][:, dh:]
        hh.append(num / jnp.maximum(jnp.abs(den), enm[i]))
    mu = [rowsum(h) * inv_dh for h in hh]
    ctr = [hh[i] - mu[i] for i in range(n_str)]
    var = [rowsum(c * c) * inv_dh for c in ctr]
    outs = []
    for i, (b, hd) in enumerate(streams):
        sl = slice(hd * dh, (hd + 1) * dh)
        hn = ctr[i] * lax.rsqrt(var[i] + LN_EPS) * ng_ref[:, sl]
        outs.append(hn * _sigmoid(o_ref[b][:, sl].astype(F32)))

    for b in range(bsz):
        tail_scr[b] = new_tails[b]
        y_ref[b] = jnp.concatenate(outs[b * H:(b + 1) * H], axis=1).astype(BF16)
    for i in range(n_str):
        c_scr[i] = decays[i] * caugs[i] + upd[i]


def _mlstm(qk, v, o, gt, conv_w, conv_b, norm_g, bsz, seq):
    width = v.shape[-1]
    L = min(ML_CHUNK, seq)
    dh = width // ML_HEADS
    qk3 = qk.reshape(bsz, seq, 2 * width)
    v3 = v.reshape(bsz, seq, width)
    o3 = o.reshape(bsz, seq, width)
    gate_rows = _gates(gt, bsz, seq, L)
    full = lambda c: (0, 0)
    return pl.pallas_call(
        functools.partial(_mlstm_kernel, bsz=bsz, chunk=L, width=width),
        grid=(seq // L,),
        in_specs=[
            pl.BlockSpec((bsz, L, 2 * width), lambda c: (0, c, 0)),
            pl.BlockSpec((bsz, L, width), lambda c: (0, c, 0)),
            pl.BlockSpec((bsz, L, width), lambda c: (0, c, 0)),
            pl.BlockSpec((6, bsz, SUBLANES, L), lambda c: (0, 0, 0, c)),
            pl.BlockSpec((CONV_WIDTH, 2 * width), full),
            pl.BlockSpec((1, 2 * width), full),
            pl.BlockSpec((1, width), full),
        ],
        out_specs=pl.BlockSpec((bsz, L, width), lambda c: (0, c, 0)),
        out_shape=jax.ShapeDtypeStruct((bsz, seq, width), BF16),
        scratch_shapes=[
            pltpu.VMEM((bsz, SUBLANES, 2 * width), F32),
            pltpu.VMEM((bsz * ML_HEADS, dh, 2 * dh), F32),
        ],
        compiler_params=pltpu.CompilerParams(
            dimension_semantics=("arbitrary",), vmem_limit_bytes=VMEM_LIMIT),
        name="mlstm",
    )(qk3, v3, o3, gate_rows, conv_w.astype(F32), conv_b.reshape(1, -1).astype(F32),
      norm_g.reshape(1, -1).astype(F32))


def _post_kernel(x_ref, gt_ref, yml_ref, wglut_ref, bglu_ref, wout_ref, nffn_ref, wrt_ref, br_ref,
                 h2_ref, xn_ref, eidx_ref, gate_ref, rank_ref, tcnt_ref, carry_scr):
    step = pl.program_id(0)

    @pl.when(step == 0)
    def _():
        carry_scr[...] = jnp.zeros_like(carry_scr)

    gt = gt_ref[0]
    s5w = gt.shape[0]
    zt = _dot(wglut_ref[...], gt) + bglu_ref[...]
    s5t = (gt.astype(F32) * _sigmoid(zt)).astype(BF16)
    h2 = (x_ref[...] + _dot_tn(s5t, wout_ref[:s5w, :])
          + _dot(yml_ref[...], wout_ref[s5w:, :]))
    h2_ref[...] = h2
    ms = jnp.mean(h2 * h2, axis=-1, keepdims=True)
    xn = h2 * lax.rsqrt(ms + RMS_EPS) * nffn_ref[...]
    xb = xn.astype(BF16)
    xn_ref[...] = xb

    tm = xn.shape[0]
    logits = _dot_nt(wrt_ref[...], xb) + br_ref[...]
    eio = lax.broadcasted_iota(I32, (N_EXPERTS, tm), 0).astype(F32)
    vals = logits
    onehot = jnp.zeros((N_EXPERTS, tm), F32)
    idxs, tops = [], []
    for _ in range(TOP_K):
        mx = jnp.max(vals, axis=0, keepdims=True)
        idx = jnp.min(jnp.where(vals == mx, eio, float(N_EXPERTS)), axis=0, keepdims=True)
        sel = eio == idx
        onehot = onehot + sel.astype(F32)
        vals = jnp.where(sel, -jnp.inf, vals)
        idxs.append(idx)
        tops.append(mx)
    exps = [jnp.exp(t - tops[0]) for t in tops]
    tot = exps[0] + exps[1] + exps[2] + exps[3]
    pad_f = jnp.zeros((SUBLANES - TOP_K, tm), F32)
    eidx_ref[...] = jnp.concatenate(idxs + [pad_f], axis=0).astype(I32)
    gate_ref[...] = jnp.concatenate([e / tot for e in exps] + [pad_f], axis=0)

    rr = lax.broadcasted_iota(I32, (tm, tm), 0)
    cc = lax.broadcasted_iota(I32, (tm, tm), 1)
    before = (rr < cc).astype(BF16)
    carry = carry_scr[:, 0:1]
    rank_ex = _dot(onehot.astype(BF16), before) + carry
    ranks = [jnp.sum(jnp.where(eio == i, rank_ex, 0.0), axis=0, keepdims=True) for i in idxs]
    rank_ref[...] = jnp.concatenate(ranks + [pad_f], axis=0).astype(I32)
    tile_cnt = jnp.sum(onehot, axis=1, keepdims=True)
    carry_scr[...] = jnp.broadcast_to(carry + tile_cnt, carry_scr.shape)
    for s in range(tm // DISPATCH_TILE):
        sub = onehot[:, s * DISPATCH_TILE:(s + 1) * DISPATCH_TILE]
        tcnt_ref[0, s] = jnp.broadcast_to(jnp.sum(sub, axis=1, keepdims=True), carry_scr.shape)


def _post(x2d, g_t, y_ml, w_glu, b_glu, w_out, norm_g, w_router, b_router):
    n_tok, d = x2d.shape
    bsz, s5w, seq = g_t.shape
    mlw = y_ml.shape[-1]
    tm = min(PROJ_TILE, seq)
    tpb = seq // tm
    full = lambda i: (0, 0)
    row = lambda i: (i, 0)
    colb = lambda i: (0, i)
    return pl.pallas_call(
        _post_kernel,
        grid=(n_tok // tm,),
        in_specs=[
            pl.BlockSpec((tm, d), row),
            pl.BlockSpec((1, s5w, tm), lambda i: (i // tpb, 0, i % tpb)),
            pl.BlockSpec((tm, mlw), row),
            pl.BlockSpec((s5w, s5w), full),
            pl.BlockSpec((s5w, 1), full),
            pl.BlockSpec((d, d), full),
            pl.BlockSpec((1, d), full),
            pl.BlockSpec((N_EXPERTS, d), full),
            pl.BlockSpec((N_EXPERTS, 1), full),
        ],
        out_specs=[
            pl.BlockSpec((tm, d), row),
            pl.BlockSpec((tm, d), row),
            pl.BlockSpec((SUBLANES, tm), colb),
            pl.BlockSpec((SUBLANES, tm), colb),
            pl.BlockSpec((SUBLANES, tm), colb),
            pl.BlockSpec((1, tm // DISPATCH_TILE, N_EXPERTS, LANES), lambda i: (i, 0, 0, 0)),
        ],
        out_shape=[
            jax.ShapeDtypeStruct((n_tok, d), F32),
            jax.ShapeDtypeStruct((n_tok, d), BF16),
            jax.ShapeDtypeStruct((SUBLANES, n_tok), I32),
            jax.ShapeDtypeStruct((SUBLANES, n_tok), F32),
            jax.ShapeDtypeStruct((SUBLANES, n_tok), I32),
            jax.ShapeDtypeStruct((n_tok // tm, tm // DISPATCH_TILE, N_EXPERTS, LANES), F32),
        ],
        scratch_shapes=[pltpu.VMEM((N_EXPERTS, LANES), F32)],
        compiler_params=pltpu.CompilerParams(
            dimension_semantics=("arbitrary",), vmem_limit_bytes=VMEM_LIMIT),
        name="post_router",
    )(x2d, g_t, y_ml, w_glu.T.astype(BF16), b_glu.reshape(-1, 1).astype(F32), w_out.astype(BF16),
      norm_g.reshape(1, -1).astype(F32), w_router.T.astype(BF16),
      b_router.reshape(-1, 1).astype(F32))


def _slab_loop(base, nbig_ref, nsmall_ref, fn, n_slabs=N_EXPERTS):
    def per_expert(e, carry):
        idx = base + e
        nbig = nbig_ref[idx]

        def big(j, c2):
            fn(idx, j * BIG_CHUNK, BIG_CHUNK)
            return c2

        def small(j, c2):
            fn(idx, nbig * BIG_CHUNK + j * SLAB_ALIGN, SLAB_ALIGN)
            return c2

        lax.fori_loop(0, nbig, big, 0)
        lax.fori_loop(0, nsmall_ref[idx], small, 0)
        return carry

    lax.fori_loop(0, n_slabs, per_expert, 0)


def _plan_copies(step, plan, fn):
    nbig_ref, nsmall_ref, _, bsrc_ref, bdst_ref, ssrc_ref, sdst_ref = plan

    def big(j, carry):
        k = step * MAX_BIG + j
        fn(bsrc_ref[k], bdst_ref[k], BIG_CHUNK, 0)
        return carry

    def small(j, carry):
        k = step * MAX_SMALL + j
        fn(ssrc_ref[k], sdst_ref[k], SLAB_ALIGN, 1)
        return carry

    lax.fori_loop(0, nbig_ref[step], big, 0)
    lax.fori_loop(0, nsmall_ref[step], small, 0)


def _plan_wait(step, plan, copy):
    copy(0, 0, DISPATCH_TILE * TOP_K).wait()

    def one(j, carry):
        copy(0, 0, SLAB_ALIGN).wait()
        return carry

    lax.fori_loop(0, plan[2][step], one, 0)


def _unpack_pairs(words):
    lo = lax.bitcast_convert_type(words << 16, F32).astype(BF16)
    hi = lax.bitcast_convert_type(words & jnp.uint32(0xFFFF0000), F32).astype(BF16)
    return lo, hi


def _pack_pairs(lo_f32, hi_f32):
    lo = lax.bitcast_convert_type(lo_f32, U32) >> 16
    hi = lax.bitcast_convert_type(hi_f32, U32) & jnp.uint32(0xFFFF0000)
    return hi | lo


def _dispatch_kernel(*refs, n_steps):
    plan = refs[:7]
    znbig_ref, znsmall_ref, zrow_ref, xn_ref, pos_ref, xs_hbm, stage, zbuf, sem, zsem = refs[7:]
    i = pl.program_id(0)
    slot = i % 2
    tw, d = xn_ref.shape
    n_rows = stage.shape[1]
    half = d // 2

    def copy(s, stage_row, buf_row, rows):
        src = stage.at[s, pl.ds(pl.multiple_of(stage_row, SLAB_ALIGN), rows), :]
        dst = xs_hbm.at[pl.ds(pl.multiple_of(buf_row, SLAB_ALIGN), rows), :]
        return pltpu.make_async_copy(src, dst, sem)

    @pl.when(i == 0)
    def _():
        zbuf[...] = jnp.zeros_like(zbuf)

        def tail(idx, r0, rows):
            dst = xs_hbm.at[pl.ds(pl.multiple_of(zrow_ref[idx] + r0, SLAB_ALIGN), rows), :]
            return pltpu.make_async_copy(zbuf.at[pl.ds(0, rows), :], dst, zsem)

        _slab_loop(0, znbig_ref, znsmall_ref, lambda *a: tail(*a).start(), N_EXPERTS + 1)
        _slab_loop(0, znbig_ref, znsmall_ref, lambda *a: tail(*a).wait(), N_EXPERTS + 1)

    xt = xn_ref[...]
    blk = STAGE_BLOCK
    for r0 in range(0, n_rows, blk):
        rid = lax.broadcasted_iota(I32, (blk, tw), 0) + r0
        hit = rid == pos_ref[0:1, :]
        for k in range(1, TOP_K):
            hit = jnp.logical_or(hit, rid == pos_ref[k:k + 1, :])
        srt = _dot(jnp.where(hit, 1.0, 0.0).astype(BF16), xt)
        stage[slot, r0:r0 + blk, :] = _pack_pairs(srt[:, :half], srt[:, half:])

    @pl.when(i > 0)
    def _():
        _plan_wait(i - 1, plan, functools.partial(copy, 1 - slot))

    _plan_copies(i, plan, lambda r0, r1, n, prio: copy(slot, r0, r1, n).start(priority=prio))

    @pl.when(i == n_steps - 1)
    def _():
        _plan_wait(i, plan, functools.partial(copy, slot))


def _dispatch(xn, pos8, plan, tails, n_slots):
    n_tok, d = xn.shape
    tw = DISPATCH_TILE
    n_steps = n_tok // tw
    return pl.pallas_call(
        functools.partial(_dispatch_kernel, n_steps=n_steps),
        grid_spec=pltpu.PrefetchScalarGridSpec(
            num_scalar_prefetch=10,
            grid=(n_steps,),
            in_specs=[
                pl.BlockSpec((tw, d), lambda i, *_: (i, 0)),
                pl.BlockSpec((SUBLANES, tw), lambda i, *_: (0, i)),
            ],
            out_specs=pl.BlockSpec(memory_space=pl.ANY),
            scratch_shapes=[
                pltpu.VMEM((2, STAGE_ROWS, d // 2), U32),
                pltpu.VMEM((BIG_CHUNK, d // 2), U32),
                pltpu.SemaphoreType.DMA(()),
                pltpu.SemaphoreType.DMA(()),
            ],
        ),
        out_shape=jax.ShapeDtypeStruct((n_slots, d // 2), U32),
        compiler_params=pltpu.CompilerParams(
            dimension_semantics=("arbitrary",), vmem_limit_bytes=VMEM_LIMIT),
        name="dispatch",
    )(*plan, *tails, xn, pos8)


def _expert_kernel(te_ref, nu_ref, x_ref, wup_ref, bup_ref, wdn_ref, bdn_ref, y_ref,
                   wup_bf, wdn_bf, *, d_ff):
    i = pl.program_id(0)
    n_used = nu_ref[0]

    @pl.when(jnp.logical_and(i < n_used,
                             jnp.logical_or(i == 0, te_ref[i] != te_ref[jnp.maximum(i - 1, 0)])))
    def _():
        wup_bf[...] = wup_ref[0].astype(BF16)
        wdn_bf[...] = wdn_ref[0].astype(BF16)

    @pl.when(i < n_used)
    def _():
        lo, hi = _unpack_pairs(x_ref[...])
        half = lo.shape[1]
        h = _dot(lo, wup_bf[:half, :]) + _dot(hi, wup_bf[half:, :]) + bup_ref[0]
        gl = jnp.minimum(h[:, :d_ff], SWIGLU_LIMIT)
        lin = jnp.clip(h[:, d_ff:], -SWIGLU_LIMIT, SWIGLU_LIMIT)
        act = gl * _sigmoid(SWIGLU_ALPHA * gl) * (lin + 1.0)
        y = _dot(act.astype(BF16), wdn_bf[...]) + bdn_ref[0]
        yb = y.astype(BF16).astype(F32)
        y_ref[...] = _pack_pairs(yb[:, :half], yb[:, half:])

    @pl.when(i >= n_used)
    def _():
        y_ref[...] = jnp.zeros_like(y_ref)


def _experts(xs, tile_e, n_used, w_up, b_up, w_down, b_down):
    n_slots, half = xs.shape
    d = 2 * half
    tm = EXPERT_TILE
    n_tiles = n_slots // tm
    d_ff = w_down.shape[1]
    return pl.pallas_call(
        functools.partial(_expert_kernel, d_ff=d_ff),
        grid_spec=pltpu.PrefetchScalarGridSpec(
            num_scalar_prefetch=2,
            grid=(n_tiles,),
            in_specs=[
                pl.BlockSpec((tm, half), lambda i, te, nu: (jnp.minimum(i, nu[0] - 1), 0)),
                pl.BlockSpec((1, d, 2 * d_ff), lambda i, te, nu: (te[i], 0, 0)),
                pl.BlockSpec((1, 1, 2 * d_ff), lambda i, te, nu: (te[i], 0, 0)),
                pl.BlockSpec((1, d_ff, d), lambda i, te, nu: (te[i], 0, 0)),
                pl.BlockSpec((1, 1, d), lambda i, te, nu: (te[i], 0, 0)),
            ],
            out_specs=pl.BlockSpec((tm, half), lambda i, te, nu: (i, 0)),
            scratch_shapes=[
                pltpu.VMEM((d, 2 * d_ff), BF16),
                pltpu.VMEM((d_ff, d), BF16),
            ],
        ),
        out_shape=jax.ShapeDtypeStruct((n_slots, half), U32),
        compiler_params=pltpu.CompilerParams(
            dimension_semantics=("arbitrary",), vmem_limit_bytes=VMEM_LIMIT),
        name="experts",
    )(tile_e, n_used, xs, w_up, b_up.reshape(N_EXPERTS, 1, -1),
      w_down, b_down.reshape(N_EXPERTS, 1, -1))


def _combine_kernel(*refs, n_steps):
    plan = refs[:7]
    ys_hbm, h2_ref, posc_ref, gatec_ref, ng_ref, out_ref, stage, sem = refs[7:]
    i = pl.program_id(0)
    slot = i % 2
    tw, d = h2_ref.shape
    n_rows = stage.shape[1]

    def copy(s, stage_row, buf_row, rows):
        src = ys_hbm.at[pl.ds(pl.multiple_of(buf_row, SLAB_ALIGN), rows), :]
        dst = stage.at[s, pl.ds(pl.multiple_of(stage_row, SLAB_ALIGN), rows), :]
        return pltpu.make_async_copy(src, dst, sem.at[s])

    @pl.when(i == 0)
    def _():
        stage[...] = jnp.zeros_like(stage)
        _plan_copies(0, plan, lambda r0, r1, n, prio: copy(0, r0, r1, n).start(priority=prio))

    @pl.when(i + 1 < n_steps)
    def _():
        _plan_copies(i + 1, plan,
                     lambda r0, r1, n, prio: copy(1 - slot, r0, r1, n).start(priority=prio))

    _plan_wait(i, plan, functools.partial(copy, slot))

    y_lo, y_hi = _unpack_pairs(stage[slot])
    blk = 128
    for t0 in range(0, tw, blk):
        cid = lax.broadcasted_iota(I32, (blk, n_rows), 1)
        pg = jnp.zeros((blk, n_rows), F32)
        for k in range(TOP_K):
            pg = jnp.where(cid == posc_ref[t0:t0 + blk, k:k + 1],
                           gatec_ref[t0:t0 + blk, k:k + 1], pg)
        pgb = pg.astype(BF16)
        moe = jnp.concatenate([_dot(pgb, y_lo), _dot(pgb, y_hi)], axis=1)
        acc = h2_ref[t0:t0 + blk, :] + moe
        ms = jnp.mean(acc * acc, axis=-1, keepdims=True)
        out_ref[t0:t0 + blk, :] = acc * lax.rsqrt(ms + RMS_EPS) * ng_ref[...]


def _combine(ys, h2, pos_cols, gate_cols, plan, norm_g):
    n_tok, d = h2.shape
    tw = DISPATCH_TILE
    n_steps = n_tok // tw
    return pl.pallas_call(
        functools.partial(_combine_kernel, n_steps=n_steps),
        grid_spec=pltpu.PrefetchScalarGridSpec(
            num_scalar_prefetch=7,
            grid=(n_steps,),
            in_specs=[
                pl.BlockSpec(memory_space=pl.ANY),
                pl.BlockSpec((tw, d), lambda i, *_: (i, 0)),
                pl.BlockSpec((tw, SUBLANES), lambda i, *_: (i, 0)),
                pl.BlockSpec((tw, SUBLANES), lambda i, *_: (i, 0)),
                pl.BlockSpec((1, d), lambda i, *_: (0, 0)),
            ],
            out_specs=pl.BlockSpec((tw, d), lambda i, *_: (i, 0)),
            scratch_shapes=[
                pltpu.VMEM((2, STAGE_ROWS, d // 2), U32),
                pltpu.SemaphoreType.DMA((2,)),
            ],
        ),
        out_shape=jax.ShapeDtypeStruct((n_tok, d), F32),
        compiler_params=pltpu.CompilerParams(
            dimension_semantics=("arbitrary",), vmem_limit_bytes=VMEM_LIMIT),
        name="combine",
    )(*plan, ys, h2, pos_cols, gate_cols, norm_g.reshape(1, -1).astype(F32))


def _moe(h2, xn, eidx, gate, rank, tcnt, w_up, b_up, w_down, b_down, norm_final_g):
    n_tok, d = h2.shape
    tm = EXPERT_TILE
    tw = DISPATCH_TILE
    n_steps = n_tok // tw
    n_exp = N_EXPERTS
    e_ids = jnp.arange(n_exp, dtype=I32)
    al = SLAB_ALIGN
    tile_cnt = tcnt.reshape(n_steps, n_exp, LANES)[:, :, 0].astype(I32)
    slab_rows = (tile_cnt + al - 1) // al * al
    exp_rows = jnp.sum(slab_rows, axis=0)
    region = (exp_rows + tm - 1) // tm * tm
    pad_end = jnp.cumsum(region)
    pad_start = pad_end - region
    n_slots = (n_tok * TOP_K + n_steps * n_exp * (al - 1) + n_exp * (tm - 1) + tm - 1) // tm * tm
    n_tiles = n_slots // tm

    tile_carry = jnp.cumsum(tile_cnt, axis=0) - tile_cnt
    tile_off = jnp.cumsum(slab_rows, axis=1) - slab_rows
    slab_row0 = pad_start[None, :] + jnp.cumsum(slab_rows, axis=0) - slab_rows

    def chunks(rows):
        return rows // BIG_CHUNK, rows % BIG_CHUNK // al

    def flat(cnt, src0, dst0, size, n_max):
        run = jnp.cumsum(cnt, axis=1)
        j = jnp.arange(n_max, dtype=I32)
        e_of = jnp.sum(run[:, None, :] <= j[None, :, None], axis=-1)
        sel = e_of[:, :, None] == e_ids[None, None, :]
        pick = lambda a: jnp.sum(jnp.where(sel, a[:, None, :], 0), axis=-1)
        local = (j[None, :] - pick(run - cnt)) * size
        return run[:, -1], (pick(src0) + local).reshape(-1), (pick(dst0) + local).reshape(-1)

    nbig, nsmall = chunks(slab_rows)
    big_tot, big_src, big_dst = flat(nbig, tile_off, slab_row0, BIG_CHUNK, MAX_BIG)
    small_tot, small_src, small_dst = flat(nsmall, tile_off + nbig * BIG_CHUNK,
                                           slab_row0 + nbig * BIG_CHUNK, al, MAX_SMALL)
    pad_groups = (jnp.sum(slab_rows, axis=1) - tw * TOP_K) // al
    plan = (big_tot, small_tot, pad_groups, big_src, big_dst, small_src, small_dst)
    tail_rows = jnp.concatenate([region - exp_rows, n_slots - pad_end[-1:]])
    tails = (*chunks(tail_rows), jnp.concatenate([pad_start + exp_rows, pad_end[-1:]]))

    base_t = jnp.repeat((tile_off - tile_carry).T, tw, axis=1)
    e_sel = eidx[:TOP_K][None] == e_ids[:, None, None]
    pos = jnp.sum(jnp.where(e_sel, base_t[:, None, :], 0), axis=0) + rank[:TOP_K]
    pos8 = jnp.concatenate([pos, jnp.full((SUBLANES - TOP_K, n_tok), -1, I32)], axis=0)

    tile_start = jnp.arange(n_tiles, dtype=I32) * tm
    tile_e = jnp.minimum(jnp.sum(pad_end[None, :] <= tile_start[:, None], axis=-1),
                         n_exp - 1).astype(I32)
    n_used = (pad_end[-1] // tm).astype(I32).reshape(1)

    xs = _dispatch(xn, pos8, plan, tails, n_slots)
    ys = _experts(xs, tile_e, n_used, w_up, b_up, w_down, b_down)
    return _combine(ys, h2, pos8.T, gate.T, plan, norm_final_g)


def kernel(x, norm_mix_g, w_in, s5_log_dt, s5_a_re, s5_a_im, s5_b_re, s5_b_im, s5_c_re, s5_c_im,
           s5_d, s5_w_glu, s5_b_glu, ml_conv_w, ml_conv_b, ml_b_gates, ml_norm_g, w_out,
           norm_ffn_g, w_router, b_router, w_up, b_up, w_down, b_down, norm_final_g):
    bsz, seq, d = x.shape
    depth = w_in.shape[0]
    assert depth == 1, "single-layer block"
    l = 0
    x2d = x.reshape(bsz * seq, d)
    qk, v, o, ut, gt = _in_proj(x2d, norm_mix_g[l], w_in[l], ml_b_gates[l], bsz, seq)
    g_t = _s5(ut, s5_log_dt[l], s5_a_re[l], s5_a_im[l], s5_b_re[l], s5_b_im[l],
              s5_c_re[l], s5_c_im[l], s5_d[l])
    y_ml = _mlstm(qk, v, o, gt, ml_conv_w[l], ml_conv_b[l], ml_norm_g[l], bsz, seq)
    y_ml = y_ml.reshape(bsz * seq, -1)
    h2, xn, eidx, gate, rank, tcnt = _post(x2d, g_t, y_ml, s5_w_glu[l], s5_b_glu[l], w_out[l],
                                           norm_ffn_g[l], w_router[l], b_router[l])
    out = _moe(h2, xn, eidx, gate, rank, tcnt, w_up[l], b_up[l], w_down[l], b_down[l],
               norm_final_g)
    return out.reshape(bsz, seq, d)
```

```python
import functools
import math

import jax
import jax.numpy as jnp
from jax import lax
from jax.experimental import pallas as pl
from jax.experimental.pallas import tpu as pltpu

F32 = jnp.float32
BF16 = jnp.bfloat16
I32 = jnp.int32
U32 = jnp.uint32

S5_GROUP = 16
S5_STATE = 64
ML_HEADS = 4
CONV_WIDTH = 4
N_EXPERTS = 32
TOP_K = 4
SWIGLU_LIMIT = 7.0
SWIGLU_ALPHA = 1.702
RMS_EPS = 1e-5
LN_EPS = 1e-6

LANES = 128
SUBLANES = 8
S5_CHUNK = LANES
ML_CHUNK = 128
GATE_CHUNKS = 8
EXP_CAP = 1e38
PROJ_TILE = 1024
EXPERT_TILE = 512
DISPATCH_TILE = 256
SLAB_ALIGN = SUBLANES
BIG_CHUNK = 32
STAGE_ROWS = DISPATCH_TILE * TOP_K + N_EXPERTS * SLAB_ALIGN
STAGE_BLOCK = STAGE_ROWS // 2
MAX_BIG = STAGE_ROWS // BIG_CHUNK
MAX_SMALL = N_EXPERTS * (BIG_CHUNK // SLAB_ALIGN - 1)
VMEM_LIMIT = 56 * 1024 * 1024

_NT = (((1,), (1,)), ((), ()))
_TN = (((0,), (0,)), ((), ()))


def _dot(a, b):
    return jnp.dot(a, b, preferred_element_type=F32)


def _dot_nt(a, b):
    return lax.dot_general(a, b, _NT, preferred_element_type=F32)


def _dot_tn(a, b):
    return lax.dot_general(a, b, _TN, preferred_element_type=F32)


def _split3(x):
    p1 = x.astype(BF16)
    r1 = x - p1.astype(F32)
    p2 = r1.astype(BF16)
    r2 = r1 - p2.astype(F32)
    return p1, p2, r2.astype(BF16)


def _sigmoid(x):
    return 0.5 * jnp.tanh(0.5 * x) + 0.5


def _inproj_kernel(x_ref, g_ref, wnat_ref, wut_ref, wgt_ref, bg_ref,
                   qk_ref, v_ref, o_ref, ut_ref, gt_ref, *, width):
    x = x_ref[...]
    ms = jnp.mean(x * x, axis=-1, keepdims=True)
    hn = (x * lax.rsqrt(ms + RMS_EPS) * g_ref[...]).astype(BF16)
    nat = _dot(hn, wnat_ref[...])
    qk_ref[...] = nat[:, :2 * width].astype(BF16)
    v_ref[...] = nat[:, 2 * width:3 * width].astype(BF16)
    o_ref[...] = nat[:, 3 * width:].astype(BF16)
    ut_ref[0] = _dot_nt(wut_ref[...], hn).astype(BF16)
    gt_ref[0] = _dot_nt(wgt_ref[...], hn) + bg_ref[...]


def _in_proj(x2d, norm_g, w_in, b_gates, bsz, seq):
    n_tok, d = x2d.shape
    s5w = d // 2
    mlw = d - s5w
    tm = min(PROJ_TILE, seq)
    tpb = seq // tm
    w_bf = w_in.astype(BF16)
    w_nat = w_bf[:, s5w:s5w + 4 * mlw]
    w_ut = w_bf[:, :s5w].T
    n_gate = 2 * ML_HEADS
    w_gt = jnp.zeros((16, d), BF16).at[:n_gate].set(w_bf[:, s5w + 4 * mlw:].T)
    b_g = jnp.zeros((16, 1), F32).at[:n_gate, 0].set(b_gates.astype(F32))
    grid = (n_tok // tm,)
    full = lambda i: (0, 0)
    return pl.pallas_call(
        functools.partial(_inproj_kernel, width=mlw),
        grid=grid,
        in_specs=[
            pl.BlockSpec((tm, d), lambda i: (i, 0)),
            pl.BlockSpec((1, d), full),
            pl.BlockSpec((d, 4 * mlw), full),
            pl.BlockSpec((s5w, d), full),
            pl.BlockSpec((16, d), full),
            pl.BlockSpec((16, 1), full),
        ],
        out_specs=[
            pl.BlockSpec((tm, 2 * mlw), lambda i: (i, 0)),
            pl.BlockSpec((tm, mlw), lambda i: (i, 0)),
            pl.BlockSpec((tm, mlw), lambda i: (i, 0)),
            pl.BlockSpec((1, s5w, tm), lambda i: (i // tpb, 0, i % tpb)),
            pl.BlockSpec((1, 16, tm), lambda i: (i // tpb, 0, i % tpb)),
        ],
        out_shape=[
            jax.ShapeDtypeStruct((n_tok, 2 * mlw), BF16),
            jax.ShapeDtypeStruct((n_tok, mlw), BF16),
            jax.ShapeDtypeStruct((n_tok, mlw), BF16),
            jax.ShapeDtypeStruct((bsz, s5w, seq), BF16),
            jax.ShapeDtypeStruct((bsz, 16, seq), F32),
        ],
        compiler_params=pltpu.CompilerParams(
            dimension_semantics=("parallel",), vmem_limit_bytes=VMEM_LIMIT),
        name="in_proj",
    )(x2d, norm_g.reshape(1, d).astype(F32), w_nat, w_ut, w_gt, b_g)


def _s5_kernel(d_ref, x_ref, prm_ref, c_ref, bt_ref, out_ref, *, bsz, nc):
    L = S5_CHUNK
    P = S5_GROUP
    N = S5_STATE
    grp = pl.program_id(0)
    lane = lax.broadcasted_iota(I32, (1, 2 * N), 1)
    lo = lane < N
    a_re = jnp.minimum(prm_ref[0, 0:1, :], -1e-4)
    a_im = prm_ref[0, 1:2, :]
    dt = jnp.exp(prm_ref[0, 2:3, :])
    zr = dt * a_re
    zi = dt * a_im

    quarter = jnp.where(lo, 0.0, 0.5 * math.pi)

    def powrows(e):
        return jnp.exp(e * zr) * jnp.cos(e * zi - quarter)

    def swap(tab):
        return pltpu.roll(tab, N, 1)

    def cmul(tab, c):
        return c[0] * tab + jnp.where(lo, -c[1], c[1]) * swap(tab)

    n_dbl = max(nc - 1, 0).bit_length()
    exps = [1] + [SUBLANES << k for k in range(int(math.log2(L // SUBLANES)))] + [L << k for k in range(n_dbl)]
    e_col = jnp.concatenate([jnp.full((1, 1), float(e), F32) for e in exps]
                            + [jnp.zeros((-len(exps) % SUBLANES, 1), F32)], axis=0)
    mag = jnp.exp(e_col * zr)
    pw_r = mag * jnp.cos(e_col * zi)
    pw_i = mag * jnp.sin(e_col * zi)
    apow = {e: (pw_r[k:k + 1], pw_i[k:k + 1]) for k, e in enumerate(exps)}

    def powtab(descending):
        i8 = lax.broadcasted_iota(I32, (SUBLANES, 1), 0).astype(F32)
        tab = powrows(SUBLANES - 1.0 - i8 if descending else i8)
        rows = SUBLANES
        while rows < L:
            more = cmul(tab, apow[rows])
            tab = jnp.concatenate([more, tab] if descending else [tab, more], axis=0)
            rows *= 2
        return tab

    er, ei = apow[1]
    den = a_re * a_re + a_im * a_im
    coef_r = ((er - 1.0) * a_re + ei * a_im) / den
    coef_i = (ei * a_re - (er - 1.0) * a_im) / den
    c_r = c_ref[0, 0]
    c_i = c_ref[0, 1]
    bb_r = coef_r * bt_ref[0, 0] - coef_i * bt_ref[0, 1]
    bb_i = coef_r * bt_ref[0, 1] + coef_i * bt_ref[0, 0]

    cb_rows = []
    for q in range(P):
        cbr = c_r * bb_r[q:q + 1] - c_i * bb_i[q:q + 1]
        cbi = c_r * bb_i[q:q + 1] + c_i * bb_r[q:q + 1]
        cb_rows.append(jnp.where(lo, cbr, -cbi))
    cb = jnp.concatenate(cb_rows, axis=0)
    pt0 = powtab(False)
    c1, c2, c3 = _split3(cb)
    t1, t2, t3 = _split3(pt0)
    kmat = (_dot_nt(c1, t1) + _dot_nt(c1, t2) + _dot_nt(c2, t1)
            + _dot_nt(c2, t2) + _dot_nt(c1, t3) + _dot_nt(c3, t1))

    pt_rev = powtab(True)
    pt_rev_sw = swap(pt_rev)
    f_rows = []
    for q in range(P):
        a1 = bb_r[q:q + 1]
        a2 = jnp.where(lo, -bb_i[q:q + 1], bb_i[q:q + 1])
        f_rows.append((a1 * pt_rev + a2 * pt_rev_sw).astype(BF16))
    fmat = jnp.concatenate(f_rows, axis=0)
    pt1 = cmul(pt0, apow[1])
    pt1_sw = swap(pt1)
    e_rows = []
    for p in range(P):
        b1 = jnp.where(lo, c_r[p:p + 1], -c_r[p:p + 1])
        b2 = -c_i[p:p + 1]
        e_rows.append((b1 * pt1 + b2 * pt1_sw).astype(BF16))
    emat_t = jnp.concatenate(e_rows, axis=0)

    lhs = jnp.concatenate(
        [jnp.concatenate([x_ref[b, q] for q in range(P)], axis=1) for b in range(bsz)], axis=0)
    s_end = _dot(lhs, fmat)

    rr = lax.broadcasted_iota(I32, (L, L), 0)
    cc = lax.broadcasted_iota(I32, (L, L), 1)
    causal = cc >= rr
    y = None
    for q0 in range(0, P, 2):
        rows = []
        for q in (q0, q0 + 1):
            tiles = []
            for p in range(P):
                kb = jnp.broadcast_to(kmat[q * P + p:q * P + p + 1, :], (L, L))
                toe = pltpu.roll(kb, 0, 1, stride=1, stride_axis=0)
                tiles.append(jnp.where(causal, toe, 0.0).astype(BF16))
            rows.append(jnp.concatenate(tiles, axis=1))
        part = _dot(lhs[:, q0 * L:(q0 + 2) * L], jnp.concatenate(rows, axis=0))
        y = part if y is None else y + part

    m_rows = bsz * nc
    ridx = lax.broadcasted_iota(I32, (m_rows, 1), 0)
    cidx = ridx % nc
    h = jnp.where(cidx >= 1, pltpu.roll(s_end, 1, 0), 0.0)
    d = 1
    while d < nc:
        sh = jnp.where(cidx >= d, pltpu.roll(h, d, 0), 0.0)
        h = h + cmul(sh, apow[d * L])
        d *= 2
    y = y + _dot_nt(h.astype(BF16), emat_t)

    for b in range(bsz):
        for p in range(P):
            yp = (y[b * nc:(b + 1) * nc, p * L:(p + 1) * L]
                  + d_ref[grp * P + p] * x_ref[b, p].astype(F32))
            out_ref[b, p] = jax.nn.gelu(yp).astype(BF16)


def _s5(ut, log_dt, a_re, a_im, b_re, b_im, c_re, c_im, d_skip):
    bsz, s5w, seq = ut.shape
    L = S5_CHUNK
    nc = seq // L
    groups = s5w // S5_GROUP
    n = S5_STATE
    x4 = ut.reshape(bsz, s5w, nc, L)
    dup = lambda t: jnp.concatenate([t, t], axis=-1).astype(F32)
    prm = jnp.zeros((groups, SUBLANES, 2 * n), F32)
    prm = prm.at[:, 0].set(dup(a_re)).at[:, 1].set(dup(a_im))
    prm = prm.at[:, 2].set(jnp.broadcast_to(log_dt.astype(F32)[:, None], (groups, 2 * n)))
    cpar = jnp.stack([dup(c_re), dup(c_im)], axis=1)
    btpar = jnp.stack([dup(jnp.swapaxes(b_re, 1, 2)), dup(jnp.swapaxes(b_im, 1, 2))], axis=1)
    out = pl.pallas_call(
        functools.partial(_s5_kernel, bsz=bsz, nc=nc),
        grid_spec=pltpu.PrefetchScalarGridSpec(
            num_scalar_prefetch=1,
            grid=(groups,),
            in_specs=[
                pl.BlockSpec((bsz, S5_GROUP, nc, L), lambda g, d: (0, g, 0, 0)),
                pl.BlockSpec((1, SUBLANES, 2 * n), lambda g, d: (g, 0, 0)),
                pl.BlockSpec((1, 2, S5_GROUP, 2 * n), lambda g, d: (g, 0, 0, 0)),
                pl.BlockSpec((1, 2, S5_GROUP, 2 * n), lambda g, d: (g, 0, 0, 0)),
            ],
            out_specs=pl.BlockSpec((bsz, S5_GROUP, nc, L), lambda g, d: (0, g, 0, 0)),
        ),
        out_shape=jax.ShapeDtypeStruct((bsz, s5w, nc, L), BF16),
        compiler_params=pltpu.CompilerParams(
            dimension_semantics=("parallel",), vmem_limit_bytes=VMEM_LIMIT),
        name="s5",
    )(d_skip.astype(F32), x4, prm, cpar, btpar)
    return out.reshape(bsz, s5w, seq)


def _log_sigmoid(x):
    return jnp.minimum(x, 0.0) - jnp.log(1.0 + jnp.exp(-jnp.abs(x)))


def _gates_kernel(gt_ref, out_ref, m_scr, *, bsz, chunk, n_sub):
    L = chunk
    H = ML_HEADS
    step = pl.program_id(0)

    @pl.when(step == 0)
    def _():
        m_scr[...] = jnp.zeros_like(m_scr)

    rr = lax.broadcasted_iota(I32, (L, L), 0)
    cc = lax.broadcasted_iota(I32, (L, L), 1)
    utri = (rr <= cc).astype(BF16)
    grow = lax.broadcasted_iota(I32, (16, 1), 0)
    lane_row = lax.broadcasted_iota(I32, (SUBLANES, L), 1)
    m_prev = [m_scr[b][0:H, 0:1] for b in range(bsz)]
    pairs = [(j, b) for j in range(n_sub) for b in range(bsz)]

    irow, brow, cm = {}, {}, {}
    for j, b in pairs:
        g = gt_ref[b, :, j * L:(j + 1) * L]
        g2 = jnp.where(grow >= H, _log_sigmoid(g), g)
        p1, p2, p3 = _split3(g2)
        brow[j, b] = (_dot(p1, utri) + _dot(p2, utri) + _dot(p3, utri))[H:2 * H]
        irow[j, b] = g[0:H]
    for j, b in pairs:
        ib = irow[j, b] - brow[j, b]
        x = jnp.concatenate([ib, ib], axis=0)
        sft = 1
        while sft < L:
            x = jnp.maximum(x, jnp.where(lane_row >= sft, pltpu.roll(x, sft, 1), -jnp.inf))
            sft *= 2
        cm[j, b] = x[0:H]
    for j, b in pairs:
        mp = m_prev[b]
        mm = jnp.maximum(cm[j, b], mp)
        b_last = brow[j, b][:, L - 1:L]
        m_next = b_last + jnp.maximum(mp, cm[j, b][:, L - 1:L])
        planes = (mm, jnp.exp(mp - mm), jnp.minimum(jnp.exp(-(brow[j, b] + mm)), EXP_CAP),
                  jnp.exp(b_last - brow[j, b] + irow[j, b] - m_next), irow[j, b] - brow[j, b],
                  jnp.broadcast_to(jnp.exp(b_last + mp - m_next), (H, L)))
        for k, rows in enumerate(planes):
            out_ref[k, b, :, j * L:(j + 1) * L] = jnp.concatenate([rows, rows], axis=0)
        m_prev[b] = m_next
    for b in range(bsz):
        m_scr[b] = jnp.broadcast_to(jnp.concatenate([m_prev[b], m_prev[b]], axis=0),
                                    (SUBLANES, LANES))


def _gates(gt, bsz, seq, chunk):
    n_sub = min(GATE_CHUNKS, seq // chunk)
    blk = n_sub * chunk
    return pl.pallas_call(
        functools.partial(_gates_kernel, bsz=bsz, chunk=chunk, n_sub=n_sub),
        grid=(seq // blk,),
        in_specs=[pl.BlockSpec((bsz, 16, blk), lambda c: (0, 0, c))],
        out_specs=pl.BlockSpec((6, bsz, SUBLANES, blk), lambda c: (0, 0, 0, c)),
        out_shape=jax.ShapeDtypeStruct((6, bsz, SUBLANES, seq), F32),
        scratch_shapes=[pltpu.VMEM((bsz, SUBLANES, LANES), F32)],
        compiler_params=pltpu.CompilerParams(
            dimension_semantics=("arbitrary",), vmem_limit_bytes=VMEM_LIMIT),
        name="gates",
    )(gt)


def _mlstm_kernel(qk_ref, v_ref, o_ref, gr_ref, cw_ref, cb_ref, ng_ref, y_ref,
                  tail_scr, c_scr, *, bsz, chunk, width):
    L = chunk
    H = ML_HEADS
    dh = width // H
    step = pl.program_id(0)
    assert L == LANES and dh == LANES, "column replication below uses one 128x128 tile per head"

    @pl.when(step == 0)
    def _():
        tail_scr[...] = jnp.zeros_like(tail_scr)
        c_scr[...] = jnp.zeros_like(c_scr)

    rr = lax.broadcasted_iota(I32, (L, L), 0)
    cc = lax.broadcasted_iota(I32, (L, L), 1)
    causal = cc <= rr
    eye = (rr == cc).astype(BF16)
    ones_blk = jnp.ones((L, dh), BF16)
    scale = 1.0 / math.sqrt(dh)
    inv_dh = 1.0 / dh
    n_str = bsz * H
    streams = [(b, hd) for b in range(bsz) for hd in range(H)]

    def replicate(rows, two_terms):
        p = jnp.concatenate([jnp.broadcast_to(rows[h:h + 1], (LANES, L)) for h in range(H)], axis=0)
        p1 = p.astype(BF16)
        if not two_terms:
            return _dot_nt(eye, p1)
        return _dot_nt(eye, p1) + _dot_nt(eye, (p - p1.astype(F32)).astype(BF16))

    def rowsum(x):
        hi = x.astype(BF16)
        lo = (x - hi.astype(F32)).astype(BF16)
        return _dot(hi, ones_blk) + _dot(lo, ones_blk)

    tails = [tail_scr[b] for b in range(bsz)]
    caugs = [c_scr[i] for i in range(n_str)]
    reps = [[replicate(gr_ref[k, b][0:H], k == 0) for b in range(bsz)] for k in range(4)]
    col = lambda k: [reps[k][b][:, hd * LANES:(hd + 1) * LANES] for b, hd in streams]
    mm_rep, w_inter, enm, ws_rep = col(0), col(1), col(2), col(3)
    ib_row = [gr_ref[4, b][hd:hd + 1] for b, hd in streams]
    decays = [gr_ref[5, b][hd:hd + 1, 0:1] for b, hd in streams]

    q_all, k_all, new_tails = [], [], []
    for b in range(bsz):
        xqk = qk_ref[b].astype(F32)
        ext = jnp.concatenate([tails[b], xqk], axis=0)
        acc = jnp.broadcast_to(cb_ref[...], xqk.shape)
        for j in range(CONV_WIDTH):
            back = CONV_WIDTH - 1 - j
            sh = ext if back == 0 else pltpu.roll(ext, back, 0)
            acc = acc + cw_ref[j:j + 1, :] * sh[SUBLANES:]
        new_tails.append(xqk[L - SUBLANES:])
        qkc = acc * _sigmoid(acc)
        q_all.append(qkc[:, :width].astype(BF16))
        k_all.append((qkc[:, width:] * scale).astype(BF16))

    q_h = [q_all[b][:, hd * dh:(hd + 1) * dh] for b, hd in streams]
    k_h = [k_all[b][:, hd * dh:(hd + 1) * dh] for b, hd in streams]
    vaug = [jnp.concatenate([v_ref[b][:, hd * dh:(hd + 1) * dh], ones_blk], axis=1)
            for b, hd in streams]
    scores = [_dot_nt(q_h[i], k_h[i]) for i in range(n_str)]
    inter = [_dot(q_h[i], caugs[i].astype(BF16)) for i in range(n_str)]
    s_bf = [(scores[i] * jnp.exp(jnp.where(causal, ib_row[i] - mm_rep[i], -jnp.inf))).astype(BF16)
            for i in range(n_str)]
    kw = [(k_h[i].astype(F32) * ws_rep[i]).astype(BF16) for i in range(n_str)]
    intra = [_dot(s_bf[i], vaug[i]) for i in range(n_str)]
    upd = [_dot_tn(kw[i], vaug[i]) for i in range(n_str)]

    hh = []
    for i in range(n_str):
        num = w_inter[i] * inter[i][:, :dh] + intra[i][:, :dh]
        den = w_inter[i] * inter[i][:, dh:] + intra[i][:, dh:]
        hh.append(num / jnp.maximum(jnp.abs(den), enm[i]))
    mu = [rowsum(h) * inv_dh for h in hh]
    ctr = [hh[i] - mu[i] for i in range(n_str)]
    var = [rowsum(c * c) * inv_dh for c in ctr]
    outs = []
    for i, (b, hd) in enumerate(streams):
        sl = slice(hd * dh, (hd + 1) * dh)
        hn = ctr[i] * lax.rsqrt(var[i] + LN_EPS) * ng_ref[:, sl]
        outs.append(hn * _sigmoid(o_ref[b][:, sl].astype(F32)))

    for b in range(bsz):
        tail_scr[b] = new_tails[b]
        y_ref[b] = jnp.concatenate(outs[b * H:(b + 1) * H], axis=1).astype(BF16)
    for i in range(n_str):
        c_scr[i] = decays[i] * caugs[i] + upd[i]


def _mlstm(qk, v, o, gt, conv_w, conv_b, norm_g, bsz, seq):
    width = v.shape[-1]
    L = min(ML_CHUNK, seq)
    dh = width // ML_HEADS
    qk3 = qk.reshape(bsz, seq, 2 * width)
    v3 = v.reshape(bsz, seq, width)
    o3 = o.reshape(bsz, seq, width)
    gate_rows = _gates(gt, bsz, seq, L)
    full = lambda c: (0, 0)
    return pl.pallas_call(
        functools.partial(_mlstm_kernel, bsz=bsz, chunk=L, width=width),
        grid=(seq // L,),
        in_specs=[
            pl.BlockSpec((bsz, L, 2 * width), lambda c: (0, c, 0)),
            pl.BlockSpec((bsz, L, width), lambda c: (0, c, 0)),
            pl.BlockSpec((bsz, L, width), lambda c: (0, c, 0)),
            pl.BlockSpec((6, bsz, SUBLANES, L), lambda c: (0, 0, 0, c)),
            pl.BlockSpec((CONV_WIDTH, 2 * width), full),
            pl.BlockSpec((1, 2 * width), full),
            pl.BlockSpec((1, width), full),
        ],
        out_specs=pl.BlockSpec((bsz, L, width), lambda c: (0, c, 0)),
        out_shape=jax.ShapeDtypeStruct((bsz, seq, width), BF16),
        scratch_shapes=[
            pltpu.VMEM((bsz, SUBLANES, 2 * width), F32),
            pltpu.VMEM((bsz * ML_HEADS, dh, 2 * dh), F32),
        ],
        compiler_params=pltpu.CompilerParams(
            dimension_semantics=("arbitrary",), vmem_limit_bytes=VMEM_LIMIT),
        name="mlstm",
    )(qk3, v3, o3, gate_rows, conv_w.astype(F32), conv_b.reshape(1, -1).astype(F32),
      norm_g.reshape(1, -1).astype(F32))


def _post_kernel(x_ref, gt_ref, yml_ref, wglut_ref, bglu_ref, wout_ref, nffn_ref, wrt_ref, br_ref,
                 h2_ref, xn_ref, eidx_ref, gate_ref, rank_ref, tcnt_ref, carry_scr):
    step = pl.program_id(0)

    @pl.when(step == 0)
    def _():
        carry_scr[...] = jnp.zeros_like(carry_scr)

    gt = gt_ref[0]
    s5w = gt.shape[0]
    zt = _dot(wglut_ref[...], gt) + bglu_ref[...]
    s5t = (gt.astype(F32) * _sigmoid(zt)).astype(BF16)
    h2 = (x_ref[...] + _dot_tn(s5t, wout_ref[:s5w, :])
          + _dot(yml_ref[...], wout_ref[s5w:, :]))
    h2_ref[...] = h2
    ms = jnp.mean(h2 * h2, axis=-1, keepdims=True)
    xn = h2 * lax.rsqrt(ms + RMS_EPS) * nffn_ref[...]
    xb = xn.astype(BF16)
    xn_ref[...] = xb

    tm = xn.shape[0]
    logits = _dot_nt(wrt_ref[...], xb) + br_ref[...]
    eio = lax.broadcasted_iota(I32, (N_EXPERTS, tm), 0).astype(F32)
    vals = logits
    onehot = jnp.zeros((N_EXPERTS, tm), F32)
    idxs, tops = [], []
    for _ in range(TOP_K):
        mx = jnp.max(vals, axis=0, keepdims=True)
        idx = jnp.min(jnp.where(vals == mx, eio, float(N_EXPERTS)), axis=0, keepdims=True)
        sel = eio == idx
        onehot = onehot + sel.astype(F32)
        vals = jnp.where(sel, -jnp.inf, vals)
        idxs.append(idx)
        tops.append(mx)
    exps = [jnp.exp(t - tops[0]) for t in tops]
    tot = exps[0] + exps[1] + exps[2] + exps[3]
    pad_f = jnp.zeros((SUBLANES - TOP_K, tm), F32)
    eidx_ref[...] = jnp.concatenate(idxs + [pad_f], axis=0).astype(I32)
    gate_ref[...] = jnp.concatenate([e / tot for e in exps] + [pad_f], axis=0)

    rr = lax.broadcasted_iota(I32, (tm, tm), 0)
    cc = lax.broadcasted_iota(I32, (tm, tm), 1)
    before = (rr < cc).astype(BF16)
    carry = carry_scr[:, 0:1]
    rank_ex = _dot(onehot.astype(BF16), before) + carry
    ranks = [jnp.sum(jnp.where(eio == i, rank_ex, 0.0), axis=0, keepdims=True) for i in idxs]
    rank_ref[...] = jnp.concatenate(ranks + [pad_f], axis=0).astype(I32)
    tile_cnt = jnp.sum(onehot, axis=1, keepdims=True)
    carry_scr[...] = jnp.broadcast_to(carry + tile_cnt, carry_scr.shape)
    for s in range(tm // DISPATCH_TILE):
        sub = onehot[:, s * DISPATCH_TILE:(s + 1) * DISPATCH_TILE]
        tcnt_ref[0, s] = jnp.broadcast_to(jnp.sum(sub, axis=1, keepdims=True), carry_scr.shape)


def _post(x2d, g_t, y_ml, w_glu, b_glu, w_out, norm_g, w_router, b_router):
    n_tok, d = x2d.shape
    bsz, s5w, seq = g_t.shape
    mlw = y_ml.shape[-1]
    tm = min(PROJ_TILE, seq)
    tpb = seq // tm
    full = lambda i: (0, 0)
    row = lambda i: (i, 0)
    colb = lambda i: (0, i)
    return pl.pallas_call(
        _post_kernel,
        grid=(n_tok // tm,),
        in_specs=[
            pl.BlockSpec((tm, d), row),
            pl.BlockSpec((1, s5w, tm), lambda i: (i // tpb, 0, i % tpb)),
            pl.BlockSpec((tm, mlw), row),
            pl.BlockSpec((s5w, s5w), full),
            pl.BlockSpec((s5w, 1), full),
            pl.BlockSpec((d, d), full),
            pl.BlockSpec((1, d), full),
            pl.BlockSpec((N_EXPERTS, d), full),
            pl.BlockSpec((N_EXPERTS, 1), full),
        ],
        out_specs=[
            pl.BlockSpec((tm, d), row),
            pl.BlockSpec((tm, d), row),
            pl.BlockSpec((SUBLANES, tm), colb),
            pl.BlockSpec((SUBLANES, tm), colb),
            pl.BlockSpec((SUBLANES, tm), colb),
            pl.BlockSpec((1, tm // DISPATCH_TILE, N_EXPERTS, LANES), lambda i: (i, 0, 0, 0)),
        ],
        out_shape=[
            jax.ShapeDtypeStruct((n_tok, d), F32),
            jax.ShapeDtypeStruct((n_tok, d), BF16),
            jax.ShapeDtypeStruct((SUBLANES, n_tok), I32),
            jax.ShapeDtypeStruct((SUBLANES, n_tok), F32),
            jax.ShapeDtypeStruct((SUBLANES, n_tok), I32),
            jax.ShapeDtypeStruct((n_tok // tm, tm // DISPATCH_TILE, N_EXPERTS, LANES), F32),
        ],
        scratch_shapes=[pltpu.VMEM((N_EXPERTS, LANES), F32)],
        compiler_params=pltpu.CompilerParams(
            dimension_semantics=("arbitrary",), vmem_limit_bytes=VMEM_LIMIT),
        name="post_router",
    )(x2d, g_t, y_ml, w_glu.T.astype(BF16), b_glu.reshape(-1, 1).astype(F32), w_out.astype(BF16),
      norm_g.reshape(1, -1).astype(F32), w_router.T.astype(BF16),
      b_router.reshape(-1, 1).astype(F32))


def _slab_loop(base, nbig_ref, nsmall_ref, fn, n_slabs=N_EXPERTS):
    def per_expert(e, carry):
        idx = base + e
        nbig = nbig_ref[idx]

        def big(j, c2):
            fn(idx, j * BIG_CHUNK, BIG_CHUNK)
            return c2

        def small(j, c2):
            fn(idx, nbig * BIG_CHUNK + j * SLAB_ALIGN, SLAB_ALIGN)
            return c2

        lax.fori_loop(0, nbig, big, 0)
        lax.fori_loop(0, nsmall_ref[idx], small, 0)
        return carry

    lax.fori_loop(0, n_slabs, per_expert, 0)


def _plan_copies(step, plan, fn):
    nbig_ref, nsmall_ref, _, bsrc_ref, bdst_ref, ssrc_ref, sdst_ref = plan

    def big(j, carry):
        k = step * MAX_BIG + j
        fn(bsrc_ref[k], bdst_ref[k], BIG_CHUNK)
        return carry

    def small(j, carry):
        k = step * MAX_SMALL + j
        fn(ssrc_ref[k], sdst_ref[k], SLAB_ALIGN)
        return carry

    lax.fori_loop(0, nbig_ref[step], big, 0)
    lax.fori_loop(0, nsmall_ref[step], small, 0)


def _plan_wait(step, plan, copy):
    copy(0, 0, DISPATCH_TILE * TOP_K).wait()

    def one(j, carry):
        copy(0, 0, SLAB_ALIGN).wait()
        return carry

    lax.fori_loop(0, plan[2][step], one, 0)


def _unpack_pairs(words):
    lo = lax.bitcast_convert_type(words << 16, F32).astype(BF16)
    hi = lax.bitcast_convert_type(words & jnp.uint32(0xFFFF0000), F32).astype(BF16)
    return lo, hi


def _pack_pairs(lo_f32, hi_f32):
    lo = lax.bitcast_convert_type(lo_f32, U32) >> 16
    hi = lax.bitcast_convert_type(hi_f32, U32) & jnp.uint32(0xFFFF0000)
    return hi | lo


def _dispatch_kernel(*refs, n_steps):
    plan = refs[:7]
    znbig_ref, znsmall_ref, zrow_ref, xn_ref, pos_ref, xs_hbm, stage, zbuf, sem, zsem = refs[7:]
    i = pl.program_id(0)
    slot = i % 2
    tw, d = xn_ref.shape
    n_rows = stage.shape[1]
    half = d // 2

    def copy(s, stage_row, buf_row, rows):
        src = stage.at[s, pl.ds(pl.multiple_of(stage_row, SLAB_ALIGN), rows), :]
        dst = xs_hbm.at[pl.ds(pl.multiple_of(buf_row, SLAB_ALIGN), rows), :]
        return pltpu.make_async_copy(src, dst, sem)

    @pl.when(i == 0)
    def _():
        zbuf[...] = jnp.zeros_like(zbuf)

        def tail(idx, r0, rows):
            dst = xs_hbm.at[pl.ds(pl.multiple_of(zrow_ref[idx] + r0, SLAB_ALIGN), rows), :]
            return pltpu.make_async_copy(zbuf.at[pl.ds(0, rows), :], dst, zsem)

        _slab_loop(0, znbig_ref, znsmall_ref, lambda *a: tail(*a).start(), N_EXPERTS + 1)
        _slab_loop(0, znbig_ref, znsmall_ref, lambda *a: tail(*a).wait(), N_EXPERTS + 1)

    xt = xn_ref[...]
    blk = STAGE_BLOCK
    for r0 in range(0, n_rows, blk):
        rid = lax.broadcasted_iota(I32, (blk, tw), 0) + r0
        hit = rid == pos_ref[0:1, :]
        for k in range(1, TOP_K):
            hit = jnp.logical_or(hit, rid == pos_ref[k:k + 1, :])
        srt = _dot(jnp.where(hit, 1.0, 0.0).astype(BF16), xt)
        stage[slot, r0:r0 + blk, :] = _pack_pairs(srt[:, :half], srt[:, half:])

    @pl.when(i > 0)
    def _():
        _plan_wait(i - 1, plan, functools.partial(copy, 1 - slot))

    _plan_copies(i, plan, lambda *a: copy(slot, *a).start())

    @pl.when(i == n_steps - 1)
    def _():
        _plan_wait(i, plan, functools.partial(copy, slot))


def _dispatch(xn, pos8, plan, tails, n_slots):
    n_tok, d = xn.shape
    tw = DISPATCH_TILE
    n_steps = n_tok // tw
    return pl.pallas_call(
        functools.partial(_dispatch_kernel, n_steps=n_steps),
        grid_spec=pltpu.PrefetchScalarGridSpec(
            num_scalar_prefetch=10,
            grid=(n_steps,),
            in_specs=[
                pl.BlockSpec((tw, d), lambda i, *_: (i, 0)),
                pl.BlockSpec((SUBLANES, tw), lambda i, *_: (0, i)),
            ],
            out_specs=pl.BlockSpec(memory_space=pl.ANY),
            scratch_shapes=[
                pltpu.VMEM((2, STAGE_ROWS, d // 2), U32),
                pltpu.VMEM((BIG_CHUNK, d // 2), U32),
                pltpu.SemaphoreType.DMA(()),
                pltpu.SemaphoreType.DMA(()),
            ],
        ),
        out_shape=jax.ShapeDtypeStruct((n_slots, d // 2), U32),
        compiler_params=pltpu.CompilerParams(
            dimension_semantics=("arbitrary",), vmem_limit_bytes=VMEM_LIMIT),
        name="dispatch",
    )(*plan, *tails, xn, pos8)


def _expert_kernel(te_ref, nu_ref, first_ref, nxt_ref, par_ref, x_ref, wup_hbm, bup_ref, wdn_hbm,
                   bdn_ref, y_ref, wup_f32, wdn_f32, wup_bf, wdn_bf, sem, *, d_ff):
    i = pl.program_id(0)
    n_used = nu_ref[0]

    def fetch(e, s):
        return (pltpu.make_async_copy(wup_hbm.at[e], wup_f32.at[s], sem.at[s, 0]),
                pltpu.make_async_copy(wdn_hbm.at[e], wdn_f32.at[s], sem.at[s, 1]))

    @pl.when(jnp.logical_and(i < n_used, first_ref[i] == 1))
    def _():
        s = par_ref[i]

        @pl.when(i == 0)
        def _():
            for c in fetch(te_ref[i], s):
                c.start()

        for c in fetch(te_ref[i], s):
            c.wait()
        wup_bf[...] = wup_f32[s].astype(BF16)
        wdn_bf[...] = wdn_f32[s].astype(BF16)

        @pl.when(nxt_ref[i] >= 0)
        def _():
            for c in fetch(nxt_ref[i], 1 - s):
                c.start()

    @pl.when(i < n_used)
    def _():
        lo, hi = _unpack_pairs(x_ref[...])
        half = lo.shape[1]
        h = _dot(lo, wup_bf[:half, :]) + _dot(hi, wup_bf[half:, :]) + bup_ref[0]
        gl = jnp.minimum(h[:, :d_ff], SWIGLU_LIMIT)
        lin = jnp.clip(h[:, d_ff:], -SWIGLU_LIMIT, SWIGLU_LIMIT)
        act = gl * _sigmoid(SWIGLU_ALPHA * gl) * (lin + 1.0)
        y = _dot(act.astype(BF16), wdn_bf[...]) + bdn_ref[0]
        yb = y.astype(BF16).astype(F32)
        y_ref[...] = _pack_pairs(yb[:, :half], yb[:, half:])

    @pl.when(i >= n_used)
    def _():
        y_ref[...] = jnp.zeros_like(y_ref)


def _experts(xs, tile_e, n_used, first, nxt, parity, w_up, b_up, w_down, b_down):
    n_slots, half = xs.shape
    d = 2 * half
    tm = EXPERT_TILE
    n_tiles = n_slots // tm
    d_ff = w_down.shape[1]
    return pl.pallas_call(
        functools.partial(_expert_kernel, d_ff=d_ff),
        grid_spec=pltpu.PrefetchScalarGridSpec(
            num_scalar_prefetch=5,
            grid=(n_tiles,),
            in_specs=[
                pl.BlockSpec((tm, half), lambda i, te, nu, *_: (jnp.minimum(i, nu[0] - 1), 0)),
                pl.BlockSpec(memory_space=pl.ANY),
                pl.BlockSpec((1, 1, 2 * d_ff), lambda i, te, *_: (te[i], 0, 0)),
                pl.BlockSpec(memory_space=pl.ANY),
                pl.BlockSpec((1, 1, d), lambda i, te, *_: (te[i], 0, 0)),
            ],
            out_specs=pl.BlockSpec((tm, half), lambda i, *_: (i, 0)),
            scratch_shapes=[
                pltpu.VMEM((2, d, 2 * d_ff), F32),
                pltpu.VMEM((2, d_ff, d), F32),
                pltpu.VMEM((d, 2 * d_ff), BF16),
                pltpu.VMEM((d_ff, d), BF16),
                pltpu.SemaphoreType.DMA((2, 2)),
            ],
        ),
        out_shape=jax.ShapeDtypeStruct((n_slots, half), U32),
        compiler_params=pltpu.CompilerParams(
            dimension_semantics=("arbitrary",), vmem_limit_bytes=VMEM_LIMIT),
        name="experts",
    )(tile_e, n_used, first, nxt, parity, xs, w_up, b_up.reshape(N_EXPERTS, 1, -1),
      w_down, b_down.reshape(N_EXPERTS, 1, -1))


def _combine_kernel(*refs, n_steps):
    plan = refs[:7]
    ys_hbm, h2_ref, posc_ref, gatec_ref, ng_ref, out_ref, stage, sem = refs[7:]
    i = pl.program_id(0)
    slot = i % 2
    tw, d = h2_ref.shape
    n_rows = stage.shape[1]

    def copy(s, stage_row, buf_row, rows):
        src = ys_hbm.at[pl.ds(pl.multiple_of(buf_row, SLAB_ALIGN), rows), :]
        dst = stage.at[s, pl.ds(pl.multiple_of(stage_row, SLAB_ALIGN), rows), :]
        return pltpu.make_async_copy(src, dst, sem.at[s])

    @pl.when(i == 0)
    def _():
        stage[...] = jnp.zeros_like(stage)
        _plan_copies(0, plan, lambda *a: copy(0, *a).start())

    @pl.when(i + 1 < n_steps)
    def _():
        _plan_copies(i + 1, plan, lambda *a: copy(1 - slot, *a).start())

    _plan_wait(i, plan, functools.partial(copy, slot))

    y_lo, y_hi = _unpack_pairs(stage[slot])
    blk = 128
    for t0 in range(0, tw, blk):
        cid = lax.broadcasted_iota(I32, (blk, n_rows), 1)
        pg = jnp.zeros((blk, n_rows), F32)
        for k in range(TOP_K):
            pg = jnp.where(cid == posc_ref[t0:t0 + blk, k:k + 1],
                           gatec_ref[t0:t0 + blk, k:k + 1], pg)
        pgb = pg.astype(BF16)
        moe = jnp.concatenate([_dot(pgb, y_lo), _dot(pgb, y_hi)], axis=1)
        acc = h2_ref[t0:t0 + blk, :] + moe
        ms = jnp.mean(acc * acc, axis=-1, keepdims=True)
        out_ref[t0:t0 + blk, :] = acc * lax.rsqrt(ms + RMS_EPS) * ng_ref[...]


def _combine(ys, h2, pos_cols, gate_cols, plan, norm_g):
    n_tok, d = h2.shape
    tw = DISPATCH_TILE
    n_steps = n_tok // tw
    return pl.pallas_call(
        functools.partial(_combine_kernel, n_steps=n_steps),
        grid_spec=pltpu.PrefetchScalarGridSpec(
            num_scalar_prefetch=7,
            grid=(n_steps,),
            in_specs=[
                pl.BlockSpec(memory_space=pl.ANY),
                pl.BlockSpec((tw, d), lambda i, *_: (i, 0)),
                pl.BlockSpec((tw, SUBLANES), lambda i, *_: (i, 0)),
                pl.BlockSpec((tw, SUBLANES), lambda i, *_: (i, 0)),
                pl.BlockSpec((1, d), lambda i, *_: (0, 0)),
            ],
            out_specs=pl.BlockSpec((tw, d), lambda i, *_: (i, 0)),
            scratch_shapes=[
                pltpu.VMEM((2, STAGE_ROWS, d // 2), U32),
                pltpu.SemaphoreType.DMA((2,)),
            ],
        ),
        out_shape=jax.ShapeDtypeStruct((n_tok, d), F32),
        compiler_params=pltpu.CompilerParams(
            dimension_semantics=("arbitrary",), vmem_limit_bytes=VMEM_LIMIT),
        name="combine",
    )(*plan, ys, h2, pos_cols, gate_cols, norm_g.reshape(1, -1).astype(F32))


def _moe(h2, xn, eidx, gate, rank, tcnt, w_up, b_up, w_down, b_down, norm_final_g):
    n_tok, d = h2.shape
    tm = EXPERT_TILE
    tw = DISPATCH_TILE
    n_steps = n_tok // tw
    n_exp = N_EXPERTS
    e_ids = jnp.arange(n_exp, dtype=I32)
    al = SLAB_ALIGN
    tile_cnt = tcnt.reshape(n_steps, n_exp, LANES)[:, :, 0].astype(I32)
    slab_rows = (tile_cnt + al - 1) // al * al
    exp_rows = jnp.sum(slab_rows, axis=0)
    region = (exp_rows + tm - 1) // tm * tm
    pad_end = jnp.cumsum(region)
    pad_start = pad_end - region
    n_slots = (n_tok * TOP_K + n_steps * n_exp * (al - 1) + n_exp * (tm - 1) + tm - 1) // tm * tm
    n_tiles = n_slots // tm

    tile_carry = jnp.cumsum(tile_cnt, axis=0) - tile_cnt
    tile_off = jnp.cumsum(slab_rows, axis=1) - slab_rows
    slab_row0 = pad_start[None, :] + jnp.cumsum(slab_rows, axis=0) - slab_rows

    def chunks(rows):
        return rows // BIG_CHUNK, rows % BIG_CHUNK // al

    def flat(cnt, src0, dst0, size, n_max):
        run = jnp.cumsum(cnt, axis=1)
        j = jnp.arange(n_max, dtype=I32)
        e_of = jnp.sum(run[:, None, :] <= j[None, :, None], axis=-1)
        sel = e_of[:, :, None] == e_ids[None, None, :]
        pick = lambda a: jnp.sum(jnp.where(sel, a[:, None, :], 0), axis=-1)
        local = (j[None, :] - pick(run - cnt)) * size
        return run[:, -1], (pick(src0) + local).reshape(-1), (pick(dst0) + local).reshape(-1)

    nbig, nsmall = chunks(slab_rows)
    big_tot, big_src, big_dst = flat(nbig, tile_off, slab_row0, BIG_CHUNK, MAX_BIG)
    small_tot, small_src, small_dst = flat(nsmall, tile_off + nbig * BIG_CHUNK,
                                           slab_row0 + nbig * BIG_CHUNK, al, MAX_SMALL)
    pad_groups = (jnp.sum(slab_rows, axis=1) - tw * TOP_K) // al
    plan = (big_tot, small_tot, pad_groups, big_src, big_dst, small_src, small_dst)
    tail_rows = jnp.concatenate([region - exp_rows, n_slots - pad_end[-1:]])
    tails = (*chunks(tail_rows), jnp.concatenate([pad_start + exp_rows, pad_end[-1:]]))

    base_t = jnp.repeat((tile_off - tile_carry).T, tw, axis=1)
    e_sel = eidx[:TOP_K][None] == e_ids[:, None, None]
    pos = jnp.sum(jnp.where(e_sel, base_t[:, None, :], 0), axis=0) + rank[:TOP_K]
    pos8 = jnp.concatenate([pos, jnp.full((SUBLANES - TOP_K, n_tok), -1, I32)], axis=0)

    tile_start = jnp.arange(n_tiles, dtype=I32) * tm
    tile_e = jnp.minimum(jnp.sum(pad_end[None, :] <= tile_start[:, None], axis=-1),
                         n_exp - 1).astype(I32)
    n_used = (pad_end[-1] // tm).astype(I32).reshape(1)
    t_ids = jnp.arange(n_tiles, dtype=I32)
    first = jnp.logical_and(jnp.logical_or(t_ids == 0, tile_e != jnp.roll(tile_e, 1)),
                            t_ids < n_used[0])
    later = jnp.logical_and(first[None, :], t_ids[None, :] > t_ids[:, None])
    nxt_tile = jnp.min(jnp.where(later, t_ids[None, :], n_tiles), axis=1)
    nxt_e = jnp.sum(jnp.where(t_ids[None, :] == nxt_tile[:, None], tile_e[None, :], 0), axis=1)
    nxt = jnp.where(nxt_tile < n_tiles, nxt_e, -1).astype(I32)
    parity = ((jnp.cumsum(first.astype(I32)) - 1) % 2).astype(I32)
    first = first.astype(I32)

    xs = _dispatch(xn, pos8, plan, tails, n_slots)
    ys = _experts(xs, tile_e, n_used, first, nxt, parity, w_up, b_up, w_down, b_down)
    return _combine(ys, h2, pos8.T, gate.T, plan, norm_final_g)


def kernel(x, norm_mix_g, w_in, s5_log_dt, s5_a_re, s5_a_im, s5_b_re, s5_b_im, s5_c_re, s5_c_im,
           s5_d, s5_w_glu, s5_b_glu, ml_conv_w, ml_conv_b, ml_b_gates, ml_norm_g, w_out,
           norm_ffn_g, w_router, b_router, w_up, b_up, w_down, b_down, norm_final_g):
    bsz, seq, d = x.shape
    depth = w_in.shape[0]
    assert depth == 1, "single-layer block"
    l = 0
    x2d = x.reshape(bsz * seq, d)
    qk, v, o, ut, gt = _in_proj(x2d, norm_mix_g[l], w_in[l], ml_b_gates[l], bsz, seq)
    g_t = _s5(ut, s5_log_dt[l], s5_a_re[l], s5_a_im[l], s5_b_re[l], s5_b_im[l],
              s5_c_re[l], s5_c_im[l], s5_d[l])
    y_ml = _mlstm(qk, v, o, gt, ml_conv_w[l], ml_conv_b[l], ml_norm_g[l], bsz, seq)
    y_ml = y_ml.reshape(bsz * seq, -1)
    h2, xn, eidx, gate, rank, tcnt = _post(x2d, g_t, y_ml, s5_w_glu[l], s5_b_glu[l], w_out[l],
                                           norm_ffn_g[l], w_router[l], b_router[l])
    out = _moe(h2, xn, eidx, gate, rank, tcnt, w_up[l], b_up[l], w_down[l], b_down[l],
               norm_final_g)
    return out.reshape(bsz, seq, d)
```

```python
import functools
import math

import jax
import jax.numpy as jnp
from jax import lax
from jax.experimental import pallas as pl
from jax.experimental.pallas import tpu as pltpu

F32 = jnp.float32
BF16 = jnp.bfloat16
I32 = jnp.int32
U32 = jnp.uint32

S5_GROUP = 16
S5_STATE = 64
ML_HEADS = 4
CONV_WIDTH = 4
N_EXPERTS = 32
TOP_K = 4
SWIGLU_LIMIT = 7.0
SWIGLU_ALPHA = 1.702
RMS_EPS = 1e-5
LN_EPS = 1e-6

LANES = 128
SUBLANES = 8
S5_CHUNK = LANES
ML_CHUNK = 128
GATE_CHUNKS = 8
EXP_CAP = 1e38
PROJ_TILE = 1024
EXPERT_TILE = 512
DISPATCH_TILE = 256
SLAB_ALIGN = SUBLANES
BIG_CHUNK = 32
STAGE_ROWS = DISPATCH_TILE * TOP_K + N_EXPERTS * SLAB_ALIGN
STAGE_BLOCK = STAGE_ROWS // 2
MAX_BIG = STAGE_ROWS // BIG_CHUNK
MAX_SMALL = N_EXPERTS * (BIG_CHUNK // SLAB_ALIGN - 1)
VMEM_LIMIT = 56 * 1024 * 1024

_NT = (((1,), (1,)), ((), ()))
_TN = (((0,), (0,)), ((), ()))


def _dot(a, b):
    return jnp.dot(a, b, preferred_element_type=F32)


def _dot_nt(a, b):
    return lax.dot_general(a, b, _NT, preferred_element_type=F32)


def _dot_tn(a, b):
    return lax.dot_general(a, b, _TN, preferred_element_type=F32)


def _split3(x):
    p1 = x.astype(BF16)
    r1 = x - p1.astype(F32)
    p2 = r1.astype(BF16)
    r2 = r1 - p2.astype(F32)
    return p1, p2, r2.astype(BF16)


def _sigmoid(x):
    return 0.5 * jnp.tanh(0.5 * x) + 0.5


def _inproj_kernel(x_ref, g_ref, wnat_ref, wut_ref, wgt_ref, bg_ref,
                   qk_ref, v_ref, o_ref, ut_ref, gt_ref, *, width):
    x = x_ref[...]
    ms = jnp.mean(x * x, axis=-1, keepdims=True)
    hn = (x * lax.rsqrt(ms + RMS_EPS) * g_ref[...]).astype(BF16)
    nat = _dot(hn, wnat_ref[...])
    qk_ref[...] = nat[:, :2 * width].astype(BF16)
    v_ref[...] = nat[:, 2 * width:3 * width].astype(BF16)
    o_ref[...] = nat[:, 3 * width:].astype(BF16)
    ut_ref[0] = _dot_nt(wut_ref[...], hn).astype(BF16)
    gt_ref[0] = _dot_nt(wgt_ref[...], hn) + bg_ref[...]


def _in_proj(x2d, norm_g, w_in, b_gates, bsz, seq):
    n_tok, d = x2d.shape
    s5w = d // 2
    mlw = d - s5w
    tm = min(PROJ_TILE, seq)
    tpb = seq // tm
    w_bf = w_in.astype(BF16)
    w_nat = w_bf[:, s5w:s5w + 4 * mlw]
    w_ut = w_bf[:, :s5w].T
    n_gate = 2 * ML_HEADS
    w_gt = jnp.zeros((16, d), BF16).at[:n_gate].set(w_bf[:, s5w + 4 * mlw:].T)
    b_g = jnp.zeros((16, 1), F32).at[:n_gate, 0].set(b_gates.astype(F32))
    grid = (n_tok // tm,)
    full = lambda i: (0, 0)
    return pl.pallas_call(
        functools.partial(_inproj_kernel, width=mlw),
        grid=grid,
        in_specs=[
            pl.BlockSpec((tm, d), lambda i: (i, 0)),
            pl.BlockSpec((1, d), full),
            pl.BlockSpec((d, 4 * mlw), full),
            pl.BlockSpec((s5w, d), full),
            pl.BlockSpec((16, d), full),
            pl.BlockSpec((16, 1), full),
        ],
        out_specs=[
            pl.BlockSpec((tm, 2 * mlw), lambda i: (i, 0)),
            pl.BlockSpec((tm, mlw), lambda i: (i, 0)),
            pl.BlockSpec((tm, mlw), lambda i: (i, 0)),
            pl.BlockSpec((1, s5w, tm), lambda i: (i // tpb, 0, i % tpb)),
            pl.BlockSpec((1, 16, tm), lambda i: (i // tpb, 0, i % tpb)),
        ],
        out_shape=[
            jax.ShapeDtypeStruct((n_tok, 2 * mlw), BF16),
            jax.ShapeDtypeStruct((n_tok, mlw), BF16),
            jax.ShapeDtypeStruct((n_tok, mlw), BF16),
            jax.ShapeDtypeStruct((bsz, s5w, seq), BF16),
            jax.ShapeDtypeStruct((bsz, 16, seq), F32),
        ],
        compiler_params=pltpu.CompilerParams(
            dimension_semantics=("parallel",), vmem_limit_bytes=VMEM_LIMIT),
        name="in_proj",
    )(x2d, norm_g.reshape(1, d).astype(F32), w_nat, w_ut, w_gt, b_g)


def _s5_kernel(d_ref, x_ref, prm_ref, c_ref, bt_ref, out_ref, *, bsz, nc):
    L = S5_CHUNK
    P = S5_GROUP
    N = S5_STATE
    grp = pl.program_id(0)
    lane = lax.broadcasted_iota(I32, (1, 2 * N), 1)
    lo = lane < N
    a_re = jnp.minimum(prm_ref[0, 0:1, :], -1e-4)
    a_im = prm_ref[0, 1:2, :]
    dt = jnp.exp(prm_ref[0, 2:3, :])
    zr = dt * a_re
    zi = dt * a_im

    quarter = jnp.where(lo, 0.0, 0.5 * math.pi)

    def powrows(e):
        return jnp.exp(e * zr) * jnp.cos(e * zi - quarter)

    def swap(tab):
        return pltpu.roll(tab, N, 1)

    def cmul(tab, c):
        return c[0] * tab + jnp.where(lo, -c[1], c[1]) * swap(tab)

    n_dbl = max(nc - 1, 0).bit_length()
    exps = [1] + [SUBLANES << k for k in range(int(math.log2(L // SUBLANES)))] + [L << k for k in range(n_dbl)]
    e_col = jnp.concatenate([jnp.full((1, 1), float(e), F32) for e in exps]
                            + [jnp.zeros((-len(exps) % SUBLANES, 1), F32)], axis=0)
    mag = jnp.exp(e_col * zr)
    pw_r = mag * jnp.cos(e_col * zi)
    pw_i = mag * jnp.sin(e_col * zi)
    apow = {e: (pw_r[k:k + 1], pw_i[k:k + 1]) for k, e in enumerate(exps)}

    def powtab(descending):
        i8 = lax.broadcasted_iota(I32, (SUBLANES, 1), 0).astype(F32)
        tab = powrows(SUBLANES - 1.0 - i8 if descending else i8)
        rows = SUBLANES
        while rows < L:
            more = cmul(tab, apow[rows])
            tab = jnp.concatenate([more, tab] if descending else [tab, more], axis=0)
            rows *= 2
        return tab

    er, ei = apow[1]
    den = a_re * a_re + a_im * a_im
    coef_r = ((er - 1.0) * a_re + ei * a_im) / den
    coef_i = (ei * a_re - (er - 1.0) * a_im) / den
    c_r = c_ref[0, 0]
    c_i = c_ref[0, 1]
    bb_r = coef_r * bt_ref[0, 0] - coef_i * bt_ref[0, 1]
    bb_i = coef_r * bt_ref[0, 1] + coef_i * bt_ref[0, 0]

    cb_rows = []
    for q in range(P):
        cbr = c_r * bb_r[q:q + 1] - c_i * bb_i[q:q + 1]
        cbi = c_r * bb_i[q:q + 1] + c_i * bb_r[q:q + 1]
        cb_rows.append(jnp.where(lo, cbr, -cbi))
    cb = jnp.concatenate(cb_rows, axis=0)
    pt0 = powtab(False)
    c1, c2, c3 = _split3(cb)
    t1, t2, t3 = _split3(pt0)
    kmat = (_dot_nt(c1, t1) + _dot_nt(c1, t2) + _dot_nt(c2, t1)
            + _dot_nt(c2, t2) + _dot_nt(c1, t3) + _dot_nt(c3, t1))

    pt_rev = powtab(True)
    pt_rev_sw = swap(pt_rev)
    f_rows = []
    for q in range(P):
        a1 = bb_r[q:q + 1]
        a2 = jnp.where(lo, -bb_i[q:q + 1], bb_i[q:q + 1])
        f_rows.append((a1 * pt_rev + a2 * pt_rev_sw).astype(BF16))
    fmat = jnp.concatenate(f_rows, axis=0)
    pt1 = cmul(pt0, apow[1])
    pt1_sw = swap(pt1)
    e_rows = []
    for p in range(P):
        b1 = jnp.where(lo, c_r[p:p + 1], -c_r[p:p + 1])
        b2 = -c_i[p:p + 1]
        e_rows.append((b1 * pt1 + b2 * pt1_sw).astype(BF16))
    emat_t = jnp.concatenate(e_rows, axis=0)

    lhs = jnp.concatenate(
        [jnp.concatenate([x_ref[b, q] for q in range(P)], axis=1) for b in range(bsz)], axis=0)
    s_end = _dot(lhs, fmat)

    rr = lax.broadcasted_iota(I32, (L, L), 0)
    cc = lax.broadcasted_iota(I32, (L, L), 1)
    causal = cc >= rr
    y = None
    for q0 in range(0, P, 2):
        rows = []
        for q in (q0, q0 + 1):
            tiles = []
            for p in range(P):
                kb = jnp.broadcast_to(kmat[q * P + p:q * P + p + 1, :], (L, L))
                toe = pltpu.roll(kb, 0, 1, stride=1, stride_axis=0)
                tiles.append(jnp.where(causal, toe, 0.0).astype(BF16))
            rows.append(jnp.concatenate(tiles, axis=1))
        part = _dot(lhs[:, q0 * L:(q0 + 2) * L], jnp.concatenate(rows, axis=0))
        y = part if y is None else y + part

    m_rows = bsz * nc
    ridx = lax.broadcasted_iota(I32, (m_rows, 1), 0)
    cidx = ridx % nc
    h = jnp.where(cidx >= 1, pltpu.roll(s_end, 1, 0), 0.0)
    d = 1
    while d < nc:
        sh = jnp.where(cidx >= d, pltpu.roll(h, d, 0), 0.0)
        h = h + cmul(sh, apow[d * L])
        d *= 2
    y = y + _dot_nt(h.astype(BF16), emat_t)

    for b in range(bsz):
        for p in range(P):
            yp = (y[b * nc:(b + 1) * nc, p * L:(p + 1) * L]
                  + d_ref[grp * P + p] * x_ref[b, p].astype(F32))
            out_ref[b, :, p, :] = jax.nn.gelu(yp)


def _s5(ut, log_dt, a_re, a_im, b_re, b_im, c_re, c_im, d_skip):
    bsz, s5w, seq = ut.shape
    L = S5_CHUNK
    nc = seq // L
    groups = s5w // S5_GROUP
    n = S5_STATE
    x4 = ut.reshape(bsz, s5w, nc, L)
    dup = lambda t: jnp.concatenate([t, t], axis=-1).astype(F32)
    prm = jnp.zeros((groups, SUBLANES, 2 * n), F32)
    prm = prm.at[:, 0].set(dup(a_re)).at[:, 1].set(dup(a_im))
    prm = prm.at[:, 2].set(jnp.broadcast_to(log_dt.astype(F32)[:, None], (groups, 2 * n)))
    cpar = jnp.stack([dup(c_re), dup(c_im)], axis=1)
    btpar = jnp.stack([dup(jnp.swapaxes(b_re, 1, 2)), dup(jnp.swapaxes(b_im, 1, 2))], axis=1)
    out = pl.pallas_call(
        functools.partial(_s5_kernel, bsz=bsz, nc=nc),
        grid_spec=pltpu.PrefetchScalarGridSpec(
            num_scalar_prefetch=1,
            grid=(groups,),
            in_specs=[
                pl.BlockSpec((bsz, S5_GROUP, nc, L), lambda g, d: (0, g, 0, 0)),
                pl.BlockSpec((1, SUBLANES, 2 * n), lambda g, d: (g, 0, 0)),
                pl.BlockSpec((1, 2, S5_GROUP, 2 * n), lambda g, d: (g, 0, 0, 0)),
                pl.BlockSpec((1, 2, S5_GROUP, 2 * n), lambda g, d: (g, 0, 0, 0)),
            ],
            out_specs=pl.BlockSpec((bsz, nc, S5_GROUP, L), lambda g, d: (0, 0, g, 0)),
        ),
        out_shape=jax.ShapeDtypeStruct((bsz, nc, s5w, L), F32),
        compiler_params=pltpu.CompilerParams(
            dimension_semantics=("parallel",), vmem_limit_bytes=VMEM_LIMIT),
        name="s5",
    )(d_skip.astype(F32), x4, prm, cpar, btpar)
    return out


def _log_sigmoid(x):
    return jnp.minimum(x, 0.0) - jnp.log(1.0 + jnp.exp(-jnp.abs(x)))


def _gates_kernel(gt_ref, out_ref, m_scr, *, bsz, chunk, n_sub):
    L = chunk
    H = ML_HEADS
    step = pl.program_id(0)

    @pl.when(step == 0)
    def _():
        m_scr[...] = jnp.zeros_like(m_scr)

    rr = lax.broadcasted_iota(I32, (L, L), 0)
    cc = lax.broadcasted_iota(I32, (L, L), 1)
    utri = (rr <= cc).astype(BF16)
    grow = lax.broadcasted_iota(I32, (16, 1), 0)
    lane_row = lax.broadcasted_iota(I32, (SUBLANES, L), 1)
    m_prev = [m_scr[b][0:H, 0:1] for b in range(bsz)]
    pairs = [(j, b) for j in range(n_sub) for b in range(bsz)]

    irow, brow, cm = {}, {}, {}
    for j, b in pairs:
        g = gt_ref[b, :, j * L:(j + 1) * L]
        g2 = jnp.where(grow >= H, _log_sigmoid(g), g)
        p1, p2, p3 = _split3(g2)
        brow[j, b] = (_dot(p1, utri) + _dot(p2, utri) + _dot(p3, utri))[H:2 * H]
        irow[j, b] = g[0:H]
    for j, b in pairs:
        ib = irow[j, b] - brow[j, b]
        x = jnp.concatenate([ib, ib], axis=0)
        sft = 1
        while sft < L:
            x = jnp.maximum(x, jnp.where(lane_row >= sft, pltpu.roll(x, sft, 1), -jnp.inf))
            sft *= 2
        cm[j, b] = x[0:H]
    for j, b in pairs:
        mp = m_prev[b]
        mm = jnp.maximum(cm[j, b], mp)
        b_last = brow[j, b][:, L - 1:L]
        m_next = b_last + jnp.maximum(mp, cm[j, b][:, L - 1:L])
        planes = (mm, jnp.exp(mp - mm), jnp.minimum(jnp.exp(-(brow[j, b] + mm)), EXP_CAP),
                  jnp.exp(b_last - brow[j, b] + irow[j, b] - m_next), irow[j, b] - brow[j, b],
                  jnp.broadcast_to(jnp.exp(b_last + mp - m_next), (H, L)))
        for k, rows in enumerate(planes):
            out_ref[k, b, :, j * L:(j + 1) * L] = jnp.concatenate([rows, rows], axis=0)
        m_prev[b] = m_next
    for b in range(bsz):
        m_scr[b] = jnp.broadcast_to(jnp.concatenate([m_prev[b], m_prev[b]], axis=0),
                                    (SUBLANES, LANES))


def _gates(gt, bsz, seq, chunk):
    n_sub = min(GATE_CHUNKS, seq // chunk)
    blk = n_sub * chunk
    return pl.pallas_call(
        functools.partial(_gates_kernel, bsz=bsz, chunk=chunk, n_sub=n_sub),
        grid=(seq // blk,),
        in_specs=[pl.BlockSpec((bsz, 16, blk), lambda c: (0, 0, c))],
        out_specs=pl.BlockSpec((6, bsz, SUBLANES, blk), lambda c: (0, 0, 0, c)),
        out_shape=jax.ShapeDtypeStruct((6, bsz, SUBLANES, seq), F32),
        scratch_shapes=[pltpu.VMEM((bsz, SUBLANES, LANES), F32)],
        compiler_params=pltpu.CompilerParams(
            dimension_semantics=("arbitrary",), vmem_limit_bytes=VMEM_LIMIT),
        name="gates",
    )(gt)


def _mlstm_kernel(qk_ref, v_ref, o_ref, gr_ref, cw_ref, cb_ref, ng_ref, y_ref,
                  tail_scr, c_scr, *, bsz, chunk, width):
    L = chunk
    H = ML_HEADS
    dh = width // H
    step = pl.program_id(0)
    assert L == LANES and dh == LANES, "column replication below uses one 128x128 tile per head"

    @pl.when(step == 0)
    def _():
        tail_scr[...] = jnp.zeros_like(tail_scr)
        c_scr[...] = jnp.zeros_like(c_scr)

    rr = lax.broadcasted_iota(I32, (L, L), 0)
    cc = lax.broadcasted_iota(I32, (L, L), 1)
    causal = cc <= rr
    eye = (rr == cc).astype(BF16)
    ones_blk = jnp.ones((L, dh), BF16)
    scale = 1.0 / math.sqrt(dh)
    inv_dh = 1.0 / dh
    n_str = bsz * H
    streams = [(b, hd) for b in range(bsz) for hd in range(H)]

    def replicate(rows, two_terms):
        p = jnp.concatenate([jnp.broadcast_to(rows[h:h + 1], (LANES, L)) for h in range(H)], axis=0)
        p1 = p.astype(BF16)
        if not two_terms:
            return _dot_nt(eye, p1)
        return _dot_nt(eye, p1) + _dot_nt(eye, (p - p1.astype(F32)).astype(BF16))

    def rowsum(x):
        hi = x.astype(BF16)
        lo = (x - hi.astype(F32)).astype(BF16)
        return _dot(hi, ones_blk) + _dot(lo, ones_blk)

    tails = [tail_scr[b] for b in range(bsz)]
    caugs = [c_scr[i] for i in range(n_str)]
    reps = [[replicate(gr_ref[k, b][0:H], k == 0) for b in range(bsz)] for k in range(4)]
    col = lambda k: [reps[k][b][:, hd * LANES:(hd + 1) * LANES] for b, hd in streams]
    mm_rep, w_inter, enm, ws_rep = col(0), col(1), col(2), col(3)
    ib_row = [gr_ref[4, b][hd:hd + 1] for b, hd in streams]
    decays = [gr_ref[5, b][hd:hd + 1, 0:1] for b, hd in streams]

    q_all, k_all, new_tails = [], [], []
    for b in range(bsz):
        xqk = qk_ref[b].astype(F32)
        ext = jnp.concatenate([tails[b], xqk], axis=0)
        acc = jnp.broadcast_to(cb_ref[...], xqk.shape)
        for j in range(CONV_WIDTH):
            back = CONV_WIDTH - 1 - j
            sh = ext if back == 0 else pltpu.roll(ext, back, 0)
            acc = acc + cw_ref[j:j + 1, :] * sh[SUBLANES:]
        new_tails.append(xqk[L - SUBLANES:])
        qkc = acc * _sigmoid(acc)
        q_all.append(qkc[:, :width].astype(BF16))
        k_all.append((qkc[:, width:] * scale).astype(BF16))

    q_h = [q_all[b][:, hd * dh:(hd + 1) * dh] for b, hd in streams]
    k_h = [k_all[b][:, hd * dh:(hd + 1) * dh] for b, hd in streams]
    vaug = [jnp.concatenate([v_ref[b][:, hd * dh:(hd + 1) * dh], ones_blk], axis=1)
            for b, hd in streams]
    scores = [_dot_nt(q_h[i], k_h[i]) for i in range(n_str)]
    inter = [_dot(q_h[i], caugs[i].astype(BF16)) for i in range(n_str)]
    s_bf = [(scores[i] * jnp.exp(jnp.where(causal, ib_row[i] - mm_rep[i], -jnp.inf))).astype(BF16)
            for i in range(n_str)]
    kw = [(k_h[i].astype(F32) * ws_rep[i]).astype(BF16) for i in range(n_str)]
    intra = [_dot(s_bf[i], vaug[i]) for i in range(n_str)]
    upd = [_dot_tn(kw[i], vaug[i]) for i in range(n_str)]

    hh = []
    for i in range(n_str):
        num = w_inter[i] * inter[i][:, :dh] + intra[i][:, :dh]
        den = w_inter[i] * inter[i][:, dh:] + intra[i][:, dh:]
        hh.append(num / jnp.maximum(jnp.abs(den), enm[i]))
    mu = [rowsum(h) * inv_dh for h in hh]
    ctr = [hh[i] - mu[i] for i in range(n_str)]
    var = [rowsum(c * c) * inv_dh for c in ctr]
    outs = []
    for i, (b, hd) in enumerate(streams):
        sl = slice(hd * dh, (hd + 1) * dh)
        hn = ctr[i] * lax.rsqrt(var[i] + LN_EPS) * ng_ref[:, sl]
        outs.append(hn * _sigmoid(o_ref[b][:, sl].astype(F32)))

    for b in range(bsz):
        tail_scr[b] = new_tails[b]
        y_ref[b] = jnp.concatenate(outs[b * H:(b + 1) * H], axis=1).astype(BF16)
    for i in range(n_str):
        c_scr[i] = decays[i] * caugs[i] + upd[i]


def _mlstm(qk, v, o, gt, conv_w, conv_b, norm_g, bsz, seq):
    width = v.shape[-1]
    L = min(ML_CHUNK, seq)
    dh = width // ML_HEADS
    qk3 = qk.reshape(bsz, seq, 2 * width)
    v3 = v.reshape(bsz, seq, width)
    o3 = o.reshape(bsz, seq, width)
    gate_rows = _gates(gt, bsz, seq, L)
    full = lambda c: (0, 0)
    return pl.pallas_call(
        functools.partial(_mlstm_kernel, bsz=bsz, chunk=L, width=width),
        grid=(seq // L,),
        in_specs=[
            pl.BlockSpec((bsz, L, 2 * width), lambda c: (0, c, 0)),
            pl.BlockSpec((bsz, L, width), lambda c: (0, c, 0)),
            pl.BlockSpec((bsz, L, width), lambda c: (0, c, 0)),
            pl.BlockSpec((6, bsz, SUBLANES, L), lambda c: (0, 0, 0, c)),
            pl.BlockSpec((CONV_WIDTH, 2 * width), full),
            pl.BlockSpec((1, 2 * width), full),
            pl.BlockSpec((1, width), full),
        ],
        out_specs=pl.BlockSpec((bsz, L, width), lambda c: (0, c, 0)),
        out_shape=jax.ShapeDtypeStruct((bsz, seq, width), BF16),
        scratch_shapes=[
            pltpu.VMEM((bsz, SUBLANES, 2 * width), F32),
            pltpu.VMEM((bsz * ML_HEADS, dh, 2 * dh), F32),
        ],
        compiler_params=pltpu.CompilerParams(
            dimension_semantics=("arbitrary",), vmem_limit_bytes=VMEM_LIMIT),
        name="mlstm",
    )(qk3, v3, o3, gate_rows, conv_w.astype(F32), conv_b.reshape(1, -1).astype(F32),
      norm_g.reshape(1, -1).astype(F32))


def _post_kernel(x_ref, gt_ref, yml_ref, wglut_ref, bglu_ref, wout_ref, nffn_ref, wrt_ref, br_ref,
                 h2_ref, xn_ref, eidx_ref, gate_ref, rank_ref, tcnt_ref, carry_scr):
    step = pl.program_id(0)

    @pl.when(step == 0)
    def _():
        carry_scr[...] = jnp.zeros_like(carry_scr)

    gt = jnp.concatenate([gt_ref[0, j] for j in range(gt_ref.shape[1])], axis=1).astype(BF16)
    s5w = gt.shape[0]
    zt = _dot(wglut_ref[...], gt) + bglu_ref[...]
    s5t = (gt.astype(F32) * _sigmoid(zt)).astype(BF16)
    h2 = (x_ref[...] + _dot_tn(s5t, wout_ref[:s5w, :])
          + _dot(yml_ref[...], wout_ref[s5w:, :]))
    h2_ref[...] = h2
    ms = jnp.mean(h2 * h2, axis=-1, keepdims=True)
    xn = h2 * lax.rsqrt(ms + RMS_EPS) * nffn_ref[...]
    xb = xn.astype(BF16)
    xn_ref[...] = xb

    tm = xn.shape[0]
    logits = _dot_nt(wrt_ref[...], xb) + br_ref[...]
    eio = lax.broadcasted_iota(I32, (N_EXPERTS, tm), 0).astype(F32)
    vals = logits
    onehot = jnp.zeros((N_EXPERTS, tm), F32)
    idxs, tops = [], []
    for _ in range(TOP_K):
        mx = jnp.max(vals, axis=0, keepdims=True)
        idx = jnp.min(jnp.where(vals == mx, eio, float(N_EXPERTS)), axis=0, keepdims=True)
        sel = eio == idx
        onehot = onehot + sel.astype(F32)
        vals = jnp.where(sel, -jnp.inf, vals)
        idxs.append(idx)
        tops.append(mx)
    exps = [jnp.exp(t - tops[0]) for t in tops]
    tot = exps[0] + exps[1] + exps[2] + exps[3]
    pad_f = jnp.zeros((SUBLANES - TOP_K, tm), F32)
    eidx_ref[...] = jnp.concatenate(idxs + [pad_f], axis=0).astype(I32)
    gate_ref[...] = jnp.concatenate([e / tot for e in exps] + [pad_f], axis=0)

    rr = lax.broadcasted_iota(I32, (tm, tm), 0)
    cc = lax.broadcasted_iota(I32, (tm, tm), 1)
    before = (rr < cc).astype(BF16)
    carry = carry_scr[:, 0:1]
    rank_ex = _dot(onehot.astype(BF16), before) + carry
    ranks = [jnp.sum(jnp.where(eio == i, rank_ex, 0.0), axis=0, keepdims=True) for i in idxs]
    rank_ref[...] = jnp.concatenate(ranks + [pad_f], axis=0).astype(I32)
    tile_cnt = jnp.sum(onehot, axis=1, keepdims=True)
    carry_scr[...] = jnp.broadcast_to(carry + tile_cnt, carry_scr.shape)
    for s in range(tm // DISPATCH_TILE):
        sub = onehot[:, s * DISPATCH_TILE:(s + 1) * DISPATCH_TILE]
        tcnt_ref[0, s] = jnp.broadcast_to(jnp.sum(sub, axis=1, keepdims=True), carry_scr.shape)


def _post(x2d, g_t, y_ml, w_glu, b_glu, w_out, norm_g, w_router, b_router):
    n_tok, d = x2d.shape
    bsz, nc, s5w, chunk = g_t.shape
    seq = nc * chunk
    mlw = y_ml.shape[-1]
    tm = min(PROJ_TILE, seq)
    tpb = seq // tm
    full = lambda i: (0, 0)
    row = lambda i: (i, 0)
    colb = lambda i: (0, i)
    return pl.pallas_call(
        _post_kernel,
        grid=(n_tok // tm,),
        in_specs=[
            pl.BlockSpec((tm, d), row),
            pl.BlockSpec((1, tm // chunk, s5w, chunk), lambda i: (i // tpb, i % tpb, 0, 0)),
            pl.BlockSpec((tm, mlw), row),
            pl.BlockSpec((s5w, s5w), full),
            pl.BlockSpec((s5w, 1), full),
            pl.BlockSpec((d, d), full),
            pl.BlockSpec((1, d), full),
            pl.BlockSpec((N_EXPERTS, d), full),
            pl.BlockSpec((N_EXPERTS, 1), full),
        ],
        out_specs=[
            pl.BlockSpec((tm, d), row),
            pl.BlockSpec((tm, d), row),
            pl.BlockSpec((SUBLANES, tm), colb),
            pl.BlockSpec((SUBLANES, tm), colb),
            pl.BlockSpec((SUBLANES, tm), colb),
            pl.BlockSpec((1, tm // DISPATCH_TILE, N_EXPERTS, LANES), lambda i: (i, 0, 0, 0)),
        ],
        out_shape=[
            jax.ShapeDtypeStruct((n_tok, d), F32),
            jax.ShapeDtypeStruct((n_tok, d), BF16),
            jax.ShapeDtypeStruct((SUBLANES, n_tok), I32),
            jax.ShapeDtypeStruct((SUBLANES, n_tok), F32),
            jax.ShapeDtypeStruct((SUBLANES, n_tok), I32),
            jax.ShapeDtypeStruct((n_tok // tm, tm // DISPATCH_TILE, N_EXPERTS, LANES), F32),
        ],
        scratch_shapes=[pltpu.VMEM((N_EXPERTS, LANES), F32)],
        compiler_params=pltpu.CompilerParams(
            dimension_semantics=("arbitrary",), vmem_limit_bytes=VMEM_LIMIT),
        name="post_router",
    )(x2d, g_t, y_ml, w_glu.T.astype(BF16), b_glu.reshape(-1, 1).astype(F32), w_out.astype(BF16),
      norm_g.reshape(1, -1).astype(F32), w_router.T.astype(BF16),
      b_router.reshape(-1, 1).astype(F32))


def _slab_loop(base, nbig_ref, nsmall_ref, fn, n_slabs=N_EXPERTS):
    def per_expert(e, carry):
        idx = base + e
        nbig = nbig_ref[idx]

        def big(j, c2):
            fn(idx, j * BIG_CHUNK, BIG_CHUNK)
            return c2

        def small(j, c2):
            fn(idx, nbig * BIG_CHUNK + j * SLAB_ALIGN, SLAB_ALIGN)
            return c2

        lax.fori_loop(0, nbig, big, 0)
        lax.fori_loop(0, nsmall_ref[idx], small, 0)
        return carry

    lax.fori_loop(0, n_slabs, per_expert, 0)


def _plan_copies(step, plan, fn):
    nbig_ref, nsmall_ref, _, bsrc_ref, bdst_ref, ssrc_ref, sdst_ref = plan

    def big(j, carry):
        k = step * MAX_BIG + j
        fn(bsrc_ref[k], bdst_ref[k], BIG_CHUNK)
        return carry

    def small(j, carry):
        k = step * MAX_SMALL + j
        fn(ssrc_ref[k], sdst_ref[k], SLAB_ALIGN)
        return carry

    lax.fori_loop(0, nbig_ref[step], big, 0)
    lax.fori_loop(0, nsmall_ref[step], small, 0)


def _plan_wait(step, plan, copy):
    copy(0, 0, DISPATCH_TILE * TOP_K).wait()

    def one(j, carry):
        copy(0, 0, SLAB_ALIGN).wait()
        return carry

    lax.fori_loop(0, plan[2][step], one, 0)


def _unpack_pairs(words):
    lo = lax.bitcast_convert_type(words << 16, F32).astype(BF16)
    hi = lax.bitcast_convert_type(words & jnp.uint32(0xFFFF0000), F32).astype(BF16)
    return lo, hi


def _pack_pairs(lo_f32, hi_f32):
    lo = lax.bitcast_convert_type(lo_f32, U32) >> 16
    hi = lax.bitcast_convert_type(hi_f32, U32) & jnp.uint32(0xFFFF0000)
    return hi | lo


def _dispatch_kernel(*refs, n_steps):
    plan = refs[:7]
    znbig_ref, znsmall_ref, zrow_ref, xn_ref, pos_ref, xs_hbm, stage, zbuf, sem, zsem = refs[7:]
    i = pl.program_id(0)
    slot = i % 2
    tw, d = xn_ref.shape
    n_rows = stage.shape[1]
    half = d // 2

    def copy(s, stage_row, buf_row, rows):
        src = stage.at[s, pl.ds(pl.multiple_of(stage_row, SLAB_ALIGN), rows), :]
        dst = xs_hbm.at[pl.ds(pl.multiple_of(buf_row, SLAB_ALIGN), rows), :]
        return pltpu.make_async_copy(src, dst, sem)

    @pl.when(i == 0)
    def _():
        zbuf[...] = jnp.zeros_like(zbuf)

        def tail(idx, r0, rows):
            dst = xs_hbm.at[pl.ds(pl.multiple_of(zrow_ref[idx] + r0, SLAB_ALIGN), rows), :]
            return pltpu.make_async_copy(zbuf.at[pl.ds(0, rows), :], dst, zsem)

        _slab_loop(0, znbig_ref, znsmall_ref, lambda *a: tail(*a).start(), N_EXPERTS + 1)
        _slab_loop(0, znbig_ref, znsmall_ref, lambda *a: tail(*a).wait(), N_EXPERTS + 1)

    xt = xn_ref[...]
    blk = STAGE_BLOCK
    for r0 in range(0, n_rows, blk):
        rid = lax.broadcasted_iota(I32, (blk, tw), 0) + r0
        hit = rid == pos_ref[0:1, :]
        for k in range(1, TOP_K):
            hit = jnp.logical_or(hit, rid == pos_ref[k:k + 1, :])
        srt = _dot(jnp.where(hit, 1.0, 0.0).astype(BF16), xt)
        stage[slot, r0:r0 + blk, :] = _pack_pairs(srt[:, :half], srt[:, half:])

    @pl.when(i > 0)
    def _():
        _plan_wait(i - 1, plan, functools.partial(copy, 1 - slot))

    _plan_copies(i, plan, lambda *a: copy(slot, *a).start())

    @pl.when(i == n_steps - 1)
    def _():
        _plan_wait(i, plan, functools.partial(copy, slot))


def _dispatch(xn, pos8, plan, tails, n_slots):
    n_tok, d = xn.shape
    tw = DISPATCH_TILE
    n_steps = n_tok // tw
    return pl.pallas_call(
        functools.partial(_dispatch_kernel, n_steps=n_steps),
        grid_spec=pltpu.PrefetchScalarGridSpec(
            num_scalar_prefetch=10,
            grid=(n_steps,),
            in_specs=[
                pl.BlockSpec((tw, d), lambda i, *_: (i, 0)),
                pl.BlockSpec((SUBLANES, tw), lambda i, *_: (0, i)),
            ],
            out_specs=pl.BlockSpec(memory_space=pl.ANY),
            scratch_shapes=[
                pltpu.VMEM((2, STAGE_ROWS, d // 2), U32),
                pltpu.VMEM((BIG_CHUNK, d // 2), U32),
                pltpu.SemaphoreType.DMA(()),
                pltpu.SemaphoreType.DMA(()),
            ],
        ),
        out_shape=jax.ShapeDtypeStruct((n_slots, d // 2), U32),
        compiler_params=pltpu.CompilerParams(
            dimension_semantics=("arbitrary",), vmem_limit_bytes=VMEM_LIMIT),
        name="dispatch",
    )(*plan, *tails, xn, pos8)


def _expert_kernel(te_ref, nu_ref, first_ref, nxt_ref, par_ref, x_ref, wup_hbm, bup_ref, wdn_hbm,
                   bdn_ref, y_ref, wup_f32, wdn_f32, wup_bf, wdn_bf, sem, *, d_ff):
    i = pl.program_id(0)
    n_used = nu_ref[0]

    def fetch(e, s):
        return (pltpu.make_async_copy(wup_hbm.at[e], wup_f32.at[s], sem.at[s, 0]),
                pltpu.make_async_copy(wdn_hbm.at[e], wdn_f32.at[s], sem.at[s, 1]))

    @pl.when(jnp.logical_and(i < n_used, first_ref[i] == 1))
    def _():
        s = par_ref[i]

        @pl.when(i == 0)
        def _():
            for c in fetch(te_ref[i], s):
                c.start()

        for c in fetch(te_ref[i], s):
            c.wait()
        wup_bf[...] = wup_f32[s].astype(BF16)
        wdn_bf[...] = wdn_f32[s].astype(BF16)

        @pl.when(nxt_ref[i] >= 0)
        def _():
            for c in fetch(nxt_ref[i], 1 - s):
                c.start()

    @pl.when(i < n_used)
    def _():
        lo, hi = _unpack_pairs(x_ref[...])
        half = lo.shape[1]
        h = _dot(lo, wup_bf[:half, :]) + _dot(hi, wup_bf[half:, :]) + bup_ref[0]
        gl = jnp.minimum(h[:, :d_ff], SWIGLU_LIMIT)
        lin = jnp.clip(h[:, d_ff:], -SWIGLU_LIMIT, SWIGLU_LIMIT)
        act = gl * _sigmoid(SWIGLU_ALPHA * gl) * (lin + 1.0)
        y = _dot(act.astype(BF16), wdn_bf[...]) + bdn_ref[0]
        yb = y.astype(BF16).astype(F32)
        y_ref[...] = _pack_pairs(yb[:, :half], yb[:, half:])

    @pl.when(i >= n_used)
    def _():
        y_ref[...] = jnp.zeros_like(y_ref)


def _experts(xs, tile_e, n_used, first, nxt, parity, w_up, b_up, w_down, b_down):
    n_slots, half = xs.shape
    d = 2 * half
    tm = EXPERT_TILE
    n_tiles = n_slots // tm
    d_ff = w_down.shape[1]
    return pl.pallas_call(
        functools.partial(_expert_kernel, d_ff=d_ff),
        grid_spec=pltpu.PrefetchScalarGridSpec(
            num_scalar_prefetch=5,
            grid=(n_tiles,),
            in_specs=[
                pl.BlockSpec((tm, half), lambda i, te, nu, *_: (jnp.minimum(i, nu[0] - 1), 0)),
                pl.BlockSpec(memory_space=pl.ANY),
                pl.BlockSpec((1, 1, 2 * d_ff), lambda i, te, *_: (te[i], 0, 0)),
                pl.BlockSpec(memory_space=pl.ANY),
                pl.BlockSpec((1, 1, d), lambda i, te, *_: (te[i], 0, 0)),
            ],
            out_specs=pl.BlockSpec((tm, half), lambda i, *_: (i, 0)),
            scratch_shapes=[
                pltpu.VMEM((2, d, 2 * d_ff), F32),
                pltpu.VMEM((2, d_ff, d), F32),
                pltpu.VMEM((d, 2 * d_ff), BF16),
                pltpu.VMEM((d_ff, d), BF16),
                pltpu.SemaphoreType.DMA((2, 2)),
            ],
        ),
        out_shape=jax.ShapeDtypeStruct((n_slots, half), U32),
        compiler_params=pltpu.CompilerParams(
            dimension_semantics=("arbitrary",), vmem_limit_bytes=VMEM_LIMIT),
        name="experts",
    )(tile_e, n_used, first, nxt, parity, xs, w_up, b_up.reshape(N_EXPERTS, 1, -1),
      w_down, b_down.reshape(N_EXPERTS, 1, -1))


def _combine_kernel(*refs, n_steps):
    plan = refs[:7]
    ys_hbm, h2_ref, posc_ref, gatec_ref, ng_ref, out_ref, stage, sem = refs[7:]
    i = pl.program_id(0)
    slot = i % 2
    tw, d = h2_ref.shape
    n_rows = stage.shape[1]

    def copy(s, stage_row, buf_row, rows):
        src = ys_hbm.at[pl.ds(pl.multiple_of(buf_row, SLAB_ALIGN), rows), :]
        dst = stage.at[s, pl.ds(pl.multiple_of(stage_row, SLAB_ALIGN), rows), :]
        return pltpu.make_async_copy(src, dst, sem.at[s])

    @pl.when(i == 0)
    def _():
        stage[...] = jnp.zeros_like(stage)
        _plan_copies(0, plan, lambda *a: copy(0, *a).start())

    @pl.when(i + 1 < n_steps)
    def _():
        _plan_copies(i + 1, plan, lambda *a: copy(1 - slot, *a).start())

    _plan_wait(i, plan, functools.partial(copy, slot))

    y_lo, y_hi = _unpack_pairs(stage[slot])
    blk = 128
    for t0 in range(0, tw, blk):
        cid = lax.broadcasted_iota(I32, (blk, n_rows), 1)
        pg = jnp.zeros((blk, n_rows), F32)
        for k in range(TOP_K):
            pg = jnp.where(cid == posc_ref[t0:t0 + blk, k:k + 1],
                           gatec_ref[t0:t0 + blk, k:k + 1], pg)
        pgb = pg.astype(BF16)
        moe = jnp.concatenate([_dot(pgb, y_lo), _dot(pgb, y_hi)], axis=1)
        acc = h2_ref[t0:t0 + blk, :] + moe
        ms = jnp.mean(acc * acc, axis=-1, keepdims=True)
        out_ref[t0:t0 + blk, :] = acc * lax.rsqrt(ms + RMS_EPS) * ng_ref[...]


def _combine(ys, h2, pos_cols, gate_cols, plan, norm_g):
    n_tok, d = h2.shape
    tw = DISPATCH_TILE
    n_steps = n_tok // tw
    return pl.pallas_call(
        functools.partial(_combine_kernel, n_steps=n_steps),
        grid_spec=pltpu.PrefetchScalarGridSpec(
            num_scalar_prefetch=7,
            grid=(n_steps,),
            in_specs=[
                pl.BlockSpec(memory_space=pl.ANY),
                pl.BlockSpec((tw, d), lambda i, *_: (i, 0)),
                pl.BlockSpec((tw, SUBLANES), lambda i, *_: (i, 0)),
                pl.BlockSpec((tw, SUBLANES), lambda i, *_: (i, 0)),
                pl.BlockSpec((1, d), lambda i, *_: (0, 0)),
            ],
            out_specs=pl.BlockSpec((tw, d), lambda i, *_: (i, 0)),
            scratch_shapes=[
                pltpu.VMEM((2, STAGE_ROWS, d // 2), U32),
                pltpu.SemaphoreType.DMA((2,)),
            ],
        ),
        out_shape=jax.ShapeDtypeStruct((n_tok, d), F32),
        compiler_params=pltpu.CompilerParams(
            dimension_semantics=("arbitrary",), vmem_limit_bytes=VMEM_LIMIT),
        name="combine",
    )(*plan, ys, h2, pos_cols, gate_cols, norm_g.reshape(1, -1).astype(F32))


def _moe(h2, xn, eidx, gate, rank, tcnt, w_up, b_up, w_down, b_down, norm_final_g):
    n_tok, d = h2.shape
    tm = EXPERT_TILE
    tw = DISPATCH_TILE
    n_steps = n_tok // tw
    n_exp = N_EXPERTS
    e_ids = jnp.arange(n_exp, dtype=I32)
    al = SLAB_ALIGN
    tile_cnt = tcnt.reshape(n_steps, n_exp, LANES)[:, :, 0].astype(I32)
    slab_rows = (tile_cnt + al - 1) // al * al
    exp_rows = jnp.sum(slab_rows, axis=0)
    region = (exp_rows + tm - 1) // tm * tm
    pad_end = jnp.cumsum(region)
    pad_start = pad_end - region
    n_slots = (n_tok * TOP_K + n_steps * n_exp * (al - 1) + n_exp * (tm - 1) + tm - 1) // tm * tm
    n_tiles = n_slots // tm

    tile_carry = jnp.cumsum(tile_cnt, axis=0) - tile_cnt
    tile_off = jnp.cumsum(slab_rows, axis=1) - slab_rows
    slab_row0 = pad_start[None, :] + jnp.cumsum(slab_rows, axis=0) - slab_rows

    def chunks(rows):
        return rows // BIG_CHUNK, rows % BIG_CHUNK // al

    def flat(cnt, src0, dst0, size, n_max):
        run = jnp.cumsum(cnt, axis=1)
        j = jnp.arange(n_max, dtype=I32)
        e_of = jnp.sum(run[:, None, :] <= j[None, :, None], axis=-1)
        sel = e_of[:, :, None] == e_ids[None, None, :]
        pick = lambda a: jnp.sum(jnp.where(sel, a[:, None, :], 0), axis=-1)
        local = (j[None, :] - pick(run - cnt)) * size
        return run[:, -1], (pick(src0) + local).reshape(-1), (pick(dst0) + local).reshape(-1)

    nbig, nsmall = chunks(slab_rows)
    big_tot, big_src, big_dst = flat(nbig, tile_off, slab_row0, BIG_CHUNK, MAX_BIG)
    small_tot, small_src, small_dst = flat(nsmall, tile_off + nbig * BIG_CHUNK,
                                           slab_row0 + nbig * BIG_CHUNK, al, MAX_SMALL)
    pad_groups = (jnp.sum(slab_rows, axis=1) - tw * TOP_K) // al
    plan = (big_tot, small_tot, pad_groups, big_src, big_dst, small_src, small_dst)
    tail_rows = jnp.concatenate([region - exp_rows, n_slots - pad_end[-1:]])
    tails = (*chunks(tail_rows), jnp.concatenate([pad_start + exp_rows, pad_end[-1:]]))

    base_t = jnp.repeat((tile_off - tile_carry).T, tw, axis=1)
    e_sel = eidx[:TOP_K][None] == e_ids[:, None, None]
    pos = jnp.sum(jnp.where(e_sel, base_t[:, None, :], 0), axis=0) + rank[:TOP_K]
    pos8 = jnp.concatenate([pos, jnp.full((SUBLANES - TOP_K, n_tok), -1, I32)], axis=0)

    tile_start = jnp.arange(n_tiles, dtype=I32) * tm
    tile_e = jnp.minimum(jnp.sum(pad_end[None, :] <= tile_start[:, None], axis=-1),
                         n_exp - 1).astype(I32)
    n_used = (pad_end[-1] // tm).astype(I32).reshape(1)
    t_ids = jnp.arange(n_tiles, dtype=I32)
    first = jnp.logical_and(jnp.logical_or(t_ids == 0, tile_e != jnp.roll(tile_e, 1)),
                            t_ids < n_used[0])
    later = jnp.logical_and(first[None, :], t_ids[None, :] > t_ids[:, None])
    nxt_tile = jnp.min(jnp.where(later, t_ids[None, :], n_tiles), axis=1)
    nxt_e = jnp.sum(jnp.where(t_ids[None, :] == nxt_tile[:, None], tile_e[None, :], 0), axis=1)
    nxt = jnp.where(nxt_tile < n_tiles, nxt_e, -1).astype(I32)
    parity = ((jnp.cumsum(first.astype(I32)) - 1) % 2).astype(I32)
    first = first.astype(I32)

    xs = _dispatch(xn, pos8, plan, tails, n_slots)
    ys = _experts(xs, tile_e, n_used, first, nxt, parity, w_up, b_up, w_down, b_down)
    return _combine(ys, h2, pos8.T, gate.T, plan, norm_final_g)


def kernel(x, norm_mix_g, w_in, s5_log_dt, s5_a_re, s5_a_im, s5_b_re, s5_b_im, s5_c_re, s5_c_im,
           s5_d, s5_w_glu, s5_b_glu, ml_conv_w, ml_conv_b, ml_b_gates, ml_norm_g, w_out,
           norm_ffn_g, w_router, b_router, w_up, b_up, w_down, b_down, norm_final_g):
    bsz, seq, d = x.shape
    depth = w_in.shape[0]
    assert depth == 1, "single-layer block"
    l = 0
    x2d = x.reshape(bsz * seq, d)
    qk, v, o, ut, gt = _in_proj(x2d, norm_mix_g[l], w_in[l], ml_b_gates[l], bsz, seq)
    g_t = _s5(ut, s5_log_dt[l], s5_a_re[l], s5_a_im[l], s5_b_re[l], s5_b_im[l],
              s5_c_re[l], s5_c_im[l], s5_d[l])
    y_ml = _mlstm(qk, v, o, gt, ml_conv_w[l], ml_conv_b[l], ml_norm_g[l], bsz, seq)
    y_ml = y_ml.reshape(bsz * seq, -1)
    h2, xn, eidx, gate, rank, tcnt = _post(x2d, g_t, y_ml, s5_w_glu[l], s5_b_glu[l], w_out[l],
                                           norm_ffn_g[l], w_router[l], b_router[l])
    out = _moe(h2, xn, eidx, gate, rank, tcnt, w_up[l], b_up[l], w_down[l], b_down[l],
               norm_final_g)
    return out.reshape(bsz, seq, d)
```

```python
import functools
import math

import jax
import jax.numpy as jnp
from jax import lax
from jax.experimental import pallas as pl
from jax.experimental.pallas import tpu as pltpu

F32 = jnp.float32
BF16 = jnp.bfloat16
I32 = jnp.int32
U32 = jnp.uint32

S5_GROUP = 16
S5_STATE = 64
ML_HEADS = 4
CONV_WIDTH = 4
N_EXPERTS = 32
TOP_K = 4
SWIGLU_LIMIT = 7.0
SWIGLU_ALPHA = 1.702
RMS_EPS = 1e-5
LN_EPS = 1e-6

LANES = 128
SUBLANES = 8
S5_CHUNK = LANES
ML_CHUNK = 128
GATE_CHUNKS = 8
EXP_CAP = 1e38
PROJ_TILE = 1024
EXPERT_TILE = 1024
DISPATCH_TILE = 256
SLAB_ALIGN = SUBLANES
BIG_CHUNK = 32
STAGE_ROWS = DISPATCH_TILE * TOP_K + N_EXPERTS * SLAB_ALIGN
STAGE_BLOCK = STAGE_ROWS // 2
MAX_BIG = STAGE_ROWS // BIG_CHUNK
MAX_SMALL = N_EXPERTS * (BIG_CHUNK // SLAB_ALIGN - 1)
VMEM_LIMIT = 56 * 1024 * 1024

_NT = (((1,), (1,)), ((), ()))
_TN = (((0,), (0,)), ((), ()))


def _dot(a, b):
    return jnp.dot(a, b, preferred_element_type=F32)


def _dot_nt(a, b):
    return lax.dot_general(a, b, _NT, preferred_element_type=F32)


def _dot_tn(a, b):
    return lax.dot_general(a, b, _TN, preferred_element_type=F32)


def _split3(x):
    p1 = x.astype(BF16)
    r1 = x - p1.astype(F32)
    p2 = r1.astype(BF16)
    r2 = r1 - p2.astype(F32)
    return p1, p2, r2.astype(BF16)


def _sigmoid(x):
    return 0.5 * jnp.tanh(0.5 * x) + 0.5


def _inproj_kernel(x_ref, g_ref, wnat_ref, wut_ref, wgt_ref, bg_ref,
                   qk_ref, v_ref, o_ref, ut_ref, gt_ref, *, width):
    x = x_ref[...]
    ms = jnp.mean(x * x, axis=-1, keepdims=True)
    hn = (x * lax.rsqrt(ms + RMS_EPS) * g_ref[...]).astype(BF16)
    nat = _dot(hn, wnat_ref[...])
    qk_ref[...] = nat[:, :2 * width].astype(BF16)
    v_ref[...] = nat[:, 2 * width:3 * width].astype(BF16)
    o_ref[...] = nat[:, 3 * width:].astype(BF16)
    ut_ref[0] = _dot_nt(wut_ref[...], hn).astype(BF16)
    gt_ref[0] = _dot_nt(wgt_ref[...], hn) + bg_ref[...]


def _in_proj(x2d, norm_g, w_in, b_gates, bsz, seq):
    n_tok, d = x2d.shape
    s5w = d // 2
    mlw = d - s5w
    tm = min(PROJ_TILE, seq)
    tpb = seq // tm
    w_bf = w_in.astype(BF16)
    w_nat = w_bf[:, s5w:s5w + 4 * mlw]
    w_ut = w_bf[:, :s5w].T
    n_gate = 2 * ML_HEADS
    w_gt = jnp.zeros((16, d), BF16).at[:n_gate].set(w_bf[:, s5w + 4 * mlw:].T)
    b_g = jnp.zeros((16, 1), F32).at[:n_gate, 0].set(b_gates.astype(F32))
    grid = (n_tok // tm,)
    full = lambda i: (0, 0)
    return pl.pallas_call(
        functools.partial(_inproj_kernel, width=mlw),
        grid=grid,
        in_specs=[
            pl.BlockSpec((tm, d), lambda i: (i, 0)),
            pl.BlockSpec((1, d), full),
            pl.BlockSpec((d, 4 * mlw), full),
            pl.BlockSpec((s5w, d), full),
            pl.BlockSpec((16, d), full),
            pl.BlockSpec((16, 1), full),
        ],
        out_specs=[
            pl.BlockSpec((tm, 2 * mlw), lambda i: (i, 0)),
            pl.BlockSpec((tm, mlw), lambda i: (i, 0)),
            pl.BlockSpec((tm, mlw), lambda i: (i, 0)),
            pl.BlockSpec((1, s5w, tm), lambda i: (i // tpb, 0, i % tpb)),
            pl.BlockSpec((1, 16, tm), lambda i: (i // tpb, 0, i % tpb)),
        ],
        out_shape=[
            jax.ShapeDtypeStruct((n_tok, 2 * mlw), BF16),
            jax.ShapeDtypeStruct((n_tok, mlw), BF16),
            jax.ShapeDtypeStruct((n_tok, mlw), BF16),
            jax.ShapeDtypeStruct((bsz, s5w, seq), BF16),
            jax.ShapeDtypeStruct((bsz, 16, seq), F32),
        ],
        compiler_params=pltpu.CompilerParams(
            dimension_semantics=("parallel",), vmem_limit_bytes=VMEM_LIMIT),
        name="in_proj",
    )(x2d, norm_g.reshape(1, d).astype(F32), w_nat, w_ut, w_gt, b_g)


def _s5_kernel(d_ref, x_ref, prm_ref, c_ref, bt_ref, out_ref, *, bsz, nc):
    L = S5_CHUNK
    P = S5_GROUP
    N = S5_STATE
    grp = pl.program_id(0)
    lane = lax.broadcasted_iota(I32, (1, 2 * N), 1)
    lo = lane < N
    a_re = jnp.minimum(prm_ref[0, 0:1, :], -1e-4)
    a_im = prm_ref[0, 1:2, :]
    dt = jnp.exp(prm_ref[0, 2:3, :])
    zr = dt * a_re
    zi = dt * a_im

    quarter = jnp.where(lo, 0.0, 0.5 * math.pi)

    def powrows(e):
        return jnp.exp(e * zr) * jnp.cos(e * zi - quarter)

    def swap(tab):
        return pltpu.roll(tab, N, 1)

    def cmul(tab, c):
        return c[0] * tab + jnp.where(lo, -c[1], c[1]) * swap(tab)

    n_dbl = max(nc - 1, 0).bit_length()
    exps = [1] + [SUBLANES << k for k in range(int(math.log2(L // SUBLANES)))] + [L << k for k in range(n_dbl)]
    e_col = jnp.concatenate([jnp.full((1, 1), float(e), F32) for e in exps]
                            + [jnp.zeros((-len(exps) % SUBLANES, 1), F32)], axis=0)
    mag = jnp.exp(e_col * zr)
    pw_r = mag * jnp.cos(e_col * zi)
    pw_i = mag * jnp.sin(e_col * zi)
    apow = {e: (pw_r[k:k + 1], pw_i[k:k + 1]) for k, e in enumerate(exps)}

    def powtab(descending):
        i8 = lax.broadcasted_iota(I32, (SUBLANES, 1), 0).astype(F32)
        tab = powrows(SUBLANES - 1.0 - i8 if descending else i8)
        rows = SUBLANES
        while rows < L:
            more = cmul(tab, apow[rows])
            tab = jnp.concatenate([more, tab] if descending else [tab, more], axis=0)
            rows *= 2
        return tab

    er, ei = apow[1]
    den = a_re * a_re + a_im * a_im
    coef_r = ((er - 1.0) * a_re + ei * a_im) / den
    coef_i = (ei * a_re - (er - 1.0) * a_im) / den
    c_r = c_ref[0, 0]
    c_i = c_ref[0, 1]
    bb_r = coef_r * bt_ref[0, 0] - coef_i * bt_ref[0, 1]
    bb_i = coef_r * bt_ref[0, 1] + coef_i * bt_ref[0, 0]

    cb_rows = []
    for q in range(P):
        cbr = c_r * bb_r[q:q + 1] - c_i * bb_i[q:q + 1]
        cbi = c_r * bb_i[q:q + 1] + c_i * bb_r[q:q + 1]
        cb_rows.append(jnp.where(lo, cbr, -cbi))
    cb = jnp.concatenate(cb_rows, axis=0)
    pt0 = powtab(False)
    c1, c2, c3 = _split3(cb)
    t1, t2, t3 = _split3(pt0)
    kmat = (_dot_nt(c1, t1) + _dot_nt(c1, t2) + _dot_nt(c2, t1)
            + _dot_nt(c2, t2) + _dot_nt(c1, t3) + _dot_nt(c3, t1))

    pt_rev = powtab(True)
    pt_rev_sw = swap(pt_rev)
    f_rows = []
    for q in range(P):
        a1 = bb_r[q:q + 1]
        a2 = jnp.where(lo, -bb_i[q:q + 1], bb_i[q:q + 1])
        f_rows.append((a1 * pt_rev + a2 * pt_rev_sw).astype(BF16))
    fmat = jnp.concatenate(f_rows, axis=0)
    pt1 = cmul(pt0, apow[1])
    pt1_sw = swap(pt1)
    e_rows = []
    for p in range(P):
        b1 = jnp.where(lo, c_r[p:p + 1], -c_r[p:p + 1])
        b2 = -c_i[p:p + 1]
        e_rows.append((b1 * pt1 + b2 * pt1_sw).astype(BF16))
    emat_t = jnp.concatenate(e_rows, axis=0)

    lhs = jnp.concatenate(
        [jnp.concatenate([x_ref[b, q] for q in range(P)], axis=1) for b in range(bsz)], axis=0)
    s_end = _dot(lhs, fmat)

    rr = lax.broadcasted_iota(I32, (L, L), 0)
    cc = lax.broadcasted_iota(I32, (L, L), 1)
    causal = cc >= rr
    y = None
    for q0 in range(0, P, 2):
        rows = []
        for q in (q0, q0 + 1):
            tiles = []
            for p in range(P):
                kb = jnp.broadcast_to(kmat[q * P + p:q * P + p + 1, :], (L, L))
                toe = pltpu.roll(kb, 0, 1, stride=1, stride_axis=0)
                tiles.append(jnp.where(causal, toe, 0.0).astype(BF16))
            rows.append(jnp.concatenate(tiles, axis=1))
        part = _dot(lhs[:, q0 * L:(q0 + 2) * L], jnp.concatenate(rows, axis=0))
        y = part if y is None else y + part

    m_rows = bsz * nc
    ridx = lax.broadcasted_iota(I32, (m_rows, 1), 0)
    cidx = ridx % nc
    h = jnp.where(cidx >= 1, pltpu.roll(s_end, 1, 0), 0.0)
    d = 1
    while d < nc:
        sh = jnp.where(cidx >= d, pltpu.roll(h, d, 0), 0.0)
        h = h + cmul(sh, apow[d * L])
        d *= 2
    y = y + _dot_nt(h.astype(BF16), emat_t)

    for b in range(bsz):
        for p in range(P):
            yp = (y[b * nc:(b + 1) * nc, p * L:(p + 1) * L]
                  + d_ref[grp * P + p] * x_ref[b, p].astype(F32))
            out_ref[b, :, p, :] = jax.nn.gelu(yp)


def _s5(ut, log_dt, a_re, a_im, b_re, b_im, c_re, c_im, d_skip):
    bsz, s5w, seq = ut.shape
    L = S5_CHUNK
    nc = seq // L
    groups = s5w // S5_GROUP
    n = S5_STATE
    x4 = ut.reshape(bsz, s5w, nc, L)
    dup = lambda t: jnp.concatenate([t, t], axis=-1).astype(F32)
    prm = jnp.zeros((groups, SUBLANES, 2 * n), F32)
    prm = prm.at[:, 0].set(dup(a_re)).at[:, 1].set(dup(a_im))
    prm = prm.at[:, 2].set(jnp.broadcast_to(log_dt.astype(F32)[:, None], (groups, 2 * n)))
    cpar = jnp.stack([dup(c_re), dup(c_im)], axis=1)
    btpar = jnp.stack([dup(jnp.swapaxes(b_re, 1, 2)), dup(jnp.swapaxes(b_im, 1, 2))], axis=1)
    out = pl.pallas_call(
        functools.partial(_s5_kernel, bsz=bsz, nc=nc),
        grid_spec=pltpu.PrefetchScalarGridSpec(
            num_scalar_prefetch=1,
            grid=(groups,),
            in_specs=[
                pl.BlockSpec((bsz, S5_GROUP, nc, L), lambda g, d: (0, g, 0, 0)),
                pl.BlockSpec((1, SUBLANES, 2 * n), lambda g, d: (g, 0, 0)),
                pl.BlockSpec((1, 2, S5_GROUP, 2 * n), lambda g, d: (g, 0, 0, 0)),
                pl.BlockSpec((1, 2, S5_GROUP, 2 * n), lambda g, d: (g, 0, 0, 0)),
            ],
            out_specs=pl.BlockSpec((bsz, nc, S5_GROUP, L), lambda g, d: (0, 0, g, 0)),
        ),
        out_shape=jax.ShapeDtypeStruct((bsz, nc, s5w, L), F32),
        compiler_params=pltpu.CompilerParams(
            dimension_semantics=("parallel",), vmem_limit_bytes=VMEM_LIMIT),
        name="s5",
    )(d_skip.astype(F32), x4, prm, cpar, btpar)
    return out


def _log_sigmoid(x):
    return jnp.minimum(x, 0.0) - jnp.log(1.0 + jnp.exp(-jnp.abs(x)))


def _gates_kernel(gt_ref, out_ref, m_scr, *, bsz, chunk, n_sub):
    L = chunk
    H = ML_HEADS
    step = pl.program_id(0)

    @pl.when(step == 0)
    def _():
        m_scr[...] = jnp.zeros_like(m_scr)

    rr = lax.broadcasted_iota(I32, (L, L), 0)
    cc = lax.broadcasted_iota(I32, (L, L), 1)
    utri = (rr <= cc).astype(BF16)
    grow = lax.broadcasted_iota(I32, (16, 1), 0)
    lane_row = lax.broadcasted_iota(I32, (SUBLANES, L), 1)
    m_prev = [m_scr[b][0:H, 0:1] for b in range(bsz)]
    pairs = [(j, b) for j in range(n_sub) for b in range(bsz)]

    irow, brow, cm = {}, {}, {}
    for j, b in pairs:
        g = gt_ref[b, :, j * L:(j + 1) * L]
        g2 = jnp.where(grow >= H, _log_sigmoid(g), g)
        p1, p2, p3 = _split3(g2)
        brow[j, b] = (_dot(p1, utri) + _dot(p2, utri) + _dot(p3, utri))[H:2 * H]
        irow[j, b] = g[0:H]
    for j, b in pairs:
        ib = irow[j, b] - brow[j, b]
        x = jnp.concatenate([ib, ib], axis=0)
        sft = 1
        while sft < L:
            x = jnp.maximum(x, jnp.where(lane_row >= sft, pltpu.roll(x, sft, 1), -jnp.inf))
            sft *= 2
        cm[j, b] = x[0:H]
    for j, b in pairs:
        mp = m_prev[b]
        mm = jnp.maximum(cm[j, b], mp)
        b_last = brow[j, b][:, L - 1:L]
        m_next = b_last + jnp.maximum(mp, cm[j, b][:, L - 1:L])
        planes = (mm, jnp.exp(mp - mm), jnp.minimum(jnp.exp(-(brow[j, b] + mm)), EXP_CAP),
                  jnp.exp(b_last - brow[j, b] + irow[j, b] - m_next), irow[j, b] - brow[j, b],
                  jnp.broadcast_to(jnp.exp(b_last + mp - m_next), (H, L)))
        for k, rows in enumerate(planes):
            out_ref[k, b, :, j * L:(j + 1) * L] = jnp.concatenate([rows, rows], axis=0)
        m_prev[b] = m_next
    for b in range(bsz):
        m_scr[b] = jnp.broadcast_to(jnp.concatenate([m_prev[b], m_prev[b]], axis=0),
                                    (SUBLANES, LANES))


def _gates(gt, bsz, seq, chunk):
    n_sub = min(GATE_CHUNKS, seq // chunk)
    blk = n_sub * chunk
    return pl.pallas_call(
        functools.partial(_gates_kernel, bsz=bsz, chunk=chunk, n_sub=n_sub),
        grid=(seq // blk,),
        in_specs=[pl.BlockSpec((bsz, 16, blk), lambda c: (0, 0, c))],
        out_specs=pl.BlockSpec((6, bsz, SUBLANES, blk), lambda c: (0, 0, 0, c)),
        out_shape=jax.ShapeDtypeStruct((6, bsz, SUBLANES, seq), F32),
        scratch_shapes=[pltpu.VMEM((bsz, SUBLANES, LANES), F32)],
        compiler_params=pltpu.CompilerParams(
            dimension_semantics=("arbitrary",), vmem_limit_bytes=VMEM_LIMIT),
        name="gates",
    )(gt)


def _mlstm_kernel(qk_ref, v_ref, o_ref, gr_ref, cw_ref, cb_ref, ng_ref, y_ref,
                  tail_scr, c_scr, *, bsz, chunk, width):
    L = chunk
    H = ML_HEADS
    dh = width // H
    step = pl.program_id(0)
    assert L == LANES and dh == LANES, "column replication below uses one 128x128 tile per head"

    @pl.when(step == 0)
    def _():
        tail_scr[...] = jnp.zeros_like(tail_scr)
        c_scr[...] = jnp.zeros_like(c_scr)

    rr = lax.broadcasted_iota(I32, (L, L), 0)
    cc = lax.broadcasted_iota(I32, (L, L), 1)
    causal = cc <= rr
    eye = (rr == cc).astype(BF16)
    ones_blk = jnp.ones((L, dh), BF16)
    scale = 1.0 / math.sqrt(dh)
    inv_dh = 1.0 / dh
    n_str = bsz * H
    streams = [(b, hd) for b in range(bsz) for hd in range(H)]

    def replicate(rows, two_terms):
        p = jnp.concatenate([jnp.broadcast_to(rows[h:h + 1], (LANES, L)) for h in range(H)], axis=0)
        p1 = p.astype(BF16)
        if not two_terms:
            return _dot_nt(eye, p1)
        return _dot_nt(eye, p1) + _dot_nt(eye, (p - p1.astype(F32)).astype(BF16))

    def rowsum(x):
        hi = x.astype(BF16)
        lo = (x - hi.astype(F32)).astype(BF16)
        return _dot(hi, ones_blk) + _dot(lo, ones_blk)

    tails = [tail_scr[b] for b in range(bsz)]
    caugs = [c_scr[i] for i in range(n_str)]
    reps = [[replicate(gr_ref[k, b][0:H], k == 0) for b in range(bsz)] for k in range(4)]
    col = lambda k: [reps[k][b][:, hd * LANES:(hd + 1) * LANES] for b, hd in streams]
    mm_rep, w_inter, enm, ws_rep = col(0), col(1), col(2), col(3)
    ib_row = [gr_ref[4, b][hd:hd + 1] for b, hd in streams]
    decays = [gr_ref[5, b][hd:hd + 1, 0:1] for b, hd in streams]

    q_all, k_all, new_tails = [], [], []
    for b in range(bsz):
        xqk = qk_ref[b].astype(F32)
        ext = jnp.concatenate([tails[b], xqk], axis=0)
        acc = jnp.broadcast_to(cb_ref[...], xqk.shape)
        for j in range(CONV_WIDTH):
            back = CONV_WIDTH - 1 - j
            sh = ext if back == 0 else pltpu.roll(ext, back, 0)
            acc = acc + cw_ref[j:j + 1, :] * sh[SUBLANES:]
        new_tails.append(xqk[L - SUBLANES:])
        qkc = acc * _sigmoid(acc)
        q_all.append(qkc[:, :width].astype(BF16))
        k_all.append((qkc[:, width:] * scale).astype(BF16))

    q_h = [q_all[b][:, hd * dh:(hd + 1) * dh] for b, hd in streams]
    k_h = [k_all[b][:, hd * dh:(hd + 1) * dh] for b, hd in streams]
    vaug = [jnp.concatenate([v_ref[b][:, hd * dh:(hd + 1) * dh], ones_blk], axis=1)
            for b, hd in streams]
    scores = [_dot_nt(q_h[i], k_h[i]) for i in range(n_str)]
    inter = [_dot(q_h[i], caugs[i].astype(BF16)) for i in range(n_str)]
    s_bf = [(scores[i] * jnp.exp(jnp.where(causal, ib_row[i] - mm_rep[i], -jnp.inf))).astype(BF16)
            for i in range(n_str)]
    kw = [(k_h[i].astype(F32) * ws_rep[i]).astype(BF16) for i in range(n_str)]
    intra = [_dot(s_bf[i], vaug[i]) for i in range(n_str)]
    upd = [_dot_tn(kw[i], vaug[i]) for i in range(n_str)]

    hh = []
    for i in range(n_str):
        num = w_inter[i] * inter[i][:, :dh] + intra[i][:, :dh]
        den = w_inter[i] * inter[i][:, dh:] + intra[i][:, dh:]
        hh.append(num / jnp.maximum(jnp.abs(den), enm[i]))
    mu = [rowsum(h) * inv_dh for h in hh]
    ctr = [hh[i] - mu[i] for i in range(n_str)]
    var = [rowsum(c * c) * inv_dh for c in ctr]
    outs = []
    for i, (b, hd) in enumerate(streams):
        sl = slice(hd * dh, (hd + 1) * dh)
        hn = ctr[i] * lax.rsqrt(var[i] + LN_EPS) * ng_ref[:, sl]
        outs.append(hn * _sigmoid(o_ref[b][:, sl].astype(F32)))

    for b in range(bsz):
        tail_scr[b] = new_tails[b]
        y_ref[b] = jnp.concatenate(outs[b * H:(b + 1) * H], axis=1).astype(BF16)
    for i in range(n_str):
        c_scr[i] = decays[i] * caugs[i] + upd[i]


def _mlstm(qk, v, o, gt, conv_w, conv_b, norm_g, bsz, seq):
    width = v.shape[-1]
    L = min(ML_CHUNK, seq)
    dh = width // ML_HEADS
    qk3 = qk.reshape(bsz, seq, 2 * width)
    v3 = v.reshape(bsz, seq, width)
    o3 = o.reshape(bsz, seq, width)
    gate_rows = _gates(gt, bsz, seq, L)
    full = lambda c: (0, 0)
    return pl.pallas_call(
        functools.partial(_mlstm_kernel, bsz=bsz, chunk=L, width=width),
        grid=(seq // L,),
        in_specs=[
            pl.BlockSpec((bsz, L, 2 * width), lambda c: (0, c, 0)),
            pl.BlockSpec((bsz, L, width), lambda c: (0, c, 0)),
            pl.BlockSpec((bsz, L, width), lambda c: (0, c, 0)),
            pl.BlockSpec((6, bsz, SUBLANES, L), lambda c: (0, 0, 0, c)),
            pl.BlockSpec((CONV_WIDTH, 2 * width), full),
            pl.BlockSpec((1, 2 * width), full),
            pl.BlockSpec((1, width), full),
        ],
        out_specs=pl.BlockSpec((bsz, L, width), lambda c: (0, c, 0)),
        out_shape=jax.ShapeDtypeStruct((bsz, seq, width), BF16),
        scratch_shapes=[
            pltpu.VMEM((bsz, SUBLANES, 2 * width), F32),
            pltpu.VMEM((bsz * ML_HEADS, dh, 2 * dh), F32),
        ],
        compiler_params=pltpu.CompilerParams(
            dimension_semantics=("arbitrary",), vmem_limit_bytes=VMEM_LIMIT),
        name="mlstm",
    )(qk3, v3, o3, gate_rows, conv_w.astype(F32), conv_b.reshape(1, -1).astype(F32),
      norm_g.reshape(1, -1).astype(F32))


def _post_kernel(x_ref, gt_ref, yml_ref, wglut_ref, bglu_ref, wout_ref, nffn_ref, wrt_ref, br_ref,
                 h2_ref, xn_ref, eidx_ref, gate_ref, rank_ref, tcnt_ref, carry_scr):
    step = pl.program_id(0)

    @pl.when(step == 0)
    def _():
        carry_scr[...] = jnp.zeros_like(carry_scr)

    gt = jnp.concatenate([gt_ref[0, j] for j in range(gt_ref.shape[1])], axis=1).astype(BF16)
    s5w = gt.shape[0]
    zt = _dot(wglut_ref[...], gt) + bglu_ref[...]
    s5t = (gt.astype(F32) * _sigmoid(zt)).astype(BF16)
    h2 = (x_ref[...] + _dot_tn(s5t, wout_ref[:s5w, :])
          + _dot(yml_ref[...], wout_ref[s5w:, :]))
    h2_ref[...] = h2
    ms = jnp.mean(h2 * h2, axis=-1, keepdims=True)
    xn = h2 * lax.rsqrt(ms + RMS_EPS) * nffn_ref[...]
    xb = xn.astype(BF16)
    xn_ref[...] = xb

    tm = xn.shape[0]
    logits = _dot_nt(wrt_ref[...], xb) + br_ref[...]
    eio = lax.broadcasted_iota(I32, (N_EXPERTS, tm), 0).astype(F32)
    vals = logits
    onehot = jnp.zeros((N_EXPERTS, tm), F32)
    idxs, tops = [], []
    for _ in range(TOP_K):
        mx = jnp.max(vals, axis=0, keepdims=True)
        idx = jnp.min(jnp.where(vals == mx, eio, float(N_EXPERTS)), axis=0, keepdims=True)
        sel = eio == idx
        onehot = onehot + sel.astype(F32)
        vals = jnp.where(sel, -jnp.inf, vals)
        idxs.append(idx)
        tops.append(mx)
    exps = [jnp.exp(t - tops[0]) for t in tops]
    tot = exps[0] + exps[1] + exps[2] + exps[3]
    pad_f = jnp.zeros((SUBLANES - TOP_K, tm), F32)
    eidx_ref[...] = jnp.concatenate(idxs + [pad_f], axis=0).astype(I32)
    gate_ref[...] = jnp.concatenate([e / tot for e in exps] + [pad_f], axis=0)

    rr = lax.broadcasted_iota(I32, (tm, tm), 0)
    cc = lax.broadcasted_iota(I32, (tm, tm), 1)
    before = (rr < cc).astype(BF16)
    carry = carry_scr[:, 0:1]
    rank_ex = _dot(onehot.astype(BF16), before) + carry
    ranks = [jnp.sum(jnp.where(eio == i, rank_ex, 0.0), axis=0, keepdims=True) for i in idxs]
    rank_ref[...] = jnp.concatenate(ranks + [pad_f], axis=0).astype(I32)
    tile_cnt = jnp.sum(onehot, axis=1, keepdims=True)
    carry_scr[...] = jnp.broadcast_to(carry + tile_cnt, carry_scr.shape)
    for s in range(tm // DISPATCH_TILE):
        sub = onehot[:, s * DISPATCH_TILE:(s + 1) * DISPATCH_TILE]
        tcnt_ref[0, s] = jnp.broadcast_to(jnp.sum(sub, axis=1, keepdims=True), carry_scr.shape)


def _post(x2d, g_t, y_ml, w_glu, b_glu, w_out, norm_g, w_router, b_router):
    n_tok, d = x2d.shape
    bsz, nc, s5w, chunk = g_t.shape
    seq = nc * chunk
    mlw = y_ml.shape[-1]
    tm = min(PROJ_TILE, seq)
    tpb = seq // tm
    full = lambda i: (0, 0)
    row = lambda i: (i, 0)
    colb = lambda i: (0, i)
    return pl.pallas_call(
        _post_kernel,
        grid=(n_tok // tm,),
        in_specs=[
            pl.BlockSpec((tm, d), row),
            pl.BlockSpec((1, tm // chunk, s5w, chunk), lambda i: (i // tpb, i % tpb, 0, 0)),
            pl.BlockSpec((tm, mlw), row),
            pl.BlockSpec((s5w, s5w), full),
            pl.BlockSpec((s5w, 1), full),
            pl.BlockSpec((d, d), full),
            pl.BlockSpec((1, d), full),
            pl.BlockSpec((N_EXPERTS, d), full),
            pl.BlockSpec((N_EXPERTS, 1), full),
        ],
        out_specs=[
            pl.BlockSpec((tm, d), row),
            pl.BlockSpec((tm, d), row),
            pl.BlockSpec((SUBLANES, tm), colb),
            pl.BlockSpec((SUBLANES, tm), colb),
            pl.BlockSpec((SUBLANES, tm), colb),
            pl.BlockSpec((1, tm // DISPATCH_TILE, N_EXPERTS, LANES), lambda i: (i, 0, 0, 0)),
        ],
        out_shape=[
            jax.ShapeDtypeStruct((n_tok, d), F32),
            jax.ShapeDtypeStruct((n_tok, d), BF16),
            jax.ShapeDtypeStruct((SUBLANES, n_tok), I32),
            jax.ShapeDtypeStruct((SUBLANES, n_tok), F32),
            jax.ShapeDtypeStruct((SUBLANES, n_tok), I32),
            jax.ShapeDtypeStruct((n_tok // tm, tm // DISPATCH_TILE, N_EXPERTS, LANES), F32),
        ],
        scratch_shapes=[pltpu.VMEM((N_EXPERTS, LANES), F32)],
        compiler_params=pltpu.CompilerParams(
            dimension_semantics=("arbitrary",), vmem_limit_bytes=VMEM_LIMIT),
        name="post_router",
    )(x2d, g_t, y_ml, w_glu.T.astype(BF16), b_glu.reshape(-1, 1).astype(F32), w_out.astype(BF16),
      norm_g.reshape(1, -1).astype(F32), w_router.T.astype(BF16),
      b_router.reshape(-1, 1).astype(F32))


def _slab_loop(base, nbig_ref, nsmall_ref, fn, n_slabs=N_EXPERTS):
    def per_expert(e, carry):
        idx = base + e
        nbig = nbig_ref[idx]

        def big(j, c2):
            fn(idx, j * BIG_CHUNK, BIG_CHUNK)
            return c2

        def small(j, c2):
            fn(idx, nbig * BIG_CHUNK + j * SLAB_ALIGN, SLAB_ALIGN)
            return c2

        lax.fori_loop(0, nbig, big, 0)
        lax.fori_loop(0, nsmall_ref[idx], small, 0)
        return carry

    lax.fori_loop(0, n_slabs, per_expert, 0)


def _plan_copies(step, plan, fn):
    nbig_ref, nsmall_ref, _, bsrc_ref, bdst_ref, ssrc_ref, sdst_ref = plan

    def big(j, carry):
        k = step * MAX_BIG + j
        fn(bsrc_ref[k], bdst_ref[k], BIG_CHUNK)
        return carry

    def small(j, carry):
        k = step * MAX_SMALL + j
        fn(ssrc_ref[k], sdst_ref[k], SLAB_ALIGN)
        return carry

    lax.fori_loop(0, nbig_ref[step], big, 0)
    lax.fori_loop(0, nsmall_ref[step], small, 0)


def _plan_wait(step, plan, copy):
    copy(0, 0, DISPATCH_TILE * TOP_K).wait()

    def one(j, carry):
        copy(0, 0, SLAB_ALIGN).wait()
        return carry

    lax.fori_loop(0, plan[2][step], one, 0)


def _unpack_pairs(words):
    lo = lax.bitcast_convert_type(words << 16, F32).astype(BF16)
    hi = lax.bitcast_convert_type(words & jnp.uint32(0xFFFF0000), F32).astype(BF16)
    return lo, hi


def _pack_pairs(lo_f32, hi_f32):
    lo = lax.bitcast_convert_type(lo_f32, U32) >> 16
    hi = lax.bitcast_convert_type(hi_f32, U32) & jnp.uint32(0xFFFF0000)
    return hi | lo


def _dispatch_kernel(*refs, n_steps):
    plan = refs[:7]
    znbig_ref, znsmall_ref, zrow_ref, xn_ref, pos_ref, xs_hbm, stage, zbuf, sem, zsem = refs[7:]
    i = pl.program_id(0)
    slot = i % 2
    tw, d = xn_ref.shape
    n_rows = stage.shape[1]
    half = d // 2

    def copy(s, stage_row, buf_row, rows):
        src = stage.at[s, pl.ds(pl.multiple_of(stage_row, SLAB_ALIGN), rows), :]
        dst = xs_hbm.at[pl.ds(pl.multiple_of(buf_row, SLAB_ALIGN), rows), :]
        return pltpu.make_async_copy(src, dst, sem)

    @pl.when(i == 0)
    def _():
        zbuf[...] = jnp.zeros_like(zbuf)

        def tail(idx, r0, rows):
            dst = xs_hbm.at[pl.ds(pl.multiple_of(zrow_ref[idx] + r0, SLAB_ALIGN), rows), :]
            return pltpu.make_async_copy(zbuf.at[pl.ds(0, rows), :], dst, zsem)

        _slab_loop(0, znbig_ref, znsmall_ref, lambda *a: tail(*a).start(), N_EXPERTS + 1)
        _slab_loop(0, znbig_ref, znsmall_ref, lambda *a: tail(*a).wait(), N_EXPERTS + 1)

    xt = xn_ref[...]
    blk = STAGE_BLOCK
    for r0 in range(0, n_rows, blk):
        rid = lax.broadcasted_iota(I32, (blk, tw), 0) + r0
        hit = rid == pos_ref[0:1, :]
        for k in range(1, TOP_K):
            hit = jnp.logical_or(hit, rid == pos_ref[k:k + 1, :])
        srt = _dot(jnp.where(hit, 1.0, 0.0).astype(BF16), xt)
        stage[slot, r0:r0 + blk, :] = _pack_pairs(srt[:, :half], srt[:, half:])

    @pl.when(i > 0)
    def _():
        _plan_wait(i - 1, plan, functools.partial(copy, 1 - slot))

    _plan_copies(i, plan, lambda *a: copy(slot, *a).start())

    @pl.when(i == n_steps - 1)
    def _():
        _plan_wait(i, plan, functools.partial(copy, slot))


def _dispatch(xn, pos8, plan, tails, n_slots):
    n_tok, d = xn.shape
    tw = DISPATCH_TILE
    n_steps = n_tok // tw
    return pl.pallas_call(
        functools.partial(_dispatch_kernel, n_steps=n_steps),
        grid_spec=pltpu.PrefetchScalarGridSpec(
            num_scalar_prefetch=10,
            grid=(n_steps,),
            in_specs=[
                pl.BlockSpec((tw, d), lambda i, *_: (i, 0)),
                pl.BlockSpec((SUBLANES, tw), lambda i, *_: (0, i)),
            ],
            out_specs=pl.BlockSpec(memory_space=pl.ANY),
            scratch_shapes=[
                pltpu.VMEM((2, STAGE_ROWS, d // 2), U32),
                pltpu.VMEM((BIG_CHUNK, d // 2), U32),
                pltpu.SemaphoreType.DMA(()),
                pltpu.SemaphoreType.DMA(()),
            ],
        ),
        out_shape=jax.ShapeDtypeStruct((n_slots, d // 2), U32),
        compiler_params=pltpu.CompilerParams(
            dimension_semantics=("arbitrary",), vmem_limit_bytes=VMEM_LIMIT),
        name="dispatch",
    )(*plan, *tails, xn, pos8)


def _expert_kernel(te_ref, nu_ref, first_ref, nxt_ref, x_ref, wup_hbm, bup_ref, wdn_hbm,
                   bdn_ref, y_ref, wup_f32, wdn_f32, wup_bf, wdn_bf, sem, *, d_ff):
    i = pl.program_id(0)
    n_used = nu_ref[0]

    def fetch(e):
        return (pltpu.make_async_copy(wup_hbm.at[e], wup_f32, sem.at[0]),
                pltpu.make_async_copy(wdn_hbm.at[e], wdn_f32, sem.at[1]))

    @pl.when(jnp.logical_and(i < n_used, first_ref[i] == 1))
    def _():
        @pl.when(i == 0)
        def _():
            for c in fetch(te_ref[i]):
                c.start()

        for c in fetch(te_ref[i]):
            c.wait()
        wup_bf[...] = wup_f32[...].astype(BF16)
        wdn_bf[...] = wdn_f32[...].astype(BF16)

        @pl.when(nxt_ref[i] >= 0)
        def _():
            for c in fetch(nxt_ref[i]):
                c.start()

    @pl.when(i < n_used)
    def _():
        lo, hi = _unpack_pairs(x_ref[...])
        half = lo.shape[1]
        h = _dot(lo, wup_bf[:half, :]) + _dot(hi, wup_bf[half:, :]) + bup_ref[0]
        gl = jnp.minimum(h[:, :d_ff], SWIGLU_LIMIT)
        lin = jnp.clip(h[:, d_ff:], -SWIGLU_LIMIT, SWIGLU_LIMIT)
        act = gl * _sigmoid(SWIGLU_ALPHA * gl) * (lin + 1.0)
        y = _dot(act.astype(BF16), wdn_bf[...]) + bdn_ref[0]
        yb = y.astype(BF16).astype(F32)
        y_ref[...] = _pack_pairs(yb[:, :half], yb[:, half:])

    @pl.when(i >= n_used)
    def _():
        y_ref[...] = jnp.zeros_like(y_ref)


def _experts(xs, tile_e, n_used, first, nxt, w_up, b_up, w_down, b_down):
    n_slots, half = xs.shape
    d = 2 * half
    tm = EXPERT_TILE
    n_tiles = n_slots // tm
    d_ff = w_down.shape[1]
    return pl.pallas_call(
        functools.partial(_expert_kernel, d_ff=d_ff),
        grid_spec=pltpu.PrefetchScalarGridSpec(
            num_scalar_prefetch=4,
            grid=(n_tiles,),
            in_specs=[
                pl.BlockSpec((tm, half), lambda i, te, nu, *_: (jnp.minimum(i, nu[0] - 1), 0)),
                pl.BlockSpec(memory_space=pl.ANY),
                pl.BlockSpec((1, 1, 2 * d_ff), lambda i, te, *_: (te[i], 0, 0)),
                pl.BlockSpec(memory_space=pl.ANY),
                pl.BlockSpec((1, 1, d), lambda i, te, *_: (te[i], 0, 0)),
            ],
            out_specs=pl.BlockSpec((tm, half), lambda i, *_: (i, 0)),
            scratch_shapes=[
                pltpu.VMEM((d, 2 * d_ff), F32),
                pltpu.VMEM((d_ff, d), F32),
                pltpu.VMEM((d, 2 * d_ff), BF16),
                pltpu.VMEM((d_ff, d), BF16),
                pltpu.SemaphoreType.DMA((2,)),
            ],
        ),
        out_shape=jax.ShapeDtypeStruct((n_slots, half), U32),
        compiler_params=pltpu.CompilerParams(
            dimension_semantics=("arbitrary",), vmem_limit_bytes=VMEM_LIMIT),
        name="experts",
    )(tile_e, n_used, first, nxt, xs, w_up, b_up.reshape(N_EXPERTS, 1, -1),
      w_down, b_down.reshape(N_EXPERTS, 1, -1))


def _combine_kernel(*refs, n_steps):
    plan = refs[:7]
    ys_hbm, h2_ref, posc_ref, gatec_ref, ng_ref, out_ref, stage, sem = refs[7:]
    i = pl.program_id(0)
    slot = i % 2
    tw, d = h2_ref.shape
    n_rows = stage.shape[1]

    def copy(s, stage_row, buf_row, rows):
        src = ys_hbm.at[pl.ds(pl.multiple_of(buf_row, SLAB_ALIGN), rows), :]
        dst = stage.at[s, pl.ds(pl.multiple_of(stage_row, SLAB_ALIGN), rows), :]
        return pltpu.make_async_copy(src, dst, sem.at[s])

    @pl.when(i == 0)
    def _():
        stage[...] = jnp.zeros_like(stage)
        _plan_copies(0, plan, lambda *a: copy(0, *a).start())

    @pl.when(i + 1 < n_steps)
    def _():
        _plan_copies(i + 1, plan, lambda *a: copy(1 - slot, *a).start())

    _plan_wait(i, plan, functools.partial(copy, slot))

    y_lo, y_hi = _unpack_pairs(stage[slot])
    blk = 128
    for t0 in range(0, tw, blk):
        cid = lax.broadcasted_iota(I32, (blk, n_rows), 1)
        pg = jnp.zeros((blk, n_rows), F32)
        for k in range(TOP_K):
            pg = jnp.where(cid == posc_ref[t0:t0 + blk, k:k + 1],
                           gatec_ref[t0:t0 + blk, k:k + 1], pg)
        pgb = pg.astype(BF16)
        moe = jnp.concatenate([_dot(pgb, y_lo), _dot(pgb, y_hi)], axis=1)
        acc = h2_ref[t0:t0 + blk, :] + moe
        ms = jnp.mean(acc * acc, axis=-1, keepdims=True)
        out_ref[t0:t0 + blk, :] = acc * lax.rsqrt(ms + RMS_EPS) * ng_ref[...]


def _combine(ys, h2, pos_cols, gate_cols, plan, norm_g):
    n_tok, d = h2.shape
    tw = DISPATCH_TILE
    n_steps = n_tok // tw
    return pl.pallas_call(
        functools.partial(_combine_kernel, n_steps=n_steps),
        grid_spec=pltpu.PrefetchScalarGridSpec(
            num_scalar_prefetch=7,
            grid=(n_steps,),
            in_specs=[
                pl.BlockSpec(memory_space=pl.ANY),
                pl.BlockSpec((tw, d), lambda i, *_: (i, 0)),
                pl.BlockSpec((tw, SUBLANES), lambda i, *_: (i, 0)),
                pl.BlockSpec((tw, SUBLANES), lambda i, *_: (i, 0)),
                pl.BlockSpec((1, d), lambda i, *_: (0, 0)),
            ],
            out_specs=pl.BlockSpec((tw, d), lambda i, *_: (i, 0)),
            scratch_shapes=[
                pltpu.VMEM((2, STAGE_ROWS, d // 2), U32),
                pltpu.SemaphoreType.DMA((2,)),
            ],
        ),
        out_shape=jax.ShapeDtypeStruct((n_tok, d), F32),
        compiler_params=pltpu.CompilerParams(
            dimension_semantics=("arbitrary",), vmem_limit_bytes=VMEM_LIMIT),
        name="combine",
    )(*plan, ys, h2, pos_cols, gate_cols, norm_g.reshape(1, -1).astype(F32))


def _moe(h2, xn, eidx, gate, rank, tcnt, w_up, b_up, w_down, b_down, norm_final_g):
    n_tok, d = h2.shape
    tm = EXPERT_TILE
    tw = DISPATCH_TILE
    n_steps = n_tok // tw
    n_exp = N_EXPERTS
    e_ids = jnp.arange(n_exp, dtype=I32)
    al = SLAB_ALIGN
    tile_cnt = tcnt.reshape(n_steps, n_exp, LANES)[:, :, 0].astype(I32)
    slab_rows = (tile_cnt + al - 1) // al * al
    exp_rows = jnp.sum(slab_rows, axis=0)
    region = (exp_rows + tm - 1) // tm * tm
    pad_end = jnp.cumsum(region)
    pad_start = pad_end - region
    n_slots = (n_tok * TOP_K + n_steps * n_exp * (al - 1) + n_exp * (tm - 1) + tm - 1) // tm * tm
    n_tiles = n_slots // tm

    tile_carry = jnp.cumsum(tile_cnt, axis=0) - tile_cnt
    tile_off = jnp.cumsum(slab_rows, axis=1) - slab_rows
    slab_row0 = pad_start[None, :] + jnp.cumsum(slab_rows, axis=0) - slab_rows

    def chunks(rows):
        return rows // BIG_CHUNK, rows % BIG_CHUNK // al

    def flat(cnt, src0, dst0, size, n_max):
        run = jnp.cumsum(cnt, axis=1)
        j = jnp.arange(n_max, dtype=I32)
        e_of = jnp.sum(run[:, None, :] <= j[None, :, None], axis=-1)
        sel = e_of[:, :, None] == e_ids[None, None, :]
        pick = lambda a: jnp.sum(jnp.where(sel, a[:, None, :], 0), axis=-1)
        local = (j[None, :] - pick(run - cnt)) * size
        return run[:, -1], (pick(src0) + local).reshape(-1), (pick(dst0) + local).reshape(-1)

    nbig, nsmall = chunks(slab_rows)
    big_tot, big_src, big_dst = flat(nbig, tile_off, slab_row0, BIG_CHUNK, MAX_BIG)
    small_tot, small_src, small_dst = flat(nsmall, tile_off + nbig * BIG_CHUNK,
                                           slab_row0 + nbig * BIG_CHUNK, al, MAX_SMALL)
    pad_groups = (jnp.sum(slab_rows, axis=1) - tw * TOP_K) // al
    plan = (big_tot, small_tot, pad_groups, big_src, big_dst, small_src, small_dst)
    tail_rows = jnp.concatenate([region - exp_rows, n_slots - pad_end[-1:]])
    tails = (*chunks(tail_rows), jnp.concatenate([pad_start + exp_rows, pad_end[-1:]]))

    base_t = jnp.repeat((tile_off - tile_carry).T, tw, axis=1)
    e_sel = eidx[:TOP_K][None] == e_ids[:, None, None]
    pos = jnp.sum(jnp.where(e_sel, base_t[:, None, :], 0), axis=0) + rank[:TOP_K]
    pos8 = jnp.concatenate([pos, jnp.full((SUBLANES - TOP_K, n_tok), -1, I32)], axis=0)

    tile_start = jnp.arange(n_tiles, dtype=I32) * tm
    tile_e = jnp.minimum(jnp.sum(pad_end[None, :] <= tile_start[:, None], axis=-1),
                         n_exp - 1).astype(I32)
    n_used = (pad_end[-1] // tm).astype(I32).reshape(1)
    t_ids = jnp.arange(n_tiles, dtype=I32)
    first = jnp.logical_and(jnp.logical_or(t_ids == 0, tile_e != jnp.roll(tile_e, 1)),
                            t_ids < n_used[0])
    later = jnp.logical_and(first[None, :], t_ids[None, :] > t_ids[:, None])
    nxt_tile = jnp.min(jnp.where(later, t_ids[None, :], n_tiles), axis=1)
    nxt_e = jnp.sum(jnp.where(t_ids[None, :] == nxt_tile[:, None], tile_e[None, :], 0), axis=1)
    nxt = jnp.where(nxt_tile < n_tiles, nxt_e, -1).astype(I32)
    first = first.astype(I32)

    xs = _dispatch(xn, pos8, plan, tails, n_slots)
    ys = _experts(xs, tile_e, n_used, first, nxt, w_up, b_up, w_down, b_down)
    return _combine(ys, h2, pos8.T, gate.T, plan, norm_final_g)


def kernel(x, norm_mix_g, w_in, s5_log_dt, s5_a_re, s5_a_im, s5_b_re, s5_b_im, s5_c_re, s5_c_im,
           s5_d, s5_w_glu, s5_b_glu, ml_conv_w, ml_conv_b, ml_b_gates, ml_norm_g, w_out,
           norm_ffn_g, w_router, b_router, w_up, b_up, w_down, b_down, norm_final_g):
    bsz, seq, d = x.shape
    depth = w_in.shape[0]
    assert depth == 1, "single-layer block"
    l = 0
    x2d = x.reshape(bsz * seq, d)
    qk, v, o, ut, gt = _in_proj(x2d, norm_mix_g[l], w_in[l], ml_b_gates[l], bsz, seq)
    g_t = _s5(ut, s5_log_dt[l], s5_a_re[l], s5_a_im[l], s5_b_re[l], s5_b_im[l],
              s5_c_re[l], s5_c_im[l], s5_d[l])
    y_ml = _mlstm(qk, v, o, gt, ml_conv_w[l], ml_conv_b[l], ml_norm_g[l], bsz, seq)
    y_ml = y_ml.reshape(bsz * seq, -1)
    h2, xn, eidx, gate, rank, tcnt = _post(x2d, g_t, y_ml, s5_w_glu[l], s5_b_glu[l], w_out[l],
                                           norm_ffn_g[l], w_router[l], b_router[l])
    out = _moe(h2, xn, eidx, gate, rank, tcnt, w_up[l], b_up[l], w_down[l], b_down[l],
               norm_final_g)
    return out.reshape(bsz, seq, d)
```

```python
import functools
import math

import jax
import jax.numpy as jnp
from jax import lax
from jax.experimental import pallas as pl
from jax.experimental.pallas import tpu as pltpu

F32 = jnp.float32
BF16 = jnp.bfloat16
I32 = jnp.int32
U32 = jnp.uint32

S5_GROUP = 16
S5_STATE = 64
ML_HEADS = 4
CONV_WIDTH = 4
N_EXPERTS = 32
TOP_K = 4
SWIGLU_LIMIT = 7.0
SWIGLU_ALPHA = 1.702
RMS_EPS = 1e-5
LN_EPS = 1e-6

LANES = 128
SUBLANES = 8
S5_CHUNK = LANES
ML_CHUNK = 128
GATE_CHUNKS = 8
EXP_CAP = 1e38
PROJ_TILE = 1024
EXPERT_TILE = 512
DISPATCH_TILE = 256
SLAB_ALIGN = SUBLANES
BIG_CHUNK = 32
STAGE_ROWS = DISPATCH_TILE * TOP_K + N_EXPERTS * SLAB_ALIGN
STAGE_BLOCK = STAGE_ROWS // 2
MAX_BIG = STAGE_ROWS // BIG_CHUNK
MAX_SMALL = N_EXPERTS * (BIG_CHUNK // SLAB_ALIGN - 1)
VMEM_LIMIT = 56 * 1024 * 1024

_NT = (((1,), (1,)), ((), ()))
_TN = (((0,), (0,)), ((), ()))


def _dot(a, b):
    return jnp.dot(a, b, preferred_element_type=F32)


def _dot_nt(a, b):
    return lax.dot_general(a, b, _NT, preferred_element_type=F32)


def _dot_tn(a, b):
    return lax.dot_general(a, b, _TN, preferred_element_type=F32)


def _split3(x):
    p1 = x.astype(BF16)
    r1 = x - p1.astype(F32)
    p2 = r1.astype(BF16)
    r2 = r1 - p2.astype(F32)
    return p1, p2, r2.astype(BF16)


def _sigmoid(x):
    return 0.5 * jnp.tanh(0.5 * x) + 0.5


def _inproj_kernel(x_ref, g_ref, wnat_ref, wut_ref, wgt_ref, bg_ref,
                   qk_ref, v_ref, o_ref, ut_ref, gt_ref, *, width):
    x = x_ref[...]
    ms = jnp.mean(x * x, axis=-1, keepdims=True)
    hn = (x * lax.rsqrt(ms + RMS_EPS) * g_ref[...]).astype(BF16)
    nat = _dot(hn, wnat_ref[...])
    qk_ref[...] = nat[:, :2 * width].astype(BF16)
    v_ref[...] = nat[:, 2 * width:3 * width].astype(BF16)
    o_ref[...] = nat[:, 3 * width:].astype(BF16)
    ut_ref[0] = _dot_nt(wut_ref[...], hn).astype(BF16)
    gt_ref[0] = _dot_nt(wgt_ref[...], hn) + bg_ref[...]


def _in_proj(x2d, norm_g, w_in, b_gates, bsz, seq):
    n_tok, d = x2d.shape
    s5w = d // 2
    mlw = d - s5w
    tm = min(PROJ_TILE, seq)
    tpb = seq // tm
    w_bf = w_in.astype(BF16)
    w_nat = w_bf[:, s5w:s5w + 4 * mlw]
    w_ut = w_bf[:, :s5w].T
    n_gate = 2 * ML_HEADS
    w_gt = jnp.zeros((16, d), BF16).at[:n_gate].set(w_bf[:, s5w + 4 * mlw:].T)
    b_g = jnp.zeros((16, 1), F32).at[:n_gate, 0].set(b_gates.astype(F32))
    grid = (n_tok // tm,)
    full = lambda i: (0, 0)
    return pl.pallas_call(
        functools.partial(_inproj_kernel, width=mlw),
        grid=grid,
        in_specs=[
            pl.BlockSpec((tm, d), lambda i: (i, 0)),
            pl.BlockSpec((1, d), full),
            pl.BlockSpec((d, 4 * mlw), full),
            pl.BlockSpec((s5w, d), full),
            pl.BlockSpec((16, d), full),
            pl.BlockSpec((16, 1), full),
        ],
        out_specs=[
            pl.BlockSpec((tm, 2 * mlw), lambda i: (i, 0)),
            pl.BlockSpec((tm, mlw), lambda i: (i, 0)),
            pl.BlockSpec((tm, mlw), lambda i: (i, 0)),
            pl.BlockSpec((1, s5w, tm), lambda i: (i // tpb, 0, i % tpb)),
            pl.BlockSpec((1, 16, tm), lambda i: (i // tpb, 0, i % tpb)),
        ],
        out_shape=[
            jax.ShapeDtypeStruct((n_tok, 2 * mlw), BF16),
            jax.ShapeDtypeStruct((n_tok, mlw), BF16),
            jax.ShapeDtypeStruct((n_tok, mlw), BF16),
            jax.ShapeDtypeStruct((bsz, s5w, seq), BF16),
            jax.ShapeDtypeStruct((bsz, 16, seq), F32),
        ],
        compiler_params=pltpu.CompilerParams(
            dimension_semantics=("parallel",), vmem_limit_bytes=VMEM_LIMIT),
        name="in_proj",
    )(x2d, norm_g.reshape(1, d).astype(F32), w_nat, w_ut, w_gt, b_g)


def _s5_kernel(d_ref, x_ref, prm_ref, c_ref, bt_ref, out_ref, *, bsz, nc):
    L = S5_CHUNK
    P = S5_GROUP
    N = S5_STATE
    grp = pl.program_id(0)
    lane = lax.broadcasted_iota(I32, (1, 2 * N), 1)
    lo = lane < N
    a_re = jnp.minimum(prm_ref[0, 0:1, :], -1e-4)
    a_im = prm_ref[0, 1:2, :]
    dt = jnp.exp(prm_ref[0, 2:3, :])
    zr = dt * a_re
    zi = dt * a_im

    quarter = jnp.where(lo, 0.0, 0.5 * math.pi)

    def powrows(e):
        return jnp.exp(e * zr) * jnp.cos(e * zi - quarter)

    def swap(tab):
        return pltpu.roll(tab, N, 1)

    def cmul(tab, c):
        return c[0] * tab + jnp.where(lo, -c[1], c[1]) * swap(tab)

    n_dbl = max(nc - 1, 0).bit_length()
    exps = [1] + [SUBLANES << k for k in range(int(math.log2(L // SUBLANES)))] + [L << k for k in range(n_dbl)]
    e_col = jnp.concatenate([jnp.full((1, 1), float(e), F32) for e in exps]
                            + [jnp.zeros((-len(exps) % SUBLANES, 1), F32)], axis=0)
    mag = jnp.exp(e_col * zr)
    pw_r = mag * jnp.cos(e_col * zi)
    pw_i = mag * jnp.sin(e_col * zi)
    apow = {e: (pw_r[k:k + 1], pw_i[k:k + 1]) for k, e in enumerate(exps)}

    def powtab(descending):
        i8 = lax.broadcasted_iota(I32, (SUBLANES, 1), 0).astype(F32)
        tab = powrows(SUBLANES - 1.0 - i8 if descending else i8)
        rows = SUBLANES
        while rows < L:
            more = cmul(tab, apow[rows])
            tab = jnp.concatenate([more, tab] if descending else [tab, more], axis=0)
            rows *= 2
        return tab

    er, ei = apow[1]
    den = a_re * a_re + a_im * a_im
    coef_r = ((er - 1.0) * a_re + ei * a_im) / den
    coef_i = (ei * a_re - (er - 1.0) * a_im) / den
    c_r = c_ref[0, 0]
    c_i = c_ref[0, 1]
    bb_r = coef_r * bt_ref[0, 0] - coef_i * bt_ref[0, 1]
    bb_i = coef_r * bt_ref[0, 1] + coef_i * bt_ref[0, 0]

    cb_rows = []
    for q in range(P):
        cbr = c_r * bb_r[q:q + 1] - c_i * bb_i[q:q + 1]
        cbi = c_r * bb_i[q:q + 1] + c_i * bb_r[q:q + 1]
        cb_rows.append(jnp.where(lo, cbr, -cbi))
    cb = jnp.concatenate(cb_rows, axis=0)
    pt0 = powtab(False)
    c1, c2, c3 = _split3(cb)
    t1, t2, t3 = _split3(pt0)
    kmat = (_dot_nt(c1, t1) + _dot_nt(c1, t2) + _dot_nt(c2, t1)
            + _dot_nt(c2, t2) + _dot_nt(c1, t3) + _dot_nt(c3, t1))

    pt_rev = powtab(True)
    pt_rev_sw = swap(pt_rev)
    f_rows = []
    for q in range(P):
        a1 = bb_r[q:q + 1]
        a2 = jnp.where(lo, -bb_i[q:q + 1], bb_i[q:q + 1])
        f_rows.append((a1 * pt_rev + a2 * pt_rev_sw).astype(BF16))
    fmat = jnp.concatenate(f_rows, axis=0)
    pt1 = cmul(pt0, apow[1])
    pt1_sw = swap(pt1)
    e_rows = []
    for p in range(P):
        b1 = jnp.where(lo, c_r[p:p + 1], -c_r[p:p + 1])
        b2 = -c_i[p:p + 1]
        e_rows.append((b1 * pt1 + b2 * pt1_sw).astype(BF16))
    emat_t = jnp.concatenate(e_rows, axis=0)

    lhs = jnp.concatenate(
        [jnp.concatenate([x_ref[b, q] for q in range(P)], axis=1) for b in range(bsz)], axis=0)
    s_end = _dot(lhs, fmat)

    rr = lax.broadcasted_iota(I32, (L, L), 0)
    cc = lax.broadcasted_iota(I32, (L, L), 1)
    causal = cc >= rr
    y = None
    for q0 in range(0, P, 2):
        rows = []
        for q in (q0, q0 + 1):
            tiles = []
            for p in range(P):
                kb = jnp.broadcast_to(kmat[q * P + p:q * P + p + 1, :], (L, L))
                toe = pltpu.roll(kb, 0, 1, stride=1, stride_axis=0)
                tiles.append(jnp.where(causal, toe, 0.0).astype(BF16))
            rows.append(jnp.concatenate(tiles, axis=1))
        part = _dot(lhs[:, q0 * L:(q0 + 2) * L], jnp.concatenate(rows, axis=0))
        y = part if y is None else y + part

    m_rows = bsz * nc
    ridx = lax.broadcasted_iota(I32, (m_rows, 1), 0)
    cidx = ridx % nc
    h = jnp.where(cidx >= 1, pltpu.roll(s_end, 1, 0), 0.0)
    d = 1
    while d < nc:
        sh = jnp.where(cidx >= d, pltpu.roll(h, d, 0), 0.0)
        h = h + cmul(sh, apow[d * L])
        d *= 2
    y = y + _dot_nt(h.astype(BF16), emat_t)

    for b in range(bsz):
        for p in range(P):
            yp = (y[b * nc:(b + 1) * nc, p * L:(p + 1) * L]
                  + d_ref[grp * P + p] * x_ref[b, p].astype(F32))
            out_ref[b, :, p, :] = jax.nn.gelu(yp)


def _s5(ut, log_dt, a_re, a_im, b_re, b_im, c_re, c_im, d_skip):
    bsz, s5w, seq = ut.shape
    L = S5_CHUNK
    nc = seq // L
    groups = s5w // S5_GROUP
    n = S5_STATE
    x4 = ut.reshape(bsz, s5w, nc, L)
    dup = lambda t: jnp.concatenate([t, t], axis=-1).astype(F32)
    prm = jnp.zeros((groups, SUBLANES, 2 * n), F32)
    prm = prm.at[:, 0].set(dup(a_re)).at[:, 1].set(dup(a_im))
    prm = prm.at[:, 2].set(jnp.broadcast_to(log_dt.astype(F32)[:, None], (groups, 2 * n)))
    cpar = jnp.stack([dup(c_re), dup(c_im)], axis=1)
    btpar = jnp.stack([dup(jnp.swapaxes(b_re, 1, 2)), dup(jnp.swapaxes(b_im, 1, 2))], axis=1)
    out = pl.pallas_call(
        functools.partial(_s5_kernel, bsz=bsz, nc=nc),
        grid_spec=pltpu.PrefetchScalarGridSpec(
            num_scalar_prefetch=1,
            grid=(groups,),
            in_specs=[
                pl.BlockSpec((bsz, S5_GROUP, nc, L), lambda g, d: (0, g, 0, 0)),
                pl.BlockSpec((1, SUBLANES, 2 * n), lambda g, d: (g, 0, 0)),
                pl.BlockSpec((1, 2, S5_GROUP, 2 * n), lambda g, d: (g, 0, 0, 0)),
                pl.BlockSpec((1, 2, S5_GROUP, 2 * n), lambda g, d: (g, 0, 0, 0)),
            ],
            out_specs=pl.BlockSpec((bsz, nc, S5_GROUP, L), lambda g, d: (0, 0, g, 0)),
        ),
        out_shape=jax.ShapeDtypeStruct((bsz, nc, s5w, L), F32),
        compiler_params=pltpu.CompilerParams(
            dimension_semantics=("parallel",), vmem_limit_bytes=VMEM_LIMIT),
        name="s5",
    )(d_skip.astype(F32), x4, prm, cpar, btpar)
    return out


def _log_sigmoid(x):
    return jnp.minimum(x, 0.0) - jnp.log(1.0 + jnp.exp(-jnp.abs(x)))


def _gates_kernel(gt_ref, out_ref, m_scr, *, bsz, chunk, n_sub):
    L = chunk
    H = ML_HEADS
    step = pl.program_id(0)

    @pl.when(step == 0)
    def _():
        m_scr[...] = jnp.zeros_like(m_scr)

    rr = lax.broadcasted_iota(I32, (L, L), 0)
    cc = lax.broadcasted_iota(I32, (L, L), 1)
    utri = (rr <= cc).astype(BF16)
    grow = lax.broadcasted_iota(I32, (16, 1), 0)
    lane_row = lax.broadcasted_iota(I32, (SUBLANES, L), 1)
    m_prev = [m_scr[b][0:H, 0:1] for b in range(bsz)]
    pairs = [(j, b) for j in range(n_sub) for b in range(bsz)]

    irow, brow, cm = {}, {}, {}
    for j, b in pairs:
        g = gt_ref[b, :, j * L:(j + 1) * L]
        g2 = jnp.where(grow >= H, _log_sigmoid(g), g)
        p1, p2, p3 = _split3(g2)
        brow[j, b] = (_dot(p1, utri) + _dot(p2, utri) + _dot(p3, utri))[H:2 * H]
        irow[j, b] = g[0:H]
    for j, b in pairs:
        ib = irow[j, b] - brow[j, b]
        x = jnp.concatenate([ib, ib], axis=0)
        sft = 1
        while sft < L:
            x = jnp.maximum(x, jnp.where(lane_row >= sft, pltpu.roll(x, sft, 1), -jnp.inf))
            sft *= 2
        cm[j, b] = x[0:H]
    for j, b in pairs:
        mp = m_prev[b]
        mm = jnp.maximum(cm[j, b], mp)
        b_last = brow[j, b][:, L - 1:L]
        m_next = b_last + jnp.maximum(mp, cm[j, b][:, L - 1:L])
        planes = (mm, jnp.exp(mp - mm), jnp.minimum(jnp.exp(-(brow[j, b] + mm)), EXP_CAP),
                  jnp.exp(b_last - brow[j, b] + irow[j, b] - m_next), irow[j, b] - brow[j, b],
                  jnp.broadcast_to(jnp.exp(b_last + mp - m_next), (H, L)))
        for k, rows in enumerate(planes):
            out_ref[k, b, :, j * L:(j + 1) * L] = jnp.concatenate([rows, rows], axis=0)
        m_prev[b] = m_next
    for b in range(bsz):
        m_scr[b] = jnp.broadcast_to(jnp.concatenate([m_prev[b], m_prev[b]], axis=0),
                                    (SUBLANES, LANES))


def _gates(gt, bsz, seq, chunk):
    n_sub = min(GATE_CHUNKS, seq // chunk)
    blk = n_sub * chunk
    return pl.pallas_call(
        functools.partial(_gates_kernel, bsz=bsz, chunk=chunk, n_sub=n_sub),
        grid=(seq // blk,),
        in_specs=[pl.BlockSpec((bsz, 16, blk), lambda c: (0, 0, c))],
        out_specs=pl.BlockSpec((6, bsz, SUBLANES, blk), lambda c: (0, 0, 0, c)),
        out_shape=jax.ShapeDtypeStruct((6, bsz, SUBLANES, seq), F32),
        scratch_shapes=[pltpu.VMEM((bsz, SUBLANES, LANES), F32)],
        compiler_params=pltpu.CompilerParams(
            dimension_semantics=("arbitrary",), vmem_limit_bytes=VMEM_LIMIT),
        name="gates",
    )(gt)


def _mlstm_kernel(qk_ref, v_ref, o_ref, gr_ref, cw_ref, cb_ref, ng_ref, y_ref,
                  tail_scr, c_scr, *, bsz, chunk, width):
    L = chunk
    H = ML_HEADS
    dh = width // H
    step = pl.program_id(0)
    assert L == LANES and dh == LANES, "column replication below uses one 128x128 tile per head"

    @pl.when(step == 0)
    def _():
        tail_scr[...] = jnp.zeros_like(tail_scr)
        c_scr[...] = jnp.zeros_like(c_scr)

    rr = lax.broadcasted_iota(I32, (L, L), 0)
    cc = lax.broadcasted_iota(I32, (L, L), 1)
    causal = cc <= rr
    eye = (rr == cc).astype(BF16)
    ones_blk = jnp.ones((L, dh), BF16)
    scale = 1.0 / math.sqrt(dh)
    inv_dh = 1.0 / dh
    n_str = bsz * H
    streams = [(b, hd) for b in range(bsz) for hd in range(H)]

    def replicate(rows, two_terms):
        p = jnp.concatenate([jnp.broadcast_to(rows[h:h + 1], (LANES, L)) for h in range(H)], axis=0)
        p1 = p.astype(BF16)
        if not two_terms:
            return _dot_nt(eye, p1)
        return _dot_nt(eye, p1) + _dot_nt(eye, (p - p1.astype(F32)).astype(BF16))

    def rowsum(x):
        hi = x.astype(BF16)
        lo = (x - hi.astype(F32)).astype(BF16)
        return _dot(hi, ones_blk) + _dot(lo, ones_blk)

    tails = [tail_scr[b] for b in range(bsz)]
    caugs = [c_scr[i] for i in range(n_str)]
    reps = [[replicate(gr_ref[k, b][0:H], k == 0) for b in range(bsz)] for k in range(4)]
    col = lambda k: [reps[k][b][:, hd * LANES:(hd + 1) * LANES] for b, hd in streams]
    mm_rep, w_inter, enm, ws_rep = col(0), col(1), col(2), col(3)
    ib_row = [gr_ref[4, b][hd:hd + 1] for b, hd in streams]
    decays = [gr_ref[5, b][hd:hd + 1, 0:1] for b, hd in streams]

    q_all, k_all, new_tails = [], [], []
    for b in range(bsz):
        xqk = qk_ref[b].astype(F32)
        ext = jnp.concatenate([tails[b], xqk], axis=0)
        acc = jnp.broadcast_to(cb_ref[...], xqk.shape)
        for j in range(CONV_WIDTH):
            back = CONV_WIDTH - 1 - j
            sh = ext if back == 0 else pltpu.roll(ext, back, 0)
            acc = acc + cw_ref[j:j + 1, :] * sh[SUBLANES:]
        new_tails.append(xqk[L - SUBLANES:])
        qkc = acc * _sigmoid(acc)
        q_all.append(qkc[:, :width].astype(BF16))
        k_all.append((qkc[:, width:] * scale).astype(BF16))

    q_h = [q_all[b][:, hd * dh:(hd + 1) * dh] for b, hd in streams]
    k_h = [k_all[b][:, hd * dh:(hd + 1) * dh] for b, hd in streams]
    vaug = [jnp.concatenate([v_ref[b][:, hd * dh:(hd + 1) * dh], ones_blk], axis=1)
            for b, hd in streams]
    scores = [_dot_nt(q_h[i], k_h[i]) for i in range(n_str)]
    inter = [_dot(q_h[i], caugs[i].astype(BF16)) for i in range(n_str)]
    s_bf = [(scores[i] * jnp.exp(jnp.where(causal, ib_row[i] - mm_rep[i], -jnp.inf))).astype(BF16)
            for i in range(n_str)]
    kw = [(k_h[i].astype(F32) * ws_rep[i]).astype(BF16) for i in range(n_str)]
    intra = [_dot(s_bf[i], vaug[i]) for i in range(n_str)]
    upd = [_dot_tn(kw[i], vaug[i]) for i in range(n_str)]

    hh = []
    for i in range(n_str):
        num = w_inter[i] * inter[i][:, :dh] + intra[i][:, :dh]
        den = w_inter[i] * inter[i][:, dh:] + intra[i][:, dh:]
        hh.append(num / jnp.maximum(jnp.abs(den), enm[i]))
    mu = [rowsum(h) * inv_dh for h in hh]
    ctr = [hh[i] - mu[i] for i in range(n_str)]
    var = [rowsum(c * c) * inv_dh for c in ctr]
    outs = []
    for i, (b, hd) in enumerate(streams):
        sl = slice(hd * dh, (hd + 1) * dh)
        hn = ctr[i] * lax.rsqrt(var[i] + LN_EPS) * ng_ref[:, sl]
        outs.append(hn * _sigmoid(o_ref[b][:, sl].astype(F32)))

    for b in range(bsz):
        tail_scr[b] = new_tails[b]
        y_ref[b] = jnp.concatenate(outs[b * H:(b + 1) * H], axis=1).astype(BF16)
    for i in range(n_str):
        c_scr[i] = decays[i] * caugs[i] + upd[i]


def _mlstm(qk, v, o, gt, conv_w, conv_b, norm_g, bsz, seq):
    width = v.shape[-1]
    L = min(ML_CHUNK, seq)
    dh = width // ML_HEADS
    qk3 = qk.reshape(bsz, seq, 2 * width)
    v3 = v.reshape(bsz, seq, width)
    o3 = o.reshape(bsz, seq, width)
    gate_rows = _gates(gt, bsz, seq, L)
    full = lambda c: (0, 0)
    return pl.pallas_call(
        functools.partial(_mlstm_kernel, bsz=bsz, chunk=L, width=width),
        grid=(seq // L,),
        in_specs=[
            pl.BlockSpec((bsz, L, 2 * width), lambda c: (0, c, 0)),
            pl.BlockSpec((bsz, L, width), lambda c: (0, c, 0)),
            pl.BlockSpec((bsz, L, width), lambda c: (0, c, 0)),
            pl.BlockSpec((6, bsz, SUBLANES, L), lambda c: (0, 0, 0, c)),
            pl.BlockSpec((CONV_WIDTH, 2 * width), full),
            pl.BlockSpec((1, 2 * width), full),
            pl.BlockSpec((1, width), full),
        ],
        out_specs=pl.BlockSpec((bsz, L, width), lambda c: (0, c, 0)),
        out_shape=jax.ShapeDtypeStruct((bsz, seq, width), BF16),
        scratch_shapes=[
            pltpu.VMEM((bsz, SUBLANES, 2 * width), F32),
            pltpu.VMEM((bsz * ML_HEADS, dh, 2 * dh), F32),
        ],
        compiler_params=pltpu.CompilerParams(
            dimension_semantics=("arbitrary",), vmem_limit_bytes=VMEM_LIMIT),
        name="mlstm",
    )(qk3, v3, o3, gate_rows, conv_w.astype(F32), conv_b.reshape(1, -1).astype(F32),
      norm_g.reshape(1, -1).astype(F32))


def _post_kernel(x_ref, gt_ref, yml_ref, wglut_ref, bglu_ref, wout_ref, nffn_ref, wrt_ref, br_ref,
                 h2_ref, xn_ref, eidx_ref, gate_ref, rank_ref, tcnt_ref, carry_scr):
    step = pl.program_id(0)

    @pl.when(step == 0)
    def _():
        carry_scr[...] = jnp.zeros_like(carry_scr)

    gt = jnp.concatenate([gt_ref[0, j] for j in range(gt_ref.shape[1])], axis=1).astype(BF16)
    s5w = gt.shape[0]
    zt = _dot(wglut_ref[...], gt) + bglu_ref[...]
    s5t = (gt.astype(F32) * _sigmoid(zt)).astype(BF16)
    h2 = (x_ref[...] + _dot_tn(s5t, wout_ref[:s5w, :])
          + _dot(yml_ref[...], wout_ref[s5w:, :]))
    h2_ref[...] = h2
    ms = jnp.mean(h2 * h2, axis=-1, keepdims=True)
    xn = h2 * lax.rsqrt(ms + RMS_EPS) * nffn_ref[...]
    xb = xn.astype(BF16)
    xn_ref[...] = xb

    tm = xn.shape[0]
    logits = _dot_nt(wrt_ref[...], xb) + br_ref[...]
    eio = lax.broadcasted_iota(I32, (N_EXPERTS, tm), 0).astype(F32)
    vals = logits
    onehot = jnp.zeros((N_EXPERTS, tm), F32)
    idxs, tops = [], []
    for _ in range(TOP_K):
        mx = jnp.max(vals, axis=0, keepdims=True)
        idx = jnp.min(jnp.where(vals == mx, eio, float(N_EXPERTS)), axis=0, keepdims=True)
        sel = eio == idx
        onehot = onehot + sel.astype(F32)
        vals = jnp.where(sel, -jnp.inf, vals)
        idxs.append(idx)
        tops.append(mx)
    exps = [jnp.exp(t - tops[0]) for t in tops]
    tot = exps[0] + exps[1] + exps[2] + exps[3]
    pad_f = jnp.zeros((SUBLANES - TOP_K, tm), F32)
    eidx_ref[...] = jnp.concatenate(idxs + [pad_f], axis=0).astype(I32)
    gate_ref[...] = jnp.concatenate([e / tot for e in exps] + [pad_f], axis=0)

    rr = lax.broadcasted_iota(I32, (tm, tm), 0)
    cc = lax.broadcasted_iota(I32, (tm, tm), 1)
    before = (rr < cc).astype(BF16)
    carry = carry_scr[:, 0:1]
    rank_ex = _dot(onehot.astype(BF16), before) + carry
    ranks = [jnp.sum(jnp.where(eio == i, rank_ex, 0.0), axis=0, keepdims=True) for i in idxs]
    rank_ref[...] = jnp.concatenate(ranks + [pad_f], axis=0).astype(I32)
    tile_cnt = jnp.sum(onehot, axis=1, keepdims=True)
    carry_scr[...] = jnp.broadcast_to(carry + tile_cnt, carry_scr.shape)
    for s in range(tm // DISPATCH_TILE):
        sub = onehot[:, s * DISPATCH_TILE:(s + 1) * DISPATCH_TILE]
        tcnt_ref[0, s] = jnp.broadcast_to(jnp.sum(sub, axis=1, keepdims=True), carry_scr.shape)


def _post(x2d, g_t, y_ml, w_glu, b_glu, w_out, norm_g, w_router, b_router):
    n_tok, d = x2d.shape
    bsz, nc, s5w, chunk = g_t.shape
    seq = nc * chunk
    mlw = y_ml.shape[-1]
    tm = min(PROJ_TILE, seq)
    tpb = seq // tm
    full = lambda i: (0, 0)
    row = lambda i: (i, 0)
    colb = lambda i: (0, i)
    return pl.pallas_call(
        _post_kernel,
        grid=(n_tok // tm,),
        in_specs=[
            pl.BlockSpec((tm, d), row),
            pl.BlockSpec((1, tm // chunk, s5w, chunk), lambda i: (i // tpb, i % tpb, 0, 0)),
            pl.BlockSpec((tm, mlw), row),
            pl.BlockSpec((s5w, s5w), full),
            pl.BlockSpec((s5w, 1), full),
            pl.BlockSpec((d, d), full),
            pl.BlockSpec((1, d), full),
            pl.BlockSpec((N_EXPERTS, d), full),
            pl.BlockSpec((N_EXPERTS, 1), full),
        ],
        out_specs=[
            pl.BlockSpec((tm, d), row),
            pl.BlockSpec((tm, d), row),
            pl.BlockSpec((SUBLANES, tm), colb),
            pl.BlockSpec((SUBLANES, tm), colb),
            pl.BlockSpec((SUBLANES, tm), colb),
            pl.BlockSpec((1, tm // DISPATCH_TILE, N_EXPERTS, LANES), lambda i: (i, 0, 0, 0)),
        ],
        out_shape=[
            jax.ShapeDtypeStruct((n_tok, d), F32),
            jax.ShapeDtypeStruct((n_tok, d), BF16),
            jax.ShapeDtypeStruct((SUBLANES, n_tok), I32),
            jax.ShapeDtypeStruct((SUBLANES, n_tok), F32),
            jax.ShapeDtypeStruct((SUBLANES, n_tok), I32),
            jax.ShapeDtypeStruct((n_tok // tm, tm // DISPATCH_TILE, N_EXPERTS, LANES), F32),
        ],
        scratch_shapes=[pltpu.VMEM((N_EXPERTS, LANES), F32)],
        compiler_params=pltpu.CompilerParams(
            dimension_semantics=("arbitrary",), vmem_limit_bytes=VMEM_LIMIT),
        name="post_router",
    )(x2d, g_t, y_ml, w_glu.T.astype(BF16), b_glu.reshape(-1, 1).astype(F32), w_out.astype(BF16),
      norm_g.reshape(1, -1).astype(F32), w_router.T.astype(BF16),
      b_router.reshape(-1, 1).astype(F32))


def _slab_loop(base, nbig_ref, nsmall_ref, fn, n_slabs=N_EXPERTS):
    def per_expert(e, carry):
        idx = base + e
        nbig = nbig_ref[idx]

        def big(j, c2):
            fn(idx, j * BIG_CHUNK, BIG_CHUNK)
            return c2

        def small(j, c2):
            fn(idx, nbig * BIG_CHUNK + j * SLAB_ALIGN, SLAB_ALIGN)
            return c2

        lax.fori_loop(0, nbig, big, 0)
        lax.fori_loop(0, nsmall_ref[idx], small, 0)
        return carry

    lax.fori_loop(0, n_slabs, per_expert, 0)


def _plan_copies(step, plan, fn):
    nbig_ref, nsmall_ref, _, bsrc_ref, bdst_ref, ssrc_ref, sdst_ref = plan

    def big(j, carry):
        k = step * MAX_BIG + j
        fn(bsrc_ref[k], bdst_ref[k], BIG_CHUNK)
        return carry

    def small(j, carry):
        k = step * MAX_SMALL + j
        fn(ssrc_ref[k], sdst_ref[k], SLAB_ALIGN)
        return carry

    lax.fori_loop(0, nbig_ref[step], big, 0)
    lax.fori_loop(0, nsmall_ref[step], small, 0)


def _plan_wait(step, plan, copy):
    copy(0, 0, DISPATCH_TILE * TOP_K).wait()

    def one(j, carry):
        copy(0, 0, SLAB_ALIGN).wait()
        return carry

    lax.fori_loop(0, plan[2][step], one, 0)


def _unpack_pairs(words):
    lo = lax.bitcast_convert_type(words << 16, F32).astype(BF16)
    hi = lax.bitcast_convert_type(words & jnp.uint32(0xFFFF0000), F32).astype(BF16)
    return lo, hi


def _pack_pairs(lo_f32, hi_f32):
    lo = lax.bitcast_convert_type(lo_f32, U32) >> 16
    hi = lax.bitcast_convert_type(hi_f32, U32) & jnp.uint32(0xFFFF0000)
    return hi | lo


def _dispatch_kernel(*refs, n_steps):
    plan = refs[:7]
    znbig_ref, znsmall_ref, zrow_ref, xn_ref, pos_ref, xs_hbm, stage, zbuf, sem, zsem = refs[7:]
    i = pl.program_id(0)
    slot = i % 2
    tw, d = xn_ref.shape
    n_rows = stage.shape[1]
    half = d // 2

    def copy(s, stage_row, buf_row, rows):
        src = stage.at[s, pl.ds(pl.multiple_of(stage_row, SLAB_ALIGN), rows), :]
        dst = xs_hbm.at[pl.ds(pl.multiple_of(buf_row, SLAB_ALIGN), rows), :]
        return pltpu.make_async_copy(src, dst, sem)

    def tail(idx, r0, rows):
        dst = xs_hbm.at[pl.ds(pl.multiple_of(zrow_ref[idx] + r0, SLAB_ALIGN), rows), :]
        return pltpu.make_async_copy(zbuf.at[pl.ds(0, rows), :], dst, zsem)

    @pl.when(i == 0)
    def _():
        zbuf[...] = jnp.zeros_like(zbuf)
        _slab_loop(0, znbig_ref, znsmall_ref, lambda *a: tail(*a).start(), N_EXPERTS + 1)

    xt = xn_ref[...]
    blk = STAGE_BLOCK
    for r0 in range(0, n_rows, blk):
        rid = lax.broadcasted_iota(I32, (blk, tw), 0) + r0
        hit = rid == pos_ref[0:1, :]
        for k in range(1, TOP_K):
            hit = jnp.logical_or(hit, rid == pos_ref[k:k + 1, :])
        srt = _dot(jnp.where(hit, 1.0, 0.0).astype(BF16), xt)
        stage[slot, r0:r0 + blk, :] = _pack_pairs(srt[:, :half], srt[:, half:])

    @pl.when(i > 0)
    def _():
        _plan_wait(i - 1, plan, functools.partial(copy, 1 - slot))

    _plan_copies(i, plan, lambda *a: copy(slot, *a).start())

    @pl.when(i == n_steps - 1)
    def _():
        _plan_wait(i, plan, functools.partial(copy, slot))
        _slab_loop(0, znbig_ref, znsmall_ref, lambda *a: tail(*a).wait(), N_EXPERTS + 1)


def _dispatch(xn, pos8, plan, tails, n_slots):
    n_tok, d = xn.shape
    tw = DISPATCH_TILE
    n_steps = n_tok // tw
    return pl.pallas_call(
        functools.partial(_dispatch_kernel, n_steps=n_steps),
        grid_spec=pltpu.PrefetchScalarGridSpec(
            num_scalar_prefetch=10,
            grid=(n_steps,),
            in_specs=[
                pl.BlockSpec((tw, d), lambda i, *_: (i, 0)),
                pl.BlockSpec((SUBLANES, tw), lambda i, *_: (0, i)),
            ],
            out_specs=pl.BlockSpec(memory_space=pl.ANY),
            scratch_shapes=[
                pltpu.VMEM((2, STAGE_ROWS, d // 2), U32),
                pltpu.VMEM((BIG_CHUNK, d // 2), U32),
                pltpu.SemaphoreType.DMA(()),
                pltpu.SemaphoreType.DMA(()),
            ],
        ),
        out_shape=jax.ShapeDtypeStruct((n_slots, d // 2), U32),
        compiler_params=pltpu.CompilerParams(
            dimension_semantics=("arbitrary",), vmem_limit_bytes=VMEM_LIMIT),
        name="dispatch",
    )(*plan, *tails, xn, pos8)


def _expert_kernel(te_ref, nu_ref, first_ref, nxt_ref, par_ref, x_ref, wup_hbm, bup_ref, wdn_hbm,
                   bdn_ref, y_ref, wup_f32, wdn_f32, wup_bf, wdn_bf, sem, *, d_ff):
    i = pl.program_id(0)
    n_used = nu_ref[0]

    def fetch(e, s):
        return (pltpu.make_async_copy(wup_hbm.at[e], wup_f32.at[s], sem.at[s, 0]),
                pltpu.make_async_copy(wdn_hbm.at[e], wdn_f32.at[s], sem.at[s, 1]))

    @pl.when(jnp.logical_and(i < n_used, first_ref[i] == 1))
    def _():
        s = par_ref[i]

        @pl.when(i == 0)
        def _():
            for c in fetch(te_ref[i], s):
                c.start()

        for c in fetch(te_ref[i], s):
            c.wait()
        wup_bf[...] = wup_f32[s].astype(BF16)
        wdn_bf[...] = wdn_f32[s].astype(BF16)

        @pl.when(nxt_ref[i] >= 0)
        def _():
            for c in fetch(nxt_ref[i], 1 - s):
                c.start()

    @pl.when(i < n_used)
    def _():
        lo, hi = _unpack_pairs(x_ref[...])
        half = lo.shape[1]
        h = _dot(lo, wup_bf[:half, :]) + _dot(hi, wup_bf[half:, :]) + bup_ref[0]
        gl = jnp.minimum(h[:, :d_ff], SWIGLU_LIMIT)
        lin = jnp.clip(h[:, d_ff:], -SWIGLU_LIMIT, SWIGLU_LIMIT)
        act = gl * _sigmoid(SWIGLU_ALPHA * gl) * (lin + 1.0)
        y = _dot(act.astype(BF16), wdn_bf[...]) + bdn_ref[0]
        yb = y.astype(BF16).astype(F32)
        y_ref[...] = _pack_pairs(yb[:, :half], yb[:, half:])

    @pl.when(i >= n_used)
    def _():
        y_ref[...] = jnp.zeros_like(y_ref)


def _experts(xs, tile_e, n_used, first, nxt, parity, w_up, b_up, w_down, b_down):
    n_slots, half = xs.shape
    d = 2 * half
    tm = EXPERT_TILE
    n_tiles = n_slots // tm
    d_ff = w_down.shape[1]
    return pl.pallas_call(
        functools.partial(_expert_kernel, d_ff=d_ff),
        grid_spec=pltpu.PrefetchScalarGridSpec(
            num_scalar_prefetch=5,
            grid=(n_tiles,),
            in_specs=[
                pl.BlockSpec((tm, half), lambda i, te, nu, *_: (jnp.minimum(i, nu[0] - 1), 0)),
                pl.BlockSpec(memory_space=pl.ANY),
                pl.BlockSpec((1, 1, 2 * d_ff), lambda i, te, *_: (te[i], 0, 0)),
                pl.BlockSpec(memory_space=pl.ANY),
                pl.BlockSpec((1, 1, d), lambda i, te, *_: (te[i], 0, 0)),
            ],
            out_specs=pl.BlockSpec((tm, half), lambda i, *_: (i, 0)),
            scratch_shapes=[
                pltpu.VMEM((2, d, 2 * d_ff), F32),
                pltpu.VMEM((2, d_ff, d), F32),
                pltpu.VMEM((d, 2 * d_ff), BF16),
                pltpu.VMEM((d_ff, d), BF16),
                pltpu.SemaphoreType.DMA((2, 2)),
            ],
        ),
        out_shape=jax.ShapeDtypeStruct((n_slots, half), U32),
        compiler_params=pltpu.CompilerParams(
            dimension_semantics=("arbitrary",), vmem_limit_bytes=VMEM_LIMIT),
        name="experts",
    )(tile_e, n_used, first, nxt, parity, xs, w_up, b_up.reshape(N_EXPERTS, 1, -1),
      w_down, b_down.reshape(N_EXPERTS, 1, -1))


def _combine_kernel(*refs, n_steps):
    plan = refs[:7]
    ys_hbm, h2_ref, posc_ref, gatec_ref, ng_ref, out_ref, stage, sem = refs[7:]
    i = pl.program_id(0)
    slot = i % 2
    tw, d = h2_ref.shape
    n_rows = stage.shape[1]

    def copy(s, stage_row, buf_row, rows):
        src = ys_hbm.at[pl.ds(pl.multiple_of(buf_row, SLAB_ALIGN), rows), :]
        dst = stage.at[s, pl.ds(pl.multiple_of(stage_row, SLAB_ALIGN), rows), :]
        return pltpu.make_async_copy(src, dst, sem.at[s])

    @pl.when(i == 0)
    def _():
        stage[...] = jnp.zeros_like(stage)
        _plan_copies(0, plan, lambda *a: copy(0, *a).start())

    @pl.when(i + 1 < n_steps)
    def _():
        _plan_copies(i + 1, plan, lambda *a: copy(1 - slot, *a).start())

    _plan_wait(i, plan, functools.partial(copy, slot))

    y_lo, y_hi = _unpack_pairs(stage[slot])
    blk = 128
    for t0 in range(0, tw, blk):
        cid = lax.broadcasted_iota(I32, (blk, n_rows), 1)
        pg = jnp.zeros((blk, n_rows), F32)
        for k in range(TOP_K):
            pg = jnp.where(cid == posc_ref[t0:t0 + blk, k:k + 1],
                           gatec_ref[t0:t0 + blk, k:k + 1], pg)
        pgb = pg.astype(BF16)
        moe = jnp.concatenate([_dot(pgb, y_lo), _dot(pgb, y_hi)], axis=1)
        acc = h2_ref[t0:t0 + blk, :] + moe
        ms = jnp.mean(acc * acc, axis=-1, keepdims=True)
        out_ref[t0:t0 + blk, :] = acc * lax.rsqrt(ms + RMS_EPS) * ng_ref[...]


def _combine(ys, h2, pos_cols, gate_cols, plan, norm_g):
    n_tok, d = h2.shape
    tw = DISPATCH_TILE
    n_steps = n_tok // tw
    return pl.pallas_call(
        functools.partial(_combine_kernel, n_steps=n_steps),
        grid_spec=pltpu.PrefetchScalarGridSpec(
            num_scalar_prefetch=7,
            grid=(n_steps,),
            in_specs=[
                pl.BlockSpec(memory_space=pl.ANY),
                pl.BlockSpec((tw, d), lambda i, *_: (i, 0)),
                pl.BlockSpec((tw, SUBLANES), lambda i, *_: (i, 0)),
                pl.BlockSpec((tw, SUBLANES), lambda i, *_: (i, 0)),
                pl.BlockSpec((1, d), lambda i, *_: (0, 0)),
            ],
            out_specs=pl.BlockSpec((tw, d), lambda i, *_: (i, 0)),
            scratch_shapes=[
                pltpu.VMEM((2, STAGE_ROWS, d // 2), U32),
                pltpu.SemaphoreType.DMA((2,)),
            ],
        ),
        out_shape=jax.ShapeDtypeStruct((n_tok, d), F32),
        compiler_params=pltpu.CompilerParams(
            dimension_semantics=("arbitrary",), vmem_limit_bytes=VMEM_LIMIT),
        name="combine",
    )(*plan, ys, h2, pos_cols, gate_cols, norm_g.reshape(1, -1).astype(F32))


def _moe(h2, xn, eidx, gate, rank, tcnt, w_up, b_up, w_down, b_down, norm_final_g):
    n_tok, d = h2.shape
    tm = EXPERT_TILE
    tw = DISPATCH_TILE
    n_steps = n_tok // tw
    n_exp = N_EXPERTS
    e_ids = jnp.arange(n_exp, dtype=I32)
    al = SLAB_ALIGN
    tile_cnt = tcnt.reshape(n_steps, n_exp, LANES)[:, :, 0].astype(I32)
    slab_rows = (tile_cnt + al - 1) // al * al
    exp_rows = jnp.sum(slab_rows, axis=0)
    region = (exp_rows + tm - 1) // tm * tm
    pad_end = jnp.cumsum(region)
    pad_start = pad_end - region
    n_slots = (n_tok * TOP_K + n_steps * n_exp * (al - 1) + n_exp * (tm - 1) + tm - 1) // tm * tm
    n_tiles = n_slots // tm

    tile_carry = jnp.cumsum(tile_cnt, axis=0) - tile_cnt
    tile_off = jnp.cumsum(slab_rows, axis=1) - slab_rows
    slab_row0 = pad_start[None, :] + jnp.cumsum(slab_rows, axis=0) - slab_rows

    def chunks(rows):
        return rows // BIG_CHUNK, rows % BIG_CHUNK // al

    def flat(cnt, src0, dst0, size, n_max):
        run = jnp.cumsum(cnt, axis=1)
        j = jnp.arange(n_max, dtype=I32)
        e_of = jnp.sum(run[:, None, :] <= j[None, :, None], axis=-1)
        sel = e_of[:, :, None] == e_ids[None, None, :]
        pick = lambda a: jnp.sum(jnp.where(sel, a[:, None, :], 0), axis=-1)
        local = (j[None, :] - pick(run - cnt)) * size
        return run[:, -1], (pick(src0) + local).reshape(-1), (pick(dst0) + local).reshape(-1)

    nbig, nsmall = chunks(slab_rows)
    big_tot, big_src, big_dst = flat(nbig, tile_off, slab_row0, BIG_CHUNK, MAX_BIG)
    small_tot, small_src, small_dst = flat(nsmall, tile_off + nbig * BIG_CHUNK,
                                           slab_row0 + nbig * BIG_CHUNK, al, MAX_SMALL)
    pad_groups = (jnp.sum(slab_rows, axis=1) - tw * TOP_K) // al
    plan = (big_tot, small_tot, pad_groups, big_src, big_dst, small_src, small_dst)
    tail_rows = jnp.concatenate([region - exp_rows, n_slots - pad_end[-1:]])
    tails = (*chunks(tail_rows), jnp.concatenate([pad_start + exp_rows, pad_end[-1:]]))

    base_t = jnp.repeat((tile_off - tile_carry).T, tw, axis=1)
    e_sel = eidx[:TOP_K][None] == e_ids[:, None, None]
    pos = jnp.sum(jnp.where(e_sel, base_t[:, None, :], 0), axis=0) + rank[:TOP_K]
    pos8 = jnp.concatenate([pos, jnp.full((SUBLANES - TOP_K, n_tok), -1, I32)], axis=0)

    tile_start = jnp.arange(n_tiles, dtype=I32) * tm
    tile_e = jnp.minimum(jnp.sum(pad_end[None, :] <= tile_start[:, None], axis=-1),
                         n_exp - 1).astype(I32)
    n_used = (pad_end[-1] // tm).astype(I32).reshape(1)
    t_ids = jnp.arange(n_tiles, dtype=I32)
    first = jnp.logical_and(jnp.logical_or(t_ids == 0, tile_e != jnp.roll(tile_e, 1)),
                            t_ids < n_used[0])
    later = jnp.logical_and(first[None, :], t_ids[None, :] > t_ids[:, None])
    nxt_tile = jnp.min(jnp.where(later, t_ids[None, :], n_tiles), axis=1)
    nxt_e = jnp.sum(jnp.where(t_ids[None, :] == nxt_tile[:, None], tile_e[None, :], 0), axis=1)
    nxt = jnp.where(nxt_tile < n_tiles, nxt_e, -1).astype(I32)
    parity = ((jnp.cumsum(first.astype(I32)) - 1) % 2).astype(I32)
    first = first.astype(I32)

    xs = _dispatch(xn, pos8, plan, tails, n_slots)
    ys = _experts(xs, tile_e, n_used, first, nxt, parity, w_up, b_up, w_down, b_down)
    return _combine(ys, h2, pos8.T, gate.T, plan, norm_final_g)


def kernel(x, norm_mix_g, w_in, s5_log_dt, s5_a_re, s5_a_im, s5_b_re, s5_b_im, s5_c_re, s5_c_im,
           s5_d, s5_w_glu, s5_b_glu, ml_conv_w, ml_conv_b, ml_b_gates, ml_norm_g, w_out,
           norm_ffn_g, w_router, b_router, w_up, b_up, w_down, b_down, norm_final_g):
    bsz, seq, d = x.shape
    depth = w_in.shape[0]
    assert depth == 1, "single-layer block"
    l = 0
    x2d = x.reshape(bsz * seq, d)
    qk, v, o, ut, gt = _in_proj(x2d, norm_mix_g[l], w_in[l], ml_b_gates[l], bsz, seq)
    g_t = _s5(ut, s5_log_dt[l], s5_a_re[l], s5_a_im[l], s5_b_re[l], s5_b_im[l],
              s5_c_re[l], s5_c_im[l], s5_d[l])
    y_ml = _mlstm(qk, v, o, gt, ml_conv_w[l], ml_conv_b[l], ml_norm_g[l], bsz, seq)
    y_ml = y_ml.reshape(bsz * seq, -1)
    h2, xn, eidx, gate, rank, tcnt = _post(x2d, g_t, y_ml, s5_w_glu[l], s5_b_glu[l], w_out[l],
                                           norm_ffn_g[l], w_router[l], b_router[l])
    out = _moe(h2, xn, eidx, gate, rank, tcnt, w_up[l], b_up[l], w_down[l], b_down[l],
               norm_final_g)
    return out.reshape(bsz, seq, d)
```

```python
import functools
import math

import jax
import jax.numpy as jnp
from jax import lax
from jax.experimental import pallas as pl
from jax.experimental.pallas import tpu as pltpu

F32 = jnp.float32
BF16 = jnp.bfloat16
I32 = jnp.int32
U32 = jnp.uint32

S5_GROUP = 16
S5_STATE = 64
ML_HEADS = 4
CONV_WIDTH = 4
N_EXPERTS = 32
TOP_K = 4
SWIGLU_LIMIT = 7.0
SWIGLU_ALPHA = 1.702
RMS_EPS = 1e-5
LN_EPS = 1e-6

LANES = 128
SUBLANES = 8
S5_CHUNK = LANES
ML_CHUNK = 128
GATE_CHUNKS = 16
EXP_CAP = 1e38
PROJ_TILE = 1024
EXPERT_TILE = 512
DISPATCH_TILE = 256
SLAB_ALIGN = SUBLANES
BIG_CHUNK = 32
STAGE_ROWS = DISPATCH_TILE * TOP_K + N_EXPERTS * SLAB_ALIGN
STAGE_BLOCK = STAGE_ROWS // 2
MAX_BIG = STAGE_ROWS // BIG_CHUNK
MAX_SMALL = N_EXPERTS * (BIG_CHUNK // SLAB_ALIGN - 1)
VMEM_LIMIT = 56 * 1024 * 1024

_NT = (((1,), (1,)), ((), ()))
_TN = (((0,), (0,)), ((), ()))


def _dot(a, b):
    return jnp.dot(a, b, preferred_element_type=F32)


def _dot_nt(a, b):
    return lax.dot_general(a, b, _NT, preferred_element_type=F32)


def _dot_tn(a, b):
    return lax.dot_general(a, b, _TN, preferred_element_type=F32)


def _split3(x):
    p1 = x.astype(BF16)
    r1 = x - p1.astype(F32)
    p2 = r1.astype(BF16)
    r2 = r1 - p2.astype(F32)
    return p1, p2, r2.astype(BF16)


def _sigmoid(x):
    return 0.5 * jnp.tanh(0.5 * x) + 0.5


def _inproj_kernel(x_ref, g_ref, wnat_ref, wut_ref, wgt_ref, bg_ref,
                   qk_ref, v_ref, o_ref, ut_ref, gt_ref, *, width):
    x = x_ref[...]
    ms = jnp.mean(x * x, axis=-1, keepdims=True)
    hn = (x * lax.rsqrt(ms + RMS_EPS) * g_ref[...]).astype(BF16)
    nat = _dot(hn, wnat_ref[...])
    qk_ref[...] = nat[:, :2 * width].astype(BF16)
    v_ref[...] = nat[:, 2 * width:3 * width].astype(BF16)
    o_ref[...] = nat[:, 3 * width:].astype(BF16)
    ut_ref[0] = _dot_nt(wut_ref[...], hn).astype(BF16)
    gt_ref[0] = _dot_nt(wgt_ref[...], hn) + bg_ref[...]


def _in_proj(x2d, norm_g, w_in, b_gates, bsz, seq):
    n_tok, d = x2d.shape
    s5w = d // 2
    mlw = d - s5w
    tm = min(PROJ_TILE, seq)
    tpb = seq // tm
    w_bf = w_in.astype(BF16)
    w_nat = w_bf[:, s5w:s5w + 4 * mlw]
    w_ut = w_bf[:, :s5w].T
    n_gate = 2 * ML_HEADS
    w_gt = jnp.zeros((16, d), BF16).at[:n_gate].set(w_bf[:, s5w + 4 * mlw:].T)
    b_g = jnp.zeros((16, 1), F32).at[:n_gate, 0].set(b_gates.astype(F32))
    grid = (n_tok // tm,)
    full = lambda i: (0, 0)
    return pl.pallas_call(
        functools.partial(_inproj_kernel, width=mlw),
        grid=grid,
        in_specs=[
            pl.BlockSpec((tm, d), lambda i: (i, 0)),
            pl.BlockSpec((1, d), full),
            pl.BlockSpec((d, 4 * mlw), full),
            pl.BlockSpec((s5w, d), full),
            pl.BlockSpec((16, d), full),
            pl.BlockSpec((16, 1), full),
        ],
        out_specs=[
            pl.BlockSpec((tm, 2 * mlw), lambda i: (i, 0)),
            pl.BlockSpec((tm, mlw), lambda i: (i, 0)),
            pl.BlockSpec((tm, mlw), lambda i: (i, 0)),
            pl.BlockSpec((1, s5w, tm), lambda i: (i // tpb, 0, i % tpb)),
            pl.BlockSpec((1, 16, tm), lambda i: (i // tpb, 0, i % tpb)),
        ],
        out_shape=[
            jax.ShapeDtypeStruct((n_tok, 2 * mlw), BF16),
            jax.ShapeDtypeStruct((n_tok, mlw), BF16),
            jax.ShapeDtypeStruct((n_tok, mlw), BF16),
            jax.ShapeDtypeStruct((bsz, s5w, seq), BF16),
            jax.ShapeDtypeStruct((bsz, 16, seq), F32),
        ],
        compiler_params=pltpu.CompilerParams(
            dimension_semantics=("parallel",), vmem_limit_bytes=VMEM_LIMIT),
        name="in_proj",
    )(x2d, norm_g.reshape(1, d).astype(F32), w_nat, w_ut, w_gt, b_g)


def _s5_kernel(d_ref, x_ref, prm_ref, c_ref, bt_ref, out_ref, *, bsz, nc):
    L = S5_CHUNK
    P = S5_GROUP
    N = S5_STATE
    grp = pl.program_id(0)
    lane = lax.broadcasted_iota(I32, (1, 2 * N), 1)
    lo = lane < N
    a_re = jnp.minimum(prm_ref[0, 0:1, :], -1e-4)
    a_im = prm_ref[0, 1:2, :]
    dt = jnp.exp(prm_ref[0, 2:3, :])
    zr = dt * a_re
    zi = dt * a_im

    quarter = jnp.where(lo, 0.0, 0.5 * math.pi)

    def powrows(e):
        return jnp.exp(e * zr) * jnp.cos(e * zi - quarter)

    def swap(tab):
        return pltpu.roll(tab, N, 1)

    def cmul(tab, c):
        return c[0] * tab + jnp.where(lo, -c[1], c[1]) * swap(tab)

    n_dbl = max(nc - 1, 0).bit_length()
    exps = [1] + [SUBLANES << k for k in range(int(math.log2(L // SUBLANES)))] + [L << k for k in range(n_dbl)]
    e_col = jnp.concatenate([jnp.full((1, 1), float(e), F32) for e in exps]
                            + [jnp.zeros((-len(exps) % SUBLANES, 1), F32)], axis=0)
    mag = jnp.exp(e_col * zr)
    pw_r = mag * jnp.cos(e_col * zi)
    pw_i = mag * jnp.sin(e_col * zi)
    apow = {e: (pw_r[k:k + 1], pw_i[k:k + 1]) for k, e in enumerate(exps)}

    def powtab(descending):
        i8 = lax.broadcasted_iota(I32, (SUBLANES, 1), 0).astype(F32)
        tab = powrows(SUBLANES - 1.0 - i8 if descending else i8)
        rows = SUBLANES
        while rows < L:
            more = cmul(tab, apow[rows])
            tab = jnp.concatenate([more, tab] if descending else [tab, more], axis=0)
            rows *= 2
        return tab

    er, ei = apow[1]
    den = a_re * a_re + a_im * a_im
    coef_r = ((er - 1.0) * a_re + ei * a_im) / den
    coef_i = (ei * a_re - (er - 1.0) * a_im) / den
    c_r = c_ref[0, 0]
    c_i = c_ref[0, 1]
    bb_r = coef_r * bt_ref[0, 0] - coef_i * bt_ref[0, 1]
    bb_i = coef_r * bt_ref[0, 1] + coef_i * bt_ref[0, 0]

    cb_rows = []
    for q in range(P):
        cbr = c_r * bb_r[q:q + 1] - c_i * bb_i[q:q + 1]
        cbi = c_r * bb_i[q:q + 1] + c_i * bb_r[q:q + 1]
        cb_rows.append(jnp.where(lo, cbr, -cbi))
    cb = jnp.concatenate(cb_rows, axis=0)
    pt0 = powtab(False)
    c1, c2, c3 = _split3(cb)
    t1, t2, t3 = _split3(pt0)
    kmat = (_dot_nt(c1, t1) + _dot_nt(c1, t2) + _dot_nt(c2, t1)
            + _dot_nt(c2, t2) + _dot_nt(c1, t3) + _dot_nt(c3, t1))

    pt_rev = powtab(True)
    pt_rev_sw = swap(pt_rev)
    f_rows = []
    for q in range(P):
        a1 = bb_r[q:q + 1]
        a2 = jnp.where(lo, -bb_i[q:q + 1], bb_i[q:q + 1])
        f_rows.append((a1 * pt_rev + a2 * pt_rev_sw).astype(BF16))
    fmat = jnp.concatenate(f_rows, axis=0)
    pt1 = cmul(pt0, apow[1])
    pt1_sw = swap(pt1)
    e_rows = []
    for p in range(P):
        b1 = jnp.where(lo, c_r[p:p + 1], -c_r[p:p + 1])
        b2 = -c_i[p:p + 1]
        e_rows.append((b1 * pt1 + b2 * pt1_sw).astype(BF16))
    emat_t = jnp.concatenate(e_rows, axis=0)

    lhs = jnp.concatenate(
        [jnp.concatenate([x_ref[b, q] for q in range(P)], axis=1) for b in range(bsz)], axis=0)
    s_end = _dot(lhs, fmat)

    rr = lax.broadcasted_iota(I32, (L, L), 0)
    cc = lax.broadcasted_iota(I32, (L, L), 1)
    causal = cc >= rr
    y = None
    for q0 in range(0, P, 2):
        rows = []
        for q in (q0, q0 + 1):
            tiles = []
            for p in range(P):
                kb = jnp.broadcast_to(kmat[q * P + p:q * P + p + 1, :], (L, L))
                toe = pltpu.roll(kb, 0, 1, stride=1, stride_axis=0)
                tiles.append(jnp.where(causal, toe, 0.0).astype(BF16))
            rows.append(jnp.concatenate(tiles, axis=1))
        part = _dot(lhs[:, q0 * L:(q0 + 2) * L], jnp.concatenate(rows, axis=0))
        y = part if y is None else y + part

    m_rows = bsz * nc
    ridx = lax.broadcasted_iota(I32, (m_rows, 1), 0)
    cidx = ridx % nc
    h = jnp.where(cidx >= 1, pltpu.roll(s_end, 1, 0), 0.0)
    d = 1
    while d < nc:
        sh = jnp.where(cidx >= d, pltpu.roll(h, d, 0), 0.0)
        h = h + cmul(sh, apow[d * L])
        d *= 2
    y = y + _dot_nt(h.astype(BF16), emat_t)

    for b in range(bsz):
        for p in range(P):
            yp = (y[b * nc:(b + 1) * nc, p * L:(p + 1) * L]
                  + d_ref[grp * P + p] * x_ref[b, p].astype(F32))
            out_ref[b, :, p, :] = jax.nn.gelu(yp)


def _s5(ut, log_dt, a_re, a_im, b_re, b_im, c_re, c_im, d_skip):
    bsz, s5w, seq = ut.shape
    L = S5_CHUNK
    nc = seq // L
    groups = s5w // S5_GROUP
    n = S5_STATE
    x4 = ut.reshape(bsz, s5w, nc, L)
    dup = lambda t: jnp.concatenate([t, t], axis=-1).astype(F32)
    prm = jnp.zeros((groups, SUBLANES, 2 * n), F32)
    prm = prm.at[:, 0].set(dup(a_re)).at[:, 1].set(dup(a_im))
    prm = prm.at[:, 2].set(jnp.broadcast_to(log_dt.astype(F32)[:, None], (groups, 2 * n)))
    cpar = jnp.stack([dup(c_re), dup(c_im)], axis=1)
    btpar = jnp.stack([dup(jnp.swapaxes(b_re, 1, 2)), dup(jnp.swapaxes(b_im, 1, 2))], axis=1)
    out = pl.pallas_call(
        functools.partial(_s5_kernel, bsz=bsz, nc=nc),
        grid_spec=pltpu.PrefetchScalarGridSpec(
            num_scalar_prefetch=1,
            grid=(groups,),
            in_specs=[
                pl.BlockSpec((bsz, S5_GROUP, nc, L), lambda g, d: (0, g, 0, 0)),
                pl.BlockSpec((1, SUBLANES, 2 * n), lambda g, d: (g, 0, 0)),
                pl.BlockSpec((1, 2, S5_GROUP, 2 * n), lambda g, d: (g, 0, 0, 0)),
                pl.BlockSpec((1, 2, S5_GROUP, 2 * n), lambda g, d: (g, 0, 0, 0)),
            ],
            out_specs=pl.BlockSpec((bsz, nc, S5_GROUP, L), lambda g, d: (0, 0, g, 0)),
        ),
        out_shape=jax.ShapeDtypeStruct((bsz, nc, s5w, L), F32),
        compiler_params=pltpu.CompilerParams(
            dimension_semantics=("parallel",), vmem_limit_bytes=VMEM_LIMIT),
        name="s5",
    )(d_skip.astype(F32), x4, prm, cpar, btpar)
    return out


def _log_sigmoid(x):
    return jnp.minimum(x, 0.0) - jnp.log(1.0 + jnp.exp(-jnp.abs(x)))


def _gates_kernel(gt_ref, out_ref, m_scr, *, bsz, chunk, n_sub):
    L = chunk
    H = ML_HEADS
    step = pl.program_id(0)

    @pl.when(step == 0)
    def _():
        m_scr[...] = jnp.zeros_like(m_scr)

    rr = lax.broadcasted_iota(I32, (L, L), 0)
    cc = lax.broadcasted_iota(I32, (L, L), 1)
    utri = (rr <= cc).astype(BF16)
    grow = lax.broadcasted_iota(I32, (16, 1), 0)
    lane_row = lax.broadcasted_iota(I32, (SUBLANES, L), 1)
    m_prev = [m_scr[b][0:H, 0:1] for b in range(bsz)]
    pairs = [(j, b) for j in range(n_sub) for b in range(bsz)]

    irow, brow, cm = {}, {}, {}
    for j, b in pairs:
        g = gt_ref[b, :, j * L:(j + 1) * L]
        g2 = jnp.where(grow >= H, _log_sigmoid(g), g)
        p1, p2, p3 = _split3(g2)
        brow[j, b] = (_dot(p1, utri) + _dot(p2, utri) + _dot(p3, utri))[H:2 * H]
        irow[j, b] = g[0:H]
    for j, b in pairs:
        ib = irow[j, b] - brow[j, b]
        x = jnp.concatenate([ib, ib], axis=0)
        sft = 1
        while sft < L:
            x = jnp.maximum(x, jnp.where(lane_row >= sft, pltpu.roll(x, sft, 1), -jnp.inf))
            sft *= 2
        cm[j, b] = x[0:H]
    for j, b in pairs:
        mp = m_prev[b]
        mm = jnp.maximum(cm[j, b], mp)
        b_last = brow[j, b][:, L - 1:L]
        m_next = b_last + jnp.maximum(mp, cm[j, b][:, L - 1:L])
        planes = (mm, jnp.exp(mp - mm), jnp.minimum(jnp.exp(-(brow[j, b] + mm)), EXP_CAP),
                  jnp.exp(b_last - brow[j, b] + irow[j, b] - m_next), irow[j, b] - brow[j, b],
                  jnp.broadcast_to(jnp.exp(b_last + mp - m_next), (H, L)))
        for k, rows in enumerate(planes):
            out_ref[k, b, :, j * L:(j + 1) * L] = jnp.concatenate([rows, rows], axis=0)
        m_prev[b] = m_next
    for b in range(bsz):
        m_scr[b] = jnp.broadcast_to(jnp.concatenate([m_prev[b], m_prev[b]], axis=0),
                                    (SUBLANES, LANES))


def _gates(gt, bsz, seq, chunk):
    n_sub = min(GATE_CHUNKS, seq // chunk)
    blk = n_sub * chunk
    return pl.pallas_call(
        functools.partial(_gates_kernel, bsz=bsz, chunk=chunk, n_sub=n_sub),
        grid=(seq // blk,),
        in_specs=[pl.BlockSpec((bsz, 16, blk), lambda c: (0, 0, c))],
        out_specs=pl.BlockSpec((6, bsz, SUBLANES, blk), lambda c: (0, 0, 0, c)),
        out_shape=jax.ShapeDtypeStruct((6, bsz, SUBLANES, seq), F32),
        scratch_shapes=[pltpu.VMEM((bsz, SUBLANES, LANES), F32)],
        compiler_params=pltpu.CompilerParams(
            dimension_semantics=("arbitrary",), vmem_limit_bytes=VMEM_LIMIT),
        name="gates",
    )(gt)


def _mlstm_kernel(qk_ref, v_ref, o_ref, gr_ref, cw_ref, cb_ref, ng_ref, y_ref,
                  tail_scr, c_scr, *, bsz, chunk, width):
    L = chunk
    H = ML_HEADS
    dh = width // H
    step = pl.program_id(0)
    assert L == LANES and dh == LANES, "column replication below uses one 128x128 tile per head"

    @pl.when(step == 0)
    def _():
        tail_scr[...] = jnp.zeros_like(tail_scr)
        c_scr[...] = jnp.zeros_like(c_scr)

    rr = lax.broadcasted_iota(I32, (L, L), 0)
    cc = lax.broadcasted_iota(I32, (L, L), 1)
    causal = cc <= rr
    eye = (rr == cc).astype(BF16)
    ones_blk = jnp.ones((L, dh), BF16)
    scale = 1.0 / math.sqrt(dh)
    inv_dh = 1.0 / dh
    n_str = bsz * H
    streams = [(b, hd) for b in range(bsz) for hd in range(H)]

    def replicate(rows, two_terms):
        p = jnp.concatenate([jnp.broadcast_to(rows[h:h + 1], (LANES, L)) for h in range(H)], axis=0)
        p1 = p.astype(BF16)
        if not two_terms:
            return _dot_nt(eye, p1)
        return _dot_nt(eye, p1) + _dot_nt(eye, (p - p1.astype(F32)).astype(BF16))

    def rowsum(x):
        hi = x.astype(BF16)
        lo = (x - hi.astype(F32)).astype(BF16)
        return _dot(hi, ones_blk) + _dot(lo, ones_blk)

    tails = [tail_scr[b] for b in range(bsz)]
    caugs = [c_scr[i] for i in range(n_str)]
    reps = [[replicate(gr_ref[k, b][0:H], k == 0) for b in range(bsz)] for k in range(4)]
    col = lambda k: [reps[k][b][:, hd * LANES:(hd + 1) * LANES] for b, hd in streams]
    mm_rep, w_inter, enm, ws_rep = col(0), col(1), col(2), col(3)
    ib_row = [gr_ref[4, b][hd:hd + 1] for b, hd in streams]
    decays = [gr_ref[5, b][hd:hd + 1, 0:1] for b, hd in streams]

    q_all, k_all, new_tails = [], [], []
    for b in range(bsz):
        xqk = qk_ref[b].astype(F32)
        ext = jnp.concatenate([tails[b], xqk], axis=0)
        acc = jnp.broadcast_to(cb_ref[...], xqk.shape)
        for j in range(CONV_WIDTH):
            back = CONV_WIDTH - 1 - j
            sh = ext if back == 0 else pltpu.roll(ext, back, 0)
            acc = acc + cw_ref[j:j + 1, :] * sh[SUBLANES:]
        new_tails.append(xqk[L - SUBLANES:])
        qkc = acc * _sigmoid(acc)
        q_all.append(qkc[:, :width].astype(BF16))
        k_all.append((qkc[:, width:] * scale).astype(BF16))

    q_h = [q_all[b][:, hd * dh:(hd + 1) * dh] for b, hd in streams]
    k_h = [k_all[b][:, hd * dh:(hd + 1) * dh] for b, hd in streams]
    vaug = [jnp.concatenate([v_ref[b][:, hd * dh:(hd + 1) * dh], ones_blk], axis=1)
            for b, hd in streams]
    scores = [_dot_nt(q_h[i], k_h[i]) for i in range(n_str)]
    inter = [_dot(q_h[i], caugs[i].astype(BF16)) for i in range(n_str)]
    s_bf = [(scores[i] * jnp.exp(jnp.where(causal, ib_row[i] - mm_rep[i], -jnp.inf))).astype(BF16)
            for i in range(n_str)]
    kw = [(k_h[i].astype(F32) * ws_rep[i]).astype(BF16) for i in range(n_str)]
    intra = [_dot(s_bf[i], vaug[i]) for i in range(n_str)]
    upd = [_dot_tn(kw[i], vaug[i]) for i in range(n_str)]

    hh = []
    for i in range(n_str):
        num = w_inter[i] * inter[i][:, :dh] + intra[i][:, :dh]
        den = w_inter[i] * inter[i][:, dh:] + intra[i][:, dh:]
        hh.append(num / jnp.maximum(jnp.abs(den), enm[i]))
    mu = [rowsum(h) * inv_dh for h in hh]
    ctr = [hh[i] - mu[i] for i in range(n_str)]
    var = [rowsum(c * c) * inv_dh for c in ctr]
    outs = []
    for i, (b, hd) in enumerate(streams):
        sl = slice(hd * dh, (hd + 1) * dh)
        hn = ctr[i] * lax.rsqrt(var[i] + LN_EPS) * ng_ref[:, sl]
        outs.append(hn * _sigmoid(o_ref[b][:, sl].astype(F32)))

    for b in range(bsz):
        tail_scr[b] = new_tails[b]
        y_ref[b] = jnp.concatenate(outs[b * H:(b + 1) * H], axis=1).astype(BF16)
    for i in range(n_str):
        c_scr[i] = decays[i] * caugs[i] + upd[i]


def _mlstm(qk, v, o, gt, conv_w, conv_b, norm_g, bsz, seq):
    width = v.shape[-1]
    L = min(ML_CHUNK, seq)
    dh = width // ML_HEADS
    qk3 = qk.reshape(bsz, seq, 2 * width)
    v3 = v.reshape(bsz, seq, width)
    o3 = o.reshape(bsz, seq, width)
    gate_rows = _gates(gt, bsz, seq, L)
    full = lambda c: (0, 0)
    return pl.pallas_call(
        functools.partial(_mlstm_kernel, bsz=bsz, chunk=L, width=width),
        grid=(seq // L,),
        in_specs=[
            pl.BlockSpec((bsz, L, 2 * width), lambda c: (0, c, 0)),
            pl.BlockSpec((bsz, L, width), lambda c: (0, c, 0)),
            pl.BlockSpec((bsz, L, width), lambda c: (0, c, 0)),
            pl.BlockSpec((6, bsz, SUBLANES, L), lambda c: (0, 0, 0, c)),
            pl.BlockSpec((CONV_WIDTH, 2 * width), full),
            pl.BlockSpec((1, 2 * width), full),
            pl.BlockSpec((1, width), full),
        ],
        out_specs=pl.BlockSpec((bsz, L, width), lambda c: (0, c, 0)),
        out_shape=jax.ShapeDtypeStruct((bsz, seq, width), BF16),
        scratch_shapes=[
            pltpu.VMEM((bsz, SUBLANES, 2 * width), F32),
            pltpu.VMEM((bsz * ML_HEADS, dh, 2 * dh), F32),
        ],
        compiler_params=pltpu.CompilerParams(
            dimension_semantics=("arbitrary",), vmem_limit_bytes=VMEM_LIMIT),
        name="mlstm",
    )(qk3, v3, o3, gate_rows, conv_w.astype(F32), conv_b.reshape(1, -1).astype(F32),
      norm_g.reshape(1, -1).astype(F32))


def _post_kernel(x_ref, gt_ref, yml_ref, wglut_ref, bglu_ref, wout_ref, nffn_ref, wrt_ref, br_ref,
                 h2_ref, xn_ref, eidx_ref, gate_ref, rank_ref, tcnt_ref, carry_scr):
    step = pl.program_id(0)

    @pl.when(step == 0)
    def _():
        carry_scr[...] = jnp.zeros_like(carry_scr)

    gt = jnp.concatenate([gt_ref[0, j] for j in range(gt_ref.shape[1])], axis=1).astype(BF16)
    s5w = gt.shape[0]
    zt = _dot(wglut_ref[...], gt) + bglu_ref[...]
    s5t = (gt.astype(F32) * _sigmoid(zt)).astype(BF16)
    h2 = (x_ref[...] + _dot_tn(s5t, wout_ref[:s5w, :])
          + _dot(yml_ref[...], wout_ref[s5w:, :]))
    h2_ref[...] = h2
    ms = jnp.mean(h2 * h2, axis=-1, keepdims=True)
    xn = h2 * lax.rsqrt(ms + RMS_EPS) * nffn_ref[...]
    xb = xn.astype(BF16)
    xn_ref[...] = xb

    tm = xn.shape[0]
    logits = _dot_nt(wrt_ref[...], xb) + br_ref[...]
    eio = lax.broadcasted_iota(I32, (N_EXPERTS, tm), 0).astype(F32)
    vals = logits
    onehot = jnp.zeros((N_EXPERTS, tm), F32)
    idxs, tops = [], []
    for _ in range(TOP_K):
        mx = jnp.max(vals, axis=0, keepdims=True)
        idx = jnp.min(jnp.where(vals == mx, eio, float(N_EXPERTS)), axis=0, keepdims=True)
        sel = eio == idx
        onehot = onehot + sel.astype(F32)
        vals = jnp.where(sel, -jnp.inf, vals)
        idxs.append(idx)
        tops.append(mx)
    exps = [jnp.exp(t - tops[0]) for t in tops]
    tot = exps[0] + exps[1] + exps[2] + exps[3]
    pad_f = jnp.zeros((SUBLANES - TOP_K, tm), F32)
    eidx_ref[...] = jnp.concatenate(idxs + [pad_f], axis=0).astype(I32)
    gate_ref[...] = jnp.concatenate([e / tot for e in exps] + [pad_f], axis=0)

    rr = lax.broadcasted_iota(I32, (tm, tm), 0)
    cc = lax.broadcasted_iota(I32, (tm, tm), 1)
    before = (rr < cc).astype(BF16)
    carry = carry_scr[:, 0:1]
    rank_ex = _dot(onehot.astype(BF16), before) + carry
    ranks = [jnp.sum(jnp.where(eio == i, rank_ex, 0.0), axis=0, keepdims=True) for i in idxs]
    rank_ref[...] = jnp.concatenate(ranks + [pad_f - float(1 << 30)], axis=0).astype(I32)
    tile_cnt = jnp.sum(onehot, axis=1, keepdims=True)
    carry_scr[...] = jnp.broadcast_to(carry + tile_cnt, carry_scr.shape)
    for s in range(tm // DISPATCH_TILE):
        sub = onehot[:, s * DISPATCH_TILE:(s + 1) * DISPATCH_TILE]
        tcnt_ref[0, s] = jnp.broadcast_to(jnp.sum(sub, axis=1, keepdims=True), carry_scr.shape)


def _post(x2d, g_t, y_ml, w_glu, b_glu, w_out, norm_g, w_router, b_router):
    n_tok, d = x2d.shape
    bsz, nc, s5w, chunk = g_t.shape
    seq = nc * chunk
    mlw = y_ml.shape[-1]
    tm = min(PROJ_TILE, seq)
    tpb = seq // tm
    full = lambda i: (0, 0)
    row = lambda i: (i, 0)
    colb = lambda i: (0, i)
    return pl.pallas_call(
        _post_kernel,
        grid=(n_tok // tm,),
        in_specs=[
            pl.BlockSpec((tm, d), row),
            pl.BlockSpec((1, tm // chunk, s5w, chunk), lambda i: (i // tpb, i % tpb, 0, 0)),
            pl.BlockSpec((tm, mlw), row),
            pl.BlockSpec((s5w, s5w), full),
            pl.BlockSpec((s5w, 1), full),
            pl.BlockSpec((d, d), full),
            pl.BlockSpec((1, d), full),
            pl.BlockSpec((N_EXPERTS, d), full),
            pl.BlockSpec((N_EXPERTS, 1), full),
        ],
        out_specs=[
            pl.BlockSpec((tm, d), row),
            pl.BlockSpec((tm, d), row),
            pl.BlockSpec((SUBLANES, tm), colb),
            pl.BlockSpec((SUBLANES, tm), colb),
            pl.BlockSpec((SUBLANES, tm), colb),
            pl.BlockSpec((1, tm // DISPATCH_TILE, N_EXPERTS, LANES), lambda i: (i, 0, 0, 0)),
        ],
        out_shape=[
            jax.ShapeDtypeStruct((n_tok, d), F32),
            jax.ShapeDtypeStruct((n_tok, d), BF16),
            jax.ShapeDtypeStruct((SUBLANES, n_tok), I32),
            jax.ShapeDtypeStruct((SUBLANES, n_tok), F32),
            jax.ShapeDtypeStruct((SUBLANES, n_tok), I32),
            jax.ShapeDtypeStruct((n_tok // tm, tm // DISPATCH_TILE, N_EXPERTS, LANES), F32),
        ],
        scratch_shapes=[pltpu.VMEM((N_EXPERTS, LANES), F32)],
        compiler_params=pltpu.CompilerParams(
            dimension_semantics=("arbitrary",), vmem_limit_bytes=VMEM_LIMIT),
        name="post_router",
    )(x2d, g_t, y_ml, w_glu.T.astype(BF16), b_glu.reshape(-1, 1).astype(F32), w_out.astype(BF16),
      norm_g.reshape(1, -1).astype(F32), w_router.T.astype(BF16),
      b_router.reshape(-1, 1).astype(F32))


def _slab_loop(base, nbig_ref, nsmall_ref, fn, n_slabs=N_EXPERTS):
    def per_expert(e, carry):
        idx = base + e
        nbig = nbig_ref[idx]

        def big(j, c2):
            fn(idx, j * BIG_CHUNK, BIG_CHUNK)
            return c2

        def small(j, c2):
            fn(idx, nbig * BIG_CHUNK + j * SLAB_ALIGN, SLAB_ALIGN)
            return c2

        lax.fori_loop(0, nbig, big, 0)
        lax.fori_loop(0, nsmall_ref[idx], small, 0)
        return carry

    lax.fori_loop(0, n_slabs, per_expert, 0)


def _plan_copies(step, plan, fn):
    nbig_ref, nsmall_ref, _, bsrc_ref, bdst_ref, ssrc_ref, sdst_ref = plan

    def big(j, carry):
        k = step * MAX_BIG + j
        fn(bsrc_ref[k], bdst_ref[k], BIG_CHUNK)
        return carry

    def small(j, carry):
        k = step * MAX_SMALL + j
        fn(ssrc_ref[k], sdst_ref[k], SLAB_ALIGN)
        return carry

    lax.fori_loop(0, nbig_ref[step], big, 0)
    lax.fori_loop(0, nsmall_ref[step], small, 0)


def _plan_wait(step, plan, copy):
    copy(0, 0, DISPATCH_TILE * TOP_K).wait()

    def one(j, carry):
        copy(0, 0, SLAB_ALIGN).wait()
        return carry

    lax.fori_loop(0, plan[2][step], one, 0)


def _unpack_pairs(words):
    lo = lax.bitcast_convert_type(words << 16, F32).astype(BF16)
    hi = lax.bitcast_convert_type(words & jnp.uint32(0xFFFF0000), F32).astype(BF16)
    return lo, hi


def _pack_pairs(lo_f32, hi_f32):
    lo = lax.bitcast_convert_type(lo_f32, U32) >> 16
    hi = lax.bitcast_convert_type(hi_f32, U32) & jnp.uint32(0xFFFF0000)
    return hi | lo


def _dispatch_kernel(*refs, n_steps):
    plan = refs[:7]
    znbig_ref, znsmall_ref, zrow_ref, xn_ref, pos_ref, xs_hbm, stage, zbuf, sem, zsem = refs[7:]
    i = pl.program_id(0)
    slot = i % 2
    tw, d = xn_ref.shape
    n_rows = stage.shape[1]
    half = d // 2

    def copy(s, stage_row, buf_row, rows):
        src = stage.at[s, pl.ds(pl.multiple_of(stage_row, SLAB_ALIGN), rows), :]
        dst = xs_hbm.at[pl.ds(pl.multiple_of(buf_row, SLAB_ALIGN), rows), :]
        return pltpu.make_async_copy(src, dst, sem)

    @pl.when(i == 0)
    def _():
        zbuf[...] = jnp.zeros_like(zbuf)

        def tail(idx, r0, rows):
            dst = xs_hbm.at[pl.ds(pl.multiple_of(zrow_ref[idx] + r0, SLAB_ALIGN), rows), :]
            return pltpu.make_async_copy(zbuf.at[pl.ds(0, rows), :], dst, zsem)

        _slab_loop(0, znbig_ref, znsmall_ref, lambda *a: tail(*a).start(), N_EXPERTS + 1)
        _slab_loop(0, znbig_ref, znsmall_ref, lambda *a: tail(*a).wait(), N_EXPERTS + 1)

    xt = xn_ref[...]
    blk = STAGE_BLOCK
    for r0 in range(0, n_rows, blk):
        rid = lax.broadcasted_iota(I32, (blk, tw), 0) + r0
        hit = rid == pos_ref[0:1, :]
        for k in range(1, TOP_K):
            hit = jnp.logical_or(hit, rid == pos_ref[k:k + 1, :])
        srt = _dot(jnp.where(hit, 1.0, 0.0).astype(BF16), xt)
        stage[slot, r0:r0 + blk, :] = _pack_pairs(srt[:, :half], srt[:, half:])

    @pl.when(i > 0)
    def _():
        _plan_wait(i - 1, plan, functools.partial(copy, 1 - slot))

    _plan_copies(i, plan, lambda *a: copy(slot, *a).start())

    @pl.when(i == n_steps - 1)
    def _():
        _plan_wait(i, plan, functools.partial(copy, slot))


def _dispatch(xn, pos8, plan, tails, n_slots):
    n_tok, d = xn.shape
    tw = DISPATCH_TILE
    n_steps = n_tok // tw
    return pl.pallas_call(
        functools.partial(_dispatch_kernel, n_steps=n_steps),
        grid_spec=pltpu.PrefetchScalarGridSpec(
            num_scalar_prefetch=10,
            grid=(n_steps,),
            in_specs=[
                pl.BlockSpec((tw, d), lambda i, *_: (i, 0)),
                pl.BlockSpec((SUBLANES, tw), lambda i, *_: (0, i)),
            ],
            out_specs=pl.BlockSpec(memory_space=pl.ANY),
            scratch_shapes=[
                pltpu.VMEM((2, STAGE_ROWS, d // 2), U32),
                pltpu.VMEM((BIG_CHUNK, d // 2), U32),
                pltpu.SemaphoreType.DMA(()),
                pltpu.SemaphoreType.DMA(()),
            ],
        ),
        out_shape=jax.ShapeDtypeStruct((n_slots, d // 2), U32),
        compiler_params=pltpu.CompilerParams(
            dimension_semantics=("arbitrary",), vmem_limit_bytes=VMEM_LIMIT),
        name="dispatch",
    )(*plan, *tails, xn, pos8)


def _expert_kernel(te_ref, nu_ref, first_ref, nxt_ref, par_ref, x_ref, wup_hbm, bup_ref, wdn_hbm,
                   bdn_ref, y_ref, wup_f32, wdn_f32, wup_bf, wdn_bf, sem, *, d_ff):
    i = pl.program_id(0)
    n_used = nu_ref[0]

    def fetch(e, s):
        return (pltpu.make_async_copy(wup_hbm.at[e], wup_f32.at[s], sem.at[s, 0]),
                pltpu.make_async_copy(wdn_hbm.at[e], wdn_f32.at[s], sem.at[s, 1]))

    @pl.when(jnp.logical_and(i < n_used, first_ref[i] == 1))
    def _():
        s = par_ref[i]

        @pl.when(i == 0)
        def _():
            for c in fetch(te_ref[i], s):
                c.start()

        for c in fetch(te_ref[i], s):
            c.wait()
        wup_bf[...] = wup_f32[s].astype(BF16)
        wdn_bf[...] = wdn_f32[s].astype(BF16)

        @pl.when(nxt_ref[i] >= 0)
        def _():
            for c in fetch(nxt_ref[i], 1 - s):
                c.start()

    @pl.when(i < n_used)
    def _():
        lo, hi = _unpack_pairs(x_ref[...])
        half = lo.shape[1]
        h = _dot(lo, wup_bf[:half, :]) + _dot(hi, wup_bf[half:, :]) + bup_ref[0]
        gl = jnp.minimum(h[:, :d_ff], SWIGLU_LIMIT)
        lin = jnp.clip(h[:, d_ff:], -SWIGLU_LIMIT, SWIGLU_LIMIT)
        act = gl * _sigmoid(SWIGLU_ALPHA * gl) * (lin + 1.0)
        y = _dot(act.astype(BF16), wdn_bf[...]) + bdn_ref[0]
        yb = y.astype(BF16).astype(F32)
        y_ref[...] = _pack_pairs(yb[:, :half], yb[:, half:])

    @pl.when(i >= n_used)
    def _():
        y_ref[...] = jnp.zeros_like(y_ref)


def _experts(xs, tile_e, n_used, first, nxt, parity, w_up, b_up, w_down, b_down):
    n_slots, half = xs.shape
    d = 2 * half
    tm = EXPERT_TILE
    n_tiles = n_slots // tm
    d_ff = w_down.shape[1]
    return pl.pallas_call(
        functools.partial(_expert_kernel, d_ff=d_ff),
        grid_spec=pltpu.PrefetchScalarGridSpec(
            num_scalar_prefetch=5,
            grid=(n_tiles,),
            in_specs=[
                pl.BlockSpec((tm, half), lambda i, te, nu, *_: (jnp.minimum(i, nu[0] - 1), 0)),
                pl.BlockSpec(memory_space=pl.ANY),
                pl.BlockSpec((1, 1, 2 * d_ff), lambda i, te, *_: (te[i], 0, 0)),
                pl.BlockSpec(memory_space=pl.ANY),
                pl.BlockSpec((1, 1, d), lambda i, te, *_: (te[i], 0, 0)),
            ],
            out_specs=pl.BlockSpec((tm, half), lambda i, *_: (i, 0)),
            scratch_shapes=[
                pltpu.VMEM((2, d, 2 * d_ff), F32),
                pltpu.VMEM((2, d_ff, d), F32),
                pltpu.VMEM((d, 2 * d_ff), BF16),
                pltpu.VMEM((d_ff, d), BF16),
                pltpu.SemaphoreType.DMA((2, 2)),
            ],
        ),
        out_shape=jax.ShapeDtypeStruct((n_slots, half), U32),
        compiler_params=pltpu.CompilerParams(
            dimension_semantics=("arbitrary",), vmem_limit_bytes=VMEM_LIMIT),
        name="experts",
    )(tile_e, n_used, first, nxt, parity, xs, w_up, b_up.reshape(N_EXPERTS, 1, -1),
      w_down, b_down.reshape(N_EXPERTS, 1, -1))


def _combine_kernel(*refs, n_steps):
    plan = refs[:7]
    ys_hbm, h2_ref, posc_ref, gatec_ref, ng_ref, out_ref, stage, sem = refs[7:]
    i = pl.program_id(0)
    slot = i % 2
    tw, d = h2_ref.shape
    n_rows = stage.shape[1]

    def copy(s, stage_row, buf_row, rows):
        src = ys_hbm.at[pl.ds(pl.multiple_of(buf_row, SLAB_ALIGN), rows), :]
        dst = stage.at[s, pl.ds(pl.multiple_of(stage_row, SLAB_ALIGN), rows), :]
        return pltpu.make_async_copy(src, dst, sem.at[s])

    @pl.when(i == 0)
    def _():
        stage[...] = jnp.zeros_like(stage)
        _plan_copies(0, plan, lambda *a: copy(0, *a).start())

    @pl.when(i + 1 < n_steps)
    def _():
        _plan_copies(i + 1, plan, lambda *a: copy(1 - slot, *a).start())

    _plan_wait(i, plan, functools.partial(copy, slot))

    y_lo, y_hi = _unpack_pairs(stage[slot])
    blk = 128
    for t0 in range(0, tw, blk):
        cid = lax.broadcasted_iota(I32, (blk, n_rows), 1)
        pg = jnp.zeros((blk, n_rows), F32)
        for k in range(TOP_K):
            pg = jnp.where(cid == posc_ref[t0:t0 + blk, k:k + 1],
                           gatec_ref[t0:t0 + blk, k:k + 1], pg)
        pgb = pg.astype(BF16)
        moe = jnp.concatenate([_dot(pgb, y_lo), _dot(pgb, y_hi)], axis=1)
        acc = h2_ref[t0:t0 + blk, :] + moe
        ms = jnp.mean(acc * acc, axis=-1, keepdims=True)
        out_ref[t0:t0 + blk, :] = acc * lax.rsqrt(ms + RMS_EPS) * ng_ref[...]


def _combine(ys, h2, pos_cols, gate_cols, plan, norm_g):
    n_tok, d = h2.shape
    tw = DISPATCH_TILE
    n_steps = n_tok // tw
    return pl.pallas_call(
        functools.partial(_combine_kernel, n_steps=n_steps),
        grid_spec=pltpu.PrefetchScalarGridSpec(
            num_scalar_prefetch=7,
            grid=(n_steps,),
            in_specs=[
                pl.BlockSpec(memory_space=pl.ANY),
                pl.BlockSpec((tw, d), lambda i, *_: (i, 0)),
                pl.BlockSpec((tw, SUBLANES), lambda i, *_: (i, 0)),
                pl.BlockSpec((tw, SUBLANES), lambda i, *_: (i, 0)),
                pl.BlockSpec((1, d), lambda i, *_: (0, 0)),
            ],
            out_specs=pl.BlockSpec((tw, d), lambda i, *_: (i, 0)),
            scratch_shapes=[
                pltpu.VMEM((2, STAGE_ROWS, d // 2), U32),
                pltpu.SemaphoreType.DMA((2,)),
            ],
        ),
        out_shape=jax.ShapeDtypeStruct((n_tok, d), F32),
        compiler_params=pltpu.CompilerParams(
            dimension_semantics=("arbitrary",), vmem_limit_bytes=VMEM_LIMIT),
        name="combine",
    )(*plan, ys, h2, pos_cols, gate_cols, norm_g.reshape(1, -1).astype(F32))


def _moe(h2, xn, eidx, gate, rank, tcnt, w_up, b_up, w_down, b_down, norm_final_g):
    n_tok, d = h2.shape
    tm = EXPERT_TILE
    tw = DISPATCH_TILE
    n_steps = n_tok // tw
    n_exp = N_EXPERTS
    e_ids = jnp.arange(n_exp, dtype=I32)
    al = SLAB_ALIGN
    tile_cnt = tcnt.reshape(n_steps, n_exp, LANES)[:, :, 0].astype(I32)
    slab_rows = (tile_cnt + al - 1) // al * al
    exp_rows = jnp.sum(slab_rows, axis=0)
    region = (exp_rows + tm - 1) // tm * tm
    pad_end = jnp.cumsum(region)
    pad_start = pad_end - region
    n_slots = (n_tok * TOP_K + n_steps * n_exp * (al - 1) + n_exp * (tm - 1) + tm - 1) // tm * tm
    n_tiles = n_slots // tm

    tile_carry = jnp.cumsum(tile_cnt, axis=0) - tile_cnt
    tile_off = jnp.cumsum(slab_rows, axis=1) - slab_rows
    slab_row0 = pad_start[None, :] + jnp.cumsum(slab_rows, axis=0) - slab_rows

    def chunks(rows):
        return rows // BIG_CHUNK, rows % BIG_CHUNK // al

    def flat(cnt, src0, dst0, size, n_max):
        run = jnp.cumsum(cnt, axis=1)
        j = jnp.arange(n_max, dtype=I32)
        e_of = jnp.sum(run[:, None, :] <= j[None, :, None], axis=-1)
        sel = e_of[:, :, None] == e_ids[None, None, :]
        pick = lambda a: jnp.sum(jnp.where(sel, a[:, None, :], 0), axis=-1)
        local = (j[None, :] - pick(run - cnt)) * size
        return run[:, -1], (pick(src0) + local).reshape(-1), (pick(dst0) + local).reshape(-1)

    nbig, nsmall = chunks(slab_rows)
    big_tot, big_src, big_dst = flat(nbig, tile_off, slab_row0, BIG_CHUNK, MAX_BIG)
    small_tot, small_src, small_dst = flat(nsmall, tile_off + nbig * BIG_CHUNK,
                                           slab_row0 + nbig * BIG_CHUNK, al, MAX_SMALL)
    pad_groups = (jnp.sum(slab_rows, axis=1) - tw * TOP_K) // al
    plan = (big_tot, small_tot, pad_groups, big_src, big_dst, small_src, small_dst)
    tail_rows = jnp.concatenate([region - exp_rows, n_slots - pad_end[-1:]])
    tails = (*chunks(tail_rows), jnp.concatenate([pad_start + exp_rows, pad_end[-1:]]))

    base_t = jnp.repeat((tile_off - tile_carry).T, tw, axis=1)
    e_sel = eidx[None] == e_ids[:, None, None]
    pos8 = jnp.sum(jnp.where(e_sel, base_t[:, None, :], 0), axis=0) + rank

    tile_start = jnp.arange(n_tiles, dtype=I32) * tm
    tile_e = jnp.minimum(jnp.sum(pad_end[None, :] <= tile_start[:, None], axis=-1),
                         n_exp - 1).astype(I32)
    n_used = (pad_end[-1] // tm).astype(I32).reshape(1)
    t_ids = jnp.arange(n_tiles, dtype=I32)
    first = jnp.logical_and(jnp.logical_or(t_ids == 0, tile_e != jnp.roll(tile_e, 1)),
                            t_ids < n_used[0])
    later = jnp.logical_and(first[None, :], t_ids[None, :] > t_ids[:, None])
    nxt_tile = jnp.min(jnp.where(later, t_ids[None, :], n_tiles), axis=1)
    nxt_e = jnp.sum(jnp.where(t_ids[None, :] == nxt_tile[:, None], tile_e[None, :], 0), axis=1)
    nxt = jnp.where(nxt_tile < n_tiles, nxt_e, -1).astype(I32)
    parity = ((jnp.cumsum(first.astype(I32)) - 1) % 2).astype(I32)
    first = first.astype(I32)

    xs = _dispatch(xn, pos8, plan, tails, n_slots)
    ys = _experts(xs, tile_e, n_used, first, nxt, parity, w_up, b_up, w_down, b_down)
    return _combine(ys, h2, pos8.T, gate.T, plan, norm_final_g)


def kernel(x, norm_mix_g, w_in, s5_log_dt, s5_a_re, s5_a_im, s5_b_re, s5_b_im, s5_c_re, s5_c_im,
           s5_d, s5_w_glu, s5_b_glu, ml_conv_w, ml_conv_b, ml_b_gates, ml_norm_g, w_out,
           norm_ffn_g, w_router, b_router, w_up, b_up, w_down, b_down, norm_final_g):
    bsz, seq, d = x.shape
    depth = w_in.shape[0]
    assert depth == 1, "single-layer block"
    l = 0
    x2d = x.reshape(bsz * seq, d)
    qk, v, o, ut, gt = _in_proj(x2d, norm_mix_g[l], w_in[l], ml_b_gates[l], bsz, seq)
    g_t = _s5(ut, s5_log_dt[l], s5_a_re[l], s5_a_im[l], s5_b_re[l], s5_b_im[l],
              s5_c_re[l], s5_c_im[l], s5_d[l])
    y_ml = _mlstm(qk, v, o, gt, ml_conv_w[l], ml_conv_b[l], ml_norm_g[l], bsz, seq)
    y_ml = y_ml.reshape(bsz * seq, -1)
    h2, xn, eidx, gate, rank, tcnt = _post(x2d, g_t, y_ml, s5_w_glu[l], s5_b_glu[l], w_out[l],
                                           norm_ffn_g[l], w_router[l], b_router[l])
    out = _moe(h2, xn, eidx, gate, rank, tcnt, w_up[l], b_up[l], w_down[l], b_down[l],
               norm_final_g)
    return out.reshape(bsz, seq, d)
```

```python
import functools
import math

import jax
import jax.numpy as jnp
from jax import lax
from jax.experimental import pallas as pl
from jax.experimental.pallas import tpu as pltpu

F32 = jnp.float32
BF16 = jnp.bfloat16
I32 = jnp.int32
U32 = jnp.uint32

S5_GROUP = 16
S5_STATE = 64
ML_HEADS = 4
CONV_WIDTH = 4
N_EXPERTS = 32
TOP_K = 4
SWIGLU_LIMIT = 7.0
SWIGLU_ALPHA = 1.702
RMS_EPS = 1e-5
LN_EPS = 1e-6

LANES = 128
SUBLANES = 8
S5_CHUNK = LANES
ML_CHUNK = 128
GATE_CHUNKS = 16
EXP_CAP = 1e38
PROJ_TILE = 1024
EXPERT_TILE = 512
DISPATCH_TILE = 256
SLAB_ALIGN = SUBLANES
BIG_CHUNK = 32
STAGE_ROWS = DISPATCH_TILE * TOP_K + N_EXPERTS * SLAB_ALIGN
STAGE_BLOCK = STAGE_ROWS // 2
MAX_BIG = STAGE_ROWS // BIG_CHUNK
MAX_SMALL = N_EXPERTS * (BIG_CHUNK // SLAB_ALIGN - 1)
VMEM_LIMIT = 56 * 1024 * 1024

_NT = (((1,), (1,)), ((), ()))
_TN = (((0,), (0,)), ((), ()))


def _dot(a, b):
    return jnp.dot(a, b, preferred_element_type=F32)


def _dot_nt(a, b):
    return lax.dot_general(a, b, _NT, preferred_element_type=F32)


def _dot_tn(a, b):
    return lax.dot_general(a, b, _TN, preferred_element_type=F32)


def _split3(x):
    p1 = x.astype(BF16)
    r1 = x - p1.astype(F32)
    p2 = r1.astype(BF16)
    r2 = r1 - p2.astype(F32)
    return p1, p2, r2.astype(BF16)


def _sigmoid(x):
    return 0.5 * jnp.tanh(0.5 * x) + 0.5


def _inproj_kernel(x_ref, g_ref, wnat_ref, wut_ref, wgt_ref, bg_ref,
                   qk_ref, v_ref, o_ref, ut_ref, gt_ref, *, width):
    x = x_ref[...]
    ms = jnp.mean(x * x, axis=-1, keepdims=True)
    hn = (x * lax.rsqrt(ms + RMS_EPS) * g_ref[...]).astype(BF16)
    nat = _dot(hn, wnat_ref[...])
    qk_ref[...] = nat[:, :2 * width].astype(BF16)
    v_ref[...] = nat[:, 2 * width:3 * width].astype(BF16)
    o_ref[...] = nat[:, 3 * width:].astype(BF16)
    ut_ref[0] = _dot_nt(wut_ref[...], hn).astype(BF16)
    gt_ref[0] = _dot_nt(wgt_ref[...], hn) + bg_ref[...]


def _in_proj(x2d, norm_g, w_in, b_gates, bsz, seq):
    n_tok, d = x2d.shape
    s5w = d // 2
    mlw = d - s5w
    tm = min(PROJ_TILE, seq)
    tpb = seq // tm
    w_bf = w_in.astype(BF16)
    w_nat = w_bf[:, s5w:s5w + 4 * mlw]
    w_ut = w_bf[:, :s5w].T
    n_gate = 2 * ML_HEADS
    w_gt = jnp.zeros((16, d), BF16).at[:n_gate].set(w_bf[:, s5w + 4 * mlw:].T)
    b_g = jnp.zeros((16, 1), F32).at[:n_gate, 0].set(b_gates.astype(F32))
    grid = (n_tok // tm,)
    full = lambda i: (0, 0)
    return pl.pallas_call(
        functools.partial(_inproj_kernel, width=mlw),
        grid=grid,
        in_specs=[
            pl.BlockSpec((tm, d), lambda i: (i, 0)),
            pl.BlockSpec((1, d), full),
            pl.BlockSpec((d, 4 * mlw), full),
            pl.BlockSpec((s5w, d), full),
            pl.BlockSpec((16, d), full),
            pl.BlockSpec((16, 1), full),
        ],
        out_specs=[
            pl.BlockSpec((tm, 2 * mlw), lambda i: (i, 0)),
            pl.BlockSpec((tm, mlw), lambda i: (i, 0)),
            pl.BlockSpec((tm, mlw), lambda i: (i, 0)),
            pl.BlockSpec((1, s5w, tm), lambda i: (i // tpb, 0, i % tpb)),
            pl.BlockSpec((1, 16, tm), lambda i: (i // tpb, 0, i % tpb)),
        ],
        out_shape=[
            jax.ShapeDtypeStruct((n_tok, 2 * mlw), BF16),
            jax.ShapeDtypeStruct((n_tok, mlw), BF16),
            jax.ShapeDtypeStruct((n_tok, mlw), BF16),
            jax.ShapeDtypeStruct((bsz, s5w, seq), BF16),
            jax.ShapeDtypeStruct((bsz, 16, seq), F32),
        ],
        compiler_params=pltpu.CompilerParams(
            dimension_semantics=("parallel",), vmem_limit_bytes=VMEM_LIMIT),
        name="in_proj",
    )(x2d, norm_g.reshape(1, d).astype(F32), w_nat, w_ut, w_gt, b_g)


def _s5_kernel(d_ref, x_ref, prm_ref, c_ref, bt_ref, out_ref, *, bsz, nc):
    L = S5_CHUNK
    P = S5_GROUP
    N = S5_STATE
    grp = pl.program_id(0)
    lane = lax.broadcasted_iota(I32, (1, 2 * N), 1)
    lo = lane < N
    a_re = jnp.minimum(prm_ref[0, 0:1, :], -1e-4)
    a_im = prm_ref[0, 1:2, :]
    dt = jnp.exp(prm_ref[0, 2:3, :])
    zr = dt * a_re
    zi = dt * a_im

    quarter = jnp.where(lo, 0.0, 0.5 * math.pi)

    def powrows(e):
        return jnp.exp(e * zr) * jnp.cos(e * zi - quarter)

    def swap(tab):
        return pltpu.roll(tab, N, 1)

    def cmul(tab, c):
        return c[0] * tab + jnp.where(lo, -c[1], c[1]) * swap(tab)

    n_dbl = max(nc - 1, 0).bit_length()
    exps = [1] + [SUBLANES << k for k in range(int(math.log2(L // SUBLANES)))] + [L << k for k in range(n_dbl)]
    e_col = jnp.concatenate([jnp.full((1, 1), float(e), F32) for e in exps]
                            + [jnp.zeros((-len(exps) % SUBLANES, 1), F32)], axis=0)
    mag = jnp.exp(e_col * zr)
    pw_r = mag * jnp.cos(e_col * zi)
    pw_i = mag * jnp.sin(e_col * zi)
    apow = {e: (pw_r[k:k + 1], pw_i[k:k + 1]) for k, e in enumerate(exps)}

    def powtab(descending):
        i8 = lax.broadcasted_iota(I32, (SUBLANES, 1), 0).astype(F32)
        tab = powrows(SUBLANES - 1.0 - i8 if descending else i8)
        rows = SUBLANES
        while rows < L:
            more = cmul(tab, apow[rows])
            tab = jnp.concatenate([more, tab] if descending else [tab, more], axis=0)
            rows *= 2
        return tab

    er, ei = apow[1]
    den = a_re * a_re + a_im * a_im
    coef_r = ((er - 1.0) * a_re + ei * a_im) / den
    coef_i = (ei * a_re - (er - 1.0) * a_im) / den
    c_r = c_ref[0, 0]
    c_i = c_ref[0, 1]
    bb_r = coef_r * bt_ref[0, 0] - coef_i * bt_ref[0, 1]
    bb_i = coef_r * bt_ref[0, 1] + coef_i * bt_ref[0, 0]

    cb_rows = []
    for q in range(P):
        cbr = c_r * bb_r[q:q + 1] - c_i * bb_i[q:q + 1]
        cbi = c_r * bb_i[q:q + 1] + c_i * bb_r[q:q + 1]
        cb_rows.append(jnp.where(lo, cbr, -cbi))
    cb = jnp.concatenate(cb_rows, axis=0)
    pt0 = powtab(False)
    c1, c2, c3 = _split3(cb)
    t1, t2, t3 = _split3(pt0)
    kmat = (_dot_nt(c1, t1) + _dot_nt(c1, t2) + _dot_nt(c2, t1)
            + _dot_nt(c2, t2) + _dot_nt(c1, t3) + _dot_nt(c3, t1))

    pt_rev = powtab(True)
    pt_rev_sw = swap(pt_rev)
    f_rows = []
    for q in range(P):
        a1 = bb_r[q:q + 1]
        a2 = jnp.where(lo, -bb_i[q:q + 1], bb_i[q:q + 1])
        f_rows.append((a1 * pt_rev + a2 * pt_rev_sw).astype(BF16))
    fmat = jnp.concatenate(f_rows, axis=0)
    pt1 = cmul(pt0, apow[1])
    pt1_sw = swap(pt1)
    e_rows = []
    for p in range(P):
        b1 = jnp.where(lo, c_r[p:p + 1], -c_r[p:p + 1])
        b2 = -c_i[p:p + 1]
        e_rows.append((b1 * pt1 + b2 * pt1_sw).astype(BF16))
    emat_t = jnp.concatenate(e_rows, axis=0)

    lhs = jnp.concatenate(
        [jnp.concatenate([x_ref[b, q] for q in range(P)], axis=1) for b in range(bsz)], axis=0)
    s_end = _dot(lhs, fmat)

    rr = lax.broadcasted_iota(I32, (L, L), 0)
    cc = lax.broadcasted_iota(I32, (L, L), 1)
    causal = cc >= rr
    y = None
    for q0 in range(0, P, 2):
        rows = []
        for q in (q0, q0 + 1):
            tiles = []
            for p in range(P):
                kb = jnp.broadcast_to(kmat[q * P + p:q * P + p + 1, :], (L, L))
                toe = pltpu.roll(kb, 0, 1, stride=1, stride_axis=0)
                tiles.append(jnp.where(causal, toe, 0.0).astype(BF16))
            rows.append(jnp.concatenate(tiles, axis=1))
        part = _dot(lhs[:, q0 * L:(q0 + 2) * L], jnp.concatenate(rows, axis=0))
        y = part if y is None else y + part

    m_rows = bsz * nc
    ridx = lax.broadcasted_iota(I32, (m_rows, 1), 0)
    cidx = ridx % nc
    h = jnp.where(cidx >= 1, pltpu.roll(s_end, 1, 0), 0.0)
    d = 1
    while d < nc:
        sh = jnp.where(cidx >= d, pltpu.roll(h, d, 0), 0.0)
        h = h + cmul(sh, apow[d * L])
        d *= 2
    y = y + _dot_nt(h.astype(BF16), emat_t)

    for b in range(bsz):
        for p in range(P):
            yp = (y[b * nc:(b + 1) * nc, p * L:(p + 1) * L]
                  + d_ref[grp * P + p] * x_ref[b, p].astype(F32))
            out_ref[b, :, p, :] = jax.nn.gelu(yp)


def _s5(ut, log_dt, a_re, a_im, b_re, b_im, c_re, c_im, d_skip):
    bsz, s5w, seq = ut.shape
    L = S5_CHUNK
    nc = seq // L
    groups = s5w // S5_GROUP
    n = S5_STATE
    x4 = ut.reshape(bsz, s5w, nc, L)
    dup = lambda t: jnp.concatenate([t, t], axis=-1).astype(F32)
    prm = jnp.zeros((groups, SUBLANES, 2 * n), F32)
    prm = prm.at[:, 0].set(dup(a_re)).at[:, 1].set(dup(a_im))
    prm = prm.at[:, 2].set(jnp.broadcast_to(log_dt.astype(F32)[:, None], (groups, 2 * n)))
    cpar = jnp.stack([dup(c_re), dup(c_im)], axis=1)
    btpar = jnp.stack([dup(jnp.swapaxes(b_re, 1, 2)), dup(jnp.swapaxes(b_im, 1, 2))], axis=1)
    out = pl.pallas_call(
        functools.partial(_s5_kernel, bsz=bsz, nc=nc),
        grid_spec=pltpu.PrefetchScalarGridSpec(
            num_scalar_prefetch=1,
            grid=(groups,),
            in_specs=[
                pl.BlockSpec((bsz, S5_GROUP, nc, L), lambda g, d: (0, g, 0, 0)),
                pl.BlockSpec((1, SUBLANES, 2 * n), lambda g, d: (g, 0, 0)),
                pl.BlockSpec((1, 2, S5_GROUP, 2 * n), lambda g, d: (g, 0, 0, 0)),
                pl.BlockSpec((1, 2, S5_GROUP, 2 * n), lambda g, d: (g, 0, 0, 0)),
            ],
            out_specs=pl.BlockSpec((bsz, nc, S5_GROUP, L), lambda g, d: (0, 0, g, 0)),
        ),
        out_shape=jax.ShapeDtypeStruct((bsz, nc, s5w, L), F32),
        compiler_params=pltpu.CompilerParams(
            dimension_semantics=("parallel",), vmem_limit_bytes=VMEM_LIMIT),
        name="s5",
    )(d_skip.astype(F32), x4, prm, cpar, btpar)
    return out


def _log_sigmoid(x):
    return jnp.minimum(x, 0.0) - jnp.log(1.0 + jnp.exp(-jnp.abs(x)))


def _gates_kernel(gt_ref, out_ref, m_scr, *, bsz, chunk, n_sub):
    L = chunk
    H = ML_HEADS
    step = pl.program_id(0)

    @pl.when(step == 0)
    def _():
        m_scr[...] = jnp.zeros_like(m_scr)

    rr = lax.broadcasted_iota(I32, (L, L), 0)
    cc = lax.broadcasted_iota(I32, (L, L), 1)
    utri = (rr <= cc).astype(BF16)
    grow = lax.broadcasted_iota(I32, (16, 1), 0)
    lane_row = lax.broadcasted_iota(I32, (SUBLANES, L), 1)
    m_prev = [m_scr[b][0:H, 0:1] for b in range(bsz)]
    pairs = [(j, b) for j in range(n_sub) for b in range(bsz)]

    irow, brow, cm = {}, {}, {}
    for j, b in pairs:
        g = gt_ref[b, :, j * L:(j + 1) * L]
        g2 = jnp.where(grow >= H, _log_sigmoid(g), g)
        p1, p2, p3 = _split3(g2)
        brow[j, b] = (_dot(p1, utri) + _dot(p2, utri) + _dot(p3, utri))[H:2 * H]
        irow[j, b] = g[0:H]
    for j, b in pairs:
        ib = irow[j, b] - brow[j, b]
        x = jnp.concatenate([ib, ib], axis=0)
        sft = 1
        while sft < L:
            x = jnp.maximum(x, jnp.where(lane_row >= sft, pltpu.roll(x, sft, 1), -jnp.inf))
            sft *= 2
        cm[j, b] = x[0:H]
    for j, b in pairs:
        mp = m_prev[b]
        mm = jnp.maximum(cm[j, b], mp)
        b_last = brow[j, b][:, L - 1:L]
        m_next = b_last + jnp.maximum(mp, cm[j, b][:, L - 1:L])
        planes = (mm, jnp.exp(mp - mm), jnp.minimum(jnp.exp(-(brow[j, b] + mm)), EXP_CAP),
                  jnp.exp(b_last - brow[j, b] + irow[j, b] - m_next), irow[j, b] - brow[j, b],
                  jnp.broadcast_to(jnp.exp(b_last + mp - m_next), (H, L)))
        for k, rows in enumerate(planes):
            out_ref[k, b, :, j * L:(j + 1) * L] = jnp.concatenate([rows, rows], axis=0)
        m_prev[b] = m_next
    for b in range(bsz):
        m_scr[b] = jnp.broadcast_to(jnp.concatenate([m_prev[b], m_prev[b]], axis=0),
                                    (SUBLANES, LANES))


def _gates(gt, bsz, seq, chunk):
    n_sub = min(GATE_CHUNKS, seq // chunk)
    blk = n_sub * chunk
    return pl.pallas_call(
        functools.partial(_gates_kernel, bsz=bsz, chunk=chunk, n_sub=n_sub),
        grid=(seq // blk,),
        in_specs=[pl.BlockSpec((bsz, 16, blk), lambda c: (0, 0, c))],
        out_specs=pl.BlockSpec((6, bsz, SUBLANES, blk), lambda c: (0, 0, 0, c)),
        out_shape=jax.ShapeDtypeStruct((6, bsz, SUBLANES, seq), F32),
        scratch_shapes=[pltpu.VMEM((bsz, SUBLANES, LANES), F32)],
        compiler_params=pltpu.CompilerParams(
            dimension_semantics=("arbitrary",), vmem_limit_bytes=VMEM_LIMIT),
        name="gates",
    )(gt)


def _mlstm_kernel(qk_ref, v_ref, o_ref, gr_ref, cw_ref, cb_ref, ng_ref, y_ref,
                  tail_scr, c_scr, *, bsz, chunk, width):
    L = chunk
    H = ML_HEADS
    dh = width // H
    step = pl.program_id(0)
    assert L == LANES and dh == LANES, "column replication below uses one 128x128 tile per head"

    @pl.when(step == 0)
    def _():
        tail_scr[...] = jnp.zeros_like(tail_scr)
        c_scr[...] = jnp.zeros_like(c_scr)

    rr = lax.broadcasted_iota(I32, (L, L), 0)
    cc = lax.broadcasted_iota(I32, (L, L), 1)
    causal = cc <= rr
    eye = (rr == cc).astype(BF16)
    ones_blk = jnp.ones((L, dh), BF16)
    scale = 1.0 / math.sqrt(dh)
    inv_dh = 1.0 / dh
    n_str = bsz * H
    streams = [(b, hd) for b in range(bsz) for hd in range(H)]

    def replicate(rows, two_terms):
        p = jnp.concatenate([jnp.broadcast_to(rows[h:h + 1], (LANES, L)) for h in range(H)], axis=0)
        p1 = p.astype(BF16)
        if not two_terms:
            return _dot_nt(eye, p1)
        return _dot_nt(eye, p1) + _dot_nt(eye, (p - p1.astype(F32)).astype(BF16))

    def rowsum(x):
        hi = x.astype(BF16)
        lo = (x - hi.astype(F32)).astype(BF16)
        return _dot(hi, ones_blk) + _dot(lo, ones_blk)

    tails = [tail_scr[b] for b in range(bsz)]
    caugs = [c_scr[i] for i in range(n_str)]
    reps = [[replicate(gr_ref[k, b][0:H], k == 0) for b in range(bsz)] for k in range(4)]
    col = lambda k: [reps[k][b][:, hd * LANES:(hd + 1) * LANES] for b, hd in streams]
    mm_rep, w_inter, enm, ws_rep = col(0), col(1), col(2), col(3)
    ib_row = [gr_ref[4, b][hd:hd + 1] for b, hd in streams]
    decays = [gr_ref[5, b][hd:hd + 1, 0:1] for b, hd in streams]

    q_all, k_all, new_tails = [], [], []
    for b in range(bsz):
        xqk = qk_ref[b].astype(F32)
        ext = jnp.concatenate([tails[b], xqk], axis=0)
        acc = jnp.broadcast_to(cb_ref[...], xqk.shape)
        for j in range(CONV_WIDTH):
            back = CONV_WIDTH - 1 - j
            sh = ext if back == 0 else pltpu.roll(ext, back, 0)
            acc = acc + cw_ref[j:j + 1, :] * sh[SUBLANES:]
        new_tails.append(xqk[L - SUBLANES:])
        qkc = acc * _sigmoid(acc)
        q_all.append(qkc[:, :width].astype(BF16))
        k_all.append((qkc[:, width:] * scale).astype(BF16))

    q_h = [q_all[b][:, hd * dh:(hd + 1) * dh] for b, hd in streams]
    k_h = [k_all[b][:, hd * dh:(hd + 1) * dh] for b, hd in streams]
    vaug = [jnp.concatenate([v_ref[b][:, hd * dh:(hd + 1) * dh], ones_blk], axis=1)
            for b, hd in streams]
    scores = [_dot_nt(q_h[i], k_h[i]) for i in range(n_str)]
    inter = [_dot(q_h[i], caugs[i].astype(BF16)) for i in range(n_str)]
    s_bf = [(scores[i] * jnp.exp(jnp.where(causal, ib_row[i] - mm_rep[i], -jnp.inf))).astype(BF16)
            for i in range(n_str)]
    kw = [(k_h[i].astype(F32) * ws_rep[i]).astype(BF16) for i in range(n_str)]
    intra = [_dot(s_bf[i], vaug[i]) for i in range(n_str)]
    upd = [_dot_tn(kw[i], vaug[i]) for i in range(n_str)]

    hh = []
    for i in range(n_str):
        num = w_inter[i] * inter[i][:, :dh] + intra[i][:, :dh]
        den = w_inter[i] * inter[i][:, dh:] + intra[i][:, dh:]
        hh.append(num / jnp.maximum(jnp.abs(den), enm[i]))
    mu = [rowsum(h) * inv_dh for h in hh]
    ctr = [hh[i] - mu[i] for i in range(n_str)]
    var = [rowsum(c * c) * inv_dh for c in ctr]
    outs = []
    for i, (b, hd) in enumerate(streams):
        sl = slice(hd * dh, (hd + 1) * dh)
        hn = ctr[i] * lax.rsqrt(var[i] + LN_EPS) * ng_ref[:, sl]
        outs.append(hn * _sigmoid(o_ref[b][:, sl].astype(F32)))

    for b in range(bsz):
        tail_scr[b] = new_tails[b]
        y_ref[b] = jnp.concatenate(outs[b * H:(b + 1) * H], axis=1).astype(BF16)
    for i in range(n_str):
        c_scr[i] = decays[i] * caugs[i] + upd[i]


def _mlstm(qk, v, o, gt, conv_w, conv_b, norm_g, bsz, seq):
    width = v.shape[-1]
    L = min(ML_CHUNK, seq)
    dh = width // ML_HEADS
    qk3 = qk.reshape(bsz, seq, 2 * width)
    v3 = v.reshape(bsz, seq, width)
    o3 = o.reshape(bsz, seq, width)
    gate_rows = _gates(gt, bsz, seq, L)
    full = lambda c: (0, 0)
    return pl.pallas_call(
        functools.partial(_mlstm_kernel, bsz=bsz, chunk=L, width=width),
        grid=(seq // L,),
        in_specs=[
            pl.BlockSpec((bsz, L, 2 * width), lambda c: (0, c, 0)),
            pl.BlockSpec((bsz, L, width), lambda c: (0, c, 0)),
            pl.BlockSpec((bsz, L, width), lambda c: (0, c, 0)),
            pl.BlockSpec((6, bsz, SUBLANES, L), lambda c: (0, 0, 0, c)),
            pl.BlockSpec((CONV_WIDTH, 2 * width), full),
            pl.BlockSpec((1, 2 * width), full),
            pl.BlockSpec((1, width), full),
        ],
        out_specs=pl.BlockSpec((bsz, L, width), lambda c: (0, c, 0)),
        out_shape=jax.ShapeDtypeStruct((bsz, seq, width), BF16),
        scratch_shapes=[
            pltpu.VMEM((bsz, SUBLANES, 2 * width), F32),
            pltpu.VMEM((bsz * ML_HEADS, dh, 2 * dh), F32),
        ],
        compiler_params=pltpu.CompilerParams(
            dimension_semantics=("arbitrary",), vmem_limit_bytes=VMEM_LIMIT),
        name="mlstm",
    )(qk3, v3, o3, gate_rows, conv_w.astype(F32), conv_b.reshape(1, -1).astype(F32),
      norm_g.reshape(1, -1).astype(F32))


def _post_kernel(x_ref, gt_ref, yml_ref, wglut_ref, bglu_ref, wout_ref, nffn_ref, wrt_ref, br_ref,
                 h2_ref, xn_ref, eidx_ref, gate_ref, rank_ref, tcnt_ref, carry_scr):
    step = pl.program_id(0)

    @pl.when(step == 0)
    def _():
        carry_scr[...] = jnp.zeros_like(carry_scr)

    gt = jnp.concatenate([gt_ref[0, j] for j in range(gt_ref.shape[1])], axis=1).astype(BF16)
    s5w = gt.shape[0]
    zt = _dot(wglut_ref[...], gt) + bglu_ref[...]
    s5t = (gt.astype(F32) * _sigmoid(zt)).astype(BF16)
    h2 = (x_ref[...] + _dot_tn(s5t, wout_ref[:s5w, :])
          + _dot(yml_ref[...], wout_ref[s5w:, :]))
    h2_ref[...] = h2
    ms = jnp.mean(h2 * h2, axis=-1, keepdims=True)
    xn = h2 * lax.rsqrt(ms + RMS_EPS) * nffn_ref[...]
    xb = xn.astype(BF16)
    xn_ref[...] = xb

    tm = xn.shape[0]
    logits = _dot_nt(wrt_ref[...], xb) + br_ref[...]
    eio = lax.broadcasted_iota(I32, (N_EXPERTS, tm), 0).astype(F32)
    vals = logits
    onehot = jnp.zeros((N_EXPERTS, tm), F32)
    idxs, tops = [], []
    for _ in range(TOP_K):
        mx = jnp.max(vals, axis=0, keepdims=True)
        idx = jnp.min(jnp.where(vals == mx, eio, float(N_EXPERTS)), axis=0, keepdims=True)
        sel = eio == idx
        onehot = onehot + sel.astype(F32)
        vals = jnp.where(sel, -jnp.inf, vals)
        idxs.append(idx)
        tops.append(mx)
    exps = [jnp.exp(t - tops[0]) for t in tops]
    tot = exps[0] + exps[1] + exps[2] + exps[3]
    pad_f = jnp.zeros((SUBLANES - TOP_K, tm), F32)
    eidx_ref[...] = jnp.concatenate(idxs + [pad_f], axis=0).astype(I32)
    gate_ref[...] = jnp.concatenate([e / tot for e in exps] + [pad_f], axis=0)

    rr = lax.broadcasted_iota(I32, (tm, tm), 0)
    cc = lax.broadcasted_iota(I32, (tm, tm), 1)
    before = (rr < cc).astype(BF16)
    carry = carry_scr[:, 0:1]
    rank_ex = _dot(onehot.astype(BF16), before) + carry
    ranks = [jnp.sum(jnp.where(eio == i, rank_ex, 0.0), axis=0, keepdims=True) for i in idxs]
    rank_ref[...] = jnp.concatenate(ranks + [pad_f - float(1 << 30)], axis=0).astype(I32)
    tile_cnt = jnp.sum(onehot, axis=1, keepdims=True)
    carry_scr[...] = jnp.broadcast_to(carry + tile_cnt, carry_scr.shape)
    for s in range(tm // DISPATCH_TILE):
        sub = onehot[:, s * DISPATCH_TILE:(s + 1) * DISPATCH_TILE]
        tcnt_ref[0, s] = jnp.broadcast_to(jnp.sum(sub, axis=1, keepdims=True), carry_scr.shape)


def _post(x2d, g_t, y_ml, w_glu, b_glu, w_out, norm_g, w_router, b_router):
    n_tok, d = x2d.shape
    bsz, nc, s5w, chunk = g_t.shape
    seq = nc * chunk
    mlw = y_ml.shape[-1]
    tm = min(PROJ_TILE, seq)
    tpb = seq // tm
    full = lambda i: (0, 0)
    row = lambda i: (i, 0)
    colb = lambda i: (0, i)
    return pl.pallas_call(
        _post_kernel,
        grid=(n_tok // tm,),
        in_specs=[
            pl.BlockSpec((tm, d), row),
            pl.BlockSpec((1, tm // chunk, s5w, chunk), lambda i: (i // tpb, i % tpb, 0, 0)),
            pl.BlockSpec((tm, mlw), row),
            pl.BlockSpec((s5w, s5w), full),
            pl.BlockSpec((s5w, 1), full),
            pl.BlockSpec((d, d), full),
            pl.BlockSpec((1, d), full),
            pl.BlockSpec((N_EXPERTS, d), full),
            pl.BlockSpec((N_EXPERTS, 1), full),
        ],
        out_specs=[
            pl.BlockSpec((tm, d), row),
            pl.BlockSpec((tm, d), row),
            pl.BlockSpec((SUBLANES, tm), colb),
            pl.BlockSpec((SUBLANES, tm), colb),
            pl.BlockSpec((SUBLANES, tm), colb),
            pl.BlockSpec((1, tm // DISPATCH_TILE, N_EXPERTS, LANES), lambda i: (i, 0, 0, 0)),
        ],
        out_shape=[
            jax.ShapeDtypeStruct((n_tok, d), F32),
            jax.ShapeDtypeStruct((n_tok, d), BF16),
            jax.ShapeDtypeStruct((SUBLANES, n_tok), I32),
            jax.ShapeDtypeStruct((SUBLANES, n_tok), F32),
            jax.ShapeDtypeStruct((SUBLANES, n_tok), I32),
            jax.ShapeDtypeStruct((n_tok // tm, tm // DISPATCH_TILE, N_EXPERTS, LANES), F32),
        ],
        scratch_shapes=[pltpu.VMEM((N_EXPERTS, LANES), F32)],
        compiler_params=pltpu.CompilerParams(
            dimension_semantics=("arbitrary",), vmem_limit_bytes=VMEM_LIMIT),
        name="post_router",
    )(x2d, g_t, y_ml, w_glu.T.astype(BF16), b_glu.reshape(-1, 1).astype(F32), w_out.astype(BF16),
      norm_g.reshape(1, -1).astype(F32), w_router.T.astype(BF16),
      b_router.reshape(-1, 1).astype(F32))


def _slab_loop(base, nbig_ref, nsmall_ref, fn, n_slabs=N_EXPERTS):
    def per_expert(e, carry):
        idx = base + e
        nbig = nbig_ref[idx]

        def big(j, c2):
            fn(idx, j * BIG_CHUNK, BIG_CHUNK)
            return c2

        def small(j, c2):
            fn(idx, nbig * BIG_CHUNK + j * SLAB_ALIGN, SLAB_ALIGN)
            return c2

        lax.fori_loop(0, nbig, big, 0)
        lax.fori_loop(0, nsmall_ref[idx], small, 0)
        return carry

    lax.fori_loop(0, n_slabs, per_expert, 0)


def _plan_copies(step, plan, fn):
    nbig_ref, nsmall_ref, _, bsrc_ref, bdst_ref, ssrc_ref, sdst_ref = plan

    def big(j, carry):
        k = step * MAX_BIG + j
        fn(bsrc_ref[k], bdst_ref[k], BIG_CHUNK)
        return carry

    def small(j, carry):
        k = step * MAX_SMALL + j
        fn(ssrc_ref[k], sdst_ref[k], SLAB_ALIGN)
        return carry

    lax.fori_loop(0, nbig_ref[step], big, 0)
    lax.fori_loop(0, nsmall_ref[step], small, 0)


def _plan_wait(step, plan, copy):
    copy(0, 0, DISPATCH_TILE * TOP_K).wait()

    def one(j, carry):
        copy(0, 0, SLAB_ALIGN).wait()
        return carry

    lax.fori_loop(0, plan[2][step], one, 0)


def _unpack_pairs(words):
    lo = lax.bitcast_convert_type(words << 16, F32).astype(BF16)
    hi = lax.bitcast_convert_type(words & jnp.uint32(0xFFFF0000), F32).astype(BF16)
    return lo, hi


def _pack_pairs(lo_f32, hi_f32):
    lo = lax.bitcast_convert_type(lo_f32, U32) >> 16
    hi = lax.bitcast_convert_type(hi_f32, U32) & jnp.uint32(0xFFFF0000)
    return hi | lo


def _dispatch_kernel(*refs, n_steps):
    plan = refs[:7]
    znbig_ref, znsmall_ref, zrow_ref, xn_ref, pos_ref, xs_hbm, stage, zbuf, sem, zsem = refs[7:]
    i = pl.program_id(0)
    slot = i % 2
    tw, d = xn_ref.shape
    n_rows = stage.shape[1]
    half = d // 2

    def copy(s, stage_row, buf_row, rows):
        src = stage.at[s, pl.ds(pl.multiple_of(stage_row, SLAB_ALIGN), rows), :]
        dst = xs_hbm.at[pl.ds(pl.multiple_of(buf_row, SLAB_ALIGN), rows), :]
        return pltpu.make_async_copy(src, dst, sem)

    @pl.when(i == 0)
    def _():
        zbuf[...] = jnp.zeros_like(zbuf)

        def tail(idx, r0, rows):
            dst = xs_hbm.at[pl.ds(pl.multiple_of(zrow_ref[idx] + r0, SLAB_ALIGN), rows), :]
            return pltpu.make_async_copy(zbuf.at[pl.ds(0, rows), :], dst, zsem)

        _slab_loop(0, znbig_ref, znsmall_ref, lambda *a: tail(*a).start(), N_EXPERTS + 1)
        _slab_loop(0, znbig_ref, znsmall_ref, lambda *a: tail(*a).wait(), N_EXPERTS + 1)

    xt = xn_ref[...]
    blk = STAGE_BLOCK
    for r0 in range(0, n_rows, blk):
        rid = lax.broadcasted_iota(I32, (blk, tw), 0) + r0
        hit = rid == pos_ref[0:1, :]
        for k in range(1, TOP_K):
            hit = jnp.logical_or(hit, rid == pos_ref[k:k + 1, :])
        srt = _dot(jnp.where(hit, 1.0, 0.0).astype(BF16), xt)
        stage[slot, r0:r0 + blk, :] = _pack_pairs(srt[:, :half], srt[:, half:])

    @pl.when(i > 0)
    def _():
        _plan_wait(i - 1, plan, functools.partial(copy, 1 - slot))

    _plan_copies(i, plan, lambda *a: copy(slot, *a).start())

    @pl.when(i == n_steps - 1)
    def _():
        _plan_wait(i, plan, functools.partial(copy, slot))


def _dispatch(xn, pos8, plan, tails, n_slots):
    n_tok, d = xn.shape
    tw = DISPATCH_TILE
    n_steps = n_tok // tw
    return pl.pallas_call(
        functools.partial(_dispatch_kernel, n_steps=n_steps),
        grid_spec=pltpu.PrefetchScalarGridSpec(
            num_scalar_prefetch=10,
            grid=(n_steps,),
            in_specs=[
                pl.BlockSpec((tw, d), lambda i, *_: (i, 0)),
                pl.BlockSpec((SUBLANES, tw), lambda i, *_: (0, i)),
            ],
            out_specs=pl.BlockSpec(memory_space=pl.ANY),
            scratch_shapes=[
                pltpu.VMEM((2, STAGE_ROWS, d // 2), U32),
                pltpu.VMEM((BIG_CHUNK, d // 2), U32),
                pltpu.SemaphoreType.DMA(()),
                pltpu.SemaphoreType.DMA(()),
            ],
        ),
        out_shape=jax.ShapeDtypeStruct((n_slots, d // 2), U32),
        compiler_params=pltpu.CompilerParams(
            dimension_semantics=("arbitrary",), vmem_limit_bytes=VMEM_LIMIT),
        name="dispatch",
    )(*plan, *tails, xn, pos8)


def _expert_kernel(te_ref, nu_ref, first_ref, nxt_ref, par_ref, x_ref, wup_hbm, bup_ref, wdn_hbm,
                   bdn_ref, y_ref, wup_f32, wdn_f32, wup_bf, wdn_bf, sem, *, d_ff):
    i = pl.program_id(0)
    n_used = nu_ref[0]

    def fetch(e, s):
        return (pltpu.make_async_copy(wup_hbm.at[e], wup_f32.at[s], sem.at[s, 0]),
                pltpu.make_async_copy(wdn_hbm.at[e], wdn_f32.at[s], sem.at[s, 1]))

    @pl.when(jnp.logical_and(i < n_used, first_ref[i] == 1))
    def _():
        s = par_ref[i]

        @pl.when(i == 0)
        def _():
            for c in fetch(te_ref[i], s):
                c.start()

        for c in fetch(te_ref[i], s):
            c.wait()
        wup_bf[...] = wup_f32[s].astype(BF16)
        wdn_bf[...] = wdn_f32[s].astype(BF16)

        @pl.when(nxt_ref[i] >= 0)
        def _():
            for c in fetch(nxt_ref[i], 1 - s):
                c.start()

    @pl.when(i < n_used)
    def _():
        lo, hi = _unpack_pairs(x_ref[...])
        half = lo.shape[1]
        h = _dot(lo, wup_bf[:half, :]) + _dot(hi, wup_bf[half:, :]) + bup_ref[0]
        gl = jnp.minimum(h[:, :d_ff], SWIGLU_LIMIT)
        lin = jnp.clip(h[:, d_ff:], -SWIGLU_LIMIT, SWIGLU_LIMIT)
        act = gl * _sigmoid(SWIGLU_ALPHA * gl) * (lin + 1.0)
        y = _dot(act.astype(BF16), wdn_bf[...]) + bdn_ref[0]
        yb = y.astype(BF16).astype(F32)
        y_ref[...] = _pack_pairs(yb[:, :half], yb[:, half:])

    @pl.when(i >= n_used)
    def _():
        y_ref[...] = jnp.zeros_like(y_ref)


def _experts(xs, tile_e, n_used, first, nxt, parity, w_up, b_up, w_down, b_down):
    n_slots, half = xs.shape
    d = 2 * half
    tm = EXPERT_TILE
    n_tiles = n_slots // tm
    d_ff = w_down.shape[1]
    return pl.pallas_call(
        functools.partial(_expert_kernel, d_ff=d_ff),
        grid_spec=pltpu.PrefetchScalarGridSpec(
            num_scalar_prefetch=5,
            grid=(n_tiles,),
            in_specs=[
                pl.BlockSpec((tm, half), lambda i, te, nu, *_: (jnp.minimum(i, nu[0] - 1), 0)),
                pl.BlockSpec(memory_space=pl.ANY),
                pl.BlockSpec((1, 1, 2 * d_ff), lambda i, te, *_: (te[i], 0, 0)),
                pl.BlockSpec(memory_space=pl.ANY),
                pl.BlockSpec((1, 1, d), lambda i, te, *_: (te[i], 0, 0)),
            ],
            out_specs=pl.BlockSpec((tm, half), lambda i, *_: (i, 0)),
            scratch_shapes=[
                pltpu.VMEM((2, d, 2 * d_ff), F32),
                pltpu.VMEM((2, d_ff, d), F32),
                pltpu.VMEM((d, 2 * d_ff), BF16),
                pltpu.VMEM((d_ff, d), BF16),
                pltpu.SemaphoreType.DMA((2, 2)),
            ],
        ),
        out_shape=jax.ShapeDtypeStruct((n_slots, half), U32),
        compiler_params=pltpu.CompilerParams(
            dimension_semantics=("arbitrary",), vmem_limit_bytes=VMEM_LIMIT),
        name="experts",
    )(tile_e, n_used, first, nxt, parity, xs, w_up, b_up.reshape(N_EXPERTS, 1, -1),
      w_down, b_down.reshape(N_EXPERTS, 1, -1))


def _combine_kernel(*refs, n_steps):
    plan = refs[:7]
    ys_hbm, h2_ref, posr_ref, gater_ref, ng_ref, out_ref, stage, sem = refs[7:]
    i = pl.program_id(0)
    slot = i % 2
    tw, d = h2_ref.shape
    n_rows = stage.shape[1]

    def copy(s, stage_row, buf_row, rows):
        src = ys_hbm.at[pl.ds(pl.multiple_of(buf_row, SLAB_ALIGN), rows), :]
        dst = stage.at[s, pl.ds(pl.multiple_of(stage_row, SLAB_ALIGN), rows), :]
        return pltpu.make_async_copy(src, dst, sem.at[s])

    @pl.when(i == 0)
    def _():
        stage[...] = jnp.zeros_like(stage)
        _plan_copies(0, plan, lambda *a: copy(0, *a).start())

    @pl.when(i + 1 < n_steps)
    def _():
        _plan_copies(i + 1, plan, lambda *a: copy(1 - slot, *a).start())

    _plan_wait(i, plan, functools.partial(copy, slot))

    y_lo, y_hi = _unpack_pairs(stage[slot])
    blk = 128
    for t0 in range(0, tw, blk):
        rid = lax.broadcasted_iota(I32, (n_rows, blk), 0)
        pg = jnp.zeros((n_rows, blk), F32)
        for k in range(TOP_K):
            pg = jnp.where(rid == posr_ref[k:k + 1, t0:t0 + blk],
                           gater_ref[k:k + 1, t0:t0 + blk], pg)
        pgb = pg.astype(BF16)
        moe = jnp.concatenate([_dot_tn(pgb, y_lo), _dot_tn(pgb, y_hi)], axis=1)
        acc = h2_ref[t0:t0 + blk, :] + moe
        ms = jnp.mean(acc * acc, axis=-1, keepdims=True)
        out_ref[t0:t0 + blk, :] = acc * lax.rsqrt(ms + RMS_EPS) * ng_ref[...]


def _combine(ys, h2, pos_rows, gate_rows, plan, norm_g):
    n_tok, d = h2.shape
    tw = DISPATCH_TILE
    n_steps = n_tok // tw
    return pl.pallas_call(
        functools.partial(_combine_kernel, n_steps=n_steps),
        grid_spec=pltpu.PrefetchScalarGridSpec(
            num_scalar_prefetch=7,
            grid=(n_steps,),
            in_specs=[
                pl.BlockSpec(memory_space=pl.ANY),
                pl.BlockSpec((tw, d), lambda i, *_: (i, 0)),
                pl.BlockSpec((SUBLANES, tw), lambda i, *_: (0, i)),
                pl.BlockSpec((SUBLANES, tw), lambda i, *_: (0, i)),
                pl.BlockSpec((1, d), lambda i, *_: (0, 0)),
            ],
            out_specs=pl.BlockSpec((tw, d), lambda i, *_: (i, 0)),
            scratch_shapes=[
                pltpu.VMEM((2, STAGE_ROWS, d // 2), U32),
                pltpu.SemaphoreType.DMA((2,)),
            ],
        ),
        out_shape=jax.ShapeDtypeStruct((n_tok, d), F32),
        compiler_params=pltpu.CompilerParams(
            dimension_semantics=("arbitrary",), vmem_limit_bytes=VMEM_LIMIT),
        name="combine",
    )(*plan, ys, h2, pos_rows, gate_rows, norm_g.reshape(1, -1).astype(F32))


def _moe(h2, xn, eidx, gate, rank, tcnt, w_up, b_up, w_down, b_down, norm_final_g):
    n_tok, d = h2.shape
    tm = EXPERT_TILE
    tw = DISPATCH_TILE
    n_steps = n_tok // tw
    n_exp = N_EXPERTS
    e_ids = jnp.arange(n_exp, dtype=I32)
    al = SLAB_ALIGN
    tile_cnt = tcnt.reshape(n_steps, n_exp, LANES)[:, :, 0].astype(I32)
    slab_rows = (tile_cnt + al - 1) // al * al
    exp_rows = jnp.sum(slab_rows, axis=0)
    region = (exp_rows + tm - 1) // tm * tm
    pad_end = jnp.cumsum(region)
    pad_start = pad_end - region
    n_slots = (n_tok * TOP_K + n_steps * n_exp * (al - 1) + n_exp * (tm - 1) + tm - 1) // tm * tm
    n_tiles = n_slots // tm

    tile_carry = jnp.cumsum(tile_cnt, axis=0) - tile_cnt
    tile_off = jnp.cumsum(slab_rows, axis=1) - slab_rows
    slab_row0 = pad_start[None, :] + jnp.cumsum(slab_rows, axis=0) - slab_rows

    def chunks(rows):
        return rows // BIG_CHUNK, rows % BIG_CHUNK // al

    def flat(cnt, src0, dst0, size, n_max):
        run = jnp.cumsum(cnt, axis=1)
        j = jnp.arange(n_max, dtype=I32)
        e_of = jnp.sum(run[:, None, :] <= j[None, :, None], axis=-1)
        sel = e_of[:, :, None] == e_ids[None, None, :]
        pick = lambda a: jnp.sum(jnp.where(sel, a[:, None, :], 0), axis=-1)
        local = (j[None, :] - pick(run - cnt)) * size
        return run[:, -1], (pick(src0) + local).reshape(-1), (pick(dst0) + local).reshape(-1)

    nbig, nsmall = chunks(slab_rows)
    big_tot, big_src, big_dst = flat(nbig, tile_off, slab_row0, BIG_CHUNK, MAX_BIG)
    small_tot, small_src, small_dst = flat(nsmall, tile_off + nbig * BIG_CHUNK,
                                           slab_row0 + nbig * BIG_CHUNK, al, MAX_SMALL)
    pad_groups = (jnp.sum(slab_rows, axis=1) - tw * TOP_K) // al
    plan = (big_tot, small_tot, pad_groups, big_src, big_dst, small_src, small_dst)
    tail_rows = jnp.concatenate([region - exp_rows, n_slots - pad_end[-1:]])
    tails = (*chunks(tail_rows), jnp.concatenate([pad_start + exp_rows, pad_end[-1:]]))

    base_t = jnp.repeat((tile_off - tile_carry).T, tw, axis=1)
    e_sel = eidx[None] == e_ids[:, None, None]
    pos8 = jnp.sum(jnp.where(e_sel, base_t[:, None, :], 0), axis=0) + rank

    tile_start = jnp.arange(n_tiles, dtype=I32) * tm
    tile_e = jnp.minimum(jnp.sum(pad_end[None, :] <= tile_start[:, None], axis=-1),
                         n_exp - 1).astype(I32)
    n_used = (pad_end[-1] // tm).astype(I32).reshape(1)
    t_ids = jnp.arange(n_tiles, dtype=I32)
    first = jnp.logical_and(jnp.logical_or(t_ids == 0, tile_e != jnp.roll(tile_e, 1)),
                            t_ids < n_used[0])
    later = jnp.logical_and(first[None, :], t_ids[None, :] > t_ids[:, None])
    nxt_tile = jnp.min(jnp.where(later, t_ids[None, :], n_tiles), axis=1)
    nxt_e = jnp.sum(jnp.where(t_ids[None, :] == nxt_tile[:, None], tile_e[None, :], 0), axis=1)
    nxt = jnp.where(nxt_tile < n_tiles, nxt_e, -1).astype(I32)
    parity = ((jnp.cumsum(first.astype(I32)) - 1) % 2).astype(I32)
    first = first.astype(I32)

    xs = _dispatch(xn, pos8, plan, tails, n_slots)
    ys = _experts(xs, tile_e, n_used, first, nxt, parity, w_up, b_up, w_down, b_down)
    return _combine(ys, h2, pos8, gate, plan, norm_final_g)


def kernel(x, norm_mix_g, w_in, s5_log_dt, s5_a_re, s5_a_im, s5_b_re, s5_b_im, s5_c_re, s5_c_im,
           s5_d, s5_w_glu, s5_b_glu, ml_conv_w, ml_conv_b, ml_b_gates, ml_norm_g, w_out,
           norm_ffn_g, w_router, b_router, w_up, b_up, w_down, b_down, norm_final_g):
    bsz, seq, d = x.shape
    depth = w_in.shape[0]
    assert depth == 1, "single-layer block"
    l = 0
    x2d = x.reshape(bsz * seq, d)
    qk, v, o, ut, gt = _in_proj(x2d, norm_mix_g[l], w_in[l], ml_b_gates[l], bsz, seq)
    g_t = _s5(ut, s5_log_dt[l], s5_a_re[l], s5_a_im[l], s5_b_re[l], s5_b_im[l],
              s5_c_re[l], s5_c_im[l], s5_d[l])
    y_ml = _mlstm(qk, v, o, gt, ml_conv_w[l], ml_conv_b[l], ml_norm_g[l], bsz, seq)
    y_ml = y_ml.reshape(bsz * seq, -1)
    h2, xn, eidx, gate, rank, tcnt = _post(x2d, g_t, y_ml, s5_w_glu[l], s5_b_glu[l], w_out[l],
                                           norm_ffn_g[l], w_router[l], b_router[l])
    out = _moe(h2, xn, eidx, gate, rank, tcnt, w_up[l], b_up[l], w_down[l], b_down[l],
               norm_final_g)
    return out.reshape(bsz, seq, d)
```

```python
import functools
import math

import jax
import jax.numpy as jnp
from jax import lax
from jax.experimental import pallas as pl
from jax.experimental.pallas import tpu as pltpu

F32 = jnp.float32
BF16 = jnp.bfloat16
I32 = jnp.int32
U32 = jnp.uint32

S5_GROUP = 16
S5_STATE = 64
ML_HEADS = 4
CONV_WIDTH = 4
N_EXPERTS = 32
TOP_K = 4
SWIGLU_LIMIT = 7.0
SWIGLU_ALPHA = 1.702
RMS_EPS = 1e-5
LN_EPS = 1e-6

LANES = 128
SUBLANES = 8
S5_CHUNK = LANES
ML_CHUNK = 128
GATE_CHUNKS = 16
EXP_CAP = 1e38
PROJ_TILE = 1024
EXPERT_TILE = 512
DISPATCH_TILE = 256
SLAB_ALIGN = SUBLANES
BIG_CHUNK = 32
STAGE_ROWS = DISPATCH_TILE * TOP_K + N_EXPERTS * SLAB_ALIGN
STAGE_BLOCK = STAGE_ROWS // 2
MAX_BIG = STAGE_ROWS // BIG_CHUNK
MAX_SMALL = N_EXPERTS * (BIG_CHUNK // SLAB_ALIGN - 1)
VMEM_LIMIT = 56 * 1024 * 1024

_NT = (((1,), (1,)), ((), ()))
_TN = (((0,), (0,)), ((), ()))


def _dot(a, b):
    return jnp.dot(a, b, preferred_element_type=F32)


def _dot_nt(a, b):
    return lax.dot_general(a, b, _NT, preferred_element_type=F32)


def _dot_tn(a, b):
    return lax.dot_general(a, b, _TN, preferred_element_type=F32)


def _split3(x):
    p1 = x.astype(BF16)
    r1 = x - p1.astype(F32)
    p2 = r1.astype(BF16)
    r2 = r1 - p2.astype(F32)
    return p1, p2, r2.astype(BF16)


def _sigmoid(x):
    return 0.5 * jnp.tanh(0.5 * x) + 0.5


def _inproj_kernel(x_ref, g_ref, wnat_ref, wut_ref, wgt_ref, bg_ref,
                   qk_ref, v_ref, o_ref, ut_ref, gt_ref, *, width):
    x = x_ref[...]
    ms = jnp.mean(x * x, axis=-1, keepdims=True)
    hn = (x * lax.rsqrt(ms + RMS_EPS) * g_ref[...]).astype(BF16)
    nat = _dot(hn, wnat_ref[...])
    qk_ref[...] = nat[:, :2 * width].astype(BF16)
    v_ref[...] = nat[:, 2 * width:3 * width].astype(BF16)
    o_ref[...] = nat[:, 3 * width:].astype(BF16)
    ut_ref[0] = _dot_nt(wut_ref[...], hn).astype(BF16)
    gt_ref[0] = _dot_nt(wgt_ref[...], hn) + bg_ref[...]


def _in_proj(x2d, norm_g, w_in, b_gates, bsz, seq):
    n_tok, d = x2d.shape
    s5w = d // 2
    mlw = d - s5w
    tm = min(PROJ_TILE, seq)
    tpb = seq // tm
    w_bf = w_in.astype(BF16)
    w_nat = w_bf[:, s5w:s5w + 4 * mlw]
    w_ut = w_bf[:, :s5w].T
    n_gate = 2 * ML_HEADS
    w_gt = jnp.zeros((16, d), BF16).at[:n_gate].set(w_bf[:, s5w + 4 * mlw:].T)
    b_g = jnp.zeros((16, 1), F32).at[:n_gate, 0].set(b_gates.astype(F32))
    grid = (n_tok // tm,)
    full = lambda i: (0, 0)
    return pl.pallas_call(
        functools.partial(_inproj_kernel, width=mlw),
        grid=grid,
        in_specs=[
            pl.BlockSpec((tm, d), lambda i: (i, 0)),
            pl.BlockSpec((1, d), full),
            pl.BlockSpec((d, 4 * mlw), full),
            pl.BlockSpec((s5w, d), full),
            pl.BlockSpec((16, d), full),
            pl.BlockSpec((16, 1), full),
        ],
        out_specs=[
            pl.BlockSpec((tm, 2 * mlw), lambda i: (i, 0)),
            pl.BlockSpec((tm, mlw), lambda i: (i, 0)),
            pl.BlockSpec((tm, mlw), lambda i: (i, 0)),
            pl.BlockSpec((1, s5w, tm), lambda i: (i // tpb, 0, i % tpb)),
            pl.BlockSpec((1, 16, tm), lambda i: (i // tpb, 0, i % tpb)),
        ],
        out_shape=[
            jax.ShapeDtypeStruct((n_tok, 2 * mlw), BF16),
            jax.ShapeDtypeStruct((n_tok, mlw), BF16),
            jax.ShapeDtypeStruct((n_tok, mlw), BF16),
            jax.ShapeDtypeStruct((bsz, s5w, seq), BF16),
            jax.ShapeDtypeStruct((bsz, 16, seq), F32),
        ],
        compiler_params=pltpu.CompilerParams(
            dimension_semantics=("parallel",), vmem_limit_bytes=VMEM_LIMIT),
        name="in_proj",
    )(x2d, norm_g.reshape(1, d).astype(F32), w_nat, w_ut, w_gt, b_g)


def _s5_kernel(d_ref, x_ref, prm_ref, c_ref, bt_ref, out_ref, *, bsz, nc):
    L = S5_CHUNK
    P = S5_GROUP
    N = S5_STATE
    grp = pl.program_id(0)
    lane = lax.broadcasted_iota(I32, (1, 2 * N), 1)
    lo = lane < N
    a_re = jnp.minimum(prm_ref[0, 0:1, :], -1e-4)
    a_im = prm_ref[0, 1:2, :]
    dt = jnp.exp(prm_ref[0, 2:3, :])
    zr = dt * a_re
    zi = dt * a_im

    quarter = jnp.where(lo, 0.0, 0.5 * math.pi)

    def powrows(e):
        return jnp.exp(e * zr) * jnp.cos(e * zi - quarter)

    def swap(tab):
        return pltpu.roll(tab, N, 1)

    def cmul(tab, c):
        return c[0] * tab + jnp.where(lo, -c[1], c[1]) * swap(tab)

    n_dbl = max(nc - 1, 0).bit_length()
    exps = [1] + [SUBLANES << k for k in range(int(math.log2(L // SUBLANES)))] + [L << k for k in range(n_dbl)]
    e_col = jnp.concatenate([jnp.full((1, 1), float(e), F32) for e in exps]
                            + [jnp.zeros((-len(exps) % SUBLANES, 1), F32)], axis=0)
    mag = jnp.exp(e_col * zr)
    pw_r = mag * jnp.cos(e_col * zi)
    pw_i = mag * jnp.sin(e_col * zi)
    apow = {e: (pw_r[k:k + 1], pw_i[k:k + 1]) for k, e in enumerate(exps)}

    def powtab(descending):
        i8 = lax.broadcasted_iota(I32, (SUBLANES, 1), 0).astype(F32)
        tab = powrows(SUBLANES - 1.0 - i8 if descending else i8)
        rows = SUBLANES
        while rows < L:
            more = cmul(tab, apow[rows])
            tab = jnp.concatenate([more, tab] if descending else [tab, more], axis=0)
            rows *= 2
        return tab

    er, ei = apow[1]
    den = a_re * a_re + a_im * a_im
    coef_r = ((er - 1.0) * a_re + ei * a_im) / den
    coef_i = (ei * a_re - (er - 1.0) * a_im) / den
    c_r = c_ref[0, 0]
    c_i = c_ref[0, 1]
    bb_r = coef_r * bt_ref[0, 0] - coef_i * bt_ref[0, 1]
    bb_i = coef_r * bt_ref[0, 1] + coef_i * bt_ref[0, 0]

    cb_rows = []
    for q in range(P):
        cbr = c_r * bb_r[q:q + 1] - c_i * bb_i[q:q + 1]
        cbi = c_r * bb_i[q:q + 1] + c_i * bb_r[q:q + 1]
        cb_rows.append(jnp.where(lo, cbr, -cbi))
    cb = jnp.concatenate(cb_rows, axis=0)
    pt0 = powtab(False)
    c1, c2, c3 = _split3(cb)
    t1, t2, t3 = _split3(pt0)
    kmat = (_dot_nt(c1, t1) + _dot_nt(c1, t2) + _dot_nt(c2, t1)
            + _dot_nt(c2, t2) + _dot_nt(c1, t3) + _dot_nt(c3, t1))

    pt_rev = powtab(True)
    pt_rev_sw = swap(pt_rev)
    f_rows = []
    for q in range(P):
        a1 = bb_r[q:q + 1]
        a2 = jnp.where(lo, -bb_i[q:q + 1], bb_i[q:q + 1])
        f_rows.append((a1 * pt_rev + a2 * pt_rev_sw).astype(BF16))
    fmat = jnp.concatenate(f_rows, axis=0)
    pt1 = cmul(pt0, apow[1])
    pt1_sw = swap(pt1)
    e_rows = []
    for p in range(P):
        b1 = jnp.where(lo, c_r[p:p + 1], -c_r[p:p + 1])
        b2 = -c_i[p:p + 1]
        e_rows.append((b1 * pt1 + b2 * pt1_sw).astype(BF16))
    emat_t = jnp.concatenate(e_rows, axis=0)

    lhs = jnp.concatenate(
        [jnp.concatenate([x_ref[b, q] for q in range(P)], axis=1) for b in range(bsz)], axis=0)
    s_end = _dot(lhs, fmat)

    rr = lax.broadcasted_iota(I32, (L, L), 0)
    cc = lax.broadcasted_iota(I32, (L, L), 1)
    causal = cc >= rr
    y = None
    for q0 in range(0, P, 2):
        rows = []
        for q in (q0, q0 + 1):
            tiles = []
            for p in range(P):
                kb = jnp.broadcast_to(kmat[q * P + p:q * P + p + 1, :], (L, L))
                toe = pltpu.roll(kb, 0, 1, stride=1, stride_axis=0)
                tiles.append(jnp.where(causal, toe, 0.0).astype(BF16))
            rows.append(jnp.concatenate(tiles, axis=1))
        part = _dot(lhs[:, q0 * L:(q0 + 2) * L], jnp.concatenate(rows, axis=0))
        y = part if y is None else y + part

    m_rows = bsz * nc
    ridx = lax.broadcasted_iota(I32, (m_rows, 1), 0)
    cidx = ridx % nc
    h = jnp.where(cidx >= 1, pltpu.roll(s_end, 1, 0), 0.0)
    d = 1
    while d < nc:
        sh = jnp.where(cidx >= d, pltpu.roll(h, d, 0), 0.0)
        h = h + cmul(sh, apow[d * L])
        d *= 2
    y = y + _dot_nt(h.astype(BF16), emat_t)

    for b in range(bsz):
        for p in range(P):
            yp = (y[b * nc:(b + 1) * nc, p * L:(p + 1) * L]
                  + d_ref[grp * P + p] * x_ref[b, p].astype(F32))
            out_ref[b, :, p, :] = jax.nn.gelu(yp)


def _s5(ut, log_dt, a_re, a_im, b_re, b_im, c_re, c_im, d_skip):
    bsz, s5w, seq = ut.shape
    L = S5_CHUNK
    nc = seq // L
    groups = s5w // S5_GROUP
    n = S5_STATE
    x4 = ut.reshape(bsz, s5w, nc, L)
    dup = lambda t: jnp.concatenate([t, t], axis=-1).astype(F32)
    prm = jnp.zeros((groups, SUBLANES, 2 * n), F32)
    prm = prm.at[:, 0].set(dup(a_re)).at[:, 1].set(dup(a_im))
    prm = prm.at[:, 2].set(jnp.broadcast_to(log_dt.astype(F32)[:, None], (groups, 2 * n)))
    cpar = jnp.stack([dup(c_re), dup(c_im)], axis=1)
    btpar = jnp.stack([dup(jnp.swapaxes(b_re, 1, 2)), dup(jnp.swapaxes(b_im, 1, 2))], axis=1)
    out = pl.pallas_call(
        functools.partial(_s5_kernel, bsz=bsz, nc=nc),
        grid_spec=pltpu.PrefetchScalarGridSpec(
            num_scalar_prefetch=1,
            grid=(groups,),
            in_specs=[
                pl.BlockSpec((bsz, S5_GROUP, nc, L), lambda g, d: (0, g, 0, 0)),
                pl.BlockSpec((1, SUBLANES, 2 * n), lambda g, d: (g, 0, 0)),
                pl.BlockSpec((1, 2, S5_GROUP, 2 * n), lambda g, d: (g, 0, 0, 0)),
                pl.BlockSpec((1, 2, S5_GROUP, 2 * n), lambda g, d: (g, 0, 0, 0)),
            ],
            out_specs=pl.BlockSpec((bsz, nc, S5_GROUP, L), lambda g, d: (0, 0, g, 0)),
        ),
        out_shape=jax.ShapeDtypeStruct((bsz, nc, s5w, L), F32),
        compiler_params=pltpu.CompilerParams(
            dimension_semantics=("parallel",), vmem_limit_bytes=VMEM_LIMIT),
        name="s5",
    )(d_skip.astype(F32), x4, prm, cpar, btpar)
    return out


def _log_sigmoid(x):
    return jnp.minimum(x, 0.0) - jnp.log(1.0 + jnp.exp(-jnp.abs(x)))


def _gates_kernel(gt_ref, out_ref, m_scr, *, bsz, chunk, n_sub):
    L = chunk
    H = ML_HEADS
    step = pl.program_id(0)

    @pl.when(step == 0)
    def _():
        m_scr[...] = jnp.zeros_like(m_scr)

    rr = lax.broadcasted_iota(I32, (L, L), 0)
    cc = lax.broadcasted_iota(I32, (L, L), 1)
    utri = (rr <= cc).astype(BF16)
    grow = lax.broadcasted_iota(I32, (16, 1), 0)
    lane_row = lax.broadcasted_iota(I32, (SUBLANES, L), 1)
    m_prev = [m_scr[b][0:H, 0:1] for b in range(bsz)]
    pairs = [(j, b) for j in range(n_sub) for b in range(bsz)]

    irow, brow, cm = {}, {}, {}
    for j, b in pairs:
        g = gt_ref[b, :, j * L:(j + 1) * L]
        g2 = jnp.where(grow >= H, _log_sigmoid(g), g)
        p1, p2, p3 = _split3(g2)
        brow[j, b] = (_dot(p1, utri) + _dot(p2, utri) + _dot(p3, utri))[H:2 * H]
        irow[j, b] = g[0:H]
    for j, b in pairs:
        ib = irow[j, b] - brow[j, b]
        x = jnp.concatenate([ib, ib], axis=0)
        sft = 1
        while sft < L:
            x = jnp.maximum(x, jnp.where(lane_row >= sft, pltpu.roll(x, sft, 1), -jnp.inf))
            sft *= 2
        cm[j, b] = x[0:H]
    for j, b in pairs:
        mp = m_prev[b]
        mm = jnp.maximum(cm[j, b], mp)
        b_last = brow[j, b][:, L - 1:L]
        m_next = b_last + jnp.maximum(mp, cm[j, b][:, L - 1:L])
        planes = (mm, jnp.exp(mp - mm), jnp.minimum(jnp.exp(-(brow[j, b] + mm)), EXP_CAP),
                  jnp.exp(b_last - brow[j, b] + irow[j, b] - m_next), irow[j, b] - brow[j, b],
                  jnp.broadcast_to(jnp.exp(b_last + mp - m_next), (H, L)))
        for k, rows in enumerate(planes):
            out_ref[k, b, :, j * L:(j + 1) * L] = jnp.concatenate([rows, rows], axis=0)
        m_prev[b] = m_next
    for b in range(bsz):
        m_scr[b] = jnp.broadcast_to(jnp.concatenate([m_prev[b], m_prev[b]], axis=0),
                                    (SUBLANES, LANES))


def _gates(gt, bsz, seq, chunk):
    n_sub = min(GATE_CHUNKS, seq // chunk)
    blk = n_sub * chunk
    return pl.pallas_call(
        functools.partial(_gates_kernel, bsz=bsz, chunk=chunk, n_sub=n_sub),
        grid=(seq // blk,),
        in_specs=[pl.BlockSpec((bsz, 16, blk), lambda c: (0, 0, c))],
        out_specs=pl.BlockSpec((6, bsz, SUBLANES, blk), lambda c: (0, 0, 0, c)),
        out_shape=jax.ShapeDtypeStruct((6, bsz, SUBLANES, seq), F32),
        scratch_shapes=[pltpu.VMEM((bsz, SUBLANES, LANES), F32)],
        compiler_params=pltpu.CompilerParams(
            dimension_semantics=("arbitrary",), vmem_limit_bytes=VMEM_LIMIT),
        name="gates",
    )(gt)


def _mlstm_kernel(qk_ref, v_ref, o_ref, gr_ref, cw_ref, cb_ref, ng_ref, y_ref,
                  tail_scr, c_scr, *, bsz, chunk, width):
    L = chunk
    H = ML_HEADS
    dh = width // H
    step = pl.program_id(0)
    assert L == LANES and dh == LANES, "column replication below uses one 128x128 tile per head"

    @pl.when(step == 0)
    def _():
        tail_scr[...] = jnp.zeros_like(tail_scr)
        c_scr[...] = jnp.zeros_like(c_scr)

    rr = lax.broadcasted_iota(I32, (L, L), 0)
    cc = lax.broadcasted_iota(I32, (L, L), 1)
    causal = cc <= rr
    eye = (rr == cc).astype(BF16)
    ones_blk = jnp.ones((L, dh), BF16)
    scale = 1.0 / math.sqrt(dh)
    inv_dh = 1.0 / dh
    n_str = bsz * H
    streams = [(b, hd) for b in range(bsz) for hd in range(H)]

    def replicate(rows, two_terms):
        p = jnp.concatenate([jnp.broadcast_to(rows[h:h + 1], (LANES, L)) for h in range(H)], axis=0)
        p1 = p.astype(BF16)
        if not two_terms:
            return _dot_nt(eye, p1)
        return _dot_nt(eye, p1) + _dot_nt(eye, (p - p1.astype(F32)).astype(BF16))

    def rowsum(x):
        hi = x.astype(BF16)
        lo = (x - hi.astype(F32)).astype(BF16)
        return _dot(hi, ones_blk) + _dot(lo, ones_blk)

    tails = [tail_scr[b] for b in range(bsz)]
    caugs = [c_scr[i] for i in range(n_str)]
    reps = [[replicate(gr_ref[k, b][0:H], k == 0) for b in range(bsz)] for k in range(4)]
    col = lambda k: [reps[k][b][:, hd * LANES:(hd + 1) * LANES] for b, hd in streams]
    mm_rep, w_inter, enm, ws_rep = col(0), col(1), col(2), col(3)
    ib_row = [gr_ref[4, b][hd:hd + 1] for b, hd in streams]
    decays = [gr_ref[5, b][hd:hd + 1, 0:1] for b, hd in streams]

    q_all, k_all, new_tails = [], [], []
    for b in range(bsz):
        xqk = qk_ref[b].astype(F32)
        ext = jnp.concatenate([tails[b], xqk], axis=0)
        acc = jnp.broadcast_to(cb_ref[...], xqk.shape)
        for j in range(CONV_WIDTH):
            back = CONV_WIDTH - 1 - j
            sh = ext if back == 0 else pltpu.roll(ext, back, 0)
            acc = acc + cw_ref[j:j + 1, :] * sh[SUBLANES:]
        new_tails.append(xqk[L - SUBLANES:])
        qkc = acc * _sigmoid(acc)
        q_all.append(qkc[:, :width].astype(BF16))
        k_all.append((qkc[:, width:] * scale).astype(BF16))

    q_h = [q_all[b][:, hd * dh:(hd + 1) * dh] for b, hd in streams]
    k_h = [k_all[b][:, hd * dh:(hd + 1) * dh] for b, hd in streams]
    vaug = [jnp.concatenate([v_ref[b][:, hd * dh:(hd + 1) * dh], ones_blk], axis=1)
            for b, hd in streams]
    scores = [_dot_nt(q_h[i], k_h[i]) for i in range(n_str)]
    inter = [_dot(q_h[i], caugs[i].astype(BF16)) for i in range(n_str)]
    s_bf = [(scores[i] * jnp.exp(jnp.where(causal, ib_row[i] - mm_rep[i], -jnp.inf))).astype(BF16)
            for i in range(n_str)]
    kw = [(k_h[i].astype(F32) * ws_rep[i]).astype(BF16) for i in range(n_str)]
    intra = [_dot(s_bf[i], vaug[i]) for i in range(n_str)]
    upd = [_dot_tn(kw[i], vaug[i]) for i in range(n_str)]

    hh = []
    for i in range(n_str):
        num = w_inter[i] * inter[i][:, :dh] + intra[i][:, :dh]
        den = w_inter[i] * inter[i][:, dh:] + intra[i][:, dh:]
        hh.append(num / jnp.maximum(jnp.abs(den), enm[i]))
    mu = [rowsum(h) * inv_dh for h in hh]
    ctr = [hh[i] - mu[i] for i in range(n_str)]
    var = [rowsum(c * c) * inv_dh for c in ctr]
    outs = []
    for i, (b, hd) in enumerate(streams):
        sl = slice(hd * dh, (hd + 1) * dh)
        hn = ctr[i] * lax.rsqrt(var[i] + LN_EPS) * ng_ref[:, sl]
        outs.append(hn * _sigmoid(o_ref[b][:, sl].astype(F32)))

    for b in range(bsz):
        tail_scr[b] = new_tails[b]
        y_ref[b] = jnp.concatenate(outs[b * H:(b + 1) * H], axis=1).astype(BF16)
    for i in range(n_str):
        c_scr[i] = decays[i] * caugs[i] + upd[i]


def _mlstm(qk, v, o, gt, conv_w, conv_b, norm_g, bsz, seq):
    width = v.shape[-1]
    L = min(ML_CHUNK, seq)
    dh = width // ML_HEADS
    qk3 = qk.reshape(bsz, seq, 2 * width)
    v3 = v.reshape(bsz, seq, width)
    o3 = o.reshape(bsz, seq, width)
    gate_rows = _gates(gt, bsz, seq, L)
    full = lambda c: (0, 0)
    return pl.pallas_call(
        functools.partial(_mlstm_kernel, bsz=bsz, chunk=L, width=width),
        grid=(seq // L,),
        in_specs=[
            pl.BlockSpec((bsz, L, 2 * width), lambda c: (0, c, 0)),
            pl.BlockSpec((bsz, L, width), lambda c: (0, c, 0)),
            pl.BlockSpec((bsz, L, width), lambda c: (0, c, 0)),
            pl.BlockSpec((6, bsz, SUBLANES, L), lambda c: (0, 0, 0, c)),
            pl.BlockSpec((CONV_WIDTH, 2 * width), full),
            pl.BlockSpec((1, 2 * width), full),
            pl.BlockSpec((1, width), full),
        ],
        out_specs=pl.BlockSpec((bsz, L, width), lambda c: (0, c, 0)),
        out_shape=jax.ShapeDtypeStruct((bsz, seq, width), BF16),
        scratch_shapes=[
            pltpu.VMEM((bsz, SUBLANES, 2 * width), F32),
            pltpu.VMEM((bsz * ML_HEADS, dh, 2 * dh), F32),
        ],
        compiler_params=pltpu.CompilerParams(
            dimension_semantics=("arbitrary",), vmem_limit_bytes=VMEM_LIMIT),
        name="mlstm",
    )(qk3, v3, o3, gate_rows, conv_w.astype(F32), conv_b.reshape(1, -1).astype(F32),
      norm_g.reshape(1, -1).astype(F32))


def _post_kernel(x_ref, gt_ref, yml_ref, wglut_ref, bglu_ref, wout_ref, nffn_ref, wrt_ref, br_ref,
                 h2_ref, xn_ref, eidx_ref, gate_ref, rank_ref, tcnt_ref, carry_scr):
    step = pl.program_id(0)

    @pl.when(step == 0)
    def _():
        carry_scr[...] = jnp.zeros_like(carry_scr)

    gt = jnp.concatenate([gt_ref[0, j] for j in range(gt_ref.shape[1])], axis=1).astype(BF16)
    s5w = gt.shape[0]
    zt = _dot(wglut_ref[...], gt) + bglu_ref[...]
    s5t = (gt.astype(F32) * _sigmoid(zt)).astype(BF16)
    h2 = (x_ref[...] + _dot_tn(s5t, wout_ref[:s5w, :])
          + _dot(yml_ref[...], wout_ref[s5w:, :]))
    h2_ref[...] = h2
    ms = jnp.mean(h2 * h2, axis=-1, keepdims=True)
    xn = h2 * lax.rsqrt(ms + RMS_EPS) * nffn_ref[...]
    xb = xn.astype(BF16)
    xn_ref[...] = xb

    tm = xn.shape[0]
    logits = _dot_nt(wrt_ref[...], xb) + br_ref[...]
    eio = lax.broadcasted_iota(I32, (N_EXPERTS, tm), 0).astype(F32)
    vals = logits
    onehot = jnp.zeros((N_EXPERTS, tm), F32)
    idxs, tops = [], []
    for _ in range(TOP_K):
        mx = jnp.max(vals, axis=0, keepdims=True)
        idx = jnp.min(jnp.where(vals == mx, eio, float(N_EXPERTS)), axis=0, keepdims=True)
        sel = eio == idx
        onehot = onehot + sel.astype(F32)
        vals = jnp.where(sel, -jnp.inf, vals)
        idxs.append(idx)
        tops.append(mx)
    exps = [jnp.exp(t - tops[0]) for t in tops]
    tot = exps[0] + exps[1] + exps[2] + exps[3]
    pad_f = jnp.zeros((SUBLANES - TOP_K, tm), F32)
    eidx_ref[...] = jnp.concatenate(idxs + [pad_f], axis=0).astype(I32)
    gate_ref[...] = jnp.concatenate([e / tot for e in exps] + [pad_f], axis=0)

    rr = lax.broadcasted_iota(I32, (tm, tm), 0)
    cc = lax.broadcasted_iota(I32, (tm, tm), 1)
    before = (rr < cc).astype(BF16)
    carry = carry_scr[:, 0:1]
    rank_ex = _dot(onehot.astype(BF16), before) + carry
    ranks = [jnp.sum(jnp.where(eio == i, rank_ex, 0.0), axis=0, keepdims=True) for i in idxs]
    rank_ref[...] = jnp.concatenate(ranks + [pad_f - float(1 << 30)], axis=0).astype(I32)
    tile_cnt = jnp.sum(onehot, axis=1, keepdims=True)
    carry_scr[...] = jnp.broadcast_to(carry + tile_cnt, carry_scr.shape)
    for s in range(tm // DISPATCH_TILE):
        sub = onehot[:, s * DISPATCH_TILE:(s + 1) * DISPATCH_TILE]
        tcnt_ref[0, s] = jnp.broadcast_to(jnp.sum(sub, axis=1, keepdims=True), carry_scr.shape)


def _post(x2d, g_t, y_ml, w_glu, b_glu, w_out, norm_g, w_router, b_router):
    n_tok, d = x2d.shape
    bsz, nc, s5w, chunk = g_t.shape
    seq = nc * chunk
    mlw = y_ml.shape[-1]
    tm = min(PROJ_TILE, seq)
    tpb = seq // tm
    full = lambda i: (0, 0)
    row = lambda i: (i, 0)
    colb = lambda i: (0, i)
    return pl.pallas_call(
        _post_kernel,
        grid=(n_tok // tm,),
        in_specs=[
            pl.BlockSpec((tm, d), row),
            pl.BlockSpec((1, tm // chunk, s5w, chunk), lambda i: (i // tpb, i % tpb, 0, 0)),
            pl.BlockSpec((tm, mlw), row),
            pl.BlockSpec((s5w, s5w), full),
            pl.BlockSpec((s5w, 1), full),
            pl.BlockSpec((d, d), full),
            pl.BlockSpec((1, d), full),
            pl.BlockSpec((N_EXPERTS, d), full),
            pl.BlockSpec((N_EXPERTS, 1), full),
        ],
        out_specs=[
            pl.BlockSpec((tm, d), row),
            pl.BlockSpec((tm, d), row),
            pl.BlockSpec((SUBLANES, tm), colb),
            pl.BlockSpec((SUBLANES, tm), colb),
            pl.BlockSpec((SUBLANES, tm), colb),
            pl.BlockSpec((1, tm // DISPATCH_TILE, N_EXPERTS, LANES), lambda i: (i, 0, 0, 0)),
        ],
        out_shape=[
            jax.ShapeDtypeStruct((n_tok, d), F32),
            jax.ShapeDtypeStruct((n_tok, d), BF16),
            jax.ShapeDtypeStruct((SUBLANES, n_tok), I32),
            jax.ShapeDtypeStruct((SUBLANES, n_tok), F32),
            jax.ShapeDtypeStruct((SUBLANES, n_tok), I32),
            jax.ShapeDtypeStruct((n_tok // tm, tm // DISPATCH_TILE, N_EXPERTS, LANES), F32),
        ],
        scratch_shapes=[pltpu.VMEM((N_EXPERTS, LANES), F32)],
        compiler_params=pltpu.CompilerParams(
            dimension_semantics=("arbitrary",), vmem_limit_bytes=VMEM_LIMIT),
        name="post_router",
    )(x2d, g_t, y_ml, w_glu.T.astype(BF16), b_glu.reshape(-1, 1).astype(F32), w_out.astype(BF16),
      norm_g.reshape(1, -1).astype(F32), w_router.T.astype(BF16),
      b_router.reshape(-1, 1).astype(F32))


def _slab_loop(base, nbig_ref, nsmall_ref, fn, n_slabs=N_EXPERTS):
    def per_expert(e, carry):
        idx = base + e
        nbig = nbig_ref[idx]

        def big(j, c2):
            fn(idx, j * BIG_CHUNK, BIG_CHUNK)
            return c2

        def small(j, c2):
            fn(idx, nbig * BIG_CHUNK + j * SLAB_ALIGN, SLAB_ALIGN)
            return c2

        lax.fori_loop(0, nbig, big, 0)
        lax.fori_loop(0, nsmall_ref[idx], small, 0)
        return carry

    lax.fori_loop(0, n_slabs, per_expert, 0)


def _plan_copies(step, plan, fn):
    nbig_ref, nsmall_ref, _, bsrc_ref, bdst_ref, ssrc_ref, sdst_ref = plan

    def big(j, carry):
        k = step * MAX_BIG + j
        fn(bsrc_ref[k], bdst_ref[k], BIG_CHUNK)
        return carry

    def small(j, carry):
        k = step * MAX_SMALL + j
        fn(ssrc_ref[k], sdst_ref[k], SLAB_ALIGN)
        return carry

    lax.fori_loop(0, nbig_ref[step], big, 0)
    lax.fori_loop(0, nsmall_ref[step], small, 0)


def _plan_wait(step, plan, copy):
    copy(0, 0, DISPATCH_TILE * TOP_K).wait()

    def one(j, carry):
        copy(0, 0, SLAB_ALIGN).wait()
        return carry

    lax.fori_loop(0, plan[2][step], one, 0)


def _unpack_pairs(words):
    lo = lax.bitcast_convert_type(words << 16, F32).astype(BF16)
    hi = lax.bitcast_convert_type(words & jnp.uint32(0xFFFF0000), F32).astype(BF16)
    return lo, hi


def _pack_pairs(lo_f32, hi_f32):
    lo = lax.bitcast_convert_type(lo_f32, U32) >> 16
    hi = lax.bitcast_convert_type(hi_f32, U32) & jnp.uint32(0xFFFF0000)
    return hi | lo


def _dispatch_kernel(*refs, n_steps):
    plan = refs[:7]
    znbig_ref, znsmall_ref, zrow_ref, xn_ref, pos_ref, xs_hbm, stage, zbuf, sem, zsem = refs[7:]
    i = pl.program_id(0)
    slot = i % 2
    tw, d = xn_ref.shape
    n_rows = stage.shape[1]
    half = d // 2

    def copy(s, stage_row, buf_row, rows):
        src = stage.at[s, pl.ds(pl.multiple_of(stage_row, SLAB_ALIGN), rows), :]
        dst = xs_hbm.at[pl.ds(pl.multiple_of(buf_row, SLAB_ALIGN), rows), :]
        return pltpu.make_async_copy(src, dst, sem)

    @pl.when(i == 0)
    def _():
        zbuf[...] = jnp.zeros_like(zbuf)

        def tail(idx, r0, rows):
            dst = xs_hbm.at[pl.ds(pl.multiple_of(zrow_ref[idx] + r0, SLAB_ALIGN), rows), :]
            return pltpu.make_async_copy(zbuf.at[pl.ds(0, rows), :], dst, zsem)

        _slab_loop(0, znbig_ref, znsmall_ref, lambda *a: tail(*a).start(), N_EXPERTS + 1)
        _slab_loop(0, znbig_ref, znsmall_ref, lambda *a: tail(*a).wait(), N_EXPERTS + 1)

    xt = xn_ref[...]
    blk = STAGE_BLOCK
    for r0 in range(0, n_rows, blk):
        rid = lax.broadcasted_iota(I32, (blk, tw), 0) + r0
        hit = rid == pos_ref[0:1, :]
        for k in range(1, TOP_K):
            hit = jnp.logical_or(hit, rid == pos_ref[k:k + 1, :])
        srt = _dot(jnp.where(hit, 1.0, 0.0).astype(BF16), xt)
        stage[slot, r0:r0 + blk, :] = _pack_pairs(srt[:, :half], srt[:, half:])

    @pl.when(i > 0)
    def _():
        _plan_wait(i - 1, plan, functools.partial(copy, 1 - slot))

    _plan_copies(i, plan, lambda *a: copy(slot, *a).start())

    @pl.when(i == n_steps - 1)
    def _():
        _plan_wait(i, plan, functools.partial(copy, slot))


def _dispatch(xn, pos8, plan, tails, n_slots):
    n_tok, d = xn.shape
    tw = DISPATCH_TILE
    n_steps = n_tok // tw
    return pl.pallas_call(
        functools.partial(_dispatch_kernel, n_steps=n_steps),
        grid_spec=pltpu.PrefetchScalarGridSpec(
            num_scalar_prefetch=10,
            grid=(n_steps,),
            in_specs=[
                pl.BlockSpec((tw, d), lambda i, *_: (i, 0)),
                pl.BlockSpec((SUBLANES, tw), lambda i, *_: (0, i)),
            ],
            out_specs=pl.BlockSpec(memory_space=pl.ANY),
            scratch_shapes=[
                pltpu.VMEM((2, STAGE_ROWS, d // 2), U32),
                pltpu.VMEM((BIG_CHUNK, d // 2), U32),
                pltpu.SemaphoreType.DMA(()),
                pltpu.SemaphoreType.DMA(()),
            ],
        ),
        out_shape=jax.ShapeDtypeStruct((n_slots, d // 2), U32),
        compiler_params=pltpu.CompilerParams(
            dimension_semantics=("arbitrary",), vmem_limit_bytes=VMEM_LIMIT),
        name="dispatch",
    )(*plan, *tails, xn, pos8)


def _expert_kernel(te_ref, nu_ref, first_ref, nxt_ref, x_ref, wup_hbm, bup_ref, wdn_hbm,
                   bdn_ref, y_ref, wup_f32, wdn_f32, wup_bf, wdn_bf, sem, *, d_ff):
    i = pl.program_id(0)
    n_used = nu_ref[0]

    def fetch(e):
        return (pltpu.make_async_copy(wup_hbm.at[e], wup_f32, sem.at[0]),
                pltpu.make_async_copy(wdn_hbm.at[e], wdn_f32, sem.at[1]))

    @pl.when(jnp.logical_and(i < n_used, first_ref[i] == 1))
    def _():
        @pl.when(i == 0)
        def _():
            for c in fetch(te_ref[i]):
                c.start()

        for c in fetch(te_ref[i]):
            c.wait()
        wup_bf[...] = wup_f32[...].astype(BF16)
        wdn_bf[...] = wdn_f32[...].astype(BF16)

        @pl.when(nxt_ref[i] >= 0)
        def _():
            for c in fetch(nxt_ref[i]):
                c.start()

    @pl.when(i < n_used)
    def _():
        lo, hi = _unpack_pairs(x_ref[...])
        half = lo.shape[1]
        h = _dot(lo, wup_bf[:half, :]) + _dot(hi, wup_bf[half:, :]) + bup_ref[0]
        gl = jnp.minimum(h[:, :d_ff], SWIGLU_LIMIT)
        lin = jnp.clip(h[:, d_ff:], -SWIGLU_LIMIT, SWIGLU_LIMIT)
        act = gl * _sigmoid(SWIGLU_ALPHA * gl) * (lin + 1.0)
        y = _dot(act.astype(BF16), wdn_bf[...]) + bdn_ref[0]
        yb = y.astype(BF16).astype(F32)
        y_ref[...] = _pack_pairs(yb[:, :half], yb[:, half:])

    @pl.when(i >= n_used)
    def _():
        y_ref[...] = jnp.zeros_like(y_ref)


def _experts(xs, tile_e, n_used, first, nxt, w_up, b_up, w_down, b_down):
    n_slots, half = xs.shape
    d = 2 * half
    tm = EXPERT_TILE
    n_tiles = n_slots // tm
    d_ff = w_down.shape[1]
    return pl.pallas_call(
        functools.partial(_expert_kernel, d_ff=d_ff),
        grid_spec=pltpu.PrefetchScalarGridSpec(
            num_scalar_prefetch=4,
            grid=(n_tiles,),
            in_specs=[
                pl.BlockSpec((tm, half), lambda i, te, nu, *_: (jnp.minimum(i, nu[0] - 1), 0)),
                pl.BlockSpec(memory_space=pl.ANY),
                pl.BlockSpec((1, 1, 2 * d_ff), lambda i, te, *_: (te[i], 0, 0)),
                pl.BlockSpec(memory_space=pl.ANY),
                pl.BlockSpec((1, 1, d), lambda i, te, *_: (te[i], 0, 0)),
            ],
            out_specs=pl.BlockSpec((tm, half), lambda i, *_: (i, 0)),
            scratch_shapes=[
                pltpu.VMEM((d, 2 * d_ff), F32),
                pltpu.VMEM((d_ff, d), F32),
                pltpu.VMEM((d, 2 * d_ff), BF16),
                pltpu.VMEM((d_ff, d), BF16),
                pltpu.SemaphoreType.DMA((2,)),
            ],
        ),
        out_shape=jax.ShapeDtypeStruct((n_slots, half), U32),
        compiler_params=pltpu.CompilerParams(
            dimension_semantics=("arbitrary",), vmem_limit_bytes=VMEM_LIMIT),
        name="experts",
    )(tile_e, n_used, first, nxt, xs, w_up, b_up.reshape(N_EXPERTS, 1, -1),
      w_down, b_down.reshape(N_EXPERTS, 1, -1))


def _combine_kernel(*refs, n_steps):
    plan = refs[:7]
    ys_hbm, h2_ref, posr_ref, gater_ref, ng_ref, out_ref, stage, sem = refs[7:]
    i = pl.program_id(0)
    slot = i % 2
    tw, d = h2_ref.shape
    n_rows = stage.shape[1]

    def copy(s, stage_row, buf_row, rows):
        src = ys_hbm.at[pl.ds(pl.multiple_of(buf_row, SLAB_ALIGN), rows), :]
        dst = stage.at[s, pl.ds(pl.multiple_of(stage_row, SLAB_ALIGN), rows), :]
        return pltpu.make_async_copy(src, dst, sem.at[s])

    @pl.when(i == 0)
    def _():
        stage[...] = jnp.zeros_like(stage)
        _plan_copies(0, plan, lambda *a: copy(0, *a).start())

    @pl.when(i + 1 < n_steps)
    def _():
        _plan_copies(i + 1, plan, lambda *a: copy(1 - slot, *a).start())

    _plan_wait(i, plan, functools.partial(copy, slot))

    y_lo, y_hi = _unpack_pairs(stage[slot])
    blk = 128
    for t0 in range(0, tw, blk):
        rid = lax.broadcasted_iota(I32, (n_rows, blk), 0)
        pg = jnp.zeros((n_rows, blk), F32)
        for k in range(TOP_K):
            pg = jnp.where(rid == posr_ref[k:k + 1, t0:t0 + blk],
                           gater_ref[k:k + 1, t0:t0 + blk], pg)
        pgb = pg.astype(BF16)
        moe = jnp.concatenate([_dot_tn(pgb, y_lo), _dot_tn(pgb, y_hi)], axis=1)
        acc = h2_ref[t0:t0 + blk, :] + moe
        ms = jnp.mean(acc * acc, axis=-1, keepdims=True)
        out_ref[t0:t0 + blk, :] = acc * lax.rsqrt(ms + RMS_EPS) * ng_ref[...]


def _combine(ys, h2, pos_rows, gate_rows, plan, norm_g):
    n_tok, d = h2.shape
    tw = DISPATCH_TILE
    n_steps = n_tok // tw
    return pl.pallas_call(
        functools.partial(_combine_kernel, n_steps=n_steps),
        grid_spec=pltpu.PrefetchScalarGridSpec(
            num_scalar_prefetch=7,
            grid=(n_steps,),
            in_specs=[
                pl.BlockSpec(memory_space=pl.ANY),
                pl.BlockSpec((tw, d), lambda i, *_: (i, 0)),
                pl.BlockSpec((SUBLANES, tw), lambda i, *_: (0, i)),
                pl.BlockSpec((SUBLANES, tw), lambda i, *_: (0, i)),
                pl.BlockSpec((1, d), lambda i, *_: (0, 0)),
            ],
            out_specs=pl.BlockSpec((tw, d), lambda i, *_: (i, 0)),
            scratch_shapes=[
                pltpu.VMEM((2, STAGE_ROWS, d // 2), U32),
                pltpu.SemaphoreType.DMA((2,)),
            ],
        ),
        out_shape=jax.ShapeDtypeStruct((n_tok, d), F32),
        compiler_params=pltpu.CompilerParams(
            dimension_semantics=("arbitrary",), vmem_limit_bytes=VMEM_LIMIT),
        name="combine",
    )(*plan, ys, h2, pos_rows, gate_rows, norm_g.reshape(1, -1).astype(F32))


def _moe(h2, xn, eidx, gate, rank, tcnt, w_up, b_up, w_down, b_down, norm_final_g):
    n_tok, d = h2.shape
    tm = EXPERT_TILE
    tw = DISPATCH_TILE
    n_steps = n_tok // tw
    n_exp = N_EXPERTS
    e_ids = jnp.arange(n_exp, dtype=I32)
    al = SLAB_ALIGN
    tile_cnt = tcnt.reshape(n_steps, n_exp, LANES)[:, :, 0].astype(I32)
    slab_rows = (tile_cnt + al - 1) // al * al
    exp_rows = jnp.sum(slab_rows, axis=0)
    region = (exp_rows + tm - 1) // tm * tm
    pad_end = jnp.cumsum(region)
    pad_start = pad_end - region
    n_slots = (n_tok * TOP_K + n_steps * n_exp * (al - 1) + n_exp * (tm - 1) + tm - 1) // tm * tm
    n_tiles = n_slots // tm

    tile_carry = jnp.cumsum(tile_cnt, axis=0) - tile_cnt
    tile_off = jnp.cumsum(slab_rows, axis=1) - slab_rows
    slab_row0 = pad_start[None, :] + jnp.cumsum(slab_rows, axis=0) - slab_rows

    def chunks(rows):
        return rows // BIG_CHUNK, rows % BIG_CHUNK // al

    def flat(cnt, src0, dst0, size, n_max):
        run = jnp.cumsum(cnt, axis=1)
        j = jnp.arange(n_max, dtype=I32)
        e_of = jnp.sum(run[:, None, :] <= j[None, :, None], axis=-1)
        sel = e_of[:, :, None] == e_ids[None, None, :]
        pick = lambda a: jnp.sum(jnp.where(sel, a[:, None, :], 0), axis=-1)
        local = (j[None, :] - pick(run - cnt)) * size
        return run[:, -1], (pick(src0) + local).reshape(-1), (pick(dst0) + local).reshape(-1)

    nbig, nsmall = chunks(slab_rows)
    big_tot, big_src, big_dst = flat(nbig, tile_off, slab_row0, BIG_CHUNK, MAX_BIG)
    small_tot, small_src, small_dst = flat(nsmall, tile_off + nbig * BIG_CHUNK,
                                           slab_row0 + nbig * BIG_CHUNK, al, MAX_SMALL)
    pad_groups = (jnp.sum(slab_rows, axis=1) - tw * TOP_K) // al
    plan = (big_tot, small_tot, pad_groups, big_src, big_dst, small_src, small_dst)
    tail_rows = jnp.concatenate([region - exp_rows, n_slots - pad_end[-1:]])
    tails = (*chunks(tail_rows), jnp.concatenate([pad_start + exp_rows, pad_end[-1:]]))

    base_t = jnp.repeat((tile_off - tile_carry).T, tw, axis=1)
    e_sel = eidx[None] == e_ids[:, None, None]
    pos8 = jnp.sum(jnp.where(e_sel, base_t[:, None, :], 0), axis=0) + rank

    tile_start = jnp.arange(n_tiles, dtype=I32) * tm
    tile_e = jnp.minimum(jnp.sum(pad_end[None, :] <= tile_start[:, None], axis=-1),
                         n_exp - 1).astype(I32)
    n_used = (pad_end[-1] // tm).astype(I32).reshape(1)
    t_ids = jnp.arange(n_tiles, dtype=I32)
    first = jnp.logical_and(jnp.logical_or(t_ids == 0, tile_e != jnp.roll(tile_e, 1)),
                            t_ids < n_used[0])
    later = jnp.logical_and(first[None, :], t_ids[None, :] > t_ids[:, None])
    nxt_tile = jnp.min(jnp.where(later, t_ids[None, :], n_tiles), axis=1)
    nxt_e = jnp.sum(jnp.where(t_ids[None, :] == nxt_tile[:, None], tile_e[None, :], 0), axis=1)
    nxt = jnp.where(nxt_tile < n_tiles, nxt_e, -1).astype(I32)
    first = first.astype(I32)

    xs = _dispatch(xn, pos8, plan, tails, n_slots)
    ys = _experts(xs, tile_e, n_used, first, nxt, w_up, b_up, w_down, b_down)
    return _combine(ys, h2, pos8, gate, plan, norm_final_g)


def kernel(x, norm_mix_g, w_in, s5_log_dt, s5_a_re, s5_a_im, s5_b_re, s5_b_im, s5_c_re, s5_c_im,
           s5_d, s5_w_glu, s5_b_glu, ml_conv_w, ml_conv_b, ml_b_gates, ml_norm_g, w_out,
           norm_ffn_g, w_router, b_router, w_up, b_up, w_down, b_down, norm_final_g):
    bsz, seq, d = x.shape
    depth = w_in.shape[0]
    assert depth == 1, "single-layer block"
    l = 0
    x2d = x.reshape(bsz * seq, d)
    qk, v, o, ut, gt = _in_proj(x2d, norm_mix_g[l], w_in[l], ml_b_gates[l], bsz, seq)
    g_t = _s5(ut, s5_log_dt[l], s5_a_re[l], s5_a_im[l], s5_b_re[l], s5_b_im[l],
              s5_c_re[l], s5_c_im[l], s5_d[l])
    y_ml = _mlstm(qk, v, o, gt, ml_conv_w[l], ml_conv_b[l], ml_norm_g[l], bsz, seq)
    y_ml = y_ml.reshape(bsz * seq, -1)
    h2, xn, eidx, gate, rank, tcnt = _post(x2d, g_t, y_ml, s5_w_glu[l], s5_b_glu[l], w_out[l],
                                           norm_ffn_g[l], w_router[l], b_router[l])
    out = _moe(h2, xn, eidx, gate, rank, tcnt, w_up[l], b_up[l], w_down[l], b_down[l],
               norm_final_g)
    return out.reshape(bsz, seq, d)
```
